```python
import math
import jax
import jax.numpy as jnp
from jax import lax
import numpy as np

D_MODEL = 1024
BATCH = 32
SEQ = 256
DEPTH = 2
DEC_BATCH = 2
DEC_SEQ = 2048
PAST_LEN = 512

GRID_W = 64
MIX_WIDTH = D_MODEL
A_WIDTH = MIX_WIDTH // 2
A_HEADS = 4
A_DV = A_WIDTH // A_HEADS
A_DK = A_DV
QK = A_HEADS * A_DK
CONV_K = 3
DELTA_CHUNK = 64
B_WIDTH = MIX_WIDTH // 4
B_GROUPS = 4
B_GC = B_WIDTH // B_GROUPS
SGU_CHUNK = 128
C_WIDTH = MIX_WIDTH - A_WIDTH - B_WIDTH
POOL_WINDOWS = (2, 4, 8, 16)
C_GC = C_WIDTH // len(POOL_WINDOWS)
IN_SPLITS = (QK, QK, A_WIDTH, A_WIDTH, 2 * A_HEADS, 2 * A_HEADS, B_WIDTH, B_WIDTH, C_WIDTH)
IN_COLS = sum(IN_SPLITS)
FF_HIDDEN = ((8 * D_MODEL // 3 + 255) // 256) * 256
N_MOD = 6
EPS = 1e-6

kernel_name = 'hybrid_delta_sgu_pool_dit_step'


def rmsnorm(x, g):
    xf = x.astype(jnp.float32)
    y = xf * lax.rsqrt(jnp.mean(xf * xf, axis=-1, keepdims=True) + EPS)
    return (y * g.astype(jnp.float32)).astype(x.dtype)


def l2norm(x):
    xf = x.astype(jnp.float32)
    return xf * lax.rsqrt(jnp.sum(xf * xf, axis=-1, keepdims=True) + EPS)


def short_conv(x, w):
    k = w.shape[0]
    pad = k // 2
    return lax.conv_general_dilated(
        x, w[:, None, :].astype(x.dtype), window_strides=(1,), padding=[(pad, k - 1 - pad)],
        dimension_numbers=('NWC', 'WIO', 'NWC'), feature_group_count=x.shape[-1])


def gated_delta_chunked(q, k, v, g, beta, s0):
    bsz, n, h, _ = q.shape
    dv = v.shape[-1]
    nc = n // DELTA_CHUNK
    cs = DELTA_CHUNK

    def chunks(t):
        t = t.reshape((bsz, nc, cs, h) + t.shape[3:])
        return jnp.moveaxis(jnp.moveaxis(t, 1, 0), 3, 2)

    q, k, v, g, beta = chunks(q), chunks(k), chunks(v), chunks(g), chunks(beta)
    gc = jnp.cumsum(g, axis=-1)
    idx = jnp.arange(cs)
    incl = idx[:, None] >= idx[None, :]
    strict = idx[:, None] > idx[None, :]
    decay = jnp.exp(jnp.where(incl, gc[..., :, None] - gc[..., None, :], -jnp.inf))
    kk = jnp.einsum('nbhid,nbhjd->nbhij', k, k)
    lower = jnp.where(strict, beta[..., :, None] * kk * decay, 0.0)
    eye = jnp.eye(cs, dtype=jnp.float32)
    t_mat = lax.linalg.triangular_solve(eye + lower, jnp.broadcast_to(eye, lower.shape),
                                        left_side=True, lower=True)
    u = jnp.einsum('nbhij,nbhjd->nbhid', t_mat, v * beta[..., None])
    w = jnp.einsum('nbhij,nbhjd->nbhid', t_mat, k * (beta * jnp.exp(gc))[..., None])
    qk = jnp.einsum('nbhid,nbhjd->nbhij', q, k) * decay
    q_dec = q * jnp.exp(gc)[..., None]
    k_dec = k * jnp.exp(gc[..., -1:] - gc)[..., None]
    g_tot = jnp.exp(gc[..., -1])

    def step(s, inp):
        qk_c, qd_c, kd_c, u_c, w_c, gt_c = inp
        v_new = u_c - jnp.einsum('bhid,bhde->bhie', w_c, s)
        o = jnp.einsum('bhid,bhde->bhie', qd_c, s) + jnp.einsum('bhij,bhje->bhie', qk_c, v_new)
        s = s * gt_c[..., None, None] + jnp.einsum('bhid,bhie->bhde', kd_c, v_new)
        return s, o

    s_fin, o = lax.scan(step, s0, (qk, q_dec, k_dec, u, w, g_tot))
    o = jnp.moveaxis(jnp.moveaxis(o, 2, 3), 0, 1).reshape(bsz, n, h, dv)
    return o, s_fin


def delta_mixer(q, k, v, z, beta_logit, alpha_logit, conv_w, a_log, dt_bias, norm_g, s0):
    bsz, n, _ = q.shape
    qkv = jax.nn.silu(short_conv(jnp.concatenate([q, k, v], axis=-1), conv_w))
    q, k, v = jnp.split(qkv, [QK, 2 * QK], axis=-1)
    q = l2norm(q.reshape(bsz, n, A_HEADS, A_DK)) * (A_DK ** -0.5)
    k = l2norm(k.reshape(bsz, n, A_HEADS, A_DK))
    v = v.reshape(bsz, n, A_HEADS, A_DV).astype(jnp.float32)
    outs = []
    states = []
    for d in range(2):
        hs = slice(d * A_HEADS, (d + 1) * A_HEADS)
        beta = jax.nn.sigmoid(beta_logit[..., hs].astype(jnp.float32))
        g = -jnp.exp(a_log[d].astype(jnp.float32)) * jax.nn.softplus(
            alpha_logit[..., hs].astype(jnp.float32) + dt_bias[d].astype(jnp.float32))
        args = (q, k, v, g, beta)
        if d == 1:
            args = tuple(jnp.flip(t, axis=1) for t in args)
        o, s = gated_delta_chunked(*args, s0[:, d].astype(jnp.float32))
        if d == 1:
            o = jnp.flip(o, axis=1)
        outs.append(o)
        states.append(s)
    o = outs[0] + outs[1]
    o = rmsnorm(o, norm_g) * jax.nn.silu(z.reshape(bsz, n, A_HEADS, A_DV).astype(jnp.float32))
    return o.reshape(bsz, n, A_WIDTH).astype(z.dtype), jnp.stack(states, axis=1)


def sgu_mixer(u, v, norm_g, w_s, b_s):
    bsz, n, _ = u.shape
    nch = n // SGU_CHUNK
    u = jax.nn.gelu(u)
    v = jax.nn.gelu(v).reshape(bsz, nch, SGU_CHUNK, B_GROUPS, B_GC)
    v = rmsnorm(v, norm_g.reshape(B_GROUPS, B_GC))
    s = jnp.einsum('gts,bnsgc->bntgc', w_s, v) + b_s.T[None, None, :, :, None]
    return u * s.reshape(bsz, n, B_WIDTH)


def pool_mixer(p, w_pool, scale, seg_len):
    bsz, n, _ = p.shape
    ps = p.reshape(bsz * (n // seg_len), seg_len, C_WIDTH).astype(jnp.float32)
    pos = jnp.arange(seg_len)
    outs = []
    for i, win in enumerate(POOL_WINDOWS):
        x = ps[..., i * C_GC:(i + 1) * C_GC]
        csum = jnp.pad(jnp.cumsum(x, axis=1), ((0, 0), (1, 0), (0, 0)))
        lo = jnp.clip(pos - win // 2, 0, seg_len)
        hi = jnp.clip(pos + win - win // 2, 0, seg_len)
        mean = (jnp.take(csum, hi, axis=1) - jnp.take(csum, lo, axis=1)) / \
            (hi - lo).astype(jnp.float32)[None, :, None]
        outs.append(jnp.einsum('bnc,cd->bnd', mean - x, w_pool[i].astype(jnp.float32)))
    y = jnp.concatenate(outs, axis=-1).reshape(bsz, n, C_WIDTH) * scale.astype(jnp.float32)
    return y.astype(p.dtype)


def grid_pos_embed(rows, d):
    quarter = d // 4
    omega = 1.0 / (10000.0 ** (jnp.arange(quarter, dtype=jnp.float32) / quarter))
    r = jnp.arange(rows, dtype=jnp.float32)[:, None] * omega
    cl = jnp.arange(GRID_W, dtype=jnp.float32)[:, None] * omega
    row_emb = jnp.concatenate([jnp.sin(r), jnp.cos(r)], axis=-1)
    col_emb = jnp.concatenate([jnp.sin(cl), jnp.cos(cl)], axis=-1)
    emb = jnp.concatenate([jnp.broadcast_to(row_emb[:, None, :], (rows, GRID_W, d // 2)),
                           jnp.broadcast_to(col_emb[None, :, :], (rows, GRID_W, d // 2))], axis=-1)
    return emb.reshape(rows * GRID_W, d)


def trunk_layer(x, mod, s0, prm, l, pool_seg):
    sh1, sc1, g1, sh2, sc2, g2 = jnp.split(mod, N_MOD, axis=-1)
    h = rmsnorm(x, prm['norm1_g'][l]) * (1 + sc1) + sh1
    cols = jnp.einsum('bnd,de->bne', h, prm['w_in'][l])
    split_idx = [int(i) for i in np.cumsum(IN_SPLITS)[:-1]]
    q, k, v, z, bl, al, u, vb, pp = jnp.split(cols, split_idx, axis=-1)
    ya, s_fin = delta_mixer(q, k, v, z, bl, al, prm['conv_w'][l], prm['a_log'][l],
                            prm['dt_bias'][l], prm['delta_norm_g'][l], s0)
    yb = sgu_mixer(u, vb, prm['sgu_norm_g'][l], prm['w_spatial'][l], prm['b_spatial'][l])
    yc = pool_mixer(pp, prm['w_pool'][l], prm['pool_scale'][l], pool_seg)
    mix = jnp.einsum('bne,ed->bnd', jnp.concatenate([ya, yb, yc], axis=-1), prm['w_out'][l])
    x = x + g1 * mix
    h = rmsnorm(x, prm['norm2_g'][l]) * (1 + sc2) + sh2
    gate, up = jnp.split(jnp.einsum('bnd,df->bnf', h, prm['w_gu'][l]), 2, axis=-1)
    x = x + g2 * jnp.einsum('bnf,fd->bnd', jax.nn.silu(gate) * up, prm['w_down'][l])
    return x, s_fin


def setup_inputs(seed: int = 0) -> dict:
    key = jax.random.key(seed)
    ks = jax.random.split(key, 24)
    f32 = jnp.float32

    def nrm(k, shape, scale):
        return jax.random.normal(k, shape, f32) * scale

    dt = jnp.exp(jax.random.uniform(ks[8], (DEPTH, 2, A_HEADS), f32, math.log(1e-3), math.log(1e-1)))
    return {
        'x_prompt': nrm(ks[0], (BATCH, SEQ, D_MODEL), 1.0),
        'x_sample': nrm(ks[1], (DEC_BATCH, DEC_SEQ, D_MODEL), 1.0),
        'state_delta': nrm(ks[2], (DEC_BATCH, DEPTH, 2, A_HEADS, A_DK, A_DV), 0.1),
        'c': nrm(ks[3], (DEC_BATCH, D_MODEL), 1.0),
        'c_ctx': nrm(ks[4], (D_MODEL,), 1.0),
        'w_in': nrm(ks[5], (DEPTH, D_MODEL, IN_COLS), D_MODEL ** -0.5),
        'conv_w': nrm(ks[6], (DEPTH, CONV_K, 2 * QK + A_WIDTH), CONV_K ** -0.5),
        'a_log': jnp.log(jax.random.uniform(ks[7], (DEPTH, 2, A_HEADS), f32, 1.0, 16.0)),
        'dt_bias': dt + jnp.log(-jnp.expm1(-dt)),
        'delta_norm_g': 1.0 + nrm(ks[9], (DEPTH, A_DV), 0.02),
        'sgu_norm_g': 1.0 + nrm(ks[10], (DEPTH, B_WIDTH), 0.02),
        'w_spatial': nrm(ks[11], (DEPTH, B_GROUPS, SGU_CHUNK, SGU_CHUNK), SGU_CHUNK ** -0.5),
        'b_spatial': 1.0 + nrm(ks[12], (DEPTH, B_GROUPS, SGU_CHUNK), 0.02),
        'w_pool': nrm(ks[13], (DEPTH, len(POOL_WINDOWS), C_GC, C_GC), C_GC ** -0.5),
        'pool_scale': 1.0 + nrm(ks[14], (DEPTH, C_WIDTH), 0.02),
        'w_out': nrm(ks[15], (DEPTH, MIX_WIDTH, D_MODEL), MIX_WIDTH ** -0.5),
        'norm1_g': 1.0 + nrm(ks[16], (DEPTH, D_MODEL), 0.02),
        'norm2_g': 1.0 + nrm(ks[17], (DEPTH, D_MODEL), 0.02),
        'w_mod': nrm(ks[18], (DEPTH, D_MODEL, N_MOD * D_MODEL), 0.5 * D_MODEL ** -0.5),
        'b_mod': nrm(ks[19], (DEPTH, N_MOD * D_MODEL), 0.02),
        'w_gu': nrm(ks[20], (DEPTH, D_MODEL, 2 * FF_HIDDEN), D_MODEL ** -0.5),
        'w_down': nrm(ks[21], (DEPTH, FF_HIDDEN, D_MODEL), FF_HIDDEN ** -0.5),
        'norm_f': 1.0 + nrm(ks[22], (D_MODEL,), 0.02),
    }


def reference(x_prompt, x_sample, state_delta, c, c_ctx, w_in, conv_w, a_log, dt_bias,
              delta_norm_g, sgu_norm_g, w_spatial, b_spatial, w_pool, pool_scale, w_out,
              norm1_g, norm2_g, w_mod, b_mod, w_gu, w_down, norm_f):
    prm = {'w_in': w_in, 'conv_w': conv_w, 'a_log': a_log, 'dt_bias': dt_bias,
           'delta_norm_g': delta_norm_g, 'sgu_norm_g': sgu_norm_g, 'w_spatial': w_spatial,
           'b_spatial': b_spatial, 'w_pool': w_pool, 'pool_scale': pool_scale, 'w_out': w_out,
           'norm1_g': norm1_g, 'norm2_g': norm2_g, 'w_gu': w_gu, 'w_down': w_down}

    xc = x_prompt
    zero_state = jnp.zeros((x_prompt.shape[0], 2, A_HEADS, A_DK, A_DV), jnp.float32)
    ctx_states = []
    for l in range(DEPTH):
        mod_ctx = (jnp.einsum('d,de->e', jax.nn.silu(c_ctx), w_mod[l]) + b_mod[l])[None, None, :]
        xc, s_ctx = trunk_layer(xc, mod_ctx, zero_state, prm, l, xc.shape[1])
        ctx_states.append(s_ctx)
    y_prompt = rmsnorm(xc, norm_f)
    new_state_delta = jnp.stack(ctx_states, axis=1).astype(x_prompt.dtype)

    rows = x_sample.shape[1] // GRID_W
    xs = x_sample + grid_pos_embed(rows, D_MODEL).astype(x_sample.dtype)[None]
    for l in range(DEPTH):
        mod_lat = (jnp.einsum('bd,de->be', jax.nn.silu(c), w_mod[l]) + b_mod[l])[:, None, :]
        xs, _ = trunk_layer(xs, mod_lat, state_delta[:, l], prm, l, GRID_W)
    y_sample = rmsnorm(xs, norm_f)

    return (y_prompt, y_sample, new_state_delta)
```

```python
import functools
import math

import numpy as np
import jax
import jax.numpy as jnp
from jax import lax
from jax.experimental import pallas as pl
from jax.experimental.pallas import tpu as pltpu

F32 = jnp.float32
BF16 = jnp.bfloat16
HIGHEST = lax.Precision.HIGHEST

D_MODEL = 1024
DEPTH = 2
GRID_W = 64
HEADS = 4
HEAD_DIM = 128
QK = HEADS * HEAD_DIM
A_WIDTH = HEADS * HEAD_DIM
QKV = 2 * QK + A_WIDTH
CHUNK = 64
B_WIDTH = 256
B_GROUPS = 4
B_GC = 64
SGU_CHUNK = 128
C_WIDTH = 256
POOL_WINDOWS = (2, 4, 8, 16)
C_GC = 64
FF_HIDDEN = 2816
FF_SPLIT = 2
N_MOD = 6
EPS = 1e-6
NEG_BIG = -1e30

DELTA_BLOCK = 2048
POOL_TILE = 256
VMEM_LIMIT = 56 * 1024 * 1024


def _dot(a, b, precision=None):
    return jnp.dot(a, b, preferred_element_type=F32, precision=precision)


def _bdot(a, b):
    return jnp.dot(a.astype(BF16), b.astype(BF16), preferred_element_type=F32)


def _sigmoid(x):
    return 1.0 / (1.0 + jnp.exp(-x))


def _silu(x):
    return x * _sigmoid(x)


def _softplus(x):
    return jnp.maximum(x, 0.0) + jnp.log1p(jnp.exp(-jnp.abs(x)))


def _gelu_tanh(x):
    c = math.sqrt(2.0 / math.pi)
    return 0.5 * x * (1.0 + jnp.tanh(c * (x + 0.044715 * (x * x * x))))


def _rms_rows(x):
    return x * lax.rsqrt(jnp.mean(x * x, axis=-1, keepdims=True) + EPS)


def _params(n_grid):
    return pltpu.CompilerParams(dimension_semantics=("arbitrary",) * n_grid,
                                vmem_limit_bytes=VMEM_LIMIT)


def _const_spec(shape):
    nd = len(shape)
    return pl.BlockSpec(shape, lambda *_: (0,) * nd, pipeline_mode=pl.Buffered(1))


def _mod_kernel(c_ref, w_ref, b_ref, o_ref):
    a = _silu(c_ref[...])
    o_ref[0] = _bdot(a, w_ref[0]) + b_ref[0]


def _modulation(cvec, w_mod, b_mod):
    tn = 1536
    n_out = N_MOD * D_MODEL
    return pl.pallas_call(
        _mod_kernel,
        grid=(DEPTH, n_out // tn),
        in_specs=[pl.BlockSpec((8, D_MODEL), lambda l, j: (0, 0)),
                  pl.BlockSpec((1, D_MODEL, tn), lambda l, j: (l, 0, j)),
                  pl.BlockSpec((1, 1, tn), lambda l, j: (l, 0, j))],
        out_specs=pl.BlockSpec((1, 8, tn), lambda l, j: (l, 0, j)),
        out_shape=jax.ShapeDtypeStruct((DEPTH, 8, n_out), F32),
        compiler_params=_params(2),
        name="modulation",
    )(cvec, w_mod, b_mod.reshape(DEPTH, 1, n_out))


def _pos_kernel(x_ref, p_ref, o_ref):
    o_ref[...] = x_ref[...] + p_ref[...]


def _add_pos(x, pos):
    b, n, d = x.shape
    tm = 512
    return pl.pallas_call(
        _pos_kernel,
        grid=(b, n // tm),
        in_specs=[pl.BlockSpec((None, tm, d), lambda i, j: (i, j, 0)),
                  pl.BlockSpec((tm, d), lambda i, j: (j, 0))],
        out_specs=pl.BlockSpec((None, tm, d), lambda i, j: (i, j, 0)),
        out_shape=jax.ShapeDtypeStruct(x.shape, F32),
        compiler_params=_params(2),
        name="add_pos",
    )(x, pos)


def _modulated_norm(x, gain, shift, scale):
    return _rms_rows(x) * gain * (1.0 + scale) + shift


def _in_kernel(*refs, seq_len, tile, row0, halo):
    if halo:
        (x_ref, xp_ref, xn_ref, mod_ref, g_ref, wqkv_ref, wz_ref, wba_ref, wuv_ref, wp_ref,
         cw_ref, qkv_ref, z_ref, ba_ref, uv_ref, p_ref) = refs
    else:
        (x_ref, mod_ref, g_ref, wqkv_ref, wz_ref, wba_ref, wuv_ref, wp_ref,
         cw_ref, qkv_ref, z_ref, ba_ref, uv_ref, p_ref) = refs
    i = pl.program_id(0)
    tiles_per_seq = max(seq_len // tile, 1)
    row = row0 + i // tiles_per_seq if row0 else 0
    m = mod_ref[pl.ds(row, 1), :]
    shift = m[:, 0:D_MODEL]
    scale = m[:, D_MODEL:2 * D_MODEL]
    gain = g_ref[...]
    hf = _modulated_norm(x_ref[...], gain, shift, scale)
    hb = hf.astype(BF16)
    if halo:
        hp = _modulated_norm(xp_ref[...], gain, shift, scale)
        hn = _modulated_norm(xn_ref[...], gain, shift, scale)
        hq = jnp.concatenate([hf, hp, hn], axis=0).astype(BF16)
    else:
        hq = hb
    pre = _dot(hq, wqkv_ref[...])
    cur = pre[0:tile]
    ridx = lax.broadcasted_iota(jnp.int32, (tile, 1), 0)
    if halo:
        pos_in_seq = (i % tiles_per_seq) * tile + ridx
        prev_row = pre[tile + 7:tile + 8]
        next_row = pre[tile + 8:tile + 9]
    else:
        pos_in_seq = ridx % seq_len
        prev_row = jnp.zeros((1, QKV), F32)
        next_row = prev_row
    prev = pltpu.roll(cur, 1, 0)
    prev = jnp.where(ridx == 0, prev_row, prev)
    prev = jnp.where(pos_in_seq == 0, 0.0, prev)
    nxt = pltpu.roll(cur, tile - 1, 0)
    nxt = jnp.where(ridx == tile - 1, next_row, nxt)
    nxt = jnp.where(pos_in_seq == seq_len - 1, 0.0, nxt)
    cw = cw_ref[...]
    act = _silu(prev * cw[0:1] + cur * cw[1:2] + nxt * cw[2:3])
    for h in range(HEADS):
        qs = slice(h * HEAD_DIM, (h + 1) * HEAD_DIM)
        ks = slice(QK + h * HEAD_DIM, QK + (h + 1) * HEAD_DIM)
        q = act[:, qs]
        k = act[:, ks]
        qkv_ref[:, qs] = q * (lax.rsqrt(jnp.sum(q * q, axis=-1, keepdims=True) + EPS)
                              * (HEAD_DIM ** -0.5))
        qkv_ref[:, ks] = k * lax.rsqrt(jnp.sum(k * k, axis=-1, keepdims=True) + EPS)
    qkv_ref[:, 2 * QK:] = act[:, 2 * QK:]
    z_ref[...] = _dot(hb, wz_ref[...])
    ba_ref[...] = _dot(hb, wba_ref[...])
    uv_ref[...] = _dot(hb, wuv_ref[...])
    p_ref[...] = _dot(hb, wp_ref[...])


def _input_stage(x, mods_l, norm_g, w, conv_w, *, seq_len, row0):
    n = x.shape[0]
    tile = 512
    halo = seq_len > tile
    tiles_per_seq = max(seq_len // tile, 1)
    n_tiles = n // tile
    h8 = tile // 8
    last8 = n // 8 - 1
    in_specs = [pl.BlockSpec((tile, D_MODEL), lambda i: (i, 0))]
    args = [x]
    if halo:
        in_specs += [pl.BlockSpec((8, D_MODEL), lambda i: (jnp.maximum(i * h8 - 1, 0), 0)),
                     pl.BlockSpec((8, D_MODEL), lambda i: (jnp.minimum((i + 1) * h8, last8), 0))]
        args += [x, x]
    in_specs += [_const_spec((8, N_MOD * D_MODEL)), _const_spec((1, D_MODEL)),
                 _const_spec((D_MODEL, QKV)), _const_spec((D_MODEL, A_WIDTH)),
                 _const_spec((D_MODEL, 256)), _const_spec((D_MODEL, 2 * B_WIDTH)),
                 _const_spec((D_MODEL, C_WIDTH)), _const_spec((3, QKV))]
    args += [mods_l, norm_g, w["qkv"], w["z"], w["ba"], w["uv"], w["p"], conv_w]
    widths = (QKV, A_WIDTH, 256, 2 * B_WIDTH, C_WIDTH)
    return pl.pallas_call(
        functools.partial(_in_kernel, seq_len=seq_len, tile=tile, row0=row0, halo=halo),
        grid=(n_tiles,),
        in_specs=in_specs,
        out_specs=[pl.BlockSpec((tile, c), lambda i: (i, 0)) for c in widths],
        out_shape=[jax.ShapeDtypeStruct((n, c), F32) for c in widths],
        compiler_params=_params(1),
        name="input_stage",
    )(*args)


def _neumann_inverse(low):
    ri = lax.broadcasted_iota(jnp.int32, (CHUNK, CHUNK), 0)
    ci = lax.broadcasted_iota(jnp.int32, (CHUNK, CHUNK), 1)
    acc = jnp.where(ri == ci, 1.0, 0.0) - low
    power = low
    n_sq = int(math.log2(CHUNK)) - 1
    for s in range(n_sq):
        power = _dot(power, power, HIGHEST)
        acc = acc + _dot(acc, power, HIGHEST)
    return acc


def _delta_kernel(*refs, n_sub, n_chunks, has_init):
    if has_init:
        (qkv_ref, ba_ref, alog_ref, dtb_ref, s0_ref, o_ref,
         s_ref, gc_ref, beta_ref, gct_ref) = refs
        st_ref = None
    else:
        (qkv_ref, ba_ref, alog_ref, dtb_ref, o_ref, st_ref,
         s_ref, gc_ref, beta_ref, gct_ref) = refs
    n_total = n_sub * n_chunks
    ri = lax.broadcasted_iota(jnp.int32, (CHUNK, CHUNK), 0)
    ci = lax.broadcasted_iota(jnp.int32, (CHUNK, CHUNK), 1)
    tril = jnp.where(ri >= ci, 1.0, 0.0).astype(F32)
    triu = jnp.where(ri <= ci, 1.0, 0.0).astype(F32)
    lane = lax.broadcasted_iota(jnp.int32, (CHUNK, 128), 1)
    a_row = -jnp.exp(alog_ref[...])
    dtb_row = dtb_ref[...]

    def pre_body(c, carry):
        r0 = pl.multiple_of(c * CHUNK, CHUNK)
        ba = ba_ref[pl.ds(r0, CHUNK), :]
        beta = _sigmoid(ba[:, 0:128])
        g = a_row * _softplus(ba[:, 128:256] + dtb_row)
        gcf = _dot(tril, g, HIGHEST)
        gcb = _dot(triu, g, HIGHEST)
        gc = jnp.where(lane < HEADS, gcf, gcb)
        gc_ref[pl.ds(r0, CHUNK), :] = gc
        beta_ref[pl.ds(r0, CHUNK), :] = beta
        gct_ref[c] = gc.T[0:8, :]
        return carry

    lax.fori_loop(0, n_total, pre_body, 0)
    o_ref[...] = jnp.zeros(o_ref.shape, F32)

    def chain_step(cidx, j):
        d, h = divmod(j, HEADS)
        r0 = pl.multiple_of(cidx * CHUNK, CHUNK)
        rows = pl.ds(r0, CHUNK)
        q = qkv_ref[rows, h * HEAD_DIM:(h + 1) * HEAD_DIM]
        k = qkv_ref[rows, QK + h * HEAD_DIM:QK + (h + 1) * HEAD_DIM]
        v = qkv_ref[rows, 2 * QK + h * HEAD_DIM:2 * QK + (h + 1) * HEAD_DIM]
        gc_col = gc_ref[rows, j:j + 1]
        beta_col = beta_ref[rows, j:j + 1]
        gc_row = gct_ref[cidx][j:j + 1, :]
        incl = (ri >= ci) if d == 0 else (ri <= ci)
        strict = (ri > ci) if d == 0 else (ri < ci)
        decay = jnp.exp(jnp.where(incl, gc_col - gc_row, NEG_BIG))
        k_t = k.T
        qkk = _bdot(jnp.concatenate([q, k], axis=0), k_t)
        qk = qkk[0:CHUNK] * decay
        low = jnp.where(strict, beta_col * qkk[CHUNK:] * decay, 0.0)
        t_mat = _neumann_inverse(low)
        e_col = jnp.exp(gc_col)
        rhs = jnp.concatenate([v * beta_col, k * (beta_col * e_col)], axis=1)
        uw = _bdot(t_mat, rhs)
        u = uw[:, 0:HEAD_DIM]
        w = uw[:, HEAD_DIM:]
        tot = gc_row[:, CHUNK - 1:CHUNK] if d == 0 else gc_row[:, 0:1]
        kd_t = k_t * jnp.exp(tot - gc_row)
        s = s_ref[j]
        wq = _bdot(jnp.concatenate([w, q * e_col], axis=0), s)
        v_new = u - wq[0:CHUNK]
        both = _bdot(jnp.concatenate([qk, kd_t], axis=0), v_new)
        o = wq[CHUNK:] + both[0:CHUNK]
        s_ref[j] = s * jnp.exp(tot) + both[CHUNK:]
        o_ref[rows, h * HEAD_DIM:(h + 1) * HEAD_DIM] += o

    def sub_body(sub, carry):
        if has_init:
            for j in range(2 * HEADS):
                s_ref[j] = s0_ref[j]
        else:
            s_ref[...] = jnp.zeros(s_ref.shape, F32)

        def step(c, carry2):
            cf = sub * n_chunks + c
            cb = sub * n_chunks + (n_chunks - 1 - c)
            for j in range(2 * HEADS):
                chain_step(cf if j < HEADS else cb, j)
            return carry2

        lax.fori_loop(0, n_chunks, step, 0)
        if st_ref is not None:
            st_ref[sub] = s_ref[...]
        return carry

    lax.fori_loop(0, n_sub, sub_body, 0)


def _delta_stage(qkv, ba, alog_row, dtb_row, s0, *, seq_len):
    n = qkv.shape[0]
    n_blocks = n // DELTA_BLOCK
    n_sub = DELTA_BLOCK // seq_len
    n_chunks = seq_len // CHUNK
    has_init = s0 is not None
    in_specs = [pl.BlockSpec((DELTA_BLOCK, QKV), lambda i: (i, 0)),
                pl.BlockSpec((DELTA_BLOCK, 256), lambda i: (i, 0)),
                _const_spec((1, 128)), _const_spec((1, 128))]
    args = [qkv, ba, alog_row, dtb_row]
    o_spec = pl.BlockSpec((DELTA_BLOCK, A_WIDTH), lambda i: (i, 0))
    o_shape = jax.ShapeDtypeStruct((n, A_WIDTH), F32)
    st_shape = (2 * HEADS, HEAD_DIM, HEAD_DIM)
    if has_init:
        assert n_sub == 1
        in_specs.append(pl.BlockSpec((None,) + st_shape, lambda i: (i, 0, 0, 0)))
        args.append(s0)
        out_specs, out_shape = o_spec, o_shape
    else:
        out_specs = [o_spec, pl.BlockSpec((n_sub,) + st_shape, lambda i: (i, 0, 0, 0))]
        out_shape = [o_shape, jax.ShapeDtypeStruct((n // seq_len,) + st_shape, F32)]
    return pl.pallas_call(
        functools.partial(_delta_kernel, n_sub=n_sub, n_chunks=n_chunks, has_init=has_init),
        grid=(n_blocks,),
        in_specs=in_specs,
        out_specs=out_specs,
        out_shape=out_shape,
        scratch_shapes=[pltpu.VMEM(st_shape, F32),
                        pltpu.VMEM((DELTA_BLOCK, 128), F32),
                        pltpu.VMEM((DELTA_BLOCK, 128), F32),
                        pltpu.VMEM((DELTA_BLOCK // CHUNK, 8, CHUNK), F32)],
        compiler_params=_params(1),
        name="delta_stage",
    )(*args)


def _local_kernel(o_ref, z_ref, uv_ref, p_ref, dg_ref, sg_ref, ws_ref, bs_ref, seg_ref,
                  band_ref, icnt_ref, wp_ref, ps_ref, mix_ref, *, tile):
    dg = dg_ref[...]
    for h in range(HEADS):
        cs = slice(h * HEAD_DIM, (h + 1) * HEAD_DIM)
        y = _rms_rows(o_ref[:, cs]) * dg * _silu(z_ref[:, cs])
        mix_ref[:, cs] = y.astype(BF16)
    lane_grp = lax.broadcasted_iota(jnp.int32, (SGU_CHUNK, B_WIDTH), 1) // B_GC
    u = _gelu_tanh(uv_ref[:, 0:B_WIDTH])
    v = _gelu_tanh(uv_ref[:, B_WIDTH:])
    ms = _dot(v * v, seg_ref[...], HIGHEST)
    vn = (v * lax.rsqrt(ms + EPS) * sg_ref[...]).astype(BF16)
    for c in range(tile // SGU_CHUNK):
        rs = slice(c * SGU_CHUNK, (c + 1) * SGU_CHUNK)
        vc = vn[rs]
        s = bs_ref[...]
        for g in range(B_GROUPS):
            s = s + jnp.where(lane_grp == g, _dot(ws_ref[g], vc), 0.0)
        mix_ref[rs, A_WIDTH:A_WIDTH + B_WIDTH] = (u[rs] * s).astype(BF16)
    lane_grp_c = lax.broadcasted_iota(jnp.int32, (POOL_TILE, C_WIDTH), 1) // C_GC
    for c in range(tile // POOL_TILE):
        rs = slice(c * POOL_TILE, (c + 1) * POOL_TILE)
        x = p_ref[rs, :]
        wsum = jnp.zeros((POOL_TILE, C_WIDTH), F32)
        for g in range(len(POOL_WINDOWS)):
            wsum = wsum + jnp.where(lane_grp_c == g, _dot(band_ref[g], x, HIGHEST), 0.0)
        diff = wsum * icnt_ref[...] - x
        y = _bdot(diff, wp_ref[...]) * ps_ref[...]
        mix_ref[rs, A_WIDTH + B_WIDTH:] = y.astype(BF16)


def _local_stage(o, z, uv, p, lw, pool_consts):
    n = o.shape[0]
    tile = 512
    band, icnt = pool_consts
    row = lambda c: pl.BlockSpec((tile, c), lambda i: (i, 0))
    in_specs = [row(A_WIDTH), row(A_WIDTH), row(2 * B_WIDTH), row(C_WIDTH),
                _const_spec((1, HEAD_DIM)), _const_spec((1, B_WIDTH)),
                _const_spec((B_GROUPS, SGU_CHUNK, SGU_CHUNK)), _const_spec((SGU_CHUNK, B_WIDTH)),
                _const_spec((B_WIDTH, B_WIDTH)),
                _const_spec((len(POOL_WINDOWS), POOL_TILE, POOL_TILE)),
                _const_spec((POOL_TILE, C_WIDTH)), _const_spec((C_WIDTH, C_WIDTH)),
                _const_spec((1, C_WIDTH))]
    return pl.pallas_call(
        functools.partial(_local_kernel, tile=tile),
        grid=(n // tile,),
        in_specs=in_specs,
        out_specs=pl.BlockSpec((tile, D_MODEL), lambda i: (i, 0)),
        out_shape=jax.ShapeDtypeStruct((n, D_MODEL), BF16),
        compiler_params=_params(1),
        name="local_stage",
    )(o, z, uv, p, lw["delta_g"], lw["sgu_g"], lw["w_spatial"], lw["b_spatial"], lw["seg"],
      band, icnt, lw["w_pool"], lw["pool_scale"])


def _out_kernel(x_ref, mix_ref, mod_ref, g2_ref, wo_ref, wgu_ref, wd_ref, nf_ref, y_ref,
                *, seq_len, tile, row0, final_norm):
    i = pl.program_id(0)
    tiles_per_seq = max(seq_len // tile, 1)
    row = row0 + i // tiles_per_seq if row0 else 0
    m = mod_ref[pl.ds(row, 1), :]
    gate1 = m[:, 2 * D_MODEL:3 * D_MODEL]
    shift2 = m[:, 3 * D_MODEL:4 * D_MODEL]
    scale2 = m[:, 4 * D_MODEL:5 * D_MODEL]
    gate2 = m[:, 5 * D_MODEL:6 * D_MODEL]
    x1 = x_ref[...] + gate1 * _dot(mix_ref[...], wo_ref[...])
    hb = _modulated_norm(x1, g2_ref[...], shift2, scale2).astype(BF16)
    fc = FF_HIDDEN // FF_SPLIT
    ff = jnp.zeros((tile, D_MODEL), F32)
    for c in range(FF_SPLIT):
        gate = _dot(hb, wgu_ref[:, c * fc:(c + 1) * fc])
        up = _dot(hb, wgu_ref[:, FF_HIDDEN + c * fc:FF_HIDDEN + (c + 1) * fc])
        act = (_silu(gate) * up).astype(BF16)
        ff = ff + _dot(act, wd_ref[c * fc:(c + 1) * fc, :])
    x2 = x1 + gate2 * ff
    if final_norm:
        x2 = _rms_rows(x2) * nf_ref[...]
    y_ref[...] = x2


def _output_stage(x, mix, mods_l, norm2_g, w, norm_f, *, seq_len, row0, final_norm):
    n = x.shape[0]
    tile = 512
    in_specs = [pl.BlockSpec((tile, D_MODEL), lambda i: (i, 0)),
                pl.BlockSpec((tile, D_MODEL), lambda i: (i, 0)),
                _const_spec((8, N_MOD * D_MODEL)), _const_spec((1, D_MODEL)),
                _const_spec((D_MODEL, D_MODEL)), _const_spec((D_MODEL, 2 * FF_HIDDEN)),
                _const_spec((FF_HIDDEN, D_MODEL)), _const_spec((1, D_MODEL))]
    return pl.pallas_call(
        functools.partial(_out_kernel, seq_len=seq_len, tile=tile, row0=row0,
                          final_norm=final_norm),
        grid=(n // tile,),
        in_specs=in_specs,
        out_specs=pl.BlockSpec((tile, D_MODEL), lambda i: (i, 0)),
        out_shape=jax.ShapeDtypeStruct((n, D_MODEL), F32),
        compiler_params=_params(1),
        name="output_stage",
    )(x, mix, mods_l, norm2_g, w["out"], w["gu"], w["down"], norm_f)


def _pool_constants(seg_len):
    pos = np.arange(POOL_TILE)
    seg = pos // seg_len
    band = np.zeros((len(POOL_WINDOWS), POOL_TILE, POOL_TILE), np.float32)
    icnt = np.zeros((POOL_TILE, C_WIDTH), np.float32)
    for g, win in enumerate(POOL_WINDOWS):
        lo = pos - win // 2
        hi = pos + win - win // 2
        inside = (pos[None, :] >= lo[:, None]) & (pos[None, :] < hi[:, None]) \
            & (seg[None, :] == seg[:, None])
        band[g] = inside
        icnt[:, g * C_GC:(g + 1) * C_GC] = (1.0 / inside.sum(axis=1))[:, None]
    return jnp.asarray(band), jnp.asarray(icnt)


def _grid_pos_embed(rows, d):
    quarter = d // 4
    omega = 1.0 / (10000.0 ** (jnp.arange(quarter, dtype=F32) / quarter))
    r = jnp.arange(rows, dtype=F32)[:, None] * omega
    cl = jnp.arange(GRID_W, dtype=F32)[:, None] * omega
    row_emb = jnp.concatenate([jnp.sin(r), jnp.cos(r)], axis=-1)
    col_emb = jnp.concatenate([jnp.sin(cl), jnp.cos(cl)], axis=-1)
    emb = jnp.concatenate([jnp.broadcast_to(row_emb[:, None, :], (rows, GRID_W, d // 2)),
                           jnp.broadcast_to(col_emb[None, :, :], (rows, GRID_W, d // 2))], axis=-1)
    return emb.reshape(rows * GRID_W, d)


def _pad_lanes(a, width):
    return jnp.pad(a, ((0, 0), (0, width - a.shape[1])))


def _layer_weights(l, w_in, w_out, w_gu, w_down, w_pool, w_spatial, b_spatial):
    wi = w_in[l]
    c0 = 2 * QK + 2 * A_WIDTH
    wba = jnp.concatenate([_pad_lanes(wi[:, c0:c0 + 2 * HEADS], 128),
                           _pad_lanes(wi[:, c0 + 2 * HEADS:c0 + 4 * HEADS], 128)], axis=1)
    c1 = c0 + 4 * HEADS
    dense = {
        "qkv": wi[:, 0:QKV].astype(BF16),
        "z": wi[:, QKV:QKV + A_WIDTH].astype(BF16),
        "ba": wba.astype(BF16),
        "uv": wi[:, c1:c1 + 2 * B_WIDTH].astype(BF16),
        "p": wi[:, c1 + 2 * B_WIDTH:].astype(BF16),
        "out": w_out[l].astype(BF16),
        "gu": w_gu[l].astype(BF16),
        "down": w_down[l].astype(BF16),
    }
    wp = jnp.zeros((C_WIDTH, C_WIDTH), F32)
    for g in range(len(POOL_WINDOWS)):
        wp = wp.at[g * C_GC:(g + 1) * C_GC, g * C_GC:(g + 1) * C_GC].set(w_pool[l, g])
    grp = np.arange(B_WIDTH) // B_GC
    seg = jnp.asarray((grp[:, None] == grp[None, :]).astype(np.float32) / B_GC)
    bs = jnp.repeat(b_spatial[l].T, B_GC, axis=1)
    return dense, wp.astype(BF16), seg, w_spatial[l].astype(BF16), bs


def kernel(x_prompt, x_sample, state_delta, c, c_ctx, w_in, conv_w, a_log, dt_bias, delta_norm_g,
           sgu_norm_g, w_spatial, b_spatial, w_pool, pool_scale, w_out, norm1_g, norm2_g, w_mod,
           b_mod, w_gu, w_down, norm_f):
    batch, seq, d = x_prompt.shape
    dec_batch, dec_seq, _ = x_sample.shape
    cvec = jnp.concatenate([c_ctx[None, :], c, jnp.zeros((8 - 1 - dec_batch, d), F32)], axis=0)
    mods = _modulation(cvec, w_mod, b_mod)
    pos = _grid_pos_embed(dec_seq // GRID_W, d)
    xs = _add_pos(x_sample, pos).reshape(dec_batch * dec_seq, d)
    xc = x_prompt.reshape(batch * seq, d)
    pool_ctx = _pool_constants(min(seq, POOL_TILE))
    pool_lat = _pool_constants(GRID_W)
    nf = norm_f.reshape(1, d)
    ctx_states = []
    for l in range(DEPTH):
        dense, wp, seg, ws, bs = _layer_weights(l, w_in, w_out, w_gu, w_down, w_pool, w_spatial,
                                                b_spatial)
        lw = {"delta_g": delta_norm_g[l].reshape(1, HEAD_DIM),
              "sgu_g": sgu_norm_g[l].reshape(1, B_WIDTH), "w_spatial": ws, "b_spatial": bs,
              "seg": seg, "w_pool": wp, "pool_scale": pool_scale[l].reshape(1, C_WIDTH)}
        alog_row = _pad_lanes(a_log[l].reshape(1, 2 * HEADS), 128)
        dtb_row = _pad_lanes(dt_bias[l].reshape(1, 2 * HEADS), 128)
        n1 = norm1_g[l].reshape(1, d)
        n2 = norm2_g[l].reshape(1, d)
        last = l == DEPTH - 1
        s0_lat = state_delta[:, l].reshape(dec_batch, 2 * HEADS, HEAD_DIM, HEAD_DIM)
        streams = (("ctx", xc, seq, 0, None, pool_ctx), ("lat", xs, dec_seq, 1, s0_lat, pool_lat))
        outs = []
        for name, x, seq_len, row0, s0, pool_consts in streams:
            qkv, z, ba, uv, p = _input_stage(x, mods[l], n1, dense, conv_w[l], seq_len=seq_len,
                                             row0=row0)
            res = _delta_stage(qkv, ba, alog_row, dtb_row, s0, seq_len=seq_len)
            if s0 is None:
                o, st = res
                ctx_states.append(st.reshape(batch, 2, HEADS, HEAD_DIM, HEAD_DIM))
            else:
                o = res
            mix = _local_stage(o, z, uv, p, lw, pool_consts)
            outs.append(_output_stage(x, mix, mods[l], n2, dense, nf, seq_len=seq_len, row0=row0,
                                      final_norm=last))
        xc, xs = outs
    y_prompt = xc.reshape(batch, seq, d)
    y_sample = xs.reshape(dec_batch, dec_seq, d)
    new_state = jnp.stack(ctx_states, axis=1)
    return (y_prompt, y_sample, new_state)
```

```python
import functools
import math

import numpy as np
import jax
import jax.numpy as jnp
from jax import lax
from jax.experimental import pallas as pl
from jax.experimental.pallas import tpu as pltpu

F32 = jnp.float32
BF16 = jnp.bfloat16
HIGHEST = lax.Precision.HIGHEST

D_MODEL = 1024
DEPTH = 2
GRID_W = 64
HEADS = 4
HEAD_DIM = 128
QK = HEADS * HEAD_DIM
A_WIDTH = HEADS * HEAD_DIM
QKV = 2 * QK + A_WIDTH
CHUNK = 64
B_WIDTH = 256
B_GROUPS = 4
B_GC = 64
SGU_CHUNK = 128
C_WIDTH = 256
POOL_WINDOWS = (2, 4, 8, 16)
C_GC = 64
FF_HIDDEN = 2816
FF_SPLIT = 2
N_MOD = 6
EPS = 1e-6
NEG_BIG = -1e30

DELTA_BLOCK = 2048
POOL_TILE = 256
VMEM_LIMIT = 56 * 1024 * 1024


def _dot(a, b, precision=None):
    return jnp.dot(a, b, preferred_element_type=F32, precision=precision)


def _bdot(a, b):
    return jnp.dot(a.astype(BF16), b.astype(BF16), preferred_element_type=F32)


def _sigmoid(x):
    return 1.0 / (1.0 + jnp.exp(-x))


def _silu(x):
    return x * _sigmoid(x)


def _softplus(x):
    return jnp.maximum(x, 0.0) + jnp.log1p(jnp.exp(-jnp.abs(x)))


def _gelu_tanh(x):
    c = math.sqrt(2.0 / math.pi)
    return 0.5 * x * (1.0 + jnp.tanh(c * (x + 0.044715 * (x * x * x))))


def _rms_rows(x):
    return x * lax.rsqrt(jnp.mean(x * x, axis=-1, keepdims=True) + EPS)


def _params(n_grid):
    return pltpu.CompilerParams(dimension_semantics=("arbitrary",) * n_grid,
                                vmem_limit_bytes=VMEM_LIMIT)


def _const_spec(shape):
    nd = len(shape)
    return pl.BlockSpec(shape, lambda *_: (0,) * nd, pipeline_mode=pl.Buffered(1))


def _mod_kernel(c_ref, w_ref, b_ref, o_ref):
    a = _silu(c_ref[...])
    o_ref[0] = _bdot(a, w_ref[0]) + b_ref[0]


def _modulation(cvec, w_mod, b_mod):
    tn = 1536
    n_out = N_MOD * D_MODEL
    return pl.pallas_call(
        _mod_kernel,
        grid=(DEPTH, n_out // tn),
        in_specs=[pl.BlockSpec((8, D_MODEL), lambda l, j: (0, 0)),
                  pl.BlockSpec((1, D_MODEL, tn), lambda l, j: (l, 0, j)),
                  pl.BlockSpec((1, 1, tn), lambda l, j: (l, 0, j))],
        out_specs=pl.BlockSpec((1, 8, tn), lambda l, j: (l, 0, j)),
        out_shape=jax.ShapeDtypeStruct((DEPTH, 8, n_out), F32),
        compiler_params=_params(2),
        name="modulation",
    )(cvec, w_mod, b_mod.reshape(DEPTH, 1, n_out))


def _pos_kernel(x_ref, p_ref, o_ref):
    o_ref[...] = x_ref[...] + p_ref[...]


def _add_pos(x, pos):
    b, n, d = x.shape
    tm = 512
    return pl.pallas_call(
        _pos_kernel,
        grid=(b, n // tm),
        in_specs=[pl.BlockSpec((None, tm, d), lambda i, j: (i, j, 0)),
                  pl.BlockSpec((tm, d), lambda i, j: (j, 0))],
        out_specs=pl.BlockSpec((None, tm, d), lambda i, j: (i, j, 0)),
        out_shape=jax.ShapeDtypeStruct(x.shape, F32),
        compiler_params=_params(2),
        name="add_pos",
    )(x, pos)


def _modulated_norm(x, gain, shift, scale):
    return _rms_rows(x) * gain * (1.0 + scale) + shift


def _in_kernel(*refs, seq_len, tile, row0, halo):
    if halo:
        (x_ref, xp_ref, xn_ref, mod_ref, g_ref, wqkv_ref, wz_ref, wba_ref, wuv_ref, wp_ref,
         cw_ref, qkv_ref, z_ref, ba_ref, uv_ref, p_ref) = refs
    else:
        (x_ref, mod_ref, g_ref, wqkv_ref, wz_ref, wba_ref, wuv_ref, wp_ref,
         cw_ref, qkv_ref, z_ref, ba_ref, uv_ref, p_ref) = refs
    i = pl.program_id(0)
    tiles_per_seq = max(seq_len // tile, 1)
    row = row0 + i // tiles_per_seq if row0 else 0
    m = mod_ref[pl.ds(row, 1), :]
    shift = m[:, 0:D_MODEL]
    scale = m[:, D_MODEL:2 * D_MODEL]
    gain = g_ref[...]
    hf = _modulated_norm(x_ref[...], gain, shift, scale)
    hb = hf.astype(BF16)
    if halo:
        hp = _modulated_norm(xp_ref[...], gain, shift, scale)
        hn = _modulated_norm(xn_ref[...], gain, shift, scale)
        hq = jnp.concatenate([hf, hp, hn], axis=0).astype(BF16)
    else:
        hq = hb
    pre = _dot(hq, wqkv_ref[...])
    cur = pre[0:tile]
    ridx = lax.broadcasted_iota(jnp.int32, (tile, 1), 0)
    if halo:
        pos_in_seq = (i % tiles_per_seq) * tile + ridx
        prev_row = pre[tile + 7:tile + 8]
        next_row = pre[tile + 8:tile + 9]
    else:
        pos_in_seq = ridx % seq_len
        prev_row = jnp.zeros((1, QKV), F32)
        next_row = prev_row
    prev = pltpu.roll(cur, 1, 0)
    prev = jnp.where(ridx == 0, prev_row, prev)
    prev = jnp.where(pos_in_seq == 0, 0.0, prev)
    nxt = pltpu.roll(cur, tile - 1, 0)
    nxt = jnp.where(ridx == tile - 1, next_row, nxt)
    nxt = jnp.where(pos_in_seq == seq_len - 1, 0.0, nxt)
    cw = cw_ref[...]
    act = _silu(prev * cw[0:1] + cur * cw[1:2] + nxt * cw[2:3])
    for h in range(HEADS):
        qs = slice(h * HEAD_DIM, (h + 1) * HEAD_DIM)
        ks = slice(QK + h * HEAD_DIM, QK + (h + 1) * HEAD_DIM)
        q = act[:, qs]
        k = act[:, ks]
        qkv_ref[:, qs] = q * (lax.rsqrt(jnp.sum(q * q, axis=-1, keepdims=True) + EPS)
                              * (HEAD_DIM ** -0.5))
        qkv_ref[:, ks] = k * lax.rsqrt(jnp.sum(k * k, axis=-1, keepdims=True) + EPS)
    qkv_ref[:, 2 * QK:] = act[:, 2 * QK:]
    z_ref[...] = _dot(hb, wz_ref[...])
    ba_ref[...] = _dot(hb, wba_ref[...])
    uv_ref[...] = _dot(hb, wuv_ref[...])
    p_ref[...] = _dot(hb, wp_ref[...])


def _input_stage(x, mods_l, norm_g, w, conv_w, *, seq_len, row0):
    n = x.shape[0]
    tile = 512
    halo = seq_len > tile
    tiles_per_seq = max(seq_len // tile, 1)
    n_tiles = n // tile
    h8 = tile // 8
    last8 = n // 8 - 1
    in_specs = [pl.BlockSpec((tile, D_MODEL), lambda i: (i, 0))]
    args = [x]
    if halo:
        in_specs += [pl.BlockSpec((8, D_MODEL), lambda i: (jnp.maximum(i * h8 - 1, 0), 0)),
                     pl.BlockSpec((8, D_MODEL), lambda i: (jnp.minimum((i + 1) * h8, last8), 0))]
        args += [x, x]
    in_specs += [_const_spec((8, N_MOD * D_MODEL)), _const_spec((1, D_MODEL)),
                 _const_spec((D_MODEL, QKV)), _const_spec((D_MODEL, A_WIDTH)),
                 _const_spec((D_MODEL, 256)), _const_spec((D_MODEL, 2 * B_WIDTH)),
                 _const_spec((D_MODEL, C_WIDTH)), _const_spec((3, QKV))]
    args += [mods_l, norm_g, w["qkv"], w["z"], w["ba"], w["uv"], w["p"], conv_w]
    widths = (QKV, A_WIDTH, 256, 2 * B_WIDTH, C_WIDTH)
    return pl.pallas_call(
        functools.partial(_in_kernel, seq_len=seq_len, tile=tile, row0=row0, halo=halo),
        grid=(n_tiles,),
        in_specs=in_specs,
        out_specs=[pl.BlockSpec((tile, c), lambda i: (i, 0)) for c in widths],
        out_shape=[jax.ShapeDtypeStruct((n, c), F32) for c in widths],
        compiler_params=_params(1),
        name="input_stage",
    )(*args)


def _triangular_inverse(tri, lower):
    ri = lax.broadcasted_iota(jnp.int32, (CHUNK, CHUNK), 0)
    ci = lax.broadcasted_iota(jnp.int32, (CHUNK, CHUNK), 1)
    if not lower:
        ri, ci = ci, ri
    t = jnp.where(ri == ci, 1.0, 0.0) - jnp.where((ri % 2 == 1) & (ci == ri - 1), tri, 0.0)
    s = 2
    while s < CHUNK:
        couple = (ri // (2 * s) == ci // (2 * s)) & ((ri // s) % 2 == 1) & ((ci // s) % 2 == 0)
        c = jnp.where(couple, tri, 0.0)
        t = t - (_bdot(_bdot(t, c), t) if lower else _bdot(t, _bdot(c, t)))
        s *= 2
    return t


def _delta_kernel(*refs, n_sub, n_chunks, has_init):
    if has_init:
        (qkv_ref, ba_ref, alog_ref, dtb_ref, s0_ref, o_ref,
         s_ref, gc_ref, beta_ref, gct_ref) = refs
        st_ref = None
    else:
        (qkv_ref, ba_ref, alog_ref, dtb_ref, o_ref, st_ref,
         s_ref, gc_ref, beta_ref, gct_ref) = refs
    n_total = n_sub * n_chunks
    ri = lax.broadcasted_iota(jnp.int32, (CHUNK, CHUNK), 0)
    ci = lax.broadcasted_iota(jnp.int32, (CHUNK, CHUNK), 1)
    tril = jnp.where(ri >= ci, 1.0, 0.0).astype(F32)
    triu = jnp.where(ri <= ci, 1.0, 0.0).astype(F32)
    lane = lax.broadcasted_iota(jnp.int32, (CHUNK, 128), 1)
    a_row = -jnp.exp(alog_ref[...])
    dtb_row = dtb_ref[...]

    def pre_body(c, carry):
        r0 = pl.multiple_of(c * CHUNK, CHUNK)
        ba = ba_ref[pl.ds(r0, CHUNK), :]
        beta = _sigmoid(ba[:, 0:128])
        g = a_row * _softplus(ba[:, 128:256] + dtb_row)
        gcf = _dot(tril, g, HIGHEST)
        gcb = _dot(triu, g, HIGHEST)
        gc = jnp.where(lane < HEADS, gcf, gcb)
        gc_ref[pl.ds(r0, CHUNK), :] = gc
        beta_ref[pl.ds(r0, CHUNK), :] = beta
        gct_ref[c] = gc.T[0:8, :]
        return carry

    lax.fori_loop(0, n_total, pre_body, 0)
    o_ref[...] = jnp.zeros(o_ref.shape, F32)

    def chain_step(cidx, j):
        d, h = divmod(j, HEADS)
        r0 = pl.multiple_of(cidx * CHUNK, CHUNK)
        rows = pl.ds(r0, CHUNK)
        q = qkv_ref[rows, h * HEAD_DIM:(h + 1) * HEAD_DIM]
        k = qkv_ref[rows, QK + h * HEAD_DIM:QK + (h + 1) * HEAD_DIM]
        v = qkv_ref[rows, 2 * QK + h * HEAD_DIM:2 * QK + (h + 1) * HEAD_DIM]
        gc_col = gc_ref[rows, j:j + 1]
        beta_col = beta_ref[rows, j:j + 1]
        gc_row = gct_ref[cidx][j:j + 1, :]
        incl = (ri >= ci) if d == 0 else (ri <= ci)
        strict = (ri > ci) if d == 0 else (ri < ci)
        decay = jnp.exp(jnp.where(incl, gc_col - gc_row, NEG_BIG))
        k_t = k.T
        qkk = _bdot(jnp.concatenate([q, k], axis=0), k_t)
        qk = qkk[0:CHUNK] * decay
        low = jnp.where(strict, beta_col * qkk[CHUNK:] * decay, 0.0)
        t_mat = _triangular_inverse(low, d == 0)
        e_col = jnp.exp(gc_col)
        rhs = jnp.concatenate([v * beta_col, k * (beta_col * e_col)], axis=1)
        uw = _bdot(t_mat, rhs)
        u = uw[:, 0:HEAD_DIM]
        w = uw[:, HEAD_DIM:]
        tot = gc_row[:, CHUNK - 1:CHUNK] if d == 0 else gc_row[:, 0:1]
        kd_t = k_t * jnp.exp(tot - gc_row)
        s = s_ref[j]
        wq = _bdot(jnp.concatenate([w, q * e_col], axis=0), s)
        v_new = u - wq[0:CHUNK]
        both = _bdot(jnp.concatenate([qk, kd_t], axis=0), v_new)
        o = wq[CHUNK:] + both[0:CHUNK]
        s_ref[j] = s * jnp.exp(tot) + both[CHUNK:]
        o_ref[rows, h * HEAD_DIM:(h + 1) * HEAD_DIM] += o

    def sub_body(sub, carry):
        if has_init:
            for j in range(2 * HEADS):
                s_ref[j] = s0_ref[j]
        else:
            s_ref[...] = jnp.zeros(s_ref.shape, F32)

        def step(c, carry2):
            cf = sub * n_chunks + c
            cb = sub * n_chunks + (n_chunks - 1 - c)
            for j in range(2 * HEADS):
                chain_step(cf if j < HEADS else cb, j)
            return carry2

        lax.fori_loop(0, n_chunks, step, 0)
        if st_ref is not None:
            st_ref[sub] = s_ref[...]
        return carry

    lax.fori_loop(0, n_sub, sub_body, 0)


def _delta_stage(qkv, ba, alog_row, dtb_row, s0, *, seq_len):
    n = qkv.shape[0]
    n_blocks = n // DELTA_BLOCK
    n_sub = DELTA_BLOCK // seq_len
    n_chunks = seq_len // CHUNK
    has_init = s0 is not None
    in_specs = [pl.BlockSpec((DELTA_BLOCK, QKV), lambda i: (i, 0)),
                pl.BlockSpec((DELTA_BLOCK, 256), lambda i: (i, 0)),
                _const_spec((1, 128)), _const_spec((1, 128))]
    args = [qkv, ba, alog_row, dtb_row]
    o_spec = pl.BlockSpec((DELTA_BLOCK, A_WIDTH), lambda i: (i, 0))
    o_shape = jax.ShapeDtypeStruct((n, A_WIDTH), F32)
    st_shape = (2 * HEADS, HEAD_DIM, HEAD_DIM)
    if has_init:
        assert n_sub == 1
        in_specs.append(pl.BlockSpec((None,) + st_shape, lambda i: (i, 0, 0, 0)))
        args.append(s0)
        out_specs, out_shape = o_spec, o_shape
    else:
        out_specs = [o_spec, pl.BlockSpec((n_sub,) + st_shape, lambda i: (i, 0, 0, 0))]
        out_shape = [o_shape, jax.ShapeDtypeStruct((n // seq_len,) + st_shape, F32)]
    return pl.pallas_call(
        functools.partial(_delta_kernel, n_sub=n_sub, n_chunks=n_chunks, has_init=has_init),
        grid=(n_blocks,),
        in_specs=in_specs,
        out_specs=out_specs,
        out_shape=out_shape,
        scratch_shapes=[pltpu.VMEM(st_shape, F32),
                        pltpu.VMEM((DELTA_BLOCK, 128), F32),
                        pltpu.VMEM((DELTA_BLOCK, 128), F32),
                        pltpu.VMEM((DELTA_BLOCK // CHUNK, 8, CHUNK), F32)],
        compiler_params=_params(1),
        name="delta_stage",
    )(*args)


def _local_kernel(o_ref, z_ref, uv_ref, p_ref, dg_ref, sg_ref, ws_ref, bs_ref, seg_ref,
                  band_ref, icnt_ref, wp_ref, ps_ref, mix_ref, *, tile):
    dg = dg_ref[...]
    for h in range(HEADS):
        cs = slice(h * HEAD_DIM, (h + 1) * HEAD_DIM)
        y = _rms_rows(o_ref[:, cs]) * dg * _silu(z_ref[:, cs])
        mix_ref[:, cs] = y.astype(BF16)
    lane_grp = lax.broadcasted_iota(jnp.int32, (SGU_CHUNK, B_WIDTH), 1) // B_GC
    u = _gelu_tanh(uv_ref[:, 0:B_WIDTH])
    v = _gelu_tanh(uv_ref[:, B_WIDTH:])
    ms = _dot(v * v, seg_ref[...], HIGHEST)
    vn = (v * lax.rsqrt(ms + EPS) * sg_ref[...]).astype(BF16)
    for c in range(tile // SGU_CHUNK):
        rs = slice(c * SGU_CHUNK, (c + 1) * SGU_CHUNK)
        vc = vn[rs]
        s = bs_ref[...]
        for g in range(B_GROUPS):
            s = s + jnp.where(lane_grp == g, _dot(ws_ref[g], vc), 0.0)
        mix_ref[rs, A_WIDTH:A_WIDTH + B_WIDTH] = (u[rs] * s).astype(BF16)
    lane_grp_c = lax.broadcasted_iota(jnp.int32, (POOL_TILE, C_WIDTH), 1) // C_GC
    for c in range(tile // POOL_TILE):
        rs = slice(c * POOL_TILE, (c + 1) * POOL_TILE)
        x = p_ref[rs, :]
        wsum = jnp.zeros((POOL_TILE, C_WIDTH), F32)
        for g in range(len(POOL_WINDOWS)):
            wsum = wsum + jnp.where(lane_grp_c == g, _dot(band_ref[g], x, HIGHEST), 0.0)
        diff = wsum * icnt_ref[...] - x
        y = _bdot(diff, wp_ref[...]) * ps_ref[...]
        mix_ref[rs, A_WIDTH + B_WIDTH:] = y.astype(BF16)


def _local_stage(o, z, uv, p, lw, pool_consts):
    n = o.shape[0]
    tile = 512
    band, icnt = pool_consts
    row = lambda c: pl.BlockSpec((tile, c), lambda i: (i, 0))
    in_specs = [row(A_WIDTH), row(A_WIDTH), row(2 * B_WIDTH), row(C_WIDTH),
                _const_spec((1, HEAD_DIM)), _const_spec((1, B_WIDTH)),
                _const_spec((B_GROUPS, SGU_CHUNK, SGU_CHUNK)), _const_spec((SGU_CHUNK, B_WIDTH)),
                _const_spec((B_WIDTH, B_WIDTH)),
                _const_spec((len(POOL_WINDOWS), POOL_TILE, POOL_TILE)),
                _const_spec((POOL_TILE, C_WIDTH)), _const_spec((C_WIDTH, C_WIDTH)),
                _const_spec((1, C_WIDTH))]
    return pl.pallas_call(
        functools.partial(_local_kernel, tile=tile),
        grid=(n // tile,),
        in_specs=in_specs,
        out_specs=pl.BlockSpec((tile, D_MODEL), lambda i: (i, 0)),
        out_shape=jax.ShapeDtypeStruct((n, D_MODEL), BF16),
        compiler_params=_params(1),
        name="local_stage",
    )(o, z, uv, p, lw["delta_g"], lw["sgu_g"], lw["w_spatial"], lw["b_spatial"], lw["seg"],
      band, icnt, lw["w_pool"], lw["pool_scale"])


def _out_kernel(x_ref, mix_ref, mod_ref, g2_ref, wo_ref, wgu_ref, wd_ref, nf_ref, y_ref,
                *, seq_len, tile, row0, final_norm):
    i = pl.program_id(0)
    tiles_per_seq = max(seq_len // tile, 1)
    row = row0 + i // tiles_per_seq if row0 else 0
    m = mod_ref[pl.ds(row, 1), :]
    gate1 = m[:, 2 * D_MODEL:3 * D_MODEL]
    shift2 = m[:, 3 * D_MODEL:4 * D_MODEL]
    scale2 = m[:, 4 * D_MODEL:5 * D_MODEL]
    gate2 = m[:, 5 * D_MODEL:6 * D_MODEL]
    x1 = x_ref[...] + gate1 * _dot(mix_ref[...], wo_ref[...])
    hb = _modulated_norm(x1, g2_ref[...], shift2, scale2).astype(BF16)
    fc = FF_HIDDEN // FF_SPLIT
    ff = jnp.zeros((tile, D_MODEL), F32)
    for c in range(FF_SPLIT):
        gate = _dot(hb, wgu_ref[:, c * fc:(c + 1) * fc])
        up = _dot(hb, wgu_ref[:, FF_HIDDEN + c * fc:FF_HIDDEN + (c + 1) * fc])
        act = (_silu(gate) * up).astype(BF16)
        ff = ff + _dot(act, wd_ref[c * fc:(c + 1) * fc, :])
    x2 = x1 + gate2 * ff
    if final_norm:
        x2 = _rms_rows(x2) * nf_ref[...]
    y_ref[...] = x2


def _output_stage(x, mix, mods_l, norm2_g, w, norm_f, *, seq_len, row0, final_norm):
    n = x.shape[0]
    tile = 512
    in_specs = [pl.BlockSpec((tile, D_MODEL), lambda i: (i, 0)),
                pl.BlockSpec((tile, D_MODEL), lambda i: (i, 0)),
                _const_spec((8, N_MOD * D_MODEL)), _const_spec((1, D_MODEL)),
                _const_spec((D_MODEL, D_MODEL)), _const_spec((D_MODEL, 2 * FF_HIDDEN)),
                _const_spec((FF_HIDDEN, D_MODEL)), _const_spec((1, D_MODEL))]
    return pl.pallas_call(
        functools.partial(_out_kernel, seq_len=seq_len, tile=tile, row0=row0,
                          final_norm=final_norm),
        grid=(n // tile,),
        in_specs=in_specs,
        out_specs=pl.BlockSpec((tile, D_MODEL), lambda i: (i, 0)),
        out_shape=jax.ShapeDtypeStruct((n, D_MODEL), F32),
        compiler_params=_params(1),
        name="output_stage",
    )(x, mix, mods_l, norm2_g, w["out"], w["gu"], w["down"], norm_f)


def _pool_constants(seg_len):
    pos = np.arange(POOL_TILE)
    seg = pos // seg_len
    band = np.zeros((len(POOL_WINDOWS), POOL_TILE, POOL_TILE), np.float32)
    icnt = np.zeros((POOL_TILE, C_WIDTH), np.float32)
    for g, win in enumerate(POOL_WINDOWS):
        lo = pos - win // 2
        hi = pos + win - win // 2
        inside = (pos[None, :] >= lo[:, None]) & (pos[None, :] < hi[:, None]) \
            & (seg[None, :] == seg[:, None])
        band[g] = inside
        icnt[:, g * C_GC:(g + 1) * C_GC] = (1.0 / inside.sum(axis=1))[:, None]
    return jnp.asarray(band), jnp.asarray(icnt)


def _grid_pos_embed(rows, d):
    quarter = d // 4
    omega = 1.0 / (10000.0 ** (jnp.arange(quarter, dtype=F32) / quarter))
    r = jnp.arange(rows, dtype=F32)[:, None] * omega
    cl = jnp.arange(GRID_W, dtype=F32)[:, None] * omega
    row_emb = jnp.concatenate([jnp.sin(r), jnp.cos(r)], axis=-1)
    col_emb = jnp.concatenate([jnp.sin(cl), jnp.cos(cl)], axis=-1)
    emb = jnp.concatenate([jnp.broadcast_to(row_emb[:, None, :], (rows, GRID_W, d // 2)),
                           jnp.broadcast_to(col_emb[None, :, :], (rows, GRID_W, d // 2))], axis=-1)
    return emb.reshape(rows * GRID_W, d)


def _pad_lanes(a, width):
    return jnp.pad(a, ((0, 0), (0, width - a.shape[1])))


def _layer_weights(l, w_in, w_out, w_gu, w_down, w_pool, w_spatial, b_spatial):
    wi = w_in[l]
    c0 = 2 * QK + 2 * A_WIDTH
    wba = jnp.concatenate([_pad_lanes(wi[:, c0:c0 + 2 * HEADS], 128),
                           _pad_lanes(wi[:, c0 + 2 * HEADS:c0 + 4 * HEADS], 128)], axis=1)
    c1 = c0 + 4 * HEADS
    dense = {
        "qkv": wi[:, 0:QKV].astype(BF16),
        "z": wi[:, QKV:QKV + A_WIDTH].astype(BF16),
        "ba": wba.astype(BF16),
        "uv": wi[:, c1:c1 + 2 * B_WIDTH].astype(BF16),
        "p": wi[:, c1 + 2 * B_WIDTH:].astype(BF16),
        "out": w_out[l].astype(BF16),
        "gu": w_gu[l].astype(BF16),
        "down": w_down[l].astype(BF16),
    }
    wp = jnp.zeros((C_WIDTH, C_WIDTH), F32)
    for g in range(len(POOL_WINDOWS)):
        wp = wp.at[g * C_GC:(g + 1) * C_GC, g * C_GC:(g + 1) * C_GC].set(w_pool[l, g])
    grp = np.arange(B_WIDTH) // B_GC
    seg = jnp.asarray((grp[:, None] == grp[None, :]).astype(np.float32) / B_GC)
    bs = jnp.repeat(b_spatial[l].T, B_GC, axis=1)
    return dense, wp.astype(BF16), seg, w_spatial[l].astype(BF16), bs


def kernel(x_prompt, x_sample, state_delta, c, c_ctx, w_in, conv_w, a_log, dt_bias, delta_norm_g,
           sgu_norm_g, w_spatial, b_spatial, w_pool, pool_scale, w_out, norm1_g, norm2_g, w_mod,
           b_mod, w_gu, w_down, norm_f):
    batch, seq, d = x_prompt.shape
    dec_batch, dec_seq, _ = x_sample.shape
    cvec = jnp.concatenate([c_ctx[None, :], c, jnp.zeros((8 - 1 - dec_batch, d), F32)], axis=0)
    mods = _modulation(cvec, w_mod, b_mod)
    pos = _grid_pos_embed(dec_seq // GRID_W, d)
    xs = _add_pos(x_sample, pos).reshape(dec_batch * dec_seq, d)
    xc = x_prompt.reshape(batch * seq, d)
    pool_ctx = _pool_constants(min(seq, POOL_TILE))
    pool_lat = _pool_constants(GRID_W)
    nf = norm_f.reshape(1, d)
    ctx_states = []
    for l in range(DEPTH):
        dense, wp, seg, ws, bs = _layer_weights(l, w_in, w_out, w_gu, w_down, w_pool, w_spatial,
                                                b_spatial)
        lw = {"delta_g": delta_norm_g[l].reshape(1, HEAD_DIM),
              "sgu_g": sgu_norm_g[l].reshape(1, B_WIDTH), "w_spatial": ws, "b_spatial": bs,
              "seg": seg, "w_pool": wp, "pool_scale": pool_scale[l].reshape(1, C_WIDTH)}
        alog_row = _pad_lanes(a_log[l].reshape(1, 2 * HEADS), 128)
        dtb_row = _pad_lanes(dt_bias[l].reshape(1, 2 * HEADS), 128)
        n1 = norm1_g[l].reshape(1, d)
        n2 = norm2_g[l].reshape(1, d)
        last = l == DEPTH - 1
        s0_lat = state_delta[:, l].reshape(dec_batch, 2 * HEADS, HEAD_DIM, HEAD_DIM)
        streams = (("ctx", xc, seq, 0, None, pool_ctx), ("lat", xs, dec_seq, 1, s0_lat, pool_lat))
        outs = []
        for name, x, seq_len, row0, s0, pool_consts in streams:
            qkv, z, ba, uv, p = _input_stage(x, mods[l], n1, dense, conv_w[l], seq_len=seq_len,
                                             row0=row0)
            res = _delta_stage(qkv, ba, alog_row, dtb_row, s0, seq_len=seq_len)
            if s0 is None:
                o, st = res
                ctx_states.append(st.reshape(batch, 2, HEADS, HEAD_DIM, HEAD_DIM))
            else:
                o = res
            mix = _local_stage(o, z, uv, p, lw, pool_consts)
            outs.append(_output_stage(x, mix, mods[l], n2, dense, nf, seq_len=seq_len, row0=row0,
                                      final_norm=last))
        xc, xs = outs
    y_prompt = xc.reshape(batch, seq, d)
    y_sample = xs.reshape(dec_batch, dec_seq, d)
    new_state = jnp.stack(ctx_states, axis=1)
    return (y_prompt, y_sample, new_state)
```

```python
import functools
import math

import numpy as np
import jax
import jax.numpy as jnp
from jax import lax
from jax.experimental import pallas as pl
from jax.experimental.pallas import tpu as pltpu

F32 = jnp.float32
BF16 = jnp.bfloat16
HIGHEST = lax.Precision.HIGHEST

D_MODEL = 1024
DEPTH = 2
GRID_W = 64
HEADS = 4
HEAD_DIM = 128
QK = HEADS * HEAD_DIM
A_WIDTH = HEADS * HEAD_DIM
QKV = 2 * QK + A_WIDTH
CHUNK = 64
B_WIDTH = 256
B_GROUPS = 4
B_GC = 64
SGU_CHUNK = 128
C_WIDTH = 256
POOL_WINDOWS = (2, 4, 8, 16)
C_GC = 64
FF_HIDDEN = 2816
FF_SPLIT = 2
N_MOD = 6
EPS = 1e-6
NEG_BIG = -1e30

DELTA_BLOCK = 1024
POOL_TILE = 256
VMEM_LIMIT = 56 * 1024 * 1024


def _dot(a, b, precision=None):
    return jnp.dot(a, b, preferred_element_type=F32, precision=precision)


def _bdot(a, b):
    return jnp.dot(a.astype(BF16), b.astype(BF16), preferred_element_type=F32)


def _sigmoid(x):
    return 1.0 / (1.0 + jnp.exp(-x))


def _silu(x):
    return x * _sigmoid(x)


def _softplus(x):
    return jnp.maximum(x, 0.0) + jnp.log1p(jnp.exp(-jnp.abs(x)))


def _gelu_tanh(x):
    c = math.sqrt(2.0 / math.pi)
    return 0.5 * x * (1.0 + jnp.tanh(c * (x + 0.044715 * (x * x * x))))


def _rms_rows(x):
    return x * lax.rsqrt(jnp.mean(x * x, axis=-1, keepdims=True) + EPS)


def _params(n_grid):
    return pltpu.CompilerParams(dimension_semantics=("arbitrary",) * n_grid,
                                vmem_limit_bytes=VMEM_LIMIT)


def _const_spec(shape):
    nd = len(shape)
    return pl.BlockSpec(shape, lambda *_: (0,) * nd, pipeline_mode=pl.Buffered(1))


def _mod_kernel(c_ref, w_ref, b_ref, o_ref):
    a = _silu(c_ref[...])
    o_ref[0] = _bdot(a, w_ref[0]) + b_ref[0]


def _modulation(cvec, w_mod, b_mod):
    tn = 1536
    n_out = N_MOD * D_MODEL
    return pl.pallas_call(
        _mod_kernel,
        grid=(DEPTH, n_out // tn),
        in_specs=[pl.BlockSpec((8, D_MODEL), lambda l, j: (0, 0)),
                  pl.BlockSpec((1, D_MODEL, tn), lambda l, j: (l, 0, j)),
                  pl.BlockSpec((1, 1, tn), lambda l, j: (l, 0, j))],
        out_specs=pl.BlockSpec((1, 8, tn), lambda l, j: (l, 0, j)),
        out_shape=jax.ShapeDtypeStruct((DEPTH, 8, n_out), F32),
        compiler_params=_params(2),
        name="modulation",
    )(cvec, w_mod, b_mod.reshape(DEPTH, 1, n_out))


def _pos_kernel(x_ref, p_ref, o_ref):
    o_ref[...] = x_ref[...] + p_ref[...]


def _add_pos(x, pos):
    b, n, d = x.shape
    tm = 512
    return pl.pallas_call(
        _pos_kernel,
        grid=(b, n // tm),
        in_specs=[pl.BlockSpec((None, tm, d), lambda i, j: (i, j, 0)),
                  pl.BlockSpec((tm, d), lambda i, j: (j, 0))],
        out_specs=pl.BlockSpec((None, tm, d), lambda i, j: (i, j, 0)),
        out_shape=jax.ShapeDtypeStruct(x.shape, F32),
        compiler_params=_params(2),
        name="add_pos",
    )(x, pos)


def _modulated_norm(x, gain, shift, scale):
    return _rms_rows(x) * gain * (1.0 + scale) + shift


def _in_kernel(*refs, seq_len, tile, row0, halo):
    if halo:
        (x_ref, xp_ref, xn_ref, mod_ref, g_ref, wqkv_ref, wz_ref, wba_ref, wuv_ref, wp_ref,
         cw_ref, qkv_ref, z_ref, ba_ref, uv_ref, p_ref) = refs
    else:
        (x_ref, mod_ref, g_ref, wqkv_ref, wz_ref, wba_ref, wuv_ref, wp_ref,
         cw_ref, qkv_ref, z_ref, ba_ref, uv_ref, p_ref) = refs
    i = pl.program_id(0)
    tiles_per_seq = max(seq_len // tile, 1)
    row = row0 + i // tiles_per_seq if row0 else 0
    m = mod_ref[pl.ds(row, 1), :]
    shift = m[:, 0:D_MODEL]
    scale = m[:, D_MODEL:2 * D_MODEL]
    gain = g_ref[...]
    hf = _modulated_norm(x_ref[...], gain, shift, scale)
    hb = hf.astype(BF16)
    if halo:
        hp = _modulated_norm(xp_ref[...], gain, shift, scale)
        hn = _modulated_norm(xn_ref[...], gain, shift, scale)
        hq = jnp.concatenate([hf, hp, hn], axis=0).astype(BF16)
    else:
        hq = hb
    pre = _dot(hq, wqkv_ref[...])
    cur = pre[0:tile]
    ridx = lax.broadcasted_iota(jnp.int32, (tile, 1), 0)
    if halo:
        pos_in_seq = (i % tiles_per_seq) * tile + ridx
        prev_row = pre[tile + 7:tile + 8]
        next_row = pre[tile + 8:tile + 9]
    else:
        pos_in_seq = ridx % seq_len
        prev_row = jnp.zeros((1, QKV), F32)
        next_row = prev_row
    prev = pltpu.roll(cur, 1, 0)
    prev = jnp.where(ridx == 0, prev_row, prev)
    prev = jnp.where(pos_in_seq == 0, 0.0, prev)
    nxt = pltpu.roll(cur, tile - 1, 0)
    nxt = jnp.where(ridx == tile - 1, next_row, nxt)
    nxt = jnp.where(pos_in_seq == seq_len - 1, 0.0, nxt)
    cw = cw_ref[...]
    act = _silu(prev * cw[0:1] + cur * cw[1:2] + nxt * cw[2:3])
    for h in range(HEADS):
        qs = slice(h * HEAD_DIM, (h + 1) * HEAD_DIM)
        ks = slice(QK + h * HEAD_DIM, QK + (h + 1) * HEAD_DIM)
        q = act[:, qs]
        k = act[:, ks]
        qkv_ref[:, qs] = (q * (lax.rsqrt(jnp.sum(q * q, axis=-1, keepdims=True) + EPS)
                               * (HEAD_DIM ** -0.5))).astype(BF16)
        qkv_ref[:, ks] = (k * lax.rsqrt(jnp.sum(k * k, axis=-1, keepdims=True) + EPS)).astype(BF16)
    qkv_ref[:, 2 * QK:] = act[:, 2 * QK:].astype(BF16)
    z_ref[...] = _dot(hb, wz_ref[...])
    ba_ref[...] = _dot(hb, wba_ref[...])
    uv_ref[...] = _dot(hb, wuv_ref[...])
    p_ref[...] = _dot(hb, wp_ref[...])


def _input_stage(x, mods_l, norm_g, w, conv_w, *, seq_len, row0):
    n = x.shape[0]
    tile = 512
    halo = seq_len > tile
    tiles_per_seq = max(seq_len // tile, 1)
    n_tiles = n // tile
    h8 = tile // 8
    last8 = n // 8 - 1
    in_specs = [pl.BlockSpec((tile, D_MODEL), lambda i: (i, 0))]
    args = [x]
    if halo:
        in_specs += [pl.BlockSpec((8, D_MODEL), lambda i: (jnp.maximum(i * h8 - 1, 0), 0)),
                     pl.BlockSpec((8, D_MODEL), lambda i: (jnp.minimum((i + 1) * h8, last8), 0))]
        args += [x, x]
    in_specs += [_const_spec((8, N_MOD * D_MODEL)), _const_spec((1, D_MODEL)),
                 _const_spec((D_MODEL, QKV)), _const_spec((D_MODEL, A_WIDTH)),
                 _const_spec((D_MODEL, 128)), _const_spec((D_MODEL, 2 * B_WIDTH)),
                 _const_spec((D_MODEL, C_WIDTH)), _const_spec((3, QKV))]
    args += [mods_l, norm_g, w["qkv"], w["z"], w["ba"], w["uv"], w["p"], conv_w]
    widths = (QKV, A_WIDTH, 128, 2 * B_WIDTH, C_WIDTH)
    dtypes = (BF16, F32, F32, F32, F32)
    return pl.pallas_call(
        functools.partial(_in_kernel, seq_len=seq_len, tile=tile, row0=row0, halo=halo),
        grid=(n_tiles,),
        in_specs=in_specs,
        out_specs=[pl.BlockSpec((tile, c), lambda i: (i, 0)) for c in widths],
        out_shape=[jax.ShapeDtypeStruct((n, c), t) for c, t in zip(widths, dtypes)],
        compiler_params=_params(1),
        name="input_stage",
    )(*args)


N_LEVELS = int(math.log2(CHUNK))
WIDE = HEADS * CHUNK


def _delta_constants():
    r = np.arange(CHUNK)[:, None]
    c = (np.arange(WIDE) % CHUNK)[None, :]
    lvl = np.zeros((2, N_LEVELS, CHUNK, WIDE), np.float32)
    for d in range(2):
        rr, cc = (r, c) if d == 0 else (c, r)
        for i in range(N_LEVELS):
            s = 1 << i
            lvl[d, i] = (rr // (2 * s) == cc // (2 * s)) & ((rr // s) % 2 == 1) & ((cc // s) % 2 == 0)
    eye = (r == c).astype(np.float32)
    bdm = (np.arange(WIDE)[:, None] // CHUNK == np.arange(WIDE)[None, :] // CHUNK)
    ktm = (np.arange(QK)[:, None] // HEAD_DIM == np.arange(WIDE)[None, :] // CHUNK)
    tri = np.concatenate([np.tril(np.ones((CHUNK, CHUNK))), np.triu(np.ones((CHUNK, CHUNK)))], 0)
    return (jnp.asarray(lvl.reshape(2 * N_LEVELS, CHUNK, WIDE)), jnp.asarray(eye),
            jnp.asarray(bdm, BF16), jnp.asarray(ktm, BF16), jnp.asarray(tri, BF16))


def _pair(a, b, lt_half):
    return jnp.where(lt_half, a, b)


def _block_diag(blocks, zero):
    n = len(blocks)
    rows = [jnp.concatenate([blocks[i] if j == i else zero for j in range(n)], axis=1)
            for i in range(n)]
    return jnp.concatenate(rows, axis=0)


def _delta_kernel(*refs, n_sub, n_chunks, has_init, n_par, cb):
    it = iter(refs)
    qkv_ref, ba_ref, alog_ref, dtb_ref, lvl_ref, eye_ref, bdm_ref, ktm_ref, tri_ref = (
        next(it) for _ in range(9))
    s0_ref = next(it) if has_init else None
    o_ref = next(it)
    st_ref = None if has_init else next(it)
    (s_ref, gc_ref, beta_ref, gct_ref, u_ref, w_ref, qd_ref, qkd_ref, kdt_ref, gt_ref) = it
    n_total = n_sub * n_chunks
    lane = lax.broadcasted_iota(jnp.int32, (CHUNK, 128), 1)
    lt_half = lane < CHUNK
    lt_half_row = lt_half[0:1]
    rw = lax.broadcasted_iota(jnp.int32, (CHUNK, WIDE), 0)
    cw = lax.broadcasted_iota(jnp.int32, (CHUNK, WIDE), 1) & (CHUNK - 1)
    incl = (rw >= cw, rw <= cw)
    a_row = -jnp.exp(alog_ref[...])
    dtb_row = dtb_ref[...]

    def gate_body(c, carry):
        r0 = pl.multiple_of(c * CHUNK, CHUNK)
        ba = ba_ref[pl.ds(r0, CHUNK), :]
        beta = _sigmoid(ba)
        al = pltpu.roll(ba, 128 - 2 * HEADS, 1)
        g = jnp.where(lane < 2 * HEADS, a_row * _softplus(al + dtb_row), 0.0)
        g_hi = g.astype(BF16).astype(F32)
        r1 = g - g_hi
        g_mid = r1.astype(BF16).astype(F32)
        g_lo = (r1 - g_mid).astype(BF16).astype(F32)
        g3 = g_hi + pltpu.roll(g_mid, 8, 1) + pltpu.roll(g_lo, 16, 1)
        cs = _dot(tri_ref[...], g3.astype(BF16))
        cs = cs + pltpu.roll(cs, 128 - 8, 1) + pltpu.roll(cs, 128 - 16, 1)
        gc = jnp.where(lane < HEADS, cs[0:CHUNK], cs[CHUNK:])
        gc_ref[pl.ds(r0, CHUNK), :] = gc
        beta_ref[pl.ds(r0, CHUNK), :] = beta
        gct_ref[c] = jnp.concatenate([gc, gc], axis=0).T[0:8, :]
        return carry

    lax.fori_loop(0, n_total, gate_body, 0)

    zero_blk = jnp.zeros((CHUNK, HEAD_DIM), BF16)
    hs = [slice(h * HEAD_DIM, (h + 1) * HEAD_DIM) for h in range(HEADS)]

    def wide_bd(y):
        yb = y.astype(BF16)
        return jnp.concatenate([yb] * HEADS, axis=0) * bdm_ref[...]

    def pre_body(bi, carry):
        groups = []
        for i in range(cb):
            c = bi * cb + i
            r0 = pl.multiple_of(c * CHUNK, CHUNK)
            rows = pl.ds(r0, CHUNK)
            qb = qkv_ref[rows, 0:QK]
            kb16 = qkv_ref[rows, QK:2 * QK]
            qf = qb.astype(F32)
            kf = kb16.astype(F32)
            vf = qkv_ref[rows, 2 * QK:].astype(F32)
            kt = jnp.concatenate(
                [jnp.concatenate([kf[:, 0:128], kf[:, 128:256]], axis=0).T,
                 jnp.concatenate([kf[:, 256:384], kf[:, 384:512]], axis=0).T], axis=1)
            ktbd = jnp.concatenate([kt.astype(BF16)] * HEADS, axis=0) * ktm_ref[...]
            qkk = _dot(jnp.concatenate([qb, kb16], axis=0), ktbd)
            cols = gc_ref[rows, :]
            bcols = beta_ref[rows, :]
            at = gct_ref[c]
            for d in range(2):
                j0 = d * HEADS
                bg = [jnp.broadcast_to(cols[:, j0 + h:j0 + h + 1], (CHUNK, 128)) for h in range(HEADS)]
                bb = [jnp.broadcast_to(bcols[:, j0 + h:j0 + h + 1], (CHUNK, 128)) for h in range(HEADS)]
                gcw = jnp.concatenate([_pair(bg[0], bg[1], lt_half), _pair(bg[2], bg[3], lt_half)], 1)
                bw = jnp.concatenate([_pair(bb[0], bb[1], lt_half), _pair(bb[2], bb[3], lt_half)], 1)
                gr = jnp.concatenate(
                    [_pair(at[j0:j0 + 1], at[j0 + 1:j0 + 2], lt_half_row),
                     _pair(at[j0 + 2:j0 + 3], at[j0 + 3:j0 + 4], lt_half_row)], axis=1)
                decay = jnp.exp(jnp.where(incl[d], gcw - gr, NEG_BIG))
                m = bw * qkk[CHUNK:] * decay
                qkd_ref[d, rows, :] = (qkk[0:CHUNK] * decay).astype(BF16)
                en = jnp.concatenate([jnp.exp(x) for x in bg], axis=1)
                bn = jnp.concatenate(bb, axis=1)
                qd_ref[d, rows, :] = (qf * en).astype(BF16)
                vb = (vf * bn).astype(BF16)
                kbe = (kf * (bn * en)).astype(BF16)
                e0 = CHUNK - 1 if d == 0 else 0
                bt = [jnp.broadcast_to(at[j0 + h:j0 + h + 1, e0:e0 + 1], (1, 128)) for h in range(HEADS)]
                tw = jnp.concatenate([_pair(bt[0], bt[1], lt_half_row),
                                      _pair(bt[2], bt[3], lt_half_row)], axis=1)
                kdt_ref[d, pl.ds(pl.multiple_of(c * HEAD_DIM, HEAD_DIM), HEAD_DIM), :] = (
                    kt * jnp.exp(tw - gr)).astype(BF16)
                gt_ref[d, c] = jnp.broadcast_to(
                    jnp.concatenate([jnp.exp(x) for x in bt], axis=1), (8, A_WIDTH))
                groups.append((d, rows, m, vb, kbe))
        ts = [eye_ref[...] - g[2] * lvl_ref[g[0] * N_LEVELS] for g in groups]
        for lv in range(1, N_LEVELS):
            xs = [_dot(t.astype(BF16), wide_bd(g[2] * lvl_ref[g[0] * N_LEVELS + lv]))
                  for t, g in zip(ts, groups)]
            ts = [t - _dot(x.astype(BF16), wide_bd(t)) for t, x in zip(ts, xs)]
        for t, (d, rows, _, vb, kbe) in zip(ts, groups):
            rbd = jnp.concatenate(
                [_block_diag([vb[:, s] for s in hs], zero_blk),
                 _block_diag([kbe[:, s] for s in hs], zero_blk)], axis=1)
            uw = _dot(t.astype(BF16), rbd)
            u_ref[d, rows, :] = uw[:, 0:A_WIDTH]
            w_ref[d, rows, :] = uw[:, A_WIDTH:].astype(BF16)
        return carry

    lax.fori_loop(0, n_total // cb, pre_body, 0)

    o_ref[...] = jnp.zeros(o_ref.shape, F32)
    zero_s = jnp.zeros((HEAD_DIM, HEAD_DIM), BF16)

    def scan_step(sub0, c):
        groups = [(p, d) for p in range(n_par) for d in range(2)]
        cidx = [(sub0 + p) * n_chunks + (c if d == 0 else n_chunks - 1 - c) for p, d in groups]
        rows = [pl.ds(pl.multiple_of(ci * CHUNK, CHUNK), CHUNK) for ci in cidx]
        ss = [s_ref[p, d] for p, d in groups]
        wqs = []
        for (p, d), r, s in zip(groups, rows, ss):
            sb = s.astype(BF16)
            lhs = jnp.concatenate([w_ref[d, r, :], qd_ref[d, r, :]], axis=0)
            halves = []
            for hp in range(HEADS // 2):
                bd = _block_diag([sb[:, hs[2 * hp]], sb[:, hs[2 * hp + 1]]], zero_s)
                halves.append(_dot(lhs[:, hp * 2 * HEAD_DIM:(hp + 1) * 2 * HEAD_DIM], bd))
            wqs.append(jnp.concatenate(halves, axis=1))
        boths = []
        for (p, d), r, ci, wq in zip(groups, rows, cidx, wqs):
            v_new = (u_ref[d, r, :] - wq[0:CHUNK]).astype(BF16)
            vbd = _block_diag([v_new[:, s] for s in hs], zero_blk)
            kd = kdt_ref[d, pl.ds(pl.multiple_of(ci * HEAD_DIM, HEAD_DIM), HEAD_DIM), :]
            lhs2 = jnp.concatenate([qkd_ref[d, r, :], kd], axis=0)
            boths.append(_dot(lhs2, vbd))
        for (p, d), r, ci, s, wq, both in zip(groups, rows, cidx, ss, wqs, boths):
            o_ref[r, :] += wq[CHUNK:] + both[0:CHUNK]
            s_ref[p, d] = s * gt_ref[d, ci][0:1] + both[CHUNK:]

    def sub_body(sb_i, carry):
        sub0 = sb_i * n_par
        for p in range(n_par):
            for d in range(2):
                if has_init:
                    s_ref[p, d] = jnp.concatenate(
                        [s0_ref[d * HEADS + h] for h in range(HEADS)], axis=1)
                else:
                    s_ref[p, d] = jnp.zeros((HEAD_DIM, A_WIDTH), F32)

        def step(c, carry2):
            scan_step(sub0, c)
            return carry2

        lax.fori_loop(0, n_chunks, step, 0)
        if st_ref is not None:
            for p in range(n_par):
                for d in range(2):
                    s = s_ref[p, d]
                    for h in range(HEADS):
                        st_ref[sub0 + p, d * HEADS + h] = s[:, hs[h]]
        return carry

    lax.fori_loop(0, n_sub // n_par, sub_body, 0)


def _delta_stage(qkv, ba, alog_row, dtb_row, s0, consts, *, seq_len):
    n = qkv.shape[0]
    block = max(seq_len, DELTA_BLOCK)
    n_blocks = n // block
    n_sub = block // seq_len
    n_chunks = seq_len // CHUNK
    n_total = block // CHUNK
    has_init = s0 is not None
    n_par = min(n_sub, 4)
    lvl, eye, bdm, ktm, tri = consts
    in_specs = [pl.BlockSpec((block, QKV), lambda i: (i, 0)),
                pl.BlockSpec((block, 128), lambda i: (i, 0)),
                _const_spec((1, 128)), _const_spec((1, 128)),
                _const_spec(lvl.shape), _const_spec(eye.shape), _const_spec(bdm.shape),
                _const_spec(ktm.shape), _const_spec(tri.shape)]
    args = [qkv, ba, alog_row, dtb_row, lvl, eye, bdm, ktm, tri]
    o_spec = pl.BlockSpec((block, A_WIDTH), lambda i: (i, 0))
    o_shape = jax.ShapeDtypeStruct((n, A_WIDTH), F32)
    st_shape = (2 * HEADS, HEAD_DIM, HEAD_DIM)
    if has_init:
        assert n_sub == 1
        in_specs.append(pl.BlockSpec((None,) + st_shape, lambda i: (i, 0, 0, 0)))
        args.append(s0)
        out_specs, out_shape = o_spec, o_shape
    else:
        out_specs = [o_spec, pl.BlockSpec((n_sub,) + st_shape, lambda i: (i, 0, 0, 0))]
        out_shape = [o_shape, jax.ShapeDtypeStruct((n // seq_len,) + st_shape, F32)]
    scratch = [pltpu.VMEM((n_par, 2, HEAD_DIM, A_WIDTH), F32),
               pltpu.VMEM((block, 128), F32),
               pltpu.VMEM((block, 128), F32),
               pltpu.VMEM((n_total, 8, 128), F32),
               pltpu.VMEM((2, block, A_WIDTH), F32),
               pltpu.VMEM((2, block, A_WIDTH), BF16),
               pltpu.VMEM((2, block, A_WIDTH), BF16),
               pltpu.VMEM((2, block, WIDE), BF16),
               pltpu.VMEM((2, n_total * HEAD_DIM, WIDE), BF16),
               pltpu.VMEM((2, n_total, 8, A_WIDTH), F32)]
    return pl.pallas_call(
        functools.partial(_delta_kernel, n_sub=n_sub, n_chunks=n_chunks, has_init=has_init,
                          n_par=n_par, cb=4),
        grid=(n_blocks,),
        in_specs=in_specs,
        out_specs=out_specs,
        out_shape=out_shape,
        scratch_shapes=scratch,
        compiler_params=_params(1),
        name="delta_stage",
    )(*args)


def _local_kernel(o_ref, z_ref, uv_ref, p_ref, dg_ref, sg_ref, ws_ref, bs_ref, seg_ref,
                  band_ref, icnt_ref, wp_ref, ps_ref, mix_ref, *, tile):
    dg = dg_ref[...]
    for h in range(HEADS):
        cs = slice(h * HEAD_DIM, (h + 1) * HEAD_DIM)
        y = _rms_rows(o_ref[:, cs]) * dg * _silu(z_ref[:, cs])
        mix_ref[:, cs] = y.astype(BF16)
    lane_grp = lax.broadcasted_iota(jnp.int32, (SGU_CHUNK, B_WIDTH), 1) // B_GC
    u = _gelu_tanh(uv_ref[:, 0:B_WIDTH])
    v = _gelu_tanh(uv_ref[:, B_WIDTH:])
    ms = _dot(v * v, seg_ref[...], HIGHEST)
    vn = (v * lax.rsqrt(ms + EPS) * sg_ref[...]).astype(BF16)
    for c in range(tile // SGU_CHUNK):
        rs = slice(c * SGU_CHUNK, (c + 1) * SGU_CHUNK)
        vc = vn[rs]
        s = bs_ref[...]
        for g in range(B_GROUPS):
            s = s + jnp.where(lane_grp == g, _dot(ws_ref[g], vc), 0.0)
        mix_ref[rs, A_WIDTH:A_WIDTH + B_WIDTH] = (u[rs] * s).astype(BF16)
    lane_grp_c = lax.broadcasted_iota(jnp.int32, (POOL_TILE, C_WIDTH), 1) // C_GC
    for c in range(tile // POOL_TILE):
        rs = slice(c * POOL_TILE, (c + 1) * POOL_TILE)
        x = p_ref[rs, :]
        wsum = jnp.zeros((POOL_TILE, C_WIDTH), F32)
        for g in range(len(POOL_WINDOWS)):
            wsum = wsum + jnp.where(lane_grp_c == g, _dot(band_ref[g], x, HIGHEST), 0.0)
        diff = wsum * icnt_ref[...] - x
        y = _bdot(diff, wp_ref[...]) * ps_ref[...]
        mix_ref[rs, A_WIDTH + B_WIDTH:] = y.astype(BF16)


def _local_stage(o, z, uv, p, lw, pool_consts):
    n = o.shape[0]
    tile = 512
    band, icnt = pool_consts
    row = lambda c: pl.BlockSpec((tile, c), lambda i: (i, 0))
    in_specs = [row(A_WIDTH), row(A_WIDTH), row(2 * B_WIDTH), row(C_WIDTH),
                _const_spec((1, HEAD_DIM)), _const_spec((1, B_WIDTH)),
                _const_spec((B_GROUPS, SGU_CHUNK, SGU_CHUNK)), _const_spec((SGU_CHUNK, B_WIDTH)),
                _const_spec((B_WIDTH, B_WIDTH)),
                _const_spec((len(POOL_WINDOWS), POOL_TILE, POOL_TILE)),
                _const_spec((POOL_TILE, C_WIDTH)), _const_spec((C_WIDTH, C_WIDTH)),
                _const_spec((1, C_WIDTH))]
    return pl.pallas_call(
        functools.partial(_local_kernel, tile=tile),
        grid=(n // tile,),
        in_specs=in_specs,
        out_specs=pl.BlockSpec((tile, D_MODEL), lambda i: (i, 0)),
        out_shape=jax.ShapeDtypeStruct((n, D_MODEL), BF16),
        compiler_params=_params(1),
        name="local_stage",
    )(o, z, uv, p, lw["delta_g"], lw["sgu_g"], lw["w_spatial"], lw["b_spatial"], lw["seg"],
      band, icnt, lw["w_pool"], lw["pool_scale"])


def _out_kernel(x_ref, mix_ref, mod_ref, g2_ref, wo_ref, wgu_ref, wd_ref, nf_ref, y_ref,
                *, seq_len, tile, row0, final_norm):
    i = pl.program_id(0)
    tiles_per_seq = max(seq_len // tile, 1)
    row = row0 + i // tiles_per_seq if row0 else 0
    m = mod_ref[pl.ds(row, 1), :]
    gate1 = m[:, 2 * D_MODEL:3 * D_MODEL]
    shift2 = m[:, 3 * D_MODEL:4 * D_MODEL]
    scale2 = m[:, 4 * D_MODEL:5 * D_MODEL]
    gate2 = m[:, 5 * D_MODEL:6 * D_MODEL]
    x1 = x_ref[...] + gate1 * _dot(mix_ref[...], wo_ref[...])
    hb = _modulated_norm(x1, g2_ref[...], shift2, scale2).astype(BF16)
    fc = FF_HIDDEN // FF_SPLIT
    ff = jnp.zeros((tile, D_MODEL), F32)
    for c in range(FF_SPLIT):
        gate = _dot(hb, wgu_ref[:, c * fc:(c + 1) * fc])
        up = _dot(hb, wgu_ref[:, FF_HIDDEN + c * fc:FF_HIDDEN + (c + 1) * fc])
        act = (_silu(gate) * up).astype(BF16)
        ff = ff + _dot(act, wd_ref[c * fc:(c + 1) * fc, :])
    x2 = x1 + gate2 * ff
    if final_norm:
        x2 = _rms_rows(x2) * nf_ref[...]
    y_ref[...] = x2


def _output_stage(x, mix, mods_l, norm2_g, w, norm_f, *, seq_len, row0, final_norm):
    n = x.shape[0]
    tile = 512
    in_specs = [pl.BlockSpec((tile, D_MODEL), lambda i: (i, 0)),
                pl.BlockSpec((tile, D_MODEL), lambda i: (i, 0)),
                _const_spec((8, N_MOD * D_MODEL)), _const_spec((1, D_MODEL)),
                _const_spec((D_MODEL, D_MODEL)), _const_spec((D_MODEL, 2 * FF_HIDDEN)),
                _const_spec((FF_HIDDEN, D_MODEL)), _const_spec((1, D_MODEL))]
    return pl.pallas_call(
        functools.partial(_out_kernel, seq_len=seq_len, tile=tile, row0=row0,
                          final_norm=final_norm),
        grid=(n // tile,),
        in_specs=in_specs,
        out_specs=pl.BlockSpec((tile, D_MODEL), lambda i: (i, 0)),
        out_shape=jax.ShapeDtypeStruct((n, D_MODEL), F32),
        compiler_params=_params(1),
        name="output_stage",
    )(x, mix, mods_l, norm2_g, w["out"], w["gu"], w["down"], norm_f)


def _pool_constants(seg_len):
    pos = np.arange(POOL_TILE)
    seg = pos // seg_len
    band = np.zeros((len(POOL_WINDOWS), POOL_TILE, POOL_TILE), np.float32)
    icnt = np.zeros((POOL_TILE, C_WIDTH), np.float32)
    for g, win in enumerate(POOL_WINDOWS):
        lo = pos - win // 2
        hi = pos + win - win // 2
        inside = (pos[None, :] >= lo[:, None]) & (pos[None, :] < hi[:, None]) \
            & (seg[None, :] == seg[:, None])
        band[g] = inside
        icnt[:, g * C_GC:(g + 1) * C_GC] = (1.0 / inside.sum(axis=1))[:, None]
    return jnp.asarray(band), jnp.asarray(icnt)


def _grid_pos_embed(rows, d):
    quarter = d // 4
    omega = 1.0 / (10000.0 ** (jnp.arange(quarter, dtype=F32) / quarter))
    r = jnp.arange(rows, dtype=F32)[:, None] * omega
    cl = jnp.arange(GRID_W, dtype=F32)[:, None] * omega
    row_emb = jnp.concatenate([jnp.sin(r), jnp.cos(r)], axis=-1)
    col_emb = jnp.concatenate([jnp.sin(cl), jnp.cos(cl)], axis=-1)
    emb = jnp.concatenate([jnp.broadcast_to(row_emb[:, None, :], (rows, GRID_W, d // 2)),
                           jnp.broadcast_to(col_emb[None, :, :], (rows, GRID_W, d // 2))], axis=-1)
    return emb.reshape(rows * GRID_W, d)


def _pad_lanes(a, width):
    return jnp.pad(a, ((0, 0), (0, width - a.shape[1])))


def _layer_weights(l, w_in, w_out, w_gu, w_down, w_pool, w_spatial, b_spatial):
    wi = w_in[l]
    c0 = 2 * QK + 2 * A_WIDTH
    wba = _pad_lanes(wi[:, c0:c0 + 4 * HEADS], 128)
    c1 = c0 + 4 * HEADS
    dense = {
        "qkv": wi[:, 0:QKV].astype(BF16),
        "z": wi[:, QKV:QKV + A_WIDTH].astype(BF16),
        "ba": wba.astype(BF16),
        "uv": wi[:, c1:c1 + 2 * B_WIDTH].astype(BF16),
        "p": wi[:, c1 + 2 * B_WIDTH:].astype(BF16),
        "out": w_out[l].astype(BF16),
        "gu": w_gu[l].astype(BF16),
        "down": w_down[l].astype(BF16),
    }
    wp = jnp.zeros((C_WIDTH, C_WIDTH), F32)
    for g in range(len(POOL_WINDOWS)):
        wp = wp.at[g * C_GC:(g + 1) * C_GC, g * C_GC:(g + 1) * C_GC].set(w_pool[l, g])
    grp = np.arange(B_WIDTH) // B_GC
    seg = jnp.asarray((grp[:, None] == grp[None, :]).astype(np.float32) / B_GC)
    bs = jnp.repeat(b_spatial[l].T, B_GC, axis=1)
    return dense, wp.astype(BF16), seg, w_spatial[l].astype(BF16), bs


def kernel(x_prompt, x_sample, state_delta, c, c_ctx, w_in, conv_w, a_log, dt_bias, delta_norm_g,
           sgu_norm_g, w_spatial, b_spatial, w_pool, pool_scale, w_out, norm1_g, norm2_g, w_mod,
           b_mod, w_gu, w_down, norm_f):
    batch, seq, d = x_prompt.shape
    dec_batch, dec_seq, _ = x_sample.shape
    cvec = jnp.concatenate([c_ctx[None, :], c, jnp.zeros((8 - 1 - dec_batch, d), F32)], axis=0)
    mods = _modulation(cvec, w_mod, b_mod)
    pos = _grid_pos_embed(dec_seq // GRID_W, d)
    xs = _add_pos(x_sample, pos).reshape(dec_batch * dec_seq, d)
    xc = x_prompt.reshape(batch * seq, d)
    pool_ctx = _pool_constants(min(seq, POOL_TILE))
    pool_lat = _pool_constants(GRID_W)
    nf = norm_f.reshape(1, d)
    delta_consts = _delta_constants()
    ctx_states = []
    for l in range(DEPTH):
        dense, wp, seg, ws, bs = _layer_weights(l, w_in, w_out, w_gu, w_down, w_pool, w_spatial,
                                                b_spatial)
        lw = {"delta_g": delta_norm_g[l].reshape(1, HEAD_DIM),
              "sgu_g": sgu_norm_g[l].reshape(1, B_WIDTH), "w_spatial": ws, "b_spatial": bs,
              "seg": seg, "w_pool": wp, "pool_scale": pool_scale[l].reshape(1, C_WIDTH)}
        alog_row = _pad_lanes(a_log[l].reshape(1, 2 * HEADS), 128)
        dtb_row = _pad_lanes(dt_bias[l].reshape(1, 2 * HEADS), 128)
        n1 = norm1_g[l].reshape(1, d)
        n2 = norm2_g[l].reshape(1, d)
        last = l == DEPTH - 1
        s0_lat = state_delta[:, l].reshape(dec_batch, 2 * HEADS, HEAD_DIM, HEAD_DIM)
        streams = (("ctx", xc, seq, 0, None, pool_ctx), ("lat", xs, dec_seq, 1, s0_lat, pool_lat))
        outs = []
        for name, x, seq_len, row0, s0, pool_consts in streams:
            qkv, z, ba, uv, p = _input_stage(x, mods[l], n1, dense, conv_w[l], seq_len=seq_len,
                                             row0=row0)
            res = _delta_stage(qkv, ba, alog_row, dtb_row, s0, delta_consts, seq_len=seq_len)
            if s0 is None:
                o, st = res
                ctx_states.append(st.reshape(batch, 2, HEADS, HEAD_DIM, HEAD_DIM))
            else:
                o = res
            mix = _local_stage(o, z, uv, p, lw, pool_consts)
            outs.append(_output_stage(x, mix, mods[l], n2, dense, nf, seq_len=seq_len, row0=row0,
                                      final_norm=last))
        xc, xs = outs
    y_prompt = xc.reshape(batch, seq, d)
    y_sample = xs.reshape(dec_batch, dec_seq, d)
    new_state = jnp.stack(ctx_states, axis=1)
    return (y_prompt, y_sample, new_state)
```

```python
import functools
import math

import numpy as np
import jax
import jax.numpy as jnp
from jax import lax
from jax.experimental import pallas as pl
from jax.experimental.pallas import tpu as pltpu

F32 = jnp.float32
BF16 = jnp.bfloat16
HIGHEST = lax.Precision.HIGHEST

D_MODEL = 1024
DEPTH = 2
GRID_W = 64
HEADS = 4
HEAD_DIM = 128
QK = HEADS * HEAD_DIM
A_WIDTH = HEADS * HEAD_DIM
QKV = 2 * QK + A_WIDTH
CHUNK = 64
B_WIDTH = 256
B_GROUPS = 4
B_GC = 64
SGU_CHUNK = 128
C_WIDTH = 256
POOL_WINDOWS = (2, 4, 8, 16)
C_GC = 64
FF_HIDDEN = 2816
FF_SPLIT = 2
N_MOD = 6
EPS = 1e-6
NEG_BIG = -1e30

DELTA_BLOCK = 1024
POOL_TILE = 256
VMEM_LIMIT = 56 * 1024 * 1024


def _dot(a, b, precision=None):
    return jnp.dot(a, b, preferred_element_type=F32, precision=precision)


def _bdot(a, b):
    return jnp.dot(a.astype(BF16), b.astype(BF16), preferred_element_type=F32)


def _sigmoid(x):
    return 1.0 / (1.0 + jnp.exp(-x))


def _silu(x):
    return x * _sigmoid(x)


def _softplus(x):
    return jnp.maximum(x, 0.0) + jnp.log1p(jnp.exp(-jnp.abs(x)))


def _gelu_tanh(x):
    c = math.sqrt(2.0 / math.pi)
    return 0.5 * x * (1.0 + jnp.tanh(c * (x + 0.044715 * (x * x * x))))


def _rms_rows(x):
    return x * lax.rsqrt(jnp.mean(x * x, axis=-1, keepdims=True) + EPS)


def _params(n_grid):
    return pltpu.CompilerParams(dimension_semantics=("arbitrary",) * n_grid,
                                vmem_limit_bytes=VMEM_LIMIT)


def _const_spec(shape):
    nd = len(shape)
    return pl.BlockSpec(shape, lambda *_: (0,) * nd, pipeline_mode=pl.Buffered(1))


def _mod_kernel(c_ref, w_ref, b_ref, o_ref):
    a = _silu(c_ref[...])
    o_ref[0] = _bdot(a, w_ref[0]) + b_ref[0]


def _modulation(cvec, w_mod, b_mod):
    tn = 1536
    n_out = N_MOD * D_MODEL
    return pl.pallas_call(
        _mod_kernel,
        grid=(DEPTH, n_out // tn),
        in_specs=[pl.BlockSpec((8, D_MODEL), lambda l, j: (0, 0)),
                  pl.BlockSpec((1, D_MODEL, tn), lambda l, j: (l, 0, j)),
                  pl.BlockSpec((1, 1, tn), lambda l, j: (l, 0, j))],
        out_specs=pl.BlockSpec((1, 8, tn), lambda l, j: (l, 0, j)),
        out_shape=jax.ShapeDtypeStruct((DEPTH, 8, n_out), F32),
        compiler_params=_params(2),
        name="modulation",
    )(cvec, w_mod, b_mod.reshape(DEPTH, 1, n_out))


def _pos_kernel(x_ref, p_ref, o_ref):
    o_ref[...] = x_ref[...] + p_ref[...]


def _add_pos(x, pos):
    b, n, d = x.shape
    tm = 512
    return pl.pallas_call(
        _pos_kernel,
        grid=(b, n // tm),
        in_specs=[pl.BlockSpec((None, tm, d), lambda i, j: (i, j, 0)),
                  pl.BlockSpec((tm, d), lambda i, j: (j, 0))],
        out_specs=pl.BlockSpec((None, tm, d), lambda i, j: (i, j, 0)),
        out_shape=jax.ShapeDtypeStruct(x.shape, F32),
        compiler_params=_params(2),
        name="add_pos",
    )(x, pos)


def _modulated_norm(x, gain, shift, scale):
    return _rms_rows(x) * gain * (1.0 + scale) + shift


def _in_kernel(*refs, seq_len, tile, row0, halo):
    if halo:
        (x_ref, xp_ref, xn_ref, mod_ref, g_ref, wqkv_ref, wz_ref, wba_ref, wuv_ref, wp_ref,
         cw_ref, qkv_ref, z_ref, ba_ref, uv_ref, p_ref) = refs
    else:
        (x_ref, mod_ref, g_ref, wqkv_ref, wz_ref, wba_ref, wuv_ref, wp_ref,
         cw_ref, qkv_ref, z_ref, ba_ref, uv_ref, p_ref) = refs
    i = pl.program_id(0)
    tiles_per_seq = max(seq_len // tile, 1)
    row = row0 + i // tiles_per_seq if row0 else 0
    m = mod_ref[pl.ds(row, 1), :]
    shift = m[:, 0:D_MODEL]
    scale = m[:, D_MODEL:2 * D_MODEL]
    gain = g_ref[...]
    hf = _modulated_norm(x_ref[...], gain, shift, scale)
    hb = hf.astype(BF16)
    if halo:
        hp = _modulated_norm(xp_ref[...], gain, shift, scale)
        hn = _modulated_norm(xn_ref[...], gain, shift, scale)
        hq = jnp.concatenate([hf, hp, hn], axis=0).astype(BF16)
    else:
        hq = hb
    pre = _dot(hq, wqkv_ref[...])
    cur = pre[0:tile]
    ridx = lax.broadcasted_iota(jnp.int32, (tile, 1), 0)
    if halo:
        pos_in_seq = (i % tiles_per_seq) * tile + ridx
        prev_row = pre[tile + 7:tile + 8]
        next_row = pre[tile + 8:tile + 9]
    else:
        pos_in_seq = ridx % seq_len
        prev_row = jnp.zeros((1, QKV), F32)
        next_row = prev_row
    prev = pltpu.roll(cur, 1, 0)
    prev = jnp.where(ridx == 0, prev_row, prev)
    prev = jnp.where(pos_in_seq == 0, 0.0, prev)
    nxt = pltpu.roll(cur, tile - 1, 0)
    nxt = jnp.where(ridx == tile - 1, next_row, nxt)
    nxt = jnp.where(pos_in_seq == seq_len - 1, 0.0, nxt)
    cw = cw_ref[...]
    act = _silu(prev * cw[0:1] + cur * cw[1:2] + nxt * cw[2:3])
    for h in range(HEADS):
        qs = slice(h * HEAD_DIM, (h + 1) * HEAD_DIM)
        ks = slice(QK + h * HEAD_DIM, QK + (h + 1) * HEAD_DIM)
        q = act[:, qs]
        k = act[:, ks]
        qkv_ref[:, qs] = (q * (lax.rsqrt(jnp.sum(q * q, axis=-1, keepdims=True) + EPS)
                               * (HEAD_DIM ** -0.5))).astype(BF16)
        qkv_ref[:, ks] = (k * lax.rsqrt(jnp.sum(k * k, axis=-1, keepdims=True) + EPS)).astype(BF16)
    qkv_ref[:, 2 * QK:] = act[:, 2 * QK:].astype(BF16)
    z_ref[...] = _dot(hb, wz_ref[...])
    ba_ref[...] = _dot(hb, wba_ref[...])
    uv_ref[...] = _dot(hb, wuv_ref[...])
    p_ref[...] = _dot(hb, wp_ref[...])


def _input_stage(x, mods_l, norm_g, w, conv_w, *, seq_len, row0):
    n = x.shape[0]
    tile = 512
    halo = seq_len > tile
    tiles_per_seq = max(seq_len // tile, 1)
    n_tiles = n // tile
    h8 = tile // 8
    last8 = n // 8 - 1
    in_specs = [pl.BlockSpec((tile, D_MODEL), lambda i: (i, 0))]
    args = [x]
    if halo:
        in_specs += [pl.BlockSpec((8, D_MODEL), lambda i: (jnp.maximum(i * h8 - 1, 0), 0)),
                     pl.BlockSpec((8, D_MODEL), lambda i: (jnp.minimum((i + 1) * h8, last8), 0))]
        args += [x, x]
    in_specs += [_const_spec((8, N_MOD * D_MODEL)), _const_spec((1, D_MODEL)),
                 _const_spec((D_MODEL, QKV)), _const_spec((D_MODEL, A_WIDTH)),
                 _const_spec((D_MODEL, 128)), _const_spec((D_MODEL, 2 * B_WIDTH)),
                 _const_spec((D_MODEL, C_WIDTH)), _const_spec((3, QKV))]
    args += [mods_l, norm_g, w["qkv"], w["z"], w["ba"], w["uv"], w["p"], conv_w]
    widths = (QKV, A_WIDTH, 128, 2 * B_WIDTH, C_WIDTH)
    dtypes = (BF16, F32, F32, F32, F32)
    return pl.pallas_call(
        functools.partial(_in_kernel, seq_len=seq_len, tile=tile, row0=row0, halo=halo),
        grid=(n_tiles,),
        in_specs=in_specs,
        out_specs=[pl.BlockSpec((tile, c), lambda i: (i, 0)) for c in widths],
        out_shape=[jax.ShapeDtypeStruct((n, c), t) for c, t in zip(widths, dtypes)],
        compiler_params=_params(1),
        name="input_stage",
    )(*args)


N_LEVELS = int(math.log2(CHUNK))
WIDE = HEADS * CHUNK


def _delta_constants():
    r = np.arange(CHUNK)[:, None]
    c = (np.arange(WIDE) % CHUNK)[None, :]
    lvl = np.zeros((2, N_LEVELS, CHUNK, WIDE), np.float32)
    for d in range(2):
        rr, cc = (r, c) if d == 0 else (c, r)
        for i in range(N_LEVELS):
            s = 1 << i
            lvl[d, i] = (rr // (2 * s) == cc // (2 * s)) & ((rr // s) % 2 == 1) & ((cc // s) % 2 == 0)
    eye = (r == c).astype(np.float32)
    bdm = (np.arange(WIDE)[:, None] // CHUNK == np.arange(WIDE)[None, :] // CHUNK)
    ktm = (np.arange(QK)[:, None] // HEAD_DIM == np.arange(WIDE)[None, :] // CHUNK)
    tri = np.concatenate([np.tril(np.ones((CHUNK, CHUNK))), np.triu(np.ones((CHUNK, CHUNK)))], 0)
    return (jnp.asarray(lvl.reshape(2 * N_LEVELS, CHUNK, WIDE)), jnp.asarray(eye),
            jnp.asarray(bdm, BF16), jnp.asarray(ktm, BF16), jnp.asarray(tri, BF16))


def _pair(a, b, lt_half):
    return jnp.where(lt_half, a, b)


def _block_diag(blocks, zero):
    n = len(blocks)
    rows = [jnp.concatenate([blocks[i] if j == i else zero for j in range(n)], axis=1)
            for i in range(n)]
    return jnp.concatenate(rows, axis=0)


def _delta_kernel(*refs, n_sub, n_chunks, has_init, n_par, cb):
    it = iter(refs)
    qkv_ref, ba_ref, alog_ref, dtb_ref, lvl_ref, eye_ref, bdm_ref, ktm_ref, tri_ref = (
        next(it) for _ in range(9))
    s0_ref = next(it) if has_init else None
    o_ref = next(it)
    st_ref = None if has_init else next(it)
    (s_ref, gc_ref, beta_ref, gct_ref, u_ref, w_ref, qd_ref, qkd_ref, kdt_ref, gt_ref) = it
    n_total = n_sub * n_chunks
    lane = lax.broadcasted_iota(jnp.int32, (CHUNK, 128), 1)
    lt_half = lane < CHUNK
    lt_half_row = lt_half[0:1]
    rw = lax.broadcasted_iota(jnp.int32, (CHUNK, WIDE), 0)
    cw = lax.broadcasted_iota(jnp.int32, (CHUNK, WIDE), 1) & (CHUNK - 1)
    incl = (rw >= cw, rw <= cw)
    a_row = -jnp.exp(alog_ref[...])
    dtb_row = dtb_ref[...]

    def gate_body(bi, carry):
        span = cb * CHUNK
        r0 = pl.multiple_of(bi * span, span)
        ba = ba_ref[pl.ds(r0, span), :]
        beta_ref[pl.ds(r0, span), :] = _sigmoid(ba)
        al = pltpu.roll(ba, 128 - 2 * HEADS, 1)
        lane_b = lax.broadcasted_iota(jnp.int32, (span, 128), 1)
        g = jnp.where(lane_b < 2 * HEADS, a_row * _softplus(al + dtb_row), 0.0)
        g_hi = g.astype(BF16).astype(F32)
        r1 = g - g_hi
        g_mid = r1.astype(BF16).astype(F32)
        g_lo = (r1 - g_mid).astype(BF16).astype(F32)
        g3 = (g_hi + pltpu.roll(g_mid, 8, 1) + pltpu.roll(g_lo, 16, 1)).astype(BF16)
        css = [_dot(tri_ref[...], g3[i * CHUNK:(i + 1) * CHUNK]) for i in range(cb)]
        css = [cs + pltpu.roll(cs, 128 - 8, 1) + pltpu.roll(cs, 128 - 16, 1) for cs in css]
        gcs = [jnp.where(lane < HEADS, cs[0:CHUNK], cs[CHUNK:]) for cs in css]
        ats = [jnp.concatenate([gc, gc], axis=0).T[0:8, :] for gc in gcs]
        for i in range(cb):
            gc_ref[pl.ds(pl.multiple_of(r0 + i * CHUNK, CHUNK), CHUNK), :] = gcs[i]
            gct_ref[bi * cb + i] = ats[i]
        return carry

    lax.fori_loop(0, n_total // cb, gate_body, 0)

    zero_blk = jnp.zeros((CHUNK, HEAD_DIM), BF16)
    hs = [slice(h * HEAD_DIM, (h + 1) * HEAD_DIM) for h in range(HEADS)]

    def wide_bd(y):
        yb = y.astype(BF16)
        return jnp.concatenate([yb] * HEADS, axis=0) * bdm_ref[...]

    def pre_body(bi, carry):
        groups = []
        for i in range(cb):
            c = bi * cb + i
            r0 = pl.multiple_of(c * CHUNK, CHUNK)
            rows = pl.ds(r0, CHUNK)
            qb = qkv_ref[rows, 0:QK]
            kb16 = qkv_ref[rows, QK:2 * QK]
            qf = qb.astype(F32)
            kf = kb16.astype(F32)
            vf = qkv_ref[rows, 2 * QK:].astype(F32)
            kt = jnp.concatenate(
                [jnp.concatenate([kf[:, 0:128], kf[:, 128:256]], axis=0).T,
                 jnp.concatenate([kf[:, 256:384], kf[:, 384:512]], axis=0).T], axis=1)
            ktbd = jnp.concatenate([kt.astype(BF16)] * HEADS, axis=0) * ktm_ref[...]
            qkk = _dot(jnp.concatenate([qb, kb16], axis=0), ktbd)
            cols = gc_ref[rows, :]
            bcols = beta_ref[rows, :]
            at = gct_ref[c]
            for d in range(2):
                j0 = d * HEADS
                bg = [jnp.broadcast_to(cols[:, j0 + h:j0 + h + 1], (CHUNK, 128)) for h in range(HEADS)]
                bb = [jnp.broadcast_to(bcols[:, j0 + h:j0 + h + 1], (CHUNK, 128)) for h in range(HEADS)]
                gcw = jnp.concatenate([_pair(bg[0], bg[1], lt_half), _pair(bg[2], bg[3], lt_half)], 1)
                bw = jnp.concatenate([_pair(bb[0], bb[1], lt_half), _pair(bb[2], bb[3], lt_half)], 1)
                gr = jnp.concatenate(
                    [_pair(at[j0:j0 + 1], at[j0 + 1:j0 + 2], lt_half_row),
                     _pair(at[j0 + 2:j0 + 3], at[j0 + 3:j0 + 4], lt_half_row)], axis=1)
                decay = jnp.exp(jnp.where(incl[d], gcw - gr, NEG_BIG))
                m = bw * qkk[CHUNK:] * decay
                qkd_ref[d, rows, :] = (qkk[0:CHUNK] * decay).astype(BF16)
                en = jnp.concatenate([jnp.exp(x) for x in bg], axis=1)
                bn = jnp.concatenate(bb, axis=1)
                qd_ref[d, rows, :] = (qf * en).astype(BF16)
                vb = (vf * bn).astype(BF16)
                kbe = (kf * (bn * en)).astype(BF16)
                e0 = CHUNK - 1 if d == 0 else 0
                bt = [jnp.broadcast_to(at[j0 + h:j0 + h + 1, e0:e0 + 1], (1, 128)) for h in range(HEADS)]
                tw = jnp.concatenate([_pair(bt[0], bt[1], lt_half_row),
                                      _pair(bt[2], bt[3], lt_half_row)], axis=1)
                kdt_ref[d, pl.ds(pl.multiple_of(c * HEAD_DIM, HEAD_DIM), HEAD_DIM), :] = (
                    kt * jnp.exp(tw - gr)).astype(BF16)
                gt_ref[d, c] = jnp.broadcast_to(
                    jnp.concatenate([jnp.exp(x) for x in bt], axis=1), (8, A_WIDTH))
                groups.append((d, rows, m, vb, kbe))
        ts = [eye_ref[...] - g[2] * lvl_ref[g[0] * N_LEVELS] for g in groups]
        for lv in range(1, N_LEVELS):
            xs = [_dot(t.astype(BF16), wide_bd(g[2] * lvl_ref[g[0] * N_LEVELS + lv]))
                  for t, g in zip(ts, groups)]
            ts = [t - _dot(x.astype(BF16), wide_bd(t)) for t, x in zip(ts, xs)]
        for t, (d, rows, _, vb, kbe) in zip(ts, groups):
            rbd = jnp.concatenate(
                [_block_diag([vb[:, s] for s in hs], zero_blk),
                 _block_diag([kbe[:, s] for s in hs], zero_blk)], axis=1)
            uw = _dot(t.astype(BF16), rbd)
            u_ref[d, rows, :] = uw[:, 0:A_WIDTH]
            w_ref[d, rows, :] = uw[:, A_WIDTH:].astype(BF16)
        return carry

    lax.fori_loop(0, n_total // cb, pre_body, 0)

    o_ref[...] = jnp.zeros(o_ref.shape, F32)
    zero_s = jnp.zeros((HEAD_DIM, HEAD_DIM), BF16)

    def scan_step(sub0, c):
        groups = [(p, d) for p in range(n_par) for d in range(2)]
        cidx = [(sub0 + p) * n_chunks + (c if d == 0 else n_chunks - 1 - c) for p, d in groups]
        rows = [pl.ds(pl.multiple_of(ci * CHUNK, CHUNK), CHUNK) for ci in cidx]
        ss = [s_ref[p, d] for p, d in groups]
        wqs = []
        for (p, d), r, s in zip(groups, rows, ss):
            sb = s.astype(BF16)
            lhs = jnp.concatenate([w_ref[d, r, :], qd_ref[d, r, :]], axis=0)
            halves = []
            for hp in range(HEADS // 2):
                bd = _block_diag([sb[:, hs[2 * hp]], sb[:, hs[2 * hp + 1]]], zero_s)
                halves.append(_dot(lhs[:, hp * 2 * HEAD_DIM:(hp + 1) * 2 * HEAD_DIM], bd))
            wqs.append(jnp.concatenate(halves, axis=1))
        boths = []
        for (p, d), r, ci, wq in zip(groups, rows, cidx, wqs):
            v_new = (u_ref[d, r, :] - wq[0:CHUNK]).astype(BF16)
            vbd = _block_diag([v_new[:, s] for s in hs], zero_blk)
            kd = kdt_ref[d, pl.ds(pl.multiple_of(ci * HEAD_DIM, HEAD_DIM), HEAD_DIM), :]
            lhs2 = jnp.concatenate([qkd_ref[d, r, :], kd], axis=0)
            boths.append(_dot(lhs2, vbd))
        for (p, d), r, ci, s, wq, both in zip(groups, rows, cidx, ss, wqs, boths):
            o_ref[r, :] += wq[CHUNK:] + both[0:CHUNK]
            s_ref[p, d] = s * gt_ref[d, ci][0:1] + both[CHUNK:]

    def sub_body(sb_i, carry):
        sub0 = sb_i * n_par
        for p in range(n_par):
            for d in range(2):
                if has_init:
                    s_ref[p, d] = jnp.concatenate(
                        [s0_ref[d * HEADS + h] for h in range(HEADS)], axis=1)
                else:
                    s_ref[p, d] = jnp.zeros((HEAD_DIM, A_WIDTH), F32)

        def step(c, carry2):
            scan_step(sub0, c)
            return carry2

        lax.fori_loop(0, n_chunks, step, 0)
        if st_ref is not None:
            for p in range(n_par):
                for d in range(2):
                    s = s_ref[p, d]
                    for h in range(HEADS):
                        st_ref[sub0 + p, d * HEADS + h] = s[:, hs[h]]
        return carry

    lax.fori_loop(0, n_sub // n_par, sub_body, 0)


def _delta_stage(qkv, ba, alog_row, dtb_row, s0, consts, *, seq_len):
    n = qkv.shape[0]
    block = max(seq_len, DELTA_BLOCK)
    n_blocks = n // block
    n_sub = block // seq_len
    n_chunks = seq_len // CHUNK
    n_total = block // CHUNK
    has_init = s0 is not None
    n_par = min(n_sub, 4)
    lvl, eye, bdm, ktm, tri = consts
    in_specs = [pl.BlockSpec((block, QKV), lambda i: (i, 0)),
                pl.BlockSpec((block, 128), lambda i: (i, 0)),
                _const_spec((1, 128)), _const_spec((1, 128)),
                _const_spec(lvl.shape), _const_spec(eye.shape), _const_spec(bdm.shape),
                _const_spec(ktm.shape), _const_spec(tri.shape)]
    args = [qkv, ba, alog_row, dtb_row, lvl, eye, bdm, ktm, tri]
    o_spec = pl.BlockSpec((block, A_WIDTH), lambda i: (i, 0))
    o_shape = jax.ShapeDtypeStruct((n, A_WIDTH), F32)
    st_shape = (2 * HEADS, HEAD_DIM, HEAD_DIM)
    if has_init:
        assert n_sub == 1
        in_specs.append(pl.BlockSpec((None,) + st_shape, lambda i: (i, 0, 0, 0)))
        args.append(s0)
        out_specs, out_shape = o_spec, o_shape
    else:
        out_specs = [o_spec, pl.BlockSpec((n_sub,) + st_shape, lambda i: (i, 0, 0, 0))]
        out_shape = [o_shape, jax.ShapeDtypeStruct((n // seq_len,) + st_shape, F32)]
    scratch = [pltpu.VMEM((n_par, 2, HEAD_DIM, A_WIDTH), F32),
               pltpu.VMEM((block, 128), F32),
               pltpu.VMEM((block, 128), F32),
               pltpu.VMEM((n_total, 8, 128), F32),
               pltpu.VMEM((2, block, A_WIDTH), F32),
               pltpu.VMEM((2, block, A_WIDTH), BF16),
               pltpu.VMEM((2, block, A_WIDTH), BF16),
               pltpu.VMEM((2, block, WIDE), BF16),
               pltpu.VMEM((2, n_total * HEAD_DIM, WIDE), BF16),
               pltpu.VMEM((2, n_total, 8, A_WIDTH), F32)]
    return pl.pallas_call(
        functools.partial(_delta_kernel, n_sub=n_sub, n_chunks=n_chunks, has_init=has_init,
                          n_par=n_par, cb=4),
        grid=(n_blocks,),
        in_specs=in_specs,
        out_specs=out_specs,
        out_shape=out_shape,
        scratch_shapes=scratch,
        compiler_params=_params(1),
        name="delta_stage",
    )(*args)


def _group_rows(y, n_groups):
    half = y.shape[1] // 2
    gw = half // 2
    lane = lax.broadcasted_iota(jnp.int32, (y.shape[0], half), 1)
    zero = jnp.zeros((y.shape[0], half), y.dtype)
    blocks = []
    for g in range(n_groups):
        part = y[:, (g // 2) * half:(g // 2 + 1) * half]
        keep = (lane < gw) if g % 2 == 0 else (lane >= gw)
        part = jnp.where(keep, part, jnp.zeros_like(part))
        blocks.append(jnp.concatenate([part, zero] if g < 2 else [zero, part], axis=1))
    return jnp.concatenate(blocks, axis=0)


def _local_mix(o_ref, z_ref, uv_ref, p_ref, dg_ref, sg_ref, ws_ref, bs_ref, seg_ref, band_ref,
               icnt_ref, wp_ref, ps_ref, mix_ref, tile):
    dg = dg_ref[...]
    for h in range(HEADS):
        cs = slice(h * HEAD_DIM, (h + 1) * HEAD_DIM)
        y = _rms_rows(o_ref[:, cs]) * dg * _silu(z_ref[:, cs])
        mix_ref[:, cs] = y.astype(BF16)
    u = _gelu_tanh(uv_ref[:, 0:B_WIDTH])
    v = _gelu_tanh(uv_ref[:, B_WIDTH:])
    vv = v * v
    vv_hi = vv.astype(BF16)
    vv_lo = (vv - vv_hi.astype(F32)).astype(BF16)
    ms = _dot(jnp.concatenate([vv_hi, vv_lo], axis=1), seg_ref[...])
    vn = (v * lax.rsqrt(ms + EPS) * sg_ref[...]).astype(BF16)
    for c in range(tile // SGU_CHUNK):
        rs = slice(c * SGU_CHUNK, (c + 1) * SGU_CHUNK)
        s = _dot(ws_ref[...], _group_rows(vn[rs], B_GROUPS)) + bs_ref[...]
        mix_ref[rs, A_WIDTH:A_WIDTH + B_WIDTH] = (u[rs] * s).astype(BF16)
    for c in range(tile // POOL_TILE):
        rs = slice(c * POOL_TILE, (c + 1) * POOL_TILE)
        x = p_ref[rs, :]
        x_hi = x.astype(BF16)
        x_lo = (x - x_hi.astype(F32)).astype(BF16)
        n_win = len(POOL_WINDOWS)
        rhs = jnp.concatenate([_group_rows(x_hi, n_win), _group_rows(x_lo, n_win)], axis=0)
        wsum = _dot(band_ref[...], rhs)
        diff = wsum * icnt_ref[...] - x
        y = _bdot(diff, wp_ref[...]) * ps_ref[...]
        mix_ref[rs, A_WIDTH + B_WIDTH:] = y.astype(BF16)


def _out_kernel(x_ref, o_ref, z_ref, uv_ref, p_ref, dg_ref, sg_ref, ws_ref, bs_ref, seg_ref,
                band_ref, icnt_ref, wp_ref, ps_ref, mod_ref, g2_ref, wo_ref, wgu_ref, wd_ref,
                nf_ref, y_ref, mix_ref, *, seq_len, tile, row0, final_norm):
    i = pl.program_id(0)
    tiles_per_seq = max(seq_len // tile, 1)
    row = row0 + i // tiles_per_seq if row0 else 0
    m = mod_ref[pl.ds(row, 1), :]
    gate1 = m[:, 2 * D_MODEL:3 * D_MODEL]
    shift2 = m[:, 3 * D_MODEL:4 * D_MODEL]
    scale2 = m[:, 4 * D_MODEL:5 * D_MODEL]
    gate2 = m[:, 5 * D_MODEL:6 * D_MODEL]
    _local_mix(o_ref, z_ref, uv_ref, p_ref, dg_ref, sg_ref, ws_ref, bs_ref, seg_ref, band_ref,
               icnt_ref, wp_ref, ps_ref, mix_ref, tile)
    x1 = x_ref[...] + gate1 * _dot(mix_ref[...], wo_ref[...])
    hb = _modulated_norm(x1, g2_ref[...], shift2, scale2).astype(BF16)
    fc = FF_HIDDEN // FF_SPLIT
    ff = jnp.zeros((tile, D_MODEL), F32)
    for c in range(FF_SPLIT):
        gate = _dot(hb, wgu_ref[:, c * fc:(c + 1) * fc])
        up = _dot(hb, wgu_ref[:, FF_HIDDEN + c * fc:FF_HIDDEN + (c + 1) * fc])
        act = (_silu(gate) * up).astype(BF16)
        ff = ff + _dot(act, wd_ref[c * fc:(c + 1) * fc, :])
    x2 = x1 + gate2 * ff
    if final_norm:
        x2 = _rms_rows(x2) * nf_ref[...]
    y_ref[...] = x2


def _output_stage(x, o, z, uv, p, lw, pool_consts, mods_l, norm2_g, w, norm_f, *, seq_len, row0,
                  final_norm):
    n = x.shape[0]
    tile = 512
    band, icnt = pool_consts
    row = lambda c: pl.BlockSpec((tile, c), lambda i: (i, 0))
    consts = [lw["delta_g"], lw["sgu_g"], lw["w_spatial"], lw["b_spatial"], lw["seg"], band, icnt,
              lw["w_pool"], lw["pool_scale"], mods_l, norm2_g, w["out"], w["gu"], w["down"], norm_f]
    in_specs = ([row(D_MODEL), row(A_WIDTH), row(A_WIDTH), row(2 * B_WIDTH), row(C_WIDTH)]
                + [_const_spec(a.shape) for a in consts])
    return pl.pallas_call(
        functools.partial(_out_kernel, seq_len=seq_len, tile=tile, row0=row0,
                          final_norm=final_norm),
        grid=(n // tile,),
        in_specs=in_specs,
        out_specs=row(D_MODEL),
        out_shape=jax.ShapeDtypeStruct((n, D_MODEL), F32),
        scratch_shapes=[pltpu.VMEM((tile, D_MODEL), BF16)],
        compiler_params=_params(1),
        name="output_stage",
    )(x, o, z, uv, p, *consts)


def _pool_constants(seg_len):
    pos = np.arange(POOL_TILE)
    seg = pos // seg_len
    band = np.zeros((len(POOL_WINDOWS), POOL_TILE, POOL_TILE), np.float32)
    icnt = np.zeros((POOL_TILE, C_WIDTH), np.float32)
    for g, win in enumerate(POOL_WINDOWS):
        lo = pos - win // 2
        hi = pos + win - win // 2
        inside = (pos[None, :] >= lo[:, None]) & (pos[None, :] < hi[:, None]) \
            & (seg[None, :] == seg[:, None])
        band[g] = inside
        icnt[:, g * C_GC:(g + 1) * C_GC] = (1.0 / inside.sum(axis=1))[:, None]
    band_cat = np.concatenate([band[g] for g in range(len(POOL_WINDOWS))] * 2, axis=1)
    return jnp.asarray(band_cat, BF16), jnp.asarray(icnt)


def _grid_pos_embed(rows, d):
    quarter = d // 4
    omega = 1.0 / (10000.0 ** (jnp.arange(quarter, dtype=F32) / quarter))
    r = jnp.arange(rows, dtype=F32)[:, None] * omega
    cl = jnp.arange(GRID_W, dtype=F32)[:, None] * omega
    row_emb = jnp.concatenate([jnp.sin(r), jnp.cos(r)], axis=-1)
    col_emb = jnp.concatenate([jnp.sin(cl), jnp.cos(cl)], axis=-1)
    emb = jnp.concatenate([jnp.broadcast_to(row_emb[:, None, :], (rows, GRID_W, d // 2)),
                           jnp.broadcast_to(col_emb[None, :, :], (rows, GRID_W, d // 2))], axis=-1)
    return emb.reshape(rows * GRID_W, d)


def _pad_lanes(a, width):
    return jnp.pad(a, ((0, 0), (0, width - a.shape[1])))


def _layer_weights(l, w_in, w_out, w_gu, w_down, w_pool, w_spatial, b_spatial):
    wi = w_in[l]
    c0 = 2 * QK + 2 * A_WIDTH
    wba = _pad_lanes(wi[:, c0:c0 + 4 * HEADS], 128)
    c1 = c0 + 4 * HEADS
    dense = {
        "qkv": wi[:, 0:QKV].astype(BF16),
        "z": wi[:, QKV:QKV + A_WIDTH].astype(BF16),
        "ba": wba.astype(BF16),
        "uv": wi[:, c1:c1 + 2 * B_WIDTH].astype(BF16),
        "p": wi[:, c1 + 2 * B_WIDTH:].astype(BF16),
        "out": w_out[l].astype(BF16),
        "gu": w_gu[l].astype(BF16),
        "down": w_down[l].astype(BF16),
    }
    wp = jnp.zeros((C_WIDTH, C_WIDTH), F32)
    for g in range(len(POOL_WINDOWS)):
        wp = wp.at[g * C_GC:(g + 1) * C_GC, g * C_GC:(g + 1) * C_GC].set(w_pool[l, g])
    grp = np.arange(B_WIDTH) // B_GC
    seg1 = (grp[:, None] == grp[None, :]).astype(np.float32) / B_GC
    seg = jnp.asarray(np.concatenate([seg1, seg1], axis=0), BF16)
    bs = jnp.repeat(b_spatial[l].T, B_GC, axis=1)
    ws_cat = jnp.concatenate([w_spatial[l, g] for g in range(B_GROUPS)], axis=1)
    return dense, wp.astype(BF16), seg, ws_cat.astype(BF16), bs


def kernel(x_prompt, x_sample, state_delta, c, c_ctx, w_in, conv_w, a_log, dt_bias, delta_norm_g,
           sgu_norm_g, w_spatial, b_spatial, w_pool, pool_scale, w_out, norm1_g, norm2_g, w_mod,
           b_mod, w_gu, w_down, norm_f):
    batch, seq, d = x_prompt.shape
    dec_batch, dec_seq, _ = x_sample.shape
    cvec = jnp.concatenate([c_ctx[None, :], c, jnp.zeros((8 - 1 - dec_batch, d), F32)], axis=0)
    mods = _modulation(cvec, w_mod, b_mod)
    pos = _grid_pos_embed(dec_seq // GRID_W, d)
    xs = _add_pos(x_sample, pos).reshape(dec_batch * dec_seq, d)
    xc = x_prompt.reshape(batch * seq, d)
    pool_ctx = _pool_constants(min(seq, POOL_TILE))
    pool_lat = _pool_constants(GRID_W)
    nf = norm_f.reshape(1, d)
    delta_consts = _delta_constants()
    ctx_states = []
    for l in range(DEPTH):
        dense, wp, seg, ws, bs = _layer_weights(l, w_in, w_out, w_gu, w_down, w_pool, w_spatial,
                                                b_spatial)
        lw = {"delta_g": delta_norm_g[l].reshape(1, HEAD_DIM),
              "sgu_g": sgu_norm_g[l].reshape(1, B_WIDTH), "w_spatial": ws, "b_spatial": bs,
              "seg": seg, "w_pool": wp, "pool_scale": pool_scale[l].reshape(1, C_WIDTH)}
        alog_row = _pad_lanes(a_log[l].reshape(1, 2 * HEADS), 128)
        dtb_row = _pad_lanes(dt_bias[l].reshape(1, 2 * HEADS), 128)
        n1 = norm1_g[l].reshape(1, d)
        n2 = norm2_g[l].reshape(1, d)
        last = l == DEPTH - 1
        s0_lat = state_delta[:, l].reshape(dec_batch, 2 * HEADS, HEAD_DIM, HEAD_DIM)
        streams = (("ctx", xc, seq, 0, None, pool_ctx), ("lat", xs, dec_seq, 1, s0_lat, pool_lat))
        outs = []
        for name, x, seq_len, row0, s0, pool_consts in streams:
            qkv, z, ba, uv, p = _input_stage(x, mods[l], n1, dense, conv_w[l], seq_len=seq_len,
                                             row0=row0)
            res = _delta_stage(qkv, ba, alog_row, dtb_row, s0, delta_consts, seq_len=seq_len)
            if s0 is None:
                o, st = res
                ctx_states.append(st.reshape(batch, 2, HEADS, HEAD_DIM, HEAD_DIM))
            else:
                o = res
            outs.append(_output_stage(x, o, z, uv, p, lw, pool_consts, mods[l], n2, dense, nf,
                                      seq_len=seq_len, row0=row0, final_norm=last))
        xc, xs = outs
    y_prompt = xc.reshape(batch, seq, d)
    y_sample = xs.reshape(dec_batch, dec_seq, d)
    new_state = jnp.stack(ctx_states, axis=1)
    return (y_prompt, y_sample, new_state)
```

```python
import functools
import math

import numpy as np
import jax
import jax.numpy as jnp
from jax import lax
from jax.experimental import pallas as pl
from jax.experimental.pallas import tpu as pltpu

F32 = jnp.float32
BF16 = jnp.bfloat16
HIGHEST = lax.Precision.HIGHEST

D_MODEL = 1024
DEPTH = 2
GRID_W = 64
HEADS = 4
HEAD_DIM = 128
QK = HEADS * HEAD_DIM
A_WIDTH = HEADS * HEAD_DIM
QKV = 2 * QK + A_WIDTH
CHUNK = 64
B_WIDTH = 256
B_GROUPS = 4
B_GC = 64
SGU_CHUNK = 128
C_WIDTH = 256
POOL_WINDOWS = (2, 4, 8, 16)
C_GC = 64
FF_HIDDEN = 2816
FF_SPLIT = 2
N_MOD = 6
EPS = 1e-6
NEG_BIG = -1e30

DELTA_BLOCK = 1024
POOL_TILE = 256
VMEM_LIMIT = 56 * 1024 * 1024


def _dot(a, b, precision=None):
    return jnp.dot(a, b, preferred_element_type=F32, precision=precision)


def _bdot(a, b):
    return jnp.dot(a.astype(BF16), b.astype(BF16), preferred_element_type=F32)


def _sigmoid(x):
    return 1.0 / (1.0 + jnp.exp(-x))


def _silu(x):
    return x * _sigmoid(x)


def _softplus(x):
    return jnp.maximum(x, 0.0) + jnp.log1p(jnp.exp(-jnp.abs(x)))


def _gelu_tanh(x):
    c = math.sqrt(2.0 / math.pi)
    return 0.5 * x * (1.0 + jnp.tanh(c * (x + 0.044715 * (x * x * x))))


def _rms_rows(x):
    return x * lax.rsqrt(jnp.mean(x * x, axis=-1, keepdims=True) + EPS)


def _params(n_grid):
    return pltpu.CompilerParams(dimension_semantics=("arbitrary",) * n_grid,
                                vmem_limit_bytes=VMEM_LIMIT)


def _const_spec(shape):
    nd = len(shape)
    return pl.BlockSpec(shape, lambda *_: (0,) * nd, pipeline_mode=pl.Buffered(1))


def _mod_kernel(c_ref, w_ref, b_ref, o_ref):
    a = _silu(c_ref[...])
    o_ref[0] = _bdot(a, w_ref[0]) + b_ref[0]


def _modulation(cvec, w_mod, b_mod):
    tn = 1536
    n_out = N_MOD * D_MODEL
    return pl.pallas_call(
        _mod_kernel,
        grid=(DEPTH, n_out // tn),
        in_specs=[pl.BlockSpec((8, D_MODEL), lambda l, j: (0, 0)),
                  pl.BlockSpec((1, D_MODEL, tn), lambda l, j: (l, 0, j)),
                  pl.BlockSpec((1, 1, tn), lambda l, j: (l, 0, j))],
        out_specs=pl.BlockSpec((1, 8, tn), lambda l, j: (l, 0, j)),
        out_shape=jax.ShapeDtypeStruct((DEPTH, 8, n_out), F32),
        compiler_params=_params(2),
        name="modulation",
    )(cvec, w_mod, b_mod.reshape(DEPTH, 1, n_out))


def _pos_kernel(x_ref, p_ref, o_ref):
    o_ref[...] = x_ref[...] + p_ref[...]


def _add_pos(x, pos):
    b, n, d = x.shape
    tm = 512
    return pl.pallas_call(
        _pos_kernel,
        grid=(b, n // tm),
        in_specs=[pl.BlockSpec((None, tm, d), lambda i, j: (i, j, 0)),
                  pl.BlockSpec((tm, d), lambda i, j: (j, 0))],
        out_specs=pl.BlockSpec((None, tm, d), lambda i, j: (i, j, 0)),
        out_shape=jax.ShapeDtypeStruct(x.shape, F32),
        compiler_params=_params(2),
        name="add_pos",
    )(x, pos)


def _modulated_norm(x, gain, shift, scale):
    return _rms_rows(x) * gain * (1.0 + scale) + shift


def _in_kernel(*refs, seq_len, tile, row0, halo):
    if halo:
        (x_ref, xp_ref, xn_ref, mod_ref, g_ref, wqkv_ref, wz_ref, wba_ref, wuv_ref, wp_ref,
         cw_ref, qkv_ref, z_ref, ba_ref, uv_ref, p_ref) = refs
    else:
        (x_ref, mod_ref, g_ref, wqkv_ref, wz_ref, wba_ref, wuv_ref, wp_ref,
         cw_ref, qkv_ref, z_ref, ba_ref, uv_ref, p_ref) = refs
    i = pl.program_id(0)
    tiles_per_seq = max(seq_len // tile, 1)
    row = row0 + i // tiles_per_seq if row0 else 0
    m = mod_ref[pl.ds(row, 1), :]
    shift = m[:, 0:D_MODEL]
    scale = m[:, D_MODEL:2 * D_MODEL]
    gain = g_ref[...]
    hf = _modulated_norm(x_ref[...], gain, shift, scale)
    hb = hf.astype(BF16)
    if halo:
        hp = _modulated_norm(xp_ref[...], gain, shift, scale)
        hn = _modulated_norm(xn_ref[...], gain, shift, scale)
        hq = jnp.concatenate([hf, hp, hn], axis=0).astype(BF16)
    else:
        hq = hb
    ridx = lax.broadcasted_iota(jnp.int32, (tile, 1), 0)
    if halo:
        pos_in_seq = (i % tiles_per_seq) * tile + ridx
    else:
        pos_in_seq = ridx % seq_len
    first_row = ridx == 0
    last_row = ridx == tile - 1
    seq_start = pos_in_seq == 0
    seq_end = pos_in_seq == seq_len - 1
    cw = cw_ref[...]
    blk = 2 * HEAD_DIM

    def project(b):
        return _dot(hq, wqkv_ref[:, b * blk:(b + 1) * blk])

    def conv_act(b, pre):
        cols = slice(b * blk, (b + 1) * blk)
        cur = pre[0:tile]
        prev = pltpu.roll(cur, 1, 0)
        nxt = pltpu.roll(cur, tile - 1, 0)
        if halo:
            prev = jnp.where(first_row, pre[tile + 7:tile + 8], prev)
            nxt = jnp.where(last_row, pre[tile + 8:tile + 9], nxt)
        prev = jnp.where(seq_start, 0.0, prev)
        nxt = jnp.where(seq_end, 0.0, nxt)
        act = _silu(prev * cw[0:1, cols] + cur * cw[1:2, cols] + nxt * cw[2:3, cols])
        if b * blk >= 2 * QK:
            qkv_ref[:, cols] = act.astype(BF16)
            return
        scale = HEAD_DIM ** -0.5 if b * blk < QK else 1.0
        for h in range(blk // HEAD_DIM):
            a = act[:, h * HEAD_DIM:(h + 1) * HEAD_DIM]
            inv = lax.rsqrt(jnp.sum(a * a, axis=-1, keepdims=True) + EPS) * scale
            c0 = b * blk + h * HEAD_DIM
            qkv_ref[:, c0:c0 + HEAD_DIM] = (a * inv).astype(BF16)

    others = []
    for out_ref, w_ref in ((z_ref, wz_ref), (uv_ref, wuv_ref), (p_ref, wp_ref), (ba_ref, wba_ref)):
        width = out_ref.shape[1]
        others += [(out_ref, w_ref, slice(c, min(c + blk, width))) for c in range(0, width, blk)]
    n_blk = QKV // blk
    assert len(others) == n_blk
    pre = project(0)
    for b in range(n_blk):
        nxt_pre = project(b + 1) if b + 1 < n_blk else None
        out_ref, w_ref, cols = others[b]
        out_ref[:, cols] = _dot(hb, w_ref[:, cols])
        conv_act(b, pre)
        pre = nxt_pre


def _input_stage(x, mods_l, norm_g, w, conv_w, *, seq_len, row0):
    n = x.shape[0]
    tile = 512
    halo = seq_len > tile
    tiles_per_seq = max(seq_len // tile, 1)
    n_tiles = n // tile
    h8 = tile // 8
    last8 = n // 8 - 1
    in_specs = [pl.BlockSpec((tile, D_MODEL), lambda i: (i, 0))]
    args = [x]
    if halo:
        in_specs += [pl.BlockSpec((8, D_MODEL), lambda i: (jnp.maximum(i * h8 - 1, 0), 0)),
                     pl.BlockSpec((8, D_MODEL), lambda i: (jnp.minimum((i + 1) * h8, last8), 0))]
        args += [x, x]
    in_specs += [_const_spec((8, N_MOD * D_MODEL)), _const_spec((1, D_MODEL)),
                 _const_spec((D_MODEL, QKV)), _const_spec((D_MODEL, A_WIDTH)),
                 _const_spec((D_MODEL, 128)), _const_spec((D_MODEL, 2 * B_WIDTH)),
                 _const_spec((D_MODEL, C_WIDTH)), _const_spec((3, QKV))]
    args += [mods_l, norm_g, w["qkv"], w["z"], w["ba"], w["uv"], w["p"], conv_w]
    widths = (QKV, A_WIDTH, 128, 2 * B_WIDTH, C_WIDTH)
    dtypes = (BF16, F32, F32, F32, F32)
    return pl.pallas_call(
        functools.partial(_in_kernel, seq_len=seq_len, tile=tile, row0=row0, halo=halo),
        grid=(n_tiles,),
        in_specs=in_specs,
        out_specs=[pl.BlockSpec((tile, c), lambda i: (i, 0)) for c in widths],
        out_shape=[jax.ShapeDtypeStruct((n, c), t) for c, t in zip(widths, dtypes)],
        compiler_params=_params(1),
        name="input_stage",
    )(*args)


N_LEVELS = int(math.log2(CHUNK))
WIDE = HEADS * CHUNK


def _delta_constants():
    r = np.arange(CHUNK)[:, None]
    c = (np.arange(WIDE) % CHUNK)[None, :]
    lvl = np.zeros((2, N_LEVELS, CHUNK, WIDE), np.float32)
    for d in range(2):
        rr, cc = (r, c) if d == 0 else (c, r)
        for i in range(N_LEVELS):
            s = 1 << i
            lvl[d, i] = (rr // (2 * s) == cc // (2 * s)) & ((rr // s) % 2 == 1) & ((cc // s) % 2 == 0)
    eye = (r == c).astype(np.float32)
    tri = np.concatenate([np.tril(np.ones((CHUNK, CHUNK))), np.triu(np.ones((CHUNK, CHUNK)))], 0)
    return (jnp.asarray(lvl.reshape(2 * N_LEVELS, CHUNK, WIDE)), jnp.asarray(eye),
            jnp.asarray(tri, BF16))


def _pair(a, b, lt_half):
    return jnp.where(lt_half, a, b)


def _block_diag(blocks, zero):
    n = len(blocks)
    rows = [jnp.concatenate([blocks[i] if j == i else zero for j in range(n)], axis=1)
            for i in range(n)]
    return jnp.concatenate(rows, axis=0)


def _delta_kernel(*refs, n_sub, n_chunks, has_init, n_par, cb):
    it = iter(refs)
    qkv_ref, ba_ref, alog_ref, dtb_ref, lvl_ref, eye_ref, tri_ref = (next(it) for _ in range(7))
    s0_ref = next(it) if has_init else None
    o_ref = next(it)
    st_ref = None if has_init else next(it)
    (s_ref, gc_ref, beta_ref, gct_ref, u_ref, w_ref, qd_ref, qkd_ref, kdt_ref, gt_ref) = it
    n_total = n_sub * n_chunks
    lane = lax.broadcasted_iota(jnp.int32, (CHUNK, 128), 1)
    lt_half = lane < CHUNK
    lt_half_row = lt_half[0:1]
    rw = lax.broadcasted_iota(jnp.int32, (CHUNK, WIDE), 0)
    cw = lax.broadcasted_iota(jnp.int32, (CHUNK, WIDE), 1) & (CHUNK - 1)
    incl = (rw >= cw, rw <= cw)
    a_row = -jnp.exp(alog_ref[...])
    dtb_row = dtb_ref[...]

    def gate_body(bi, carry):
        span = cb * CHUNK
        r0 = pl.multiple_of(bi * span, span)
        ba = ba_ref[pl.ds(r0, span), :]
        beta_ref[pl.ds(r0, span), :] = _sigmoid(ba)
        al = pltpu.roll(ba, 128 - 2 * HEADS, 1)
        lane_b = lax.broadcasted_iota(jnp.int32, (span, 128), 1)
        g = jnp.where(lane_b < 2 * HEADS, a_row * _softplus(al + dtb_row), 0.0)
        g_hi = g.astype(BF16).astype(F32)
        r1 = g - g_hi
        g_mid = r1.astype(BF16).astype(F32)
        g_lo = (r1 - g_mid).astype(BF16).astype(F32)
        g3 = (g_hi + pltpu.roll(g_mid, 8, 1) + pltpu.roll(g_lo, 16, 1)).astype(BF16)
        css = [_dot(tri_ref[...], g3[i * CHUNK:(i + 1) * CHUNK]) for i in range(cb)]
        css = [cs + pltpu.roll(cs, 128 - 8, 1) + pltpu.roll(cs, 128 - 16, 1) for cs in css]
        gcs = [jnp.where(lane < HEADS, cs[0:CHUNK], cs[CHUNK:]) for cs in css]
        ats = [jnp.concatenate([gc, gc], axis=0).T[0:8, :] for gc in gcs]
        for i in range(cb):
            gc_ref[pl.ds(pl.multiple_of(r0 + i * CHUNK, CHUNK), CHUNK), :] = gcs[i]
            gct_ref[bi * cb + i] = ats[i]
        return carry

    lax.fori_loop(0, n_total // cb, gate_body, 0)

    zero_blk = jnp.zeros((CHUNK, HEAD_DIM), BF16)
    hs = [slice(h * HEAD_DIM, (h + 1) * HEAD_DIM) for h in range(HEADS)]

    def head_mm(x, y):
        xb = x.astype(BF16)
        yb = y.astype(BF16)
        zero = jnp.zeros((CHUNK, 128), BF16)
        outs = []
        for pr in range(HEADS // 2):
            ys = yb[:, pr * 128:(pr + 1) * 128]
            bd = jnp.concatenate([jnp.where(lt_half, ys, zero), jnp.where(lt_half, zero, ys)], axis=0)
            outs.append(_dot(xb[:, pr * 128:(pr + 1) * 128], bd))
        return jnp.concatenate(outs, axis=1)

    def pre_body(bi, carry):
        groups = []
        for i in range(cb):
            c = bi * cb + i
            r0 = pl.multiple_of(c * CHUNK, CHUNK)
            rows = pl.ds(r0, CHUNK)
            qb = qkv_ref[rows, 0:QK]
            kb16 = qkv_ref[rows, QK:2 * QK]
            qf = qb.astype(F32)
            kf = kb16.astype(F32)
            vf = qkv_ref[rows, 2 * QK:].astype(F32)
            kt = jnp.concatenate(
                [jnp.concatenate([kf[:, 0:128], kf[:, 128:256]], axis=0).T,
                 jnp.concatenate([kf[:, 256:384], kf[:, 384:512]], axis=0).T], axis=1)
            ktbd = _group_rows(kt.astype(BF16), HEADS)
            qkk = _dot(jnp.concatenate([qb, kb16], axis=0), ktbd)
            cols = gc_ref[rows, :]
            bcols = beta_ref[rows, :]
            at = gct_ref[c]
            for d in range(2):
                j0 = d * HEADS
                bg = [jnp.broadcast_to(cols[:, j0 + h:j0 + h + 1], (CHUNK, 128)) for h in range(HEADS)]
                bb = [jnp.broadcast_to(bcols[:, j0 + h:j0 + h + 1], (CHUNK, 128)) for h in range(HEADS)]
                gcw = jnp.concatenate([_pair(bg[0], bg[1], lt_half), _pair(bg[2], bg[3], lt_half)], 1)
                bw = jnp.concatenate([_pair(bb[0], bb[1], lt_half), _pair(bb[2], bb[3], lt_half)], 1)
                gr = jnp.concatenate(
                    [_pair(at[j0:j0 + 1], at[j0 + 1:j0 + 2], lt_half_row),
                     _pair(at[j0 + 2:j0 + 3], at[j0 + 3:j0 + 4], lt_half_row)], axis=1)
                decay = jnp.exp(jnp.where(incl[d], gcw - gr, NEG_BIG))
                m = bw * qkk[CHUNK:] * decay
                qkd_ref[d, rows, :] = (qkk[0:CHUNK] * decay).astype(BF16)
                en = jnp.concatenate([jnp.exp(x) for x in bg], axis=1)
                bn = jnp.concatenate(bb, axis=1)
                qd_ref[d, rows, :] = (qf * en).astype(BF16)
                vb = (vf * bn).astype(BF16)
                kbe = (kf * (bn * en)).astype(BF16)
                e0 = CHUNK - 1 if d == 0 else 0
                bt = [jnp.broadcast_to(at[j0 + h:j0 + h + 1, e0:e0 + 1], (1, 128)) for h in range(HEADS)]
                tw = jnp.concatenate([_pair(bt[0], bt[1], lt_half_row),
                                      _pair(bt[2], bt[3], lt_half_row)], axis=1)
                kdt_ref[d, pl.ds(pl.multiple_of(c * HEAD_DIM, HEAD_DIM), HEAD_DIM), :] = (
                    kt * jnp.exp(tw - gr)).astype(BF16)
                gt_ref[d, c] = jnp.broadcast_to(
                    jnp.concatenate([jnp.exp(x) for x in bt], axis=1), (8, A_WIDTH))
                groups.append((d, rows, m, vb, kbe))
        ts = [eye_ref[...] - g[2] * lvl_ref[g[0] * N_LEVELS] for g in groups]
        for lv in range(1, N_LEVELS):
            xs = [head_mm(t, g[2] * lvl_ref[g[0] * N_LEVELS + lv]) for t, g in zip(ts, groups)]
            ts = [t - head_mm(x, t) for t, x in zip(ts, xs)]
        for t, (d, rows, _, vb, kbe) in zip(ts, groups):
            tb = t.astype(BF16)
            uw = []
            for pr in range(HEADS // 2):
                h0, h1 = hs[2 * pr], hs[2 * pr + 1]
                rhs = jnp.concatenate(
                    [jnp.concatenate([vb[:, h0], zero_blk, kbe[:, h0], zero_blk], axis=1),
                     jnp.concatenate([zero_blk, vb[:, h1], zero_blk, kbe[:, h1]], axis=1)], axis=0)
                uw.append(_dot(tb[:, pr * 128:(pr + 1) * 128], rhs))
            half = A_WIDTH // 2
            u_ref[d, rows, :] = jnp.concatenate([uw[0][:, 0:half], uw[1][:, 0:half]], axis=1)
            w_ref[d, rows, :] = jnp.concatenate([uw[0][:, half:], uw[1][:, half:]],
                                                axis=1).astype(BF16)
        return carry

    lax.fori_loop(0, n_total // cb, pre_body, 0)

    o_ref[...] = jnp.zeros(o_ref.shape, F32)
    zero_s = jnp.zeros((HEAD_DIM, HEAD_DIM), BF16)

    def scan_step(sub0, c):
        groups = [(p, d) for p in range(n_par) for d in range(2)]
        cidx = [(sub0 + p) * n_chunks + (c if d == 0 else n_chunks - 1 - c) for p, d in groups]
        rows = [pl.ds(pl.multiple_of(ci * CHUNK, CHUNK), CHUNK) for ci in cidx]
        ss = [s_ref[p, d] for p, d in groups]
        wqs = []
        for (p, d), r, s in zip(groups, rows, ss):
            sb = s.astype(BF16)
            lhs = jnp.concatenate([w_ref[d, r, :], qd_ref[d, r, :]], axis=0)
            halves = []
            for hp in range(HEADS // 2):
                bd = _block_diag([sb[:, hs[2 * hp]], sb[:, hs[2 * hp + 1]]], zero_s)
                halves.append(_dot(lhs[:, hp * 2 * HEAD_DIM:(hp + 1) * 2 * HEAD_DIM], bd))
            wqs.append(jnp.concatenate(halves, axis=1))
        boths = []
        for (p, d), r, ci, wq in zip(groups, rows, cidx, wqs):
            v_new = (u_ref[d, r, :] - wq[0:CHUNK]).astype(BF16)
            vbd = _block_diag([v_new[:, s] for s in hs], zero_blk)
            kd = kdt_ref[d, pl.ds(pl.multiple_of(ci * HEAD_DIM, HEAD_DIM), HEAD_DIM), :]
            lhs2 = jnp.concatenate([qkd_ref[d, r, :], kd], axis=0)
            boths.append(_dot(lhs2, vbd))
        for (p, d), r, ci, s, wq, both in zip(groups, rows, cidx, ss, wqs, boths):
            o_ref[r, :] += wq[CHUNK:] + both[0:CHUNK]
            s_ref[p, d] = s * gt_ref[d, ci][0:1] + both[CHUNK:]

    def sub_body(sb_i, carry):
        sub0 = sb_i * n_par
        for p in range(n_par):
            for d in range(2):
                if has_init:
                    s_ref[p, d] = jnp.concatenate(
                        [s0_ref[d * HEADS + h] for h in range(HEADS)], axis=1)
                else:
                    s_ref[p, d] = jnp.zeros((HEAD_DIM, A_WIDTH), F32)

        def step(c, carry2):
            scan_step(sub0, c)
            return carry2

        lax.fori_loop(0, n_chunks, step, 0)
        if st_ref is not None:
            for p in range(n_par):
                for d in range(2):
                    s = s_ref[p, d]
                    for h in range(HEADS):
                        st_ref[sub0 + p, d * HEADS + h] = s[:, hs[h]]
        return carry

    lax.fori_loop(0, n_sub // n_par, sub_body, 0)


def _delta_stage(qkv, ba, alog_row, dtb_row, s0, consts, *, seq_len):
    n = qkv.shape[0]
    block = max(seq_len, DELTA_BLOCK)
    n_blocks = n // block
    n_sub = block // seq_len
    n_chunks = seq_len // CHUNK
    n_total = block // CHUNK
    has_init = s0 is not None
    n_par = min(n_sub, 4)
    lvl, eye, tri = consts
    in_specs = [pl.BlockSpec((block, QKV), lambda i: (i, 0)),
                pl.BlockSpec((block, 128), lambda i: (i, 0)),
                _const_spec((1, 128)), _const_spec((1, 128)),
                _const_spec(lvl.shape), _const_spec(eye.shape), _const_spec(tri.shape)]
    args = [qkv, ba, alog_row, dtb_row, lvl, eye, tri]
    o_spec = pl.BlockSpec((block, A_WIDTH), lambda i: (i, 0))
    o_shape = jax.ShapeDtypeStruct((n, A_WIDTH), F32)
    st_shape = (2 * HEADS, HEAD_DIM, HEAD_DIM)
    if has_init:
        assert n_sub == 1
        in_specs.append(pl.BlockSpec((None,) + st_shape, lambda i: (i, 0, 0, 0)))
        args.append(s0)
        out_specs, out_shape = o_spec, o_shape
    else:
        out_specs = [o_spec, pl.BlockSpec((n_sub,) + st_shape, lambda i: (i, 0, 0, 0))]
        out_shape = [o_shape, jax.ShapeDtypeStruct((n // seq_len,) + st_shape, F32)]
    scratch = [pltpu.VMEM((n_par, 2, HEAD_DIM, A_WIDTH), F32),
               pltpu.VMEM((block, 128), F32),
               pltpu.VMEM((block, 128), F32),
               pltpu.VMEM((n_total, 8, 128), F32),
               pltpu.VMEM((2, block, A_WIDTH), F32),
               pltpu.VMEM((2, block, A_WIDTH), BF16),
               pltpu.VMEM((2, block, A_WIDTH), BF16),
               pltpu.VMEM((2, block, WIDE), BF16),
               pltpu.VMEM((2, n_total * HEAD_DIM, WIDE), BF16),
               pltpu.VMEM((2, n_total, 8, A_WIDTH), F32)]
    return pl.pallas_call(
        functools.partial(_delta_kernel, n_sub=n_sub, n_chunks=n_chunks, has_init=has_init,
                          n_par=n_par, cb=4),
        grid=(n_blocks,),
        in_specs=in_specs,
        out_specs=out_specs,
        out_shape=out_shape,
        scratch_shapes=scratch,
        compiler_params=_params(1),
        name="delta_stage",
    )(*args)


def _group_rows(y, n_groups):
    half = y.shape[1] // 2
    gw = half // 2
    lane = lax.broadcasted_iota(jnp.int32, (y.shape[0], half), 1)
    zero = jnp.zeros((y.shape[0], half), y.dtype)
    blocks = []
    for g in range(n_groups):
        part = y[:, (g // 2) * half:(g // 2 + 1) * half]
        keep = (lane < gw) if g % 2 == 0 else (lane >= gw)
        part = jnp.where(keep, part, jnp.zeros_like(part))
        blocks.append(jnp.concatenate([part, zero] if g < 2 else [zero, part], axis=1))
    return jnp.concatenate(blocks, axis=0)


def _local_mix(o_ref, z_ref, uv_ref, p_ref, dg_ref, sg_ref, ws_ref, bs_ref, seg_ref, band_ref,
               icnt_ref, wp_ref, ps_ref, mix_ref, tile):
    dg = dg_ref[...]
    for h in range(HEADS):
        cs = slice(h * HEAD_DIM, (h + 1) * HEAD_DIM)
        y = _rms_rows(o_ref[:, cs]) * dg * _silu(z_ref[:, cs])
        mix_ref[:, cs] = y.astype(BF16)
    u = _gelu_tanh(uv_ref[:, 0:B_WIDTH])
    v = _gelu_tanh(uv_ref[:, B_WIDTH:])
    vv = v * v
    vv_hi = vv.astype(BF16)
    vv_lo = (vv - vv_hi.astype(F32)).astype(BF16)
    ms = _dot(jnp.concatenate([vv_hi, vv_lo], axis=1), seg_ref[...])
    vn = (v * lax.rsqrt(ms + EPS) * sg_ref[...]).astype(BF16)
    for c in range(tile // SGU_CHUNK):
        rs = slice(c * SGU_CHUNK, (c + 1) * SGU_CHUNK)
        s = _dot(ws_ref[...], _group_rows(vn[rs], B_GROUPS)) + bs_ref[...]
        mix_ref[rs, A_WIDTH:A_WIDTH + B_WIDTH] = (u[rs] * s).astype(BF16)
    for c in range(tile // POOL_TILE):
        rs = slice(c * POOL_TILE, (c + 1) * POOL_TILE)
        x = p_ref[rs, :]
        x_hi = x.astype(BF16)
        x_lo = (x - x_hi.astype(F32)).astype(BF16)
        n_win = len(POOL_WINDOWS)
        rhs = jnp.concatenate([_group_rows(x_hi, n_win), _group_rows(x_lo, n_win)], axis=0)
        wsum = _dot(band_ref[...], rhs)
        diff = wsum * icnt_ref[...] - x
        y = _bdot(diff, wp_ref[...]) * ps_ref[...]
        mix_ref[rs, A_WIDTH + B_WIDTH:] = y.astype(BF16)


def _out_kernel(x_ref, o_ref, z_ref, uv_ref, p_ref, dg_ref, sg_ref, ws_ref, bs_ref, seg_ref,
                band_ref, icnt_ref, wp_ref, ps_ref, mod_ref, g2_ref, wo_ref, wgu_ref, wd_ref,
                nf_ref, y_ref, mix_ref, *, seq_len, tile, row0, final_norm):
    i = pl.program_id(0)
    tiles_per_seq = max(seq_len // tile, 1)
    row = row0 + i // tiles_per_seq if row0 else 0
    m = mod_ref[pl.ds(row, 1), :]
    gate1 = m[:, 2 * D_MODEL:3 * D_MODEL]
    shift2 = m[:, 3 * D_MODEL:4 * D_MODEL]
    scale2 = m[:, 4 * D_MODEL:5 * D_MODEL]
    gate2 = m[:, 5 * D_MODEL:6 * D_MODEL]
    _local_mix(o_ref, z_ref, uv_ref, p_ref, dg_ref, sg_ref, ws_ref, bs_ref, seg_ref, band_ref,
               icnt_ref, wp_ref, ps_ref, mix_ref, tile)
    x1 = x_ref[...] + gate1 * _dot(mix_ref[...], wo_ref[...])
    hb = _modulated_norm(x1, g2_ref[...], shift2, scale2).astype(BF16)
    fc = FF_HIDDEN // FF_SPLIT
    ff = jnp.zeros((tile, D_MODEL), F32)
    for c in range(FF_SPLIT):
        gate = _dot(hb, wgu_ref[:, c * fc:(c + 1) * fc])
        up = _dot(hb, wgu_ref[:, FF_HIDDEN + c * fc:FF_HIDDEN + (c + 1) * fc])
        act = (_silu(gate) * up).astype(BF16)
        ff = ff + _dot(act, wd_ref[c * fc:(c + 1) * fc, :])
    x2 = x1 + gate2 * ff
    if final_norm:
        x2 = _rms_rows(x2) * nf_ref[...]
    y_ref[...] = x2


def _output_stage(x, o, z, uv, p, lw, pool_consts, mods_l, norm2_g, w, norm_f, *, seq_len, row0,
                  final_norm):
    n = x.shape[0]
    tile = 512
    band, icnt = pool_consts
    row = lambda c: pl.BlockSpec((tile, c), lambda i: (i, 0))
    consts = [lw["delta_g"], lw["sgu_g"], lw["w_spatial"], lw["b_spatial"], lw["seg"], band, icnt,
              lw["w_pool"], lw["pool_scale"], mods_l, norm2_g, w["out"], w["gu"], w["down"], norm_f]
    in_specs = ([row(D_MODEL), row(A_WIDTH), row(A_WIDTH), row(2 * B_WIDTH), row(C_WIDTH)]
                + [_const_spec(a.shape) for a in consts])
    return pl.pallas_call(
        functools.partial(_out_kernel, seq_len=seq_len, tile=tile, row0=row0,
                          final_norm=final_norm),
        grid=(n // tile,),
        in_specs=in_specs,
        out_specs=row(D_MODEL),
        out_shape=jax.ShapeDtypeStruct((n, D_MODEL), F32),
        scratch_shapes=[pltpu.VMEM((tile, D_MODEL), BF16)],
        compiler_params=_params(1),
        name="output_stage",
    )(x, o, z, uv, p, *consts)


def _pool_constants(seg_len):
    pos = np.arange(POOL_TILE)
    seg = pos // seg_len
    band = np.zeros((len(POOL_WINDOWS), POOL_TILE, POOL_TILE), np.float32)
    icnt = np.zeros((POOL_TILE, C_WIDTH), np.float32)
    for g, win in enumerate(POOL_WINDOWS):
        lo = pos - win // 2
        hi = pos + win - win // 2
        inside = (pos[None, :] >= lo[:, None]) & (pos[None, :] < hi[:, None]) \
            & (seg[None, :] == seg[:, None])
        band[g] = inside
        icnt[:, g * C_GC:(g + 1) * C_GC] = (1.0 / inside.sum(axis=1))[:, None]
    band_cat = np.concatenate([band[g] for g in range(len(POOL_WINDOWS))] * 2, axis=1)
    return jnp.asarray(band_cat, BF16), jnp.asarray(icnt)


def _grid_pos_embed(rows, d):
    quarter = d // 4
    omega = 1.0 / (10000.0 ** (jnp.arange(quarter, dtype=F32) / quarter))
    r = jnp.arange(rows, dtype=F32)[:, None] * omega
    cl = jnp.arange(GRID_W, dtype=F32)[:, None] * omega
    row_emb = jnp.concatenate([jnp.sin(r), jnp.cos(r)], axis=-1)
    col_emb = jnp.concatenate([jnp.sin(cl), jnp.cos(cl)], axis=-1)
    emb = jnp.concatenate([jnp.broadcast_to(row_emb[:, None, :], (rows, GRID_W, d // 2)),
                           jnp.broadcast_to(col_emb[None, :, :], (rows, GRID_W, d // 2))], axis=-1)
    return emb.reshape(rows * GRID_W, d)


def _pad_lanes(a, width):
    return jnp.pad(a, ((0, 0), (0, width - a.shape[1])))


def _layer_weights(l, w_in, w_out, w_gu, w_down, w_pool, w_spatial, b_spatial):
    wi = w_in[l]
    c0 = 2 * QK + 2 * A_WIDTH
    wba = _pad_lanes(wi[:, c0:c0 + 4 * HEADS], 128)
    c1 = c0 + 4 * HEADS
    dense = {
        "qkv": wi[:, 0:QKV].astype(BF16),
        "z": wi[:, QKV:QKV + A_WIDTH].astype(BF16),
        "ba": wba.astype(BF16),
        "uv": wi[:, c1:c1 + 2 * B_WIDTH].astype(BF16),
        "p": wi[:, c1 + 2 * B_WIDTH:].astype(BF16),
        "out": w_out[l].astype(BF16),
        "gu": w_gu[l].astype(BF16),
        "down": w_down[l].astype(BF16),
    }
    wp = jnp.zeros((C_WIDTH, C_WIDTH), F32)
    for g in range(len(POOL_WINDOWS)):
        wp = wp.at[g * C_GC:(g + 1) * C_GC, g * C_GC:(g + 1) * C_GC].set(w_pool[l, g])
    grp = np.arange(B_WIDTH) // B_GC
    seg1 = (grp[:, None] == grp[None, :]).astype(np.float32) / B_GC
    seg = jnp.asarray(np.concatenate([seg1, seg1], axis=0), BF16)
    bs = jnp.repeat(b_spatial[l].T, B_GC, axis=1)
    ws_cat = jnp.concatenate([w_spatial[l, g] for g in range(B_GROUPS)], axis=1)
    return dense, wp.astype(BF16), seg, ws_cat.astype(BF16), bs


def kernel(x_prompt, x_sample, state_delta, c, c_ctx, w_in, conv_w, a_log, dt_bias, delta_norm_g,
           sgu_norm_g, w_spatial, b_spatial, w_pool, pool_scale, w_out, norm1_g, norm2_g, w_mod,
           b_mod, w_gu, w_down, norm_f):
    batch, seq, d = x_prompt.shape
    dec_batch, dec_seq, _ = x_sample.shape
    cvec = jnp.concatenate([c_ctx[None, :], c, jnp.zeros((8 - 1 - dec_batch, d), F32)], axis=0)
    mods = _modulation(cvec, w_mod, b_mod)
    pos = _grid_pos_embed(dec_seq // GRID_W, d)
    xs = _add_pos(x_sample, pos).reshape(dec_batch * dec_seq, d)
    xc = x_prompt.reshape(batch * seq, d)
    pool_ctx = _pool_constants(min(seq, POOL_TILE))
    pool_lat = _pool_constants(GRID_W)
    nf = norm_f.reshape(1, d)
    delta_consts = _delta_constants()
    ctx_states = []
    for l in range(DEPTH):
        dense, wp, seg, ws, bs = _layer_weights(l, w_in, w_out, w_gu, w_down, w_pool, w_spatial,
                                                b_spatial)
        lw = {"delta_g": delta_norm_g[l].reshape(1, HEAD_DIM),
              "sgu_g": sgu_norm_g[l].reshape(1, B_WIDTH), "w_spatial": ws, "b_spatial": bs,
              "seg": seg, "w_pool": wp, "pool_scale": pool_scale[l].reshape(1, C_WIDTH)}
        alog_row = _pad_lanes(a_log[l].reshape(1, 2 * HEADS), 128)
        dtb_row = _pad_lanes(dt_bias[l].reshape(1, 2 * HEADS), 128)
        n1 = norm1_g[l].reshape(1, d)
        n2 = norm2_g[l].reshape(1, d)
        last = l == DEPTH - 1
        s0_lat = state_delta[:, l].reshape(dec_batch, 2 * HEADS, HEAD_DIM, HEAD_DIM)
        streams = (("ctx", xc, seq, 0, None, pool_ctx), ("lat", xs, dec_seq, 1, s0_lat, pool_lat))
        outs = []
        for name, x, seq_len, row0, s0, pool_consts in streams:
            qkv, z, ba, uv, p = _input_stage(x, mods[l], n1, dense, conv_w[l], seq_len=seq_len,
                                             row0=row0)
            res = _delta_stage(qkv, ba, alog_row, dtb_row, s0, delta_consts, seq_len=seq_len)
            if s0 is None:
                o, st = res
                ctx_states.append(st.reshape(batch, 2, HEADS, HEAD_DIM, HEAD_DIM))
            else:
                o = res
            outs.append(_output_stage(x, o, z, uv, p, lw, pool_consts, mods[l], n2, dense, nf,
                                      seq_len=seq_len, row0=row0, final_norm=last))
        xc, xs = outs
    y_prompt = xc.reshape(batch, seq, d)
    y_sample = xs.reshape(dec_batch, dec_seq, d)
    new_state = jnp.stack(ctx_states, axis=1)
    return (y_prompt, y_sample, new_state)
```

```python
import functools
import math

import numpy as np
import jax
import jax.numpy as jnp
from jax import lax
from jax.experimental import pallas as pl
from jax.experimental.pallas import tpu as pltpu

F32 = jnp.float32
BF16 = jnp.bfloat16
HIGHEST = lax.Precision.HIGHEST

D_MODEL = 1024
DEPTH = 2
GRID_W = 64
HEADS = 4
HEAD_DIM = 128
QK = HEADS * HEAD_DIM
A_WIDTH = HEADS * HEAD_DIM
QKV = 2 * QK + A_WIDTH
CHUNK = 64
B_WIDTH = 256
B_GROUPS = 4
B_GC = 64
SGU_CHUNK = 128
C_WIDTH = 256
POOL_WINDOWS = (2, 4, 8, 16)
C_GC = 64
FF_HIDDEN = 2816
FF_SPLIT = 2
N_MOD = 6
EPS = 1e-6
IN_Z = QKV
IN_UV = IN_Z + A_WIDTH
IN_P = IN_UV + 2 * B_WIDTH
IN_BA = IN_P + C_WIDTH
IN_PAD = IN_BA + 128
NEG_BIG = -1e30

DELTA_BLOCK = 1024
POOL_TILE = 256
VMEM_LIMIT = 56 * 1024 * 1024


def _dot(a, b, precision=None):
    return jnp.dot(a, b, preferred_element_type=F32, precision=precision)


def _bdot(a, b):
    return jnp.dot(a.astype(BF16), b.astype(BF16), preferred_element_type=F32)


def _sigmoid(x):
    return 1.0 / (1.0 + jnp.exp(-x))


def _silu(x):
    return x * _sigmoid(x)


def _softplus(x):
    return jnp.maximum(x, 0.0) + jnp.log1p(jnp.exp(-jnp.abs(x)))


def _gelu_tanh(x):
    c = math.sqrt(2.0 / math.pi)
    return 0.5 * x * (1.0 + jnp.tanh(c * (x + 0.044715 * (x * x * x))))


def _rms_rows(x):
    return x * lax.rsqrt(jnp.mean(x * x, axis=-1, keepdims=True) + EPS)


def _params(n_grid):
    return pltpu.CompilerParams(dimension_semantics=("arbitrary",) * n_grid,
                                vmem_limit_bytes=VMEM_LIMIT)


def _const_spec(shape):
    nd = len(shape)
    return pl.BlockSpec(shape, lambda *_: (0,) * nd, pipeline_mode=pl.Buffered(1))


def _layer_spec(shape, layer):
    nd = len(shape) - 1
    return pl.BlockSpec((None,) + tuple(shape[1:]), lambda *_: (layer,) + (0,) * nd,
                        pipeline_mode=pl.Buffered(1))


def _mod_kernel(c_ref, w_ref, b_ref, o_ref):
    a = _silu(c_ref[...])
    o_ref[0] = _bdot(a, w_ref[0]) + b_ref[0]


def _modulation(cvec, w_mod, b_mod):
    tn = 1536
    n_out = N_MOD * D_MODEL
    return pl.pallas_call(
        _mod_kernel,
        grid=(DEPTH, n_out // tn),
        in_specs=[pl.BlockSpec((8, D_MODEL), lambda l, j: (0, 0)),
                  pl.BlockSpec((1, D_MODEL, tn), lambda l, j: (l, 0, j)),
                  pl.BlockSpec((1, 1, tn), lambda l, j: (l, 0, j))],
        out_specs=pl.BlockSpec((1, 8, tn), lambda l, j: (l, 0, j)),
        out_shape=jax.ShapeDtypeStruct((DEPTH, 8, n_out), F32),
        compiler_params=_params(2),
        name="modulation",
    )(cvec, w_mod, b_mod.reshape(DEPTH, 1, n_out))


POS_ROWS = 8


def _pos_kernel(x_ref, r_ref, c_ref, o_ref):
    half = D_MODEL // 2
    col = c_ref[...]
    for r in range(POS_ROWS):
        rs = slice(r * GRID_W, (r + 1) * GRID_W)
        o_ref[rs, 0:half] = x_ref[rs, 0:half] + r_ref[r:r + 1, :]
        o_ref[rs, half:] = x_ref[rs, half:] + col


def _add_pos(x, row_emb, col_emb):
    b, n, d = x.shape
    tm = POS_ROWS * GRID_W
    return pl.pallas_call(
        _pos_kernel,
        grid=(b, n // tm),
        in_specs=[pl.BlockSpec((None, tm, d), lambda i, j: (i, j, 0)),
                  pl.BlockSpec((POS_ROWS, d // 2), lambda i, j: (j, 0)),
                  pl.BlockSpec((GRID_W, d // 2), lambda i, j: (0, 0))],
        out_specs=pl.BlockSpec((None, tm, d), lambda i, j: (i, j, 0)),
        out_shape=jax.ShapeDtypeStruct(x.shape, F32),
        compiler_params=_params(2),
        name="add_pos",
    )(x, row_emb, col_emb)


def _modulated_norm(x, gain, shift, scale):
    return _rms_rows(x) * gain * (1.0 + scale) + shift


def _in_kernel(*refs, seq_len, tile, row0, halo):
    if halo:
        (x_ref, xp_ref, xn_ref, mod_ref, g_ref, win_ref,
         cw_ref, qkv_ref, z_ref, ba_ref, uv_ref, p_ref) = refs
    else:
        (x_ref, mod_ref, g_ref, win_ref,
         cw_ref, qkv_ref, z_ref, ba_ref, uv_ref, p_ref) = refs
    i = pl.program_id(0)
    tiles_per_seq = max(seq_len // tile, 1)
    row = row0 + i // tiles_per_seq if row0 else 0
    m = mod_ref[pl.ds(row, 1), :]
    shift = m[:, 0:D_MODEL]
    scale = m[:, D_MODEL:2 * D_MODEL]
    gain = g_ref[...]
    hf = _modulated_norm(x_ref[...], gain, shift, scale)
    hb = hf.astype(BF16)
    if halo:
        hp = _modulated_norm(xp_ref[...], gain, shift, scale)
        hn = _modulated_norm(xn_ref[...], gain, shift, scale)
        hq = jnp.concatenate([hf, hp, hn], axis=0).astype(BF16)
    else:
        hq = hb
    ridx = lax.broadcasted_iota(jnp.int32, (tile, 1), 0)
    if halo:
        pos_in_seq = (i % tiles_per_seq) * tile + ridx
    else:
        pos_in_seq = ridx % seq_len
    first_row = ridx == 0
    last_row = ridx == tile - 1
    seq_start = pos_in_seq == 0
    seq_end = pos_in_seq == seq_len - 1
    cw = cw_ref[...]
    blk = 2 * HEAD_DIM

    def project(b):
        return _dot(hq, win_ref[:, b * blk:(b + 1) * blk])

    def conv_act(b, pre):
        cols = slice(b * blk, (b + 1) * blk)
        cur = pre[0:tile]
        prev = pltpu.roll(cur, 1, 0)
        nxt = pltpu.roll(cur, tile - 1, 0)
        if halo:
            prev = jnp.where(first_row, pre[tile + 7:tile + 8], prev)
            nxt = jnp.where(last_row, pre[tile + 8:tile + 9], nxt)
        prev = jnp.where(seq_start, 0.0, prev)
        nxt = jnp.where(seq_end, 0.0, nxt)
        act = _silu(prev * cw[0:1, cols] + cur * cw[1:2, cols] + nxt * cw[2:3, cols])
        if b * blk >= 2 * QK:
            qkv_ref[:, cols] = act.astype(BF16)
            return
        scale = HEAD_DIM ** -0.5 if b * blk < QK else 1.0
        for h in range(blk // HEAD_DIM):
            a = act[:, h * HEAD_DIM:(h + 1) * HEAD_DIM]
            inv = lax.rsqrt(jnp.sum(a * a, axis=-1, keepdims=True) + EPS) * scale
            c0 = b * blk + h * HEAD_DIM
            qkv_ref[:, c0:c0 + HEAD_DIM] = (a * inv).astype(BF16)

    others = []
    for out_ref, c0 in ((z_ref, IN_Z), (uv_ref, IN_UV), (p_ref, IN_P), (ba_ref, IN_BA)):
        width = out_ref.shape[1]
        others += [(out_ref, c0, slice(c, min(c + blk, width))) for c in range(0, width, blk)]
    n_blk = QKV // blk
    assert len(others) == n_blk
    pre = project(0)
    for b in range(n_blk):
        nxt_pre = project(b + 1) if b + 1 < n_blk else None
        out_ref, c0, cols = others[b]
        out_ref[:, cols] = _dot(hb, win_ref[:, c0 + cols.start:c0 + cols.stop])
        conv_act(b, pre)
        pre = nxt_pre


def _input_stage(x, mods_l, norm_g, w_in_b, layer, conv_w, *, seq_len, row0):
    n = x.shape[0]
    tile = 512
    halo = seq_len > tile
    tiles_per_seq = max(seq_len // tile, 1)
    n_tiles = n // tile
    h8 = tile // 8
    last8 = n // 8 - 1
    in_specs = [pl.BlockSpec((tile, D_MODEL), lambda i: (i, 0))]
    args = [x]
    if halo:
        in_specs += [pl.BlockSpec((8, D_MODEL), lambda i: (jnp.maximum(i * h8 - 1, 0), 0)),
                     pl.BlockSpec((8, D_MODEL), lambda i: (jnp.minimum((i + 1) * h8, last8), 0))]
        args += [x, x]
    in_specs += [_const_spec((8, N_MOD * D_MODEL)), _const_spec((1, D_MODEL)),
                 _layer_spec(w_in_b.shape, layer), _const_spec((3, QKV))]
    args += [mods_l, norm_g, w_in_b, conv_w]
    widths = (QKV, A_WIDTH, 128, 2 * B_WIDTH, C_WIDTH)
    dtypes = (BF16, F32, F32, F32, F32)
    return pl.pallas_call(
        functools.partial(_in_kernel, seq_len=seq_len, tile=tile, row0=row0, halo=halo),
        grid=(n_tiles,),
        in_specs=in_specs,
        out_specs=[pl.BlockSpec((tile, c), lambda i: (i, 0)) for c in widths],
        out_shape=[jax.ShapeDtypeStruct((n, c), t) for c, t in zip(widths, dtypes)],
        compiler_params=_params(1),
        name="input_stage",
    )(*args)


N_LEVELS = int(math.log2(CHUNK))
WIDE = HEADS * CHUNK


def _delta_constants():
    r = np.arange(CHUNK)[:, None]
    c = (np.arange(WIDE) % CHUNK)[None, :]
    lvl = np.zeros((2, N_LEVELS, CHUNK, WIDE), np.float32)
    for d in range(2):
        rr, cc = (r, c) if d == 0 else (c, r)
        for i in range(N_LEVELS):
            s = 1 << i
            lvl[d, i] = (rr // (2 * s) == cc // (2 * s)) & ((rr // s) % 2 == 1) & ((cc // s) % 2 == 0)
    eye = (r == c).astype(np.float32)
    tri = np.concatenate([np.tril(np.ones((CHUNK, CHUNK))), np.triu(np.ones((CHUNK, CHUNK)))], 0)
    return (jnp.asarray(lvl.reshape(2 * N_LEVELS, CHUNK, WIDE)), jnp.asarray(eye),
            jnp.asarray(tri, BF16))


def _pair(a, b, lt_half):
    return jnp.where(lt_half, a, b)


def _block_diag(blocks, zero):
    n = len(blocks)
    rows = [jnp.concatenate([blocks[i] if j == i else zero for j in range(n)], axis=1)
            for i in range(n)]
    return jnp.concatenate(rows, axis=0)


def _delta_kernel(*refs, n_sub, n_chunks, has_init, n_par, cb):
    it = iter(refs)
    qkv_ref, ba_ref, alog_ref, dtb_ref, lvl_ref, eye_ref, tri_ref = (next(it) for _ in range(7))
    s0_ref = next(it) if has_init else None
    o_ref = next(it)
    st_ref = None if has_init else next(it)
    (s_ref, gc_ref, beta_ref, gct_ref, u_ref, w_ref, qd_ref, qkd_ref, kdt_ref, gt_ref) = it
    n_total = n_sub * n_chunks
    lane = lax.broadcasted_iota(jnp.int32, (CHUNK, 128), 1)
    lt_half = lane < CHUNK
    lt_half_row = lt_half[0:1]
    rw = lax.broadcasted_iota(jnp.int32, (CHUNK, WIDE), 0)
    cw = lax.broadcasted_iota(jnp.int32, (CHUNK, WIDE), 1) & (CHUNK - 1)
    incl = (rw >= cw, rw <= cw)
    a_row = -jnp.exp(alog_ref[...])
    dtb_row = dtb_ref[...]

    def gate_body(bi, carry):
        span = cb * CHUNK
        r0 = pl.multiple_of(bi * span, span)
        ba = ba_ref[pl.ds(r0, span), :]
        beta_ref[pl.ds(r0, span), :] = _sigmoid(ba)
        al = pltpu.roll(ba, 128 - 2 * HEADS, 1)
        lane_b = lax.broadcasted_iota(jnp.int32, (span, 128), 1)
        g = jnp.where(lane_b < 2 * HEADS, a_row * _softplus(al + dtb_row), 0.0)
        g_hi = g.astype(BF16).astype(F32)
        r1 = g - g_hi
        g_mid = r1.astype(BF16).astype(F32)
        g_lo = (r1 - g_mid).astype(BF16).astype(F32)
        g3 = (g_hi + pltpu.roll(g_mid, 8, 1) + pltpu.roll(g_lo, 16, 1)).astype(BF16)
        css = [_dot(tri_ref[...], g3[i * CHUNK:(i + 1) * CHUNK]) for i in range(cb)]
        css = [cs + pltpu.roll(cs, 128 - 8, 1) + pltpu.roll(cs, 128 - 16, 1) for cs in css]
        gcs = [jnp.where(lane < HEADS, cs[0:CHUNK], cs[CHUNK:]) for cs in css]
        ats = [jnp.concatenate([gc, gc], axis=0).T[0:8, :] for gc in gcs]
        for i in range(cb):
            gc_ref[pl.ds(pl.multiple_of(r0 + i * CHUNK, CHUNK), CHUNK), :] = gcs[i]
            gct_ref[bi * cb + i] = ats[i]
        return carry

    lax.fori_loop(0, n_total // cb, gate_body, 0)

    zero_blk = jnp.zeros((CHUNK, HEAD_DIM), BF16)
    hs = [slice(h * HEAD_DIM, (h + 1) * HEAD_DIM) for h in range(HEADS)]

    def head_mm(x, y):
        xb = x.astype(BF16)
        yb = y.astype(BF16)
        zero = jnp.zeros((CHUNK, 128), BF16)
        outs = []
        for pr in range(HEADS // 2):
            ys = yb[:, pr * 128:(pr + 1) * 128]
            bd = jnp.concatenate([jnp.where(lt_half, ys, zero), jnp.where(lt_half, zero, ys)], axis=0)
            outs.append(_dot(xb[:, pr * 128:(pr + 1) * 128], bd))
        return jnp.concatenate(outs, axis=1)

    def pre_body(bi, carry):
        groups = []
        for i in range(cb):
            c = bi * cb + i
            r0 = pl.multiple_of(c * CHUNK, CHUNK)
            rows = pl.ds(r0, CHUNK)
            qb = qkv_ref[rows, 0:QK]
            kb16 = qkv_ref[rows, QK:2 * QK]
            qf = qb.astype(F32)
            kf = kb16.astype(F32)
            vf = qkv_ref[rows, 2 * QK:].astype(F32)
            kt = jnp.concatenate(
                [jnp.concatenate([kf[:, 0:128], kf[:, 128:256]], axis=0).T,
                 jnp.concatenate([kf[:, 256:384], kf[:, 384:512]], axis=0).T], axis=1)
            ktbd = _group_rows(kt.astype(BF16), HEADS)
            qkk = _dot(jnp.concatenate([qb, kb16], axis=0), ktbd)
            cols = gc_ref[rows, :]
            bcols = beta_ref[rows, :]
            at = gct_ref[c]
            for d in range(2):
                j0 = d * HEADS
                bg = [jnp.broadcast_to(cols[:, j0 + h:j0 + h + 1], (CHUNK, 128)) for h in range(HEADS)]
                bb = [jnp.broadcast_to(bcols[:, j0 + h:j0 + h + 1], (CHUNK, 128)) for h in range(HEADS)]
                gcw = jnp.concatenate([_pair(bg[0], bg[1], lt_half), _pair(bg[2], bg[3], lt_half)], 1)
                bw = jnp.concatenate([_pair(bb[0], bb[1], lt_half), _pair(bb[2], bb[3], lt_half)], 1)
                gr = jnp.concatenate(
                    [_pair(at[j0:j0 + 1], at[j0 + 1:j0 + 2], lt_half_row),
                     _pair(at[j0 + 2:j0 + 3], at[j0 + 3:j0 + 4], lt_half_row)], axis=1)
                decay = jnp.exp(jnp.where(incl[d], gcw - gr, NEG_BIG))
                m = bw * qkk[CHUNK:] * decay
                qkd_ref[d, rows, :] = (qkk[0:CHUNK] * decay).astype(BF16)
                en = jnp.concatenate([jnp.exp(x) for x in bg], axis=1)
                bn = jnp.concatenate(bb, axis=1)
                qd_ref[d, rows, :] = (qf * en).astype(BF16)
                vb = (vf * bn).astype(BF16)
                kbe = (kf * (bn * en)).astype(BF16)
                e0 = CHUNK - 1 if d == 0 else 0
                bt = [jnp.broadcast_to(at[j0 + h:j0 + h + 1, e0:e0 + 1], (1, 128)) for h in range(HEADS)]
                tw = jnp.concatenate([_pair(bt[0], bt[1], lt_half_row),
                                      _pair(bt[2], bt[3], lt_half_row)], axis=1)
                kdt_ref[d, pl.ds(pl.multiple_of(c * HEAD_DIM, HEAD_DIM), HEAD_DIM), :] = (
                    kt * jnp.exp(tw - gr)).astype(BF16)
                gt_ref[d, c] = jnp.broadcast_to(
                    jnp.concatenate([jnp.exp(x) for x in bt], axis=1), (8, A_WIDTH))
                groups.append((d, rows, m, vb, kbe))
        ts = [eye_ref[...] - g[2] * lvl_ref[g[0] * N_LEVELS] for g in groups]
        for lv in range(1, N_LEVELS):
            xs = [head_mm(t, g[2] * lvl_ref[g[0] * N_LEVELS + lv]) for t, g in zip(ts, groups)]
            ts = [t - head_mm(x, t) for t, x in zip(ts, xs)]
        for t, (d, rows, _, vb, kbe) in zip(ts, groups):
            tb = t.astype(BF16)
            uw = []
            for pr in range(HEADS // 2):
                h0, h1 = hs[2 * pr], hs[2 * pr + 1]
                rhs = jnp.concatenate(
                    [jnp.concatenate([vb[:, h0], zero_blk, kbe[:, h0], zero_blk], axis=1),
                     jnp.concatenate([zero_blk, vb[:, h1], zero_blk, kbe[:, h1]], axis=1)], axis=0)
                uw.append(_dot(tb[:, pr * 128:(pr + 1) * 128], rhs))
            half = A_WIDTH // 2
            u_ref[d, rows, :] = jnp.concatenate([uw[0][:, 0:half], uw[1][:, 0:half]], axis=1)
            w_ref[d, rows, :] = jnp.concatenate([uw[0][:, half:], uw[1][:, half:]],
                                                axis=1).astype(BF16)
        return carry

    lax.fori_loop(0, n_total // cb, pre_body, 0)

    o_ref[...] = jnp.zeros(o_ref.shape, F32)
    zero_s = jnp.zeros((HEAD_DIM, HEAD_DIM), BF16)

    def scan_step(sub0, c):
        groups = [(p, d) for p in range(n_par) for d in range(2)]
        cidx = [(sub0 + p) * n_chunks + (c if d == 0 else n_chunks - 1 - c) for p, d in groups]
        rows = [pl.ds(pl.multiple_of(ci * CHUNK, CHUNK), CHUNK) for ci in cidx]
        ss = [s_ref[p, d] for p, d in groups]
        wqs = []
        for (p, d), r, s in zip(groups, rows, ss):
            sb = s.astype(BF16)
            lhs = jnp.concatenate([w_ref[d, r, :], qd_ref[d, r, :]], axis=0)
            halves = []
            for hp in range(HEADS // 2):
                bd = _block_diag([sb[:, hs[2 * hp]], sb[:, hs[2 * hp + 1]]], zero_s)
                halves.append(_dot(lhs[:, hp * 2 * HEAD_DIM:(hp + 1) * 2 * HEAD_DIM], bd))
            wqs.append(jnp.concatenate(halves, axis=1))
        boths = []
        for (p, d), r, ci, wq in zip(groups, rows, cidx, wqs):
            v_new = (u_ref[d, r, :] - wq[0:CHUNK]).astype(BF16)
            vbd = _block_diag([v_new[:, s] for s in hs], zero_blk)
            kd = kdt_ref[d, pl.ds(pl.multiple_of(ci * HEAD_DIM, HEAD_DIM), HEAD_DIM), :]
            lhs2 = jnp.concatenate([qkd_ref[d, r, :], kd], axis=0)
            boths.append(_dot(lhs2, vbd))
        for (p, d), r, ci, s, wq, both in zip(groups, rows, cidx, ss, wqs, boths):
            o_ref[r, :] += wq[CHUNK:] + both[0:CHUNK]
            s_ref[p, d] = s * gt_ref[d, ci][0:1] + both[CHUNK:]

    def sub_body(sb_i, carry):
        sub0 = sb_i * n_par
        for p in range(n_par):
            for d in range(2):
                if has_init:
                    s_ref[p, d] = jnp.concatenate(
                        [s0_ref[d * HEADS + h] for h in range(HEADS)], axis=1)
                else:
                    s_ref[p, d] = jnp.zeros((HEAD_DIM, A_WIDTH), F32)

        def step(c, carry2):
            scan_step(sub0, c)
            return carry2

        lax.fori_loop(0, n_chunks, step, 0)
        if st_ref is not None:
            for p in range(n_par):
                for d in range(2):
                    s = s_ref[p, d]
                    for h in range(HEADS):
                        st_ref[sub0 + p, d * HEADS + h] = s[:, hs[h]]
        return carry

    lax.fori_loop(0, n_sub // n_par, sub_body, 0)


def _delta_stage(qkv, ba, alog_row, dtb_row, s0, consts, *, seq_len):
    n = qkv.shape[0]
    block = max(seq_len, DELTA_BLOCK)
    n_blocks = n // block
    n_sub = block // seq_len
    n_chunks = seq_len // CHUNK
    n_total = block // CHUNK
    has_init = s0 is not None
    n_par = min(n_sub, 4)
    lvl, eye, tri = consts
    in_specs = [pl.BlockSpec((block, QKV), lambda i: (i, 0)),
                pl.BlockSpec((block, 128), lambda i: (i, 0)),
                _const_spec((1, 128)), _const_spec((1, 128)),
                _const_spec(lvl.shape), _const_spec(eye.shape), _const_spec(tri.shape)]
    args = [qkv, ba, alog_row, dtb_row, lvl, eye, tri]
    o_spec = pl.BlockSpec((block, A_WIDTH), lambda i: (i, 0))
    o_shape = jax.ShapeDtypeStruct((n, A_WIDTH), F32)
    st_shape = (2 * HEADS, HEAD_DIM, HEAD_DIM)
    if has_init:
        assert n_sub == 1
        in_specs.append(pl.BlockSpec((None,) + st_shape, lambda i: (i, 0, 0, 0)))
        args.append(s0)
        out_specs, out_shape = o_spec, o_shape
    else:
        out_specs = [o_spec, pl.BlockSpec((n_sub,) + st_shape, lambda i: (i, 0, 0, 0))]
        out_shape = [o_shape, jax.ShapeDtypeStruct((n // seq_len,) + st_shape, F32)]
    scratch = [pltpu.VMEM((n_par, 2, HEAD_DIM, A_WIDTH), F32),
               pltpu.VMEM((block, 128), F32),
               pltpu.VMEM((block, 128), F32),
               pltpu.VMEM((n_total, 8, 128), F32),
               pltpu.VMEM((2, block, A_WIDTH), F32),
               pltpu.VMEM((2, block, A_WIDTH), BF16),
               pltpu.VMEM((2, block, A_WIDTH), BF16),
               pltpu.VMEM((2, block, WIDE), BF16),
               pltpu.VMEM((2, n_total * HEAD_DIM, WIDE), BF16),
               pltpu.VMEM((2, n_total, 8, A_WIDTH), F32)]
    return pl.pallas_call(
        functools.partial(_delta_kernel, n_sub=n_sub, n_chunks=n_chunks, has_init=has_init,
                          n_par=n_par, cb=4),
        grid=(n_blocks,),
        in_specs=in_specs,
        out_specs=out_specs,
        out_shape=out_shape,
        scratch_shapes=scratch,
        compiler_params=_params(1),
        name="delta_stage",
    )(*args)


def _group_rows(y, n_groups):
    half = y.shape[1] // 2
    gw = half // 2
    lane = lax.broadcasted_iota(jnp.int32, (y.shape[0], half), 1)
    zero = jnp.zeros((y.shape[0], half), y.dtype)
    blocks = []
    for g in range(n_groups):
        part = y[:, (g // 2) * half:(g // 2 + 1) * half]
        keep = (lane < gw) if g % 2 == 0 else (lane >= gw)
        part = jnp.where(keep, part, jnp.zeros_like(part))
        blocks.append(jnp.concatenate([part, zero] if g < 2 else [zero, part], axis=1))
    return jnp.concatenate(blocks, axis=0)


def _mix_delta(o_ref, z_ref, dg_ref, mix_ref):
    dg = dg_ref[...]
    for h in range(HEADS):
        cs = slice(h * HEAD_DIM, (h + 1) * HEAD_DIM)
        y = _rms_rows(o_ref[:, cs]) * dg * _silu(z_ref[:, cs])
        mix_ref[:, cs] = y.astype(BF16)


def _mix_sgu(uv_ref, sg_ref, ws_ref, bs_ref, seg_ref, mix_ref, tile):
    u = _gelu_tanh(uv_ref[:, 0:B_WIDTH])
    v = _gelu_tanh(uv_ref[:, B_WIDTH:])
    vv = v * v
    vv_hi = vv.astype(BF16)
    vv_lo = (vv - vv_hi.astype(F32)).astype(BF16)
    ms = _dot(jnp.concatenate([vv_hi, vv_lo], axis=1), seg_ref[...])
    vn = (v * lax.rsqrt(ms + EPS) * sg_ref[...]).astype(BF16)
    for c in range(tile // SGU_CHUNK):
        rs = slice(c * SGU_CHUNK, (c + 1) * SGU_CHUNK)
        s = _dot(ws_ref[...], _group_rows(vn[rs], B_GROUPS)) + bs_ref[...]
        mix_ref[rs, A_WIDTH:A_WIDTH + B_WIDTH] = (u[rs] * s).astype(BF16)


def _mix_pool(p_ref, band_ref, icnt_ref, wp_ref, ps_ref, mix_ref, tile):
    for c in range(tile // POOL_TILE):
        rs = slice(c * POOL_TILE, (c + 1) * POOL_TILE)
        x = p_ref[rs, :]
        x_hi = x.astype(BF16)
        x_lo = (x - x_hi.astype(F32)).astype(BF16)
        n_win = len(POOL_WINDOWS)
        rhs = jnp.concatenate([_group_rows(x_hi, n_win), _group_rows(x_lo, n_win)], axis=0)
        wsum = _dot(band_ref[...], rhs)
        diff = wsum * icnt_ref[...] - x
        y = _bdot(diff, wp_ref[...]) * ps_ref[...]
        mix_ref[rs, A_WIDTH + B_WIDTH:] = y.astype(BF16)


def _out_kernel(x_ref, o_ref, z_ref, uv_ref, p_ref, dg_ref, sg_ref, ws_ref, bs_ref, seg_ref,
                band_ref, icnt_ref, wp_ref, ps_ref, mod_ref, g2_ref, wo_ref, wgu_ref, wd_ref,
                nf_ref, y_ref, mix_ref, *, seq_len, tile, row0, final_norm):
    i = pl.program_id(0)
    tiles_per_seq = max(seq_len // tile, 1)
    row = row0 + i // tiles_per_seq if row0 else 0
    m = mod_ref[pl.ds(row, 1), :]
    gate1 = m[:, 2 * D_MODEL:3 * D_MODEL]
    shift2 = m[:, 3 * D_MODEL:4 * D_MODEL]
    scale2 = m[:, 4 * D_MODEL:5 * D_MODEL]
    gate2 = m[:, 5 * D_MODEL:6 * D_MODEL]
    _mix_delta(o_ref, z_ref, dg_ref, mix_ref)
    _mix_sgu(uv_ref, sg_ref, ws_ref, bs_ref, seg_ref, mix_ref, tile)
    _mix_pool(p_ref, band_ref, icnt_ref, wp_ref, ps_ref, mix_ref, tile)
    x1 = x_ref[...] + gate1 * _dot(mix_ref[...], wo_ref[...])
    hb = _modulated_norm(x1, g2_ref[...], shift2, scale2).astype(BF16)
    fc = FF_HIDDEN // FF_SPLIT
    ff = jnp.zeros((tile, D_MODEL), F32)
    for c in range(FF_SPLIT):
        gate = _dot(hb, wgu_ref[:, c * fc:(c + 1) * fc])
        up = _dot(hb, wgu_ref[:, FF_HIDDEN + c * fc:FF_HIDDEN + (c + 1) * fc])
        act = (_silu(gate) * up).astype(BF16)
        ff = ff + _dot(act, wd_ref[c * fc:(c + 1) * fc, :])
    x2 = x1 + gate2 * ff
    if final_norm:
        x2 = _rms_rows(x2) * nf_ref[...]
    y_ref[...] = x2


def _output_stage(x, o, z, uv, p, lw, pool_consts, mods_l, norm2_g, dense, layer, norm_f, *,
                  seq_len, row0, final_norm):
    n = x.shape[0]
    tile = 512
    band, icnt = pool_consts
    row = lambda c: pl.BlockSpec((tile, c), lambda i: (i, 0))
    small = [lw["delta_g"], lw["sgu_g"], lw["w_spatial"], lw["b_spatial"], lw["seg"], band, icnt,
             lw["w_pool"], lw["pool_scale"], mods_l, norm2_g]
    stacked = [dense["out"], dense["gu"], dense["down"]]
    consts = small + stacked + [norm_f]
    in_specs = ([row(D_MODEL), row(A_WIDTH), row(A_WIDTH), row(2 * B_WIDTH), row(C_WIDTH)]
                + [_const_spec(a.shape) for a in small]
                + [_layer_spec(a.shape, layer) for a in stacked] + [_const_spec(norm_f.shape)])
    return pl.pallas_call(
        functools.partial(_out_kernel, seq_len=seq_len, tile=tile, row0=row0,
                          final_norm=final_norm),
        grid=(n // tile,),
        in_specs=in_specs,
        out_specs=row(D_MODEL),
        out_shape=jax.ShapeDtypeStruct((n, D_MODEL), F32),
        scratch_shapes=[pltpu.VMEM((tile, D_MODEL), BF16)],
        compiler_params=_params(1),
        name="output_stage",
    )(x, o, z, uv, p, *consts)


def _pool_constants(seg_len):
    pos = np.arange(POOL_TILE)
    seg = pos // seg_len
    band = np.zeros((len(POOL_WINDOWS), POOL_TILE, POOL_TILE), np.float32)
    icnt = np.zeros((POOL_TILE, C_WIDTH), np.float32)
    for g, win in enumerate(POOL_WINDOWS):
        lo = pos - win // 2
        hi = pos + win - win // 2
        inside = (pos[None, :] >= lo[:, None]) & (pos[None, :] < hi[:, None]) \
            & (seg[None, :] == seg[:, None])
        band[g] = inside
        icnt[:, g * C_GC:(g + 1) * C_GC] = (1.0 / inside.sum(axis=1))[:, None]
    band_cat = np.concatenate([band[g] for g in range(len(POOL_WINDOWS))] * 2, axis=1)
    return jnp.asarray(band_cat, BF16), jnp.asarray(icnt)


def _grid_pos_tables(rows, d):
    quarter = d // 4
    f = np.float32
    omega = (f(1.0) / (f(10000.0) ** (np.arange(quarter, dtype=f) / f(quarter)))).astype(f)
    r = np.arange(rows, dtype=f)[:, None] * omega
    cl = np.arange(GRID_W, dtype=f)[:, None] * omega
    row_emb = np.concatenate([np.sin(r), np.cos(r)], axis=-1).astype(f)
    col_emb = np.concatenate([np.sin(cl), np.cos(cl)], axis=-1).astype(f)
    return jnp.asarray(row_emb), jnp.asarray(col_emb)


def _pad_lanes(a, width):
    return jnp.pad(a, ((0, 0), (0, width - a.shape[1])))


def _dense_weights(w_in, w_out, w_gu, w_down):
    c0 = 2 * QK + 2 * A_WIDTH
    c1 = c0 + 4 * HEADS
    pad = jnp.zeros(w_in.shape[:2] + (IN_PAD - IN_BA - 4 * HEADS,), w_in.dtype)
    w_in_b = jnp.concatenate([w_in[..., :c0], w_in[..., c1:], w_in[..., c0:c1], pad],
                             axis=-1).astype(BF16)
    return {"in": w_in_b, "out": w_out.astype(BF16), "gu": w_gu.astype(BF16),
            "down": w_down.astype(BF16)}


def _layer_weights(l, w_pool, w_spatial, b_spatial):
    wp = jnp.zeros((C_WIDTH, C_WIDTH), F32)
    for g in range(len(POOL_WINDOWS)):
        wp = wp.at[g * C_GC:(g + 1) * C_GC, g * C_GC:(g + 1) * C_GC].set(w_pool[l, g])
    grp = np.arange(B_WIDTH) // B_GC
    seg1 = (grp[:, None] == grp[None, :]).astype(np.float32) / B_GC
    seg = jnp.asarray(np.concatenate([seg1, seg1], axis=0), BF16)
    bs = jnp.repeat(b_spatial[l].T, B_GC, axis=1)
    ws_cat = jnp.concatenate([w_spatial[l, g] for g in range(B_GROUPS)], axis=1)
    return wp.astype(BF16), seg, ws_cat.astype(BF16), bs


def kernel(x_prompt, x_sample, state_delta, c, c_ctx, w_in, conv_w, a_log, dt_bias, delta_norm_g,
           sgu_norm_g, w_spatial, b_spatial, w_pool, pool_scale, w_out, norm1_g, norm2_g, w_mod,
           b_mod, w_gu, w_down, norm_f):
    batch, seq, d = x_prompt.shape
    dec_batch, dec_seq, _ = x_sample.shape
    cvec = jnp.concatenate([c_ctx[None, :], c, jnp.zeros((8 - 1 - dec_batch, d), F32)], axis=0)
    mods = _modulation(cvec, w_mod, b_mod)
    row_emb, col_emb = _grid_pos_tables(dec_seq // GRID_W, d)
    xs = _add_pos(x_sample, row_emb, col_emb).reshape(dec_batch * dec_seq, d)
    xc = x_prompt.reshape(batch * seq, d)
    pool_ctx = _pool_constants(min(seq, POOL_TILE))
    pool_lat = _pool_constants(GRID_W)
    nf = norm_f.reshape(1, d)
    delta_consts = _delta_constants()
    dense = _dense_weights(w_in, w_out, w_gu, w_down)
    ctx_states = []
    for l in range(DEPTH):
        wp, seg, ws, bs = _layer_weights(l, w_pool, w_spatial, b_spatial)
        lw = {"delta_g": delta_norm_g[l].reshape(1, HEAD_DIM),
              "sgu_g": sgu_norm_g[l].reshape(1, B_WIDTH), "w_spatial": ws, "b_spatial": bs,
              "seg": seg, "w_pool": wp, "pool_scale": pool_scale[l].reshape(1, C_WIDTH)}
        alog_row = _pad_lanes(a_log[l].reshape(1, 2 * HEADS), 128)
        dtb_row = _pad_lanes(dt_bias[l].reshape(1, 2 * HEADS), 128)
        n1 = norm1_g[l].reshape(1, d)
        n2 = norm2_g[l].reshape(1, d)
        last = l == DEPTH - 1
        s0_lat = state_delta[:, l].reshape(dec_batch, 2 * HEADS, HEAD_DIM, HEAD_DIM)
        streams = (("ctx", xc, seq, 0, None, pool_ctx), ("lat", xs, dec_seq, 1, s0_lat, pool_lat))
        outs = []
        for name, x, seq_len, row0, s0, pool_consts in streams:
            qkv, z, ba, uv, p = _input_stage(x, mods[l], n1, dense["in"], l, conv_w[l],
                                             seq_len=seq_len, row0=row0)
            res = _delta_stage(qkv, ba, alog_row, dtb_row, s0, delta_consts, seq_len=seq_len)
            if s0 is None:
                o, st = res
                ctx_states.append(st.reshape(batch, 2, HEADS, HEAD_DIM, HEAD_DIM))
            else:
                o = res
            outs.append(_output_stage(x, o, z, uv, p, lw, pool_consts, mods[l], n2, dense, l, nf,
                                      seq_len=seq_len, row0=row0, final_norm=last))
        xc, xs = outs
    y_prompt = xc.reshape(batch, seq, d)
    y_sample = xs.reshape(dec_batch, dec_seq, d)
    new_state = jnp.stack(ctx_states, axis=1)
    return (y_prompt, y_sample, new_state)
```

```python
import functools
import math

import numpy as np
import jax
import jax.numpy as jnp
from jax import lax
from jax.experimental import pallas as pl
from jax.experimental.pallas import tpu as pltpu

F32 = jnp.float32
BF16 = jnp.bfloat16
HIGHEST = lax.Precision.HIGHEST

D_MODEL = 1024
DEPTH = 2
GRID_W = 64
HEADS = 4
HEAD_DIM = 128
QK = HEADS * HEAD_DIM
A_WIDTH = HEADS * HEAD_DIM
QKV = 2 * QK + A_WIDTH
CHUNK = 64
B_WIDTH = 256
B_GROUPS = 4
B_GC = 64
SGU_CHUNK = 128
C_WIDTH = 256
POOL_WINDOWS = (2, 4, 8, 16)
C_GC = 64
FF_HIDDEN = 2816
FF_SPLIT = 2
N_MOD = 6
EPS = 1e-6
IN_COLS = 2 * QK + 2 * A_WIDTH + 4 * HEADS + 2 * B_WIDTH + C_WIDTH
IN_Z = QKV
IN_UV = IN_Z + A_WIDTH
IN_P = IN_UV + 2 * B_WIDTH
IN_BA = IN_P + C_WIDTH
IN_PAD = IN_BA + 128
NEG_BIG = -1e30

DELTA_BLOCK = 1024
POOL_TILE = 256
VMEM_LIMIT = 56 * 1024 * 1024


def _dot(a, b, precision=None):
    return jnp.dot(a, b, preferred_element_type=F32, precision=precision)


def _bdot(a, b):
    return jnp.dot(a.astype(BF16), b.astype(BF16), preferred_element_type=F32)


def _sigmoid(x):
    return 1.0 / (1.0 + jnp.exp(-x))


def _silu(x):
    return x * _sigmoid(x)


def _softplus(x):
    return jnp.maximum(x, 0.0) + jnp.log1p(jnp.exp(-jnp.abs(x)))


def _gelu_tanh(x):
    c = math.sqrt(2.0 / math.pi)
    return 0.5 * x * (1.0 + jnp.tanh(c * (x + 0.044715 * (x * x * x))))


def _rms_rows(x):
    return x * lax.rsqrt(jnp.mean(x * x, axis=-1, keepdims=True) + EPS)


def _params(n_grid):
    return pltpu.CompilerParams(dimension_semantics=("arbitrary",) * n_grid,
                                vmem_limit_bytes=VMEM_LIMIT)


def _const_spec(shape):
    nd = len(shape)
    return pl.BlockSpec(shape, lambda *_: (0,) * nd, pipeline_mode=pl.Buffered(1))


def _layer_spec(shape, layer):
    nd = len(shape) - 1
    return pl.BlockSpec((None,) + tuple(shape[1:]), lambda *_: (layer,) + (0,) * nd,
                        pipeline_mode=pl.Buffered(1))


def _mod_kernel(c_ref, w_ref, b_ref, o_ref):
    a = _silu(c_ref[...])
    o_ref[0] = _bdot(a, w_ref[0]) + b_ref[0]


def _modulation(cvec, w_mod, b_mod):
    tn = 1536
    n_out = N_MOD * D_MODEL
    return pl.pallas_call(
        _mod_kernel,
        grid=(DEPTH, n_out // tn),
        in_specs=[pl.BlockSpec((8, D_MODEL), lambda l, j: (0, 0)),
                  pl.BlockSpec((1, D_MODEL, tn), lambda l, j: (l, 0, j)),
                  pl.BlockSpec((1, 1, tn), lambda l, j: (l, 0, j))],
        out_specs=pl.BlockSpec((1, 8, tn), lambda l, j: (l, 0, j)),
        out_shape=jax.ShapeDtypeStruct((DEPTH, 8, n_out), F32),
        compiler_params=_params(2),
        name="modulation",
    )(cvec, w_mod, b_mod.reshape(DEPTH, 1, n_out))


POS_ROWS = 8


def _pos_kernel(x_ref, r_ref, c_ref, o_ref):
    half = D_MODEL // 2
    col = c_ref[...]
    for r in range(POS_ROWS):
        rs = slice(r * GRID_W, (r + 1) * GRID_W)
        o_ref[rs, 0:half] = x_ref[rs, 0:half] + r_ref[r:r + 1, :]
        o_ref[rs, half:] = x_ref[rs, half:] + col


def _add_pos(x, row_emb, col_emb):
    b, n, d = x.shape
    tm = POS_ROWS * GRID_W
    return pl.pallas_call(
        _pos_kernel,
        grid=(b, n // tm),
        in_specs=[pl.BlockSpec((None, tm, d), lambda i, j: (i, j, 0)),
                  pl.BlockSpec((POS_ROWS, d // 2), lambda i, j: (j, 0)),
                  pl.BlockSpec((GRID_W, d // 2), lambda i, j: (0, 0))],
        out_specs=pl.BlockSpec((None, tm, d), lambda i, j: (i, j, 0)),
        out_shape=jax.ShapeDtypeStruct(x.shape, F32),
        compiler_params=_params(2),
        name="add_pos",
    )(x, row_emb, col_emb)


def _modulated_norm(x, gain, shift, scale):
    return _rms_rows(x) * gain * (1.0 + scale) + shift


def _pack_w_in(wf_ref, wb_ref):
    n_logit = 4 * HEADS
    rows = 256
    lane = lax.broadcasted_iota(jnp.int32, (rows, 128), 1)
    for r in range(0, D_MODEL, rows):
        rs = slice(r, r + rows)
        wb_ref[rs, 0:IN_UV] = wf_ref[rs, 0:IN_UV].astype(BF16)
        tail = wf_ref[rs, IN_UV:IN_COLS]
        wb_ref[rs, IN_UV:IN_BA] = tail[:, n_logit:].astype(BF16)
        wb_ref[rs, IN_BA:IN_PAD] = jnp.where(lane < n_logit, tail[:, 0:128], 0.0).astype(BF16)


def _in_kernel(*refs, seq_len, tile, row0, halo):
    if halo:
        (x_ref, xp_ref, xn_ref, mod_ref, g_ref, wf_ref,
         cw_ref, qkv_ref, z_ref, ba_ref, uv_ref, p_ref, win_ref) = refs
    else:
        (x_ref, mod_ref, g_ref, wf_ref,
         cw_ref, qkv_ref, z_ref, ba_ref, uv_ref, p_ref, win_ref) = refs
    i = pl.program_id(0)

    @pl.when(i == 0)
    def _prepare_weights():
        _pack_w_in(wf_ref, win_ref)

    tiles_per_seq = max(seq_len // tile, 1)
    row = row0 + i // tiles_per_seq if row0 else 0
    m = mod_ref[pl.ds(row, 1), :]
    shift = m[:, 0:D_MODEL]
    scale = m[:, D_MODEL:2 * D_MODEL]
    gain = g_ref[...]
    hf = _modulated_norm(x_ref[...], gain, shift, scale)
    hb = hf.astype(BF16)
    if halo:
        hp = _modulated_norm(xp_ref[...], gain, shift, scale)
        hn = _modulated_norm(xn_ref[...], gain, shift, scale)
        hq = jnp.concatenate([hf, hp, hn], axis=0).astype(BF16)
    else:
        hq = hb
    ridx = lax.broadcasted_iota(jnp.int32, (tile, 1), 0)
    if halo:
        pos_in_seq = (i % tiles_per_seq) * tile + ridx
    else:
        pos_in_seq = ridx % seq_len
    first_row = ridx == 0
    last_row = ridx == tile - 1
    seq_start = pos_in_seq == 0
    seq_end = pos_in_seq == seq_len - 1
    cw = cw_ref[...]
    blk = 2 * HEAD_DIM

    def project(b):
        return _dot(hq, win_ref[:, b * blk:(b + 1) * blk])

    def conv_act(b, pre):
        cols = slice(b * blk, (b + 1) * blk)
        cur = pre[0:tile]
        prev = pltpu.roll(cur, 1, 0)
        nxt = pltpu.roll(cur, tile - 1, 0)
        if halo:
            prev = jnp.where(first_row, pre[tile + 7:tile + 8], prev)
            nxt = jnp.where(last_row, pre[tile + 8:tile + 9], nxt)
        prev = jnp.where(seq_start, 0.0, prev)
        nxt = jnp.where(seq_end, 0.0, nxt)
        act = _silu(prev * cw[0:1, cols] + cur * cw[1:2, cols] + nxt * cw[2:3, cols])
        if b * blk >= 2 * QK:
            qkv_ref[:, cols] = act.astype(BF16)
            return
        scale = HEAD_DIM ** -0.5 if b * blk < QK else 1.0
        for h in range(blk // HEAD_DIM):
            a = act[:, h * HEAD_DIM:(h + 1) * HEAD_DIM]
            inv = lax.rsqrt(jnp.sum(a * a, axis=-1, keepdims=True) + EPS) * scale
            c0 = b * blk + h * HEAD_DIM
            qkv_ref[:, c0:c0 + HEAD_DIM] = (a * inv).astype(BF16)

    others = []
    for out_ref, c0 in ((z_ref, IN_Z), (uv_ref, IN_UV), (p_ref, IN_P), (ba_ref, IN_BA)):
        width = out_ref.shape[1]
        others += [(out_ref, c0, slice(c, min(c + blk, width))) for c in range(0, width, blk)]
    n_blk = QKV // blk
    assert len(others) == n_blk
    pre = project(0)
    for b in range(n_blk):
        nxt_pre = project(b + 1) if b + 1 < n_blk else None
        out_ref, c0, cols = others[b]
        out_ref[:, cols] = _dot(hb, win_ref[:, c0 + cols.start:c0 + cols.stop])
        conv_act(b, pre)
        pre = nxt_pre


def _input_stage(x, mods_l, norm_g, w_in, layer, conv_w, *, seq_len, row0):
    n = x.shape[0]
    tile = 512
    halo = seq_len > tile
    tiles_per_seq = max(seq_len // tile, 1)
    n_tiles = n // tile
    h8 = tile // 8
    last8 = n // 8 - 1
    in_specs = [pl.BlockSpec((tile, D_MODEL), lambda i: (i, 0))]
    args = [x]
    if halo:
        in_specs += [pl.BlockSpec((8, D_MODEL), lambda i: (jnp.maximum(i * h8 - 1, 0), 0)),
                     pl.BlockSpec((8, D_MODEL), lambda i: (jnp.minimum((i + 1) * h8, last8), 0))]
        args += [x, x]
    in_specs += [_const_spec((8, N_MOD * D_MODEL)), _const_spec((1, D_MODEL)),
                 _layer_spec(w_in.shape, layer), _const_spec((3, QKV))]
    args += [mods_l, norm_g, w_in, conv_w]
    widths = (QKV, A_WIDTH, 128, 2 * B_WIDTH, C_WIDTH)
    dtypes = (BF16, F32, F32, F32, F32)
    return pl.pallas_call(
        functools.partial(_in_kernel, seq_len=seq_len, tile=tile, row0=row0, halo=halo),
        grid=(n_tiles,),
        in_specs=in_specs,
        out_specs=[pl.BlockSpec((tile, c), lambda i: (i, 0)) for c in widths],
        out_shape=[jax.ShapeDtypeStruct((n, c), t) for c, t in zip(widths, dtypes)],
        scratch_shapes=[pltpu.VMEM((D_MODEL, IN_PAD), BF16)],
        compiler_params=_params(1),
        name="input_stage",
    )(*args)


N_LEVELS = int(math.log2(CHUNK))
WIDE = HEADS * CHUNK


def _delta_constants():
    r = np.arange(CHUNK)[:, None]
    c = (np.arange(WIDE) % CHUNK)[None, :]
    lvl = np.zeros((2, N_LEVELS, CHUNK, WIDE), np.float32)
    for d in range(2):
        rr, cc = (r, c) if d == 0 else (c, r)
        for i in range(N_LEVELS):
            s = 1 << i
            lvl[d, i] = (rr // (2 * s) == cc // (2 * s)) & ((rr // s) % 2 == 1) & ((cc // s) % 2 == 0)
    eye = (r == c).astype(np.float32)
    tri = np.concatenate([np.tril(np.ones((CHUNK, CHUNK))), np.triu(np.ones((CHUNK, CHUNK)))], 0)
    return (jnp.asarray(lvl.reshape(2 * N_LEVELS, CHUNK, WIDE)), jnp.asarray(eye),
            jnp.asarray(tri, BF16))


def _pair(a, b, lt_half):
    return jnp.where(lt_half, a, b)


def _block_diag(blocks, zero):
    n = len(blocks)
    rows = [jnp.concatenate([blocks[i] if j == i else zero for j in range(n)], axis=1)
            for i in range(n)]
    return jnp.concatenate(rows, axis=0)


def _delta_kernel(*refs, n_sub, n_chunks, has_init, n_par, cb, layer):
    it = iter(refs)
    qkv_ref, ba_ref, alog_ref, dtb_ref, lvl_ref, eye_ref, tri_ref = (next(it) for _ in range(7))
    s0_ref = next(it) if has_init else None
    if not has_init and layer > 0:
        next(it)
    o_ref = next(it)
    st_ref = None if has_init else next(it)
    (s_ref, gc_ref, beta_ref, gct_ref, u_ref, w_ref, qd_ref, qkd_ref, kdt_ref, gt_ref) = it
    n_total = n_sub * n_chunks
    lane = lax.broadcasted_iota(jnp.int32, (CHUNK, 128), 1)
    lt_half = lane < CHUNK
    lt_half_row = lt_half[0:1]
    rw = lax.broadcasted_iota(jnp.int32, (CHUNK, WIDE), 0)
    cw = lax.broadcasted_iota(jnp.int32, (CHUNK, WIDE), 1) & (CHUNK - 1)
    incl = (rw >= cw, rw <= cw)
    a_row = -jnp.exp(alog_ref[...])
    dtb_row = dtb_ref[...]

    def gate_body(bi, carry):
        span = cb * CHUNK
        r0 = pl.multiple_of(bi * span, span)
        ba = ba_ref[pl.ds(r0, span), :]
        beta_ref[pl.ds(r0, span), :] = _sigmoid(ba)
        al = pltpu.roll(ba, 128 - 2 * HEADS, 1)
        lane_b = lax.broadcasted_iota(jnp.int32, (span, 128), 1)
        g = jnp.where(lane_b < 2 * HEADS, a_row * _softplus(al + dtb_row), 0.0)
        g_hi = g.astype(BF16).astype(F32)
        r1 = g - g_hi
        g_mid = r1.astype(BF16).astype(F32)
        g_lo = (r1 - g_mid).astype(BF16).astype(F32)
        g3 = (g_hi + pltpu.roll(g_mid, 8, 1) + pltpu.roll(g_lo, 16, 1)).astype(BF16)
        css = [_dot(tri_ref[...], g3[i * CHUNK:(i + 1) * CHUNK]) for i in range(cb)]
        css = [cs + pltpu.roll(cs, 128 - 8, 1) + pltpu.roll(cs, 128 - 16, 1) for cs in css]
        gcs = [jnp.where(lane < HEADS, cs[0:CHUNK], cs[CHUNK:]) for cs in css]
        ats = [jnp.concatenate([gc, gc], axis=0).T[0:8, :] for gc in gcs]
        for i in range(cb):
            gc_ref[pl.ds(pl.multiple_of(r0 + i * CHUNK, CHUNK), CHUNK), :] = gcs[i]
            gct_ref[bi * cb + i] = ats[i]
        return carry

    lax.fori_loop(0, n_total // cb, gate_body, 0)

    zero_blk = jnp.zeros((CHUNK, HEAD_DIM), BF16)
    hs = [slice(h * HEAD_DIM, (h + 1) * HEAD_DIM) for h in range(HEADS)]

    def head_mm(x, y):
        xb = x.astype(BF16)
        yb = y.astype(BF16)
        zero = jnp.zeros((CHUNK, 128), BF16)
        outs = []
        for pr in range(HEADS // 2):
            ys = yb[:, pr * 128:(pr + 1) * 128]
            bd = jnp.concatenate([jnp.where(lt_half, ys, zero), jnp.where(lt_half, zero, ys)], axis=0)
            outs.append(_dot(xb[:, pr * 128:(pr + 1) * 128], bd))
        return jnp.concatenate(outs, axis=1)

    def pre_body(bi, carry):
        groups = []
        for i in range(cb):
            c = bi * cb + i
            r0 = pl.multiple_of(c * CHUNK, CHUNK)
            rows = pl.ds(r0, CHUNK)
            qb = qkv_ref[rows, 0:QK]
            kb16 = qkv_ref[rows, QK:2 * QK]
            qf = qb.astype(F32)
            kf = kb16.astype(F32)
            vf = qkv_ref[rows, 2 * QK:].astype(F32)
            kt = jnp.concatenate(
                [jnp.concatenate([kf[:, 0:128], kf[:, 128:256]], axis=0).T,
                 jnp.concatenate([kf[:, 256:384], kf[:, 384:512]], axis=0).T], axis=1)
            ktbd = _group_rows(kt.astype(BF16), HEADS)
            qkk = _dot(jnp.concatenate([qb, kb16], axis=0), ktbd)
            cols = gc_ref[rows, :]
            bcols = beta_ref[rows, :]
            at = gct_ref[c]
            for d in range(2):
                j0 = d * HEADS
                bg = [jnp.broadcast_to(cols[:, j0 + h:j0 + h + 1], (CHUNK, 128)) for h in range(HEADS)]
                bb = [jnp.broadcast_to(bcols[:, j0 + h:j0 + h + 1], (CHUNK, 128)) for h in range(HEADS)]
                gcw = jnp.concatenate([_pair(bg[0], bg[1], lt_half), _pair(bg[2], bg[3], lt_half)], 1)
                bw = jnp.concatenate([_pair(bb[0], bb[1], lt_half), _pair(bb[2], bb[3], lt_half)], 1)
                gr = jnp.concatenate(
                    [_pair(at[j0:j0 + 1], at[j0 + 1:j0 + 2], lt_half_row),
                     _pair(at[j0 + 2:j0 + 3], at[j0 + 3:j0 + 4], lt_half_row)], axis=1)
                decay = jnp.exp(jnp.where(incl[d], gcw - gr, NEG_BIG))
                m = bw * qkk[CHUNK:] * decay
                qkd_ref[d, rows, :] = (qkk[0:CHUNK] * decay).astype(BF16)
                en = jnp.concatenate([jnp.exp(x) for x in bg], axis=1)
                bn = jnp.concatenate(bb, axis=1)
                qd_ref[d, rows, :] = (qf * en).astype(BF16)
                vb = (vf * bn).astype(BF16)
                kbe = (kf * (bn * en)).astype(BF16)
                e0 = CHUNK - 1 if d == 0 else 0
                bt = [jnp.broadcast_to(at[j0 + h:j0 + h + 1, e0:e0 + 1], (1, 128)) for h in range(HEADS)]
                tw = jnp.concatenate([_pair(bt[0], bt[1], lt_half_row),
                                      _pair(bt[2], bt[3], lt_half_row)], axis=1)
                kdt_ref[d, pl.ds(pl.multiple_of(c * HEAD_DIM, HEAD_DIM), HEAD_DIM), :] = (
                    kt * jnp.exp(tw - gr)).astype(BF16)
                gt_ref[d, c] = jnp.broadcast_to(
                    jnp.concatenate([jnp.exp(x) for x in bt], axis=1), (8, A_WIDTH))
                groups.append((d, rows, m, vb, kbe))
        ts = [eye_ref[...] - g[2] * lvl_ref[g[0] * N_LEVELS] for g in groups]
        for lv in range(1, N_LEVELS):
            xs = [head_mm(t, g[2] * lvl_ref[g[0] * N_LEVELS + lv]) for t, g in zip(ts, groups)]
            ts = [t - head_mm(x, t) for t, x in zip(ts, xs)]
        for t, (d, rows, _, vb, kbe) in zip(ts, groups):
            tb = t.astype(BF16)
            uw = []
            for pr in range(HEADS // 2):
                h0, h1 = hs[2 * pr], hs[2 * pr + 1]
                rhs = jnp.concatenate(
                    [jnp.concatenate([vb[:, h0], zero_blk, kbe[:, h0], zero_blk], axis=1),
                     jnp.concatenate([zero_blk, vb[:, h1], zero_blk, kbe[:, h1]], axis=1)], axis=0)
                uw.append(_dot(tb[:, pr * 128:(pr + 1) * 128], rhs))
            half = A_WIDTH // 2
            u_ref[d, rows, :] = jnp.concatenate([uw[0][:, 0:half], uw[1][:, 0:half]], axis=1)
            w_ref[d, rows, :] = jnp.concatenate([uw[0][:, half:], uw[1][:, half:]],
                                                axis=1).astype(BF16)
        return carry

    lax.fori_loop(0, n_total // cb, pre_body, 0)

    o_ref[...] = jnp.zeros(o_ref.shape, F32)
    zero_s = jnp.zeros((HEAD_DIM, HEAD_DIM), BF16)

    def scan_step(sub0, c):
        groups = [(p, d) for p in range(n_par) for d in range(2)]
        cidx = [(sub0 + p) * n_chunks + (c if d == 0 else n_chunks - 1 - c) for p, d in groups]
        rows = [pl.ds(pl.multiple_of(ci * CHUNK, CHUNK), CHUNK) for ci in cidx]
        ss = [s_ref[p, d] for p, d in groups]
        wqs = []
        for (p, d), r, s in zip(groups, rows, ss):
            sb = s.astype(BF16)
            lhs = jnp.concatenate([w_ref[d, r, :], qd_ref[d, r, :]], axis=0)
            halves = []
            for hp in range(HEADS // 2):
                bd = _block_diag([sb[:, hs[2 * hp]], sb[:, hs[2 * hp + 1]]], zero_s)
                halves.append(_dot(lhs[:, hp * 2 * HEAD_DIM:(hp + 1) * 2 * HEAD_DIM], bd))
            wqs.append(jnp.concatenate(halves, axis=1))
        boths = []
        for (p, d), r, ci, wq in zip(groups, rows, cidx, wqs):
            v_new = (u_ref[d, r, :] - wq[0:CHUNK]).astype(BF16)
            vbd = _block_diag([v_new[:, s] for s in hs], zero_blk)
            kd = kdt_ref[d, pl.ds(pl.multiple_of(ci * HEAD_DIM, HEAD_DIM), HEAD_DIM), :]
            lhs2 = jnp.concatenate([qkd_ref[d, r, :], kd], axis=0)
            boths.append(_dot(lhs2, vbd))
        for (p, d), r, ci, s, wq, both in zip(groups, rows, cidx, ss, wqs, boths):
            o_ref[r, :] += wq[CHUNK:] + both[0:CHUNK]
            s_ref[p, d] = s * gt_ref[d, ci][0:1] + both[CHUNK:]

    def sub_body(sb_i, carry):
        sub0 = sb_i * n_par
        for p in range(n_par):
            for d in range(2):
                if has_init:
                    s_ref[p, d] = jnp.concatenate(
                        [s0_ref[d * HEADS + h] for h in range(HEADS)], axis=1)
                else:
                    s_ref[p, d] = jnp.zeros((HEAD_DIM, A_WIDTH), F32)

        def step(c, carry2):
            scan_step(sub0, c)
            return carry2

        lax.fori_loop(0, n_chunks, step, 0)
        if st_ref is not None:
            for p in range(n_par):
                for d in range(2):
                    s = s_ref[p, d]
                    for h in range(HEADS):
                        j = d * HEADS + h
                        if layer == 0:
                            st_ref[sub0 + p, 0, j] = s[:, hs[h]]
                            for later in range(1, DEPTH):
                                st_ref[sub0 + p, later, j] = jnp.zeros((HEAD_DIM, HEAD_DIM), F32)
                        else:
                            st_ref[sub0 + p, j] = s[:, hs[h]]
        return carry

    lax.fori_loop(0, n_sub // n_par, sub_body, 0)


def _delta_stage(qkv, ba, alog_row, dtb_row, s0, consts, *, seq_len, layer=0, st_prev=None):
    n = qkv.shape[0]
    block = max(seq_len, DELTA_BLOCK)
    n_blocks = n // block
    n_sub = block // seq_len
    n_chunks = seq_len // CHUNK
    n_total = block // CHUNK
    has_init = s0 is not None
    n_par = min(n_sub, 4)
    lvl, eye, tri = consts
    in_specs = [pl.BlockSpec((block, QKV), lambda i: (i, 0)),
                pl.BlockSpec((block, 128), lambda i: (i, 0)),
                _const_spec((1, 128)), _const_spec((1, 128)),
                _const_spec(lvl.shape), _const_spec(eye.shape), _const_spec(tri.shape)]
    args = [qkv, ba, alog_row, dtb_row, lvl, eye, tri]
    o_spec = pl.BlockSpec((block, A_WIDTH), lambda i: (i, 0))
    o_shape = jax.ShapeDtypeStruct((n, A_WIDTH), F32)
    st_shape = (2 * HEADS, HEAD_DIM, HEAD_DIM)
    if has_init:
        assert n_sub == 1
        in_specs.append(pl.BlockSpec((None,) + st_shape, lambda i: (i, 0, 0, 0)))
        args.append(s0)
        out_specs, out_shape = o_spec, o_shape
        aliases = {}
    else:
        if layer == 0:
            st_spec = pl.BlockSpec((n_sub, DEPTH) + st_shape, lambda i: (i, 0, 0, 0, 0))
            aliases = {}
        else:
            in_specs.append(pl.BlockSpec(memory_space=pl.ANY))
            args.append(st_prev)
            st_spec = pl.BlockSpec((n_sub, None) + st_shape, lambda i: (i, layer, 0, 0, 0))
            aliases = {len(args) - 1: 1}
        out_specs = [o_spec, st_spec]
        out_shape = [o_shape, jax.ShapeDtypeStruct((n // seq_len, DEPTH) + st_shape, F32)]
    scratch = [pltpu.VMEM((n_par, 2, HEAD_DIM, A_WIDTH), F32),
               pltpu.VMEM((block, 128), F32),
               pltpu.VMEM((block, 128), F32),
               pltpu.VMEM((n_total, 8, 128), F32),
               pltpu.VMEM((2, block, A_WIDTH), F32),
               pltpu.VMEM((2, block, A_WIDTH), BF16),
               pltpu.VMEM((2, block, A_WIDTH), BF16),
               pltpu.VMEM((2, block, WIDE), BF16),
               pltpu.VMEM((2, n_total * HEAD_DIM, WIDE), BF16),
               pltpu.VMEM((2, n_total, 8, A_WIDTH), F32)]
    return pl.pallas_call(
        functools.partial(_delta_kernel, n_sub=n_sub, n_chunks=n_chunks, has_init=has_init,
                          n_par=n_par, cb=4, layer=layer),
        grid=(n_blocks,),
        in_specs=in_specs,
        out_specs=out_specs,
        out_shape=out_shape,
        input_output_aliases=aliases,
        scratch_shapes=scratch,
        compiler_params=_params(1),
        name="delta_stage",
    )(*args)


def _group_rows(y, n_groups):
    half = y.shape[1] // 2
    gw = half // 2
    lane = lax.broadcasted_iota(jnp.int32, (y.shape[0], half), 1)
    zero = jnp.zeros((y.shape[0], half), y.dtype)
    blocks = []
    for g in range(n_groups):
        part = y[:, (g // 2) * half:(g // 2 + 1) * half]
        keep = (lane < gw) if g % 2 == 0 else (lane >= gw)
        part = jnp.where(keep, part, jnp.zeros_like(part))
        blocks.append(jnp.concatenate([part, zero] if g < 2 else [zero, part], axis=1))
    return jnp.concatenate(blocks, axis=0)


def _mix_delta(o_ref, z_ref, dg_ref, mix_ref):
    dg = dg_ref[...]
    for h in range(HEADS):
        cs = slice(h * HEAD_DIM, (h + 1) * HEAD_DIM)
        y = _rms_rows(o_ref[:, cs]) * dg * _silu(z_ref[:, cs])
        mix_ref[:, cs] = y.astype(BF16)


def _mix_sgu(uv_ref, sg_ref, ws_ref, bs_ref, seg_ref, mix_ref, tile):
    u = _gelu_tanh(uv_ref[:, 0:B_WIDTH])
    v = _gelu_tanh(uv_ref[:, B_WIDTH:])
    vv = v * v
    vv_hi = vv.astype(BF16)
    vv_lo = (vv - vv_hi.astype(F32)).astype(BF16)
    ms = _dot(jnp.concatenate([vv_hi, vv_lo], axis=1), seg_ref[...])
    vn = (v * lax.rsqrt(ms + EPS) * sg_ref[...]).astype(BF16)
    for c in range(tile // SGU_CHUNK):
        rs = slice(c * SGU_CHUNK, (c + 1) * SGU_CHUNK)
        s = _dot(ws_ref[...], _group_rows(vn[rs], B_GROUPS)) + bs_ref[...]
        mix_ref[rs, A_WIDTH:A_WIDTH + B_WIDTH] = (u[rs] * s).astype(BF16)


def _mix_pool(p_ref, band_ref, icnt_ref, wp_ref, ps_ref, mix_ref, tile):
    for c in range(tile // POOL_TILE):
        rs = slice(c * POOL_TILE, (c + 1) * POOL_TILE)
        x = p_ref[rs, :]
        x_hi = x.astype(BF16)
        x_lo = (x - x_hi.astype(F32)).astype(BF16)
        n_win = len(POOL_WINDOWS)
        rhs = jnp.concatenate([_group_rows(x_hi, n_win), _group_rows(x_lo, n_win)], axis=0)
        wsum = _dot(band_ref[...], rhs)
        diff = wsum * icnt_ref[...] - x
        y = _bdot(diff, wp_ref[...]) * ps_ref[...]
        mix_ref[rs, A_WIDTH + B_WIDTH:] = y.astype(BF16)


def _out_kernel(x_ref, o_ref, z_ref, uv_ref, p_ref, dg_ref, sg_ref, ws_ref, bs_ref, seg_ref,
                band_ref, icnt_ref, wp_ref, ps_ref, mod_ref, g2_ref, wof_ref, wgu_ref, wd_ref,
                nf_ref, y_ref, mix_ref, wo_ref, *, seq_len, tile, row0, final_norm):
    i = pl.program_id(0)

    @pl.when(i == 0)
    def _prepare_weights():
        rows = 256
        for r in range(0, D_MODEL, rows):
            wo_ref[r:r + rows, :] = wof_ref[r:r + rows, :].astype(BF16)

    tiles_per_seq = max(seq_len // tile, 1)
    row = row0 + i // tiles_per_seq if row0 else 0
    m = mod_ref[pl.ds(row, 1), :]
    gate1 = m[:, 2 * D_MODEL:3 * D_MODEL]
    shift2 = m[:, 3 * D_MODEL:4 * D_MODEL]
    scale2 = m[:, 4 * D_MODEL:5 * D_MODEL]
    gate2 = m[:, 5 * D_MODEL:6 * D_MODEL]
    _mix_delta(o_ref, z_ref, dg_ref, mix_ref)
    _mix_sgu(uv_ref, sg_ref, ws_ref, bs_ref, seg_ref, mix_ref, tile)
    _mix_pool(p_ref, band_ref, icnt_ref, wp_ref, ps_ref, mix_ref, tile)
    x1 = x_ref[...] + gate1 * _dot(mix_ref[...], wo_ref[...])
    hb = _modulated_norm(x1, g2_ref[...], shift2, scale2).astype(BF16)
    fc = FF_HIDDEN // FF_SPLIT
    ff = jnp.zeros((tile, D_MODEL), F32)
    for c in range(FF_SPLIT):
        gate = _dot(hb, wgu_ref[:, c * fc:(c + 1) * fc])
        up = _dot(hb, wgu_ref[:, FF_HIDDEN + c * fc:FF_HIDDEN + (c + 1) * fc])
        act = (_silu(gate) * up).astype(BF16)
        ff = ff + _dot(act, wd_ref[c * fc:(c + 1) * fc, :])
    x2 = x1 + gate2 * ff
    if final_norm:
        x2 = _rms_rows(x2) * nf_ref[...]
    y_ref[...] = x2


def _output_stage(x, o, z, uv, p, lw, pool_consts, mods_l, norm2_g, dense, layer, norm_f, *,
                  seq_len, row0, final_norm):
    n = x.shape[0]
    tile = 512
    band, icnt = pool_consts
    row = lambda c: pl.BlockSpec((tile, c), lambda i: (i, 0))
    small = [lw["delta_g"], lw["sgu_g"], lw["w_spatial"], lw["b_spatial"], lw["seg"], band, icnt,
             lw["w_pool"], lw["pool_scale"], mods_l, norm2_g]
    stacked = [dense["out"], dense["gu"], dense["down"]]
    consts = small + stacked + [norm_f]
    in_specs = ([row(D_MODEL), row(A_WIDTH), row(A_WIDTH), row(2 * B_WIDTH), row(C_WIDTH)]
                + [_const_spec(a.shape) for a in small]
                + [_layer_spec(a.shape, layer) for a in stacked] + [_const_spec(norm_f.shape)])
    return pl.pallas_call(
        functools.partial(_out_kernel, seq_len=seq_len, tile=tile, row0=row0,
                          final_norm=final_norm),
        grid=(n // tile,),
        in_specs=in_specs,
        out_specs=row(D_MODEL),
        out_shape=jax.ShapeDtypeStruct((n, D_MODEL), F32),
        scratch_shapes=[pltpu.VMEM((tile, D_MODEL), BF16),
                        pltpu.VMEM((D_MODEL, D_MODEL), BF16)],
        compiler_params=_params(1),
        name="output_stage",
    )(x, o, z, uv, p, *consts)


def _pool_constants(seg_len):
    pos = np.arange(POOL_TILE)
    seg = pos // seg_len
    band = np.zeros((len(POOL_WINDOWS), POOL_TILE, POOL_TILE), np.float32)
    icnt = np.zeros((POOL_TILE, C_WIDTH), np.float32)
    for g, win in enumerate(POOL_WINDOWS):
        lo = pos - win // 2
        hi = pos + win - win // 2
        inside = (pos[None, :] >= lo[:, None]) & (pos[None, :] < hi[:, None]) \
            & (seg[None, :] == seg[:, None])
        band[g] = inside
        icnt[:, g * C_GC:(g + 1) * C_GC] = (1.0 / inside.sum(axis=1))[:, None]
    band_cat = np.concatenate([band[g] for g in range(len(POOL_WINDOWS))] * 2, axis=1)
    return jnp.asarray(band_cat, BF16), jnp.asarray(icnt)


def _grid_pos_tables(rows, d):
    quarter = d // 4
    f = np.float32
    omega = (f(1.0) / (f(10000.0) ** (np.arange(quarter, dtype=f) / f(quarter)))).astype(f)
    r = np.arange(rows, dtype=f)[:, None] * omega
    cl = np.arange(GRID_W, dtype=f)[:, None] * omega
    row_emb = np.concatenate([np.sin(r), np.cos(r)], axis=-1).astype(f)
    col_emb = np.concatenate([np.sin(cl), np.cos(cl)], axis=-1).astype(f)
    return jnp.asarray(row_emb), jnp.asarray(col_emb)


def _pad_lanes(a, width):
    return jnp.pad(a, ((0, 0), (0, width - a.shape[1])))


def _dense_weights(w_in, w_out, w_gu, w_down):
    return {"in": w_in, "out": w_out, "gu": w_gu.astype(BF16), "down": w_down.astype(BF16)}


def _layer_weights(l, w_pool, w_spatial, b_spatial):
    wp = jnp.zeros((C_WIDTH, C_WIDTH), F32)
    for g in range(len(POOL_WINDOWS)):
        wp = wp.at[g * C_GC:(g + 1) * C_GC, g * C_GC:(g + 1) * C_GC].set(w_pool[l, g])
    grp = np.arange(B_WIDTH) // B_GC
    seg1 = (grp[:, None] == grp[None, :]).astype(np.float32) / B_GC
    seg = jnp.asarray(np.concatenate([seg1, seg1], axis=0), BF16)
    bs = jnp.repeat(b_spatial[l].T, B_GC, axis=1)
    ws_cat = jnp.concatenate([w_spatial[l, g] for g in range(B_GROUPS)], axis=1)
    return wp.astype(BF16), seg, ws_cat.astype(BF16), bs


def kernel(x_prompt, x_sample, state_delta, c, c_ctx, w_in, conv_w, a_log, dt_bias, delta_norm_g,
           sgu_norm_g, w_spatial, b_spatial, w_pool, pool_scale, w_out, norm1_g, norm2_g, w_mod,
           b_mod, w_gu, w_down, norm_f):
    batch, seq, d = x_prompt.shape
    dec_batch, dec_seq, _ = x_sample.shape
    cvec = jnp.concatenate([c_ctx[None, :], c, jnp.zeros((8 - 1 - dec_batch, d), F32)], axis=0)
    mods = _modulation(cvec, w_mod, b_mod)
    row_emb, col_emb = _grid_pos_tables(dec_seq // GRID_W, d)
    xs = _add_pos(x_sample, row_emb, col_emb).reshape(dec_batch * dec_seq, d)
    xc = x_prompt.reshape(batch * seq, d)
    pool_ctx = _pool_constants(min(seq, POOL_TILE))
    pool_lat = _pool_constants(GRID_W)
    nf = norm_f.reshape(1, d)
    delta_consts = _delta_constants()
    dense = _dense_weights(w_in, w_out, w_gu, w_down)
    ctx_states = None
    for l in range(DEPTH):
        wp, seg, ws, bs = _layer_weights(l, w_pool, w_spatial, b_spatial)
        lw = {"delta_g": delta_norm_g[l].reshape(1, HEAD_DIM),
              "sgu_g": sgu_norm_g[l].reshape(1, B_WIDTH), "w_spatial": ws, "b_spatial": bs,
              "seg": seg, "w_pool": wp, "pool_scale": pool_scale[l].reshape(1, C_WIDTH)}
        alog_row = _pad_lanes(a_log[l].reshape(1, 2 * HEADS), 128)
        dtb_row = _pad_lanes(dt_bias[l].reshape(1, 2 * HEADS), 128)
        n1 = norm1_g[l].reshape(1, d)
        n2 = norm2_g[l].reshape(1, d)
        last = l == DEPTH - 1
        s0_lat = state_delta[:, l].reshape(dec_batch, 2 * HEADS, HEAD_DIM, HEAD_DIM)
        streams = (("ctx", xc, seq, 0, None, pool_ctx), ("lat", xs, dec_seq, 1, s0_lat, pool_lat))
        outs = []
        for name, x, seq_len, row0, s0, pool_consts in streams:
            qkv, z, ba, uv, p = _input_stage(x, mods[l], n1, dense["in"], l, conv_w[l],
                                             seq_len=seq_len, row0=row0)
            if s0 is None:
                o, ctx_states = _delta_stage(qkv, ba, alog_row, dtb_row, None, delta_consts,
                                             seq_len=seq_len, layer=l, st_prev=ctx_states)
            else:
                o = _delta_stage(qkv, ba, alog_row, dtb_row, s0, delta_consts, seq_len=seq_len)
            outs.append(_output_stage(x, o, z, uv, p, lw, pool_consts, mods[l], n2, dense, l, nf,
                                      seq_len=seq_len, row0=row0, final_norm=last))
        xc, xs = outs
    y_prompt = xc.reshape(batch, seq, d)
    y_sample = xs.reshape(dec_batch, dec_seq, d)
    new_state = ctx_states.reshape(batch, DEPTH, 2, HEADS, HEAD_DIM, HEAD_DIM)
    return (y_prompt, y_sample, new_state)
```

```python
import functools
import math

import numpy as np
import jax
import jax.numpy as jnp
from jax import lax
from jax.experimental import pallas as pl
from jax.experimental.pallas import tpu as pltpu

F32 = jnp.float32
BF16 = jnp.bfloat16
HIGHEST = lax.Precision.HIGHEST

D_MODEL = 1024
DEPTH = 2
GRID_W = 64
HEADS = 4
HEAD_DIM = 128
QK = HEADS * HEAD_DIM
A_WIDTH = HEADS * HEAD_DIM
QKV = 2 * QK + A_WIDTH
CHUNK = 64
B_WIDTH = 256
B_GROUPS = 4
B_GC = 64
SGU_CHUNK = 128
C_WIDTH = 256
POOL_WINDOWS = (2, 4, 8, 16)
C_GC = 64
FF_HIDDEN = 2816
FF_SPLIT = 2
N_MOD = 6
EPS = 1e-6
IN_COLS = 2 * QK + 2 * A_WIDTH + 4 * HEADS + 2 * B_WIDTH + C_WIDTH
IN_Z = QKV
IN_UV = IN_Z + A_WIDTH
IN_P = IN_UV + 2 * B_WIDTH
IN_BA = IN_P + C_WIDTH
IN_PAD = IN_BA + 128
NEG_BIG = -1e30

DELTA_BLOCK = 1024
POOL_TILE = 256
VMEM_LIMIT = 56 * 1024 * 1024


def _dot(a, b, precision=None):
    return jnp.dot(a, b, preferred_element_type=F32, precision=precision)


def _bdot(a, b):
    return jnp.dot(a.astype(BF16), b.astype(BF16), preferred_element_type=F32)


def _sigmoid(x):
    return 1.0 / (1.0 + jnp.exp(-x))


def _silu(x):
    return x * _sigmoid(x)


def _softplus(x):
    return jnp.maximum(x, 0.0) + jnp.log1p(jnp.exp(-jnp.abs(x)))


def _gelu_tanh(x):
    c = math.sqrt(2.0 / math.pi)
    return 0.5 * x * (1.0 + jnp.tanh(c * (x + 0.044715 * (x * x * x))))


def _rms_rows(x):
    return x * lax.rsqrt(jnp.mean(x * x, axis=-1, keepdims=True) + EPS)


def _params(n_grid):
    return pltpu.CompilerParams(dimension_semantics=("arbitrary",) * n_grid,
                                vmem_limit_bytes=VMEM_LIMIT)


def _const_spec(shape):
    nd = len(shape)
    return pl.BlockSpec(shape, lambda *_: (0,) * nd, pipeline_mode=pl.Buffered(1))


def _layer_spec(shape, layer):
    nd = len(shape) - 1
    return pl.BlockSpec((None,) + tuple(shape[1:]), lambda *_: (layer,) + (0,) * nd,
                        pipeline_mode=pl.Buffered(1))


def _mod_kernel(c_ref, w_ref, b_ref, o_ref):
    a = _silu(c_ref[...])
    o_ref[0] = _bdot(a, w_ref[0]) + b_ref[0]


def _modulation(cvec, w_mod, b_mod):
    tn = 1536
    n_out = N_MOD * D_MODEL
    return pl.pallas_call(
        _mod_kernel,
        grid=(DEPTH, n_out // tn),
        in_specs=[pl.BlockSpec((8, D_MODEL), lambda l, j: (0, 0)),
                  pl.BlockSpec((1, D_MODEL, tn), lambda l, j: (l, 0, j)),
                  pl.BlockSpec((1, 1, tn), lambda l, j: (l, 0, j))],
        out_specs=pl.BlockSpec((1, 8, tn), lambda l, j: (l, 0, j)),
        out_shape=jax.ShapeDtypeStruct((DEPTH, 8, n_out), F32),
        compiler_params=_params(2),
        name="modulation",
    )(cvec, w_mod, b_mod.reshape(DEPTH, 1, n_out))


POS_ROWS = 8


def _pos_kernel(x_ref, r_ref, c_ref, o_ref):
    half = D_MODEL // 2
    col = c_ref[...]
    for r in range(POS_ROWS):
        rs = slice(r * GRID_W, (r + 1) * GRID_W)
        o_ref[rs, 0:half] = x_ref[rs, 0:half] + r_ref[r:r + 1, :]
        o_ref[rs, half:] = x_ref[rs, half:] + col


def _add_pos(x, row_emb, col_emb):
    b, n, d = x.shape
    tm = POS_ROWS * GRID_W
    return pl.pallas_call(
        _pos_kernel,
        grid=(b, n // tm),
        in_specs=[pl.BlockSpec((None, tm, d), lambda i, j: (i, j, 0)),
                  pl.BlockSpec((POS_ROWS, d // 2), lambda i, j: (j, 0)),
                  pl.BlockSpec((GRID_W, d // 2), lambda i, j: (0, 0))],
        out_specs=pl.BlockSpec((None, tm, d), lambda i, j: (i, j, 0)),
        out_shape=jax.ShapeDtypeStruct(x.shape, F32),
        compiler_params=_params(2),
        name="add_pos",
    )(x, row_emb, col_emb)


def _modulated_norm(x, gain, shift, scale):
    return _rms_rows(x) * gain * (1.0 + scale) + shift


def _pack_w_in(wf_ref, wb_ref):
    n_logit = 4 * HEADS
    rows = 256
    lane = lax.broadcasted_iota(jnp.int32, (rows, 128), 1)
    for r in range(0, D_MODEL, rows):
        rs = slice(r, r + rows)
        wb_ref[rs, 0:IN_UV] = wf_ref[rs, 0:IN_UV].astype(BF16)
        tail = wf_ref[rs, IN_UV:IN_COLS]
        wb_ref[rs, IN_UV:IN_BA] = tail[:, n_logit:].astype(BF16)
        wb_ref[rs, IN_BA:IN_PAD] = jnp.where(lane < n_logit, tail[:, 0:128], 0.0).astype(BF16)


def _in_kernel(*refs, seq_len, tile, row0, halo):
    if halo:
        (x_ref, xp_ref, xn_ref, mod_ref, g_ref, wf_ref,
         cw_ref, qkv_ref, z_ref, ba_ref, uv_ref, p_ref, win_ref) = refs
    else:
        (x_ref, mod_ref, g_ref, wf_ref,
         cw_ref, qkv_ref, z_ref, ba_ref, uv_ref, p_ref, win_ref) = refs
    i = pl.program_id(0)

    @pl.when(i == 0)
    def _prepare_weights():
        _pack_w_in(wf_ref, win_ref)

    tiles_per_seq = max(seq_len // tile, 1)
    row = row0 + i // tiles_per_seq if row0 else 0
    m = mod_ref[pl.ds(row, 1), :]
    shift = m[:, 0:D_MODEL]
    scale = m[:, D_MODEL:2 * D_MODEL]
    gain = g_ref[...]
    hf = _modulated_norm(x_ref[...], gain, shift, scale)
    hb = hf.astype(BF16)
    if halo:
        hp = _modulated_norm(xp_ref[...], gain, shift, scale)
        hn = _modulated_norm(xn_ref[...], gain, shift, scale)
        hq = jnp.concatenate([hf, hp, hn], axis=0).astype(BF16)
    else:
        hq = hb
    ridx = lax.broadcasted_iota(jnp.int32, (tile, 1), 0)
    if halo:
        pos_in_seq = (i % tiles_per_seq) * tile + ridx
    else:
        pos_in_seq = ridx % seq_len
    first_row = ridx == 0
    last_row = ridx == tile - 1
    seq_start = pos_in_seq == 0
    seq_end = pos_in_seq == seq_len - 1
    cw = cw_ref[...]
    blk = 2 * HEAD_DIM

    def project(b):
        return _dot(hq, win_ref[:, b * blk:(b + 1) * blk])

    def conv_act(b, pre):
        cols = slice(b * blk, (b + 1) * blk)
        cur = pre[0:tile]
        prev = pltpu.roll(cur, 1, 0)
        nxt = pltpu.roll(cur, tile - 1, 0)
        if halo:
            prev = jnp.where(first_row, pre[tile + 7:tile + 8], prev)
            nxt = jnp.where(last_row, pre[tile + 8:tile + 9], nxt)
        prev = jnp.where(seq_start, 0.0, prev)
        nxt = jnp.where(seq_end, 0.0, nxt)
        act = _silu(prev * cw[0:1, cols] + cur * cw[1:2, cols] + nxt * cw[2:3, cols])
        if b * blk >= 2 * QK:
            qkv_ref[:, cols] = act.astype(BF16)
            return
        scale = HEAD_DIM ** -0.5 if b * blk < QK else 1.0
        for h in range(blk // HEAD_DIM):
            a = act[:, h * HEAD_DIM:(h + 1) * HEAD_DIM]
            inv = lax.rsqrt(jnp.sum(a * a, axis=-1, keepdims=True) + EPS) * scale
            c0 = b * blk + h * HEAD_DIM
            qkv_ref[:, c0:c0 + HEAD_DIM] = (a * inv).astype(BF16)

    others = []
    for out_ref, c0 in ((z_ref, IN_Z), (uv_ref, IN_UV), (p_ref, IN_P), (ba_ref, IN_BA)):
        width = out_ref.shape[1]
        others += [(out_ref, c0, slice(c, min(c + blk, width))) for c in range(0, width, blk)]
    n_blk = QKV // blk
    assert len(others) == n_blk
    pre = project(0)
    for b in range(n_blk):
        nxt_pre = project(b + 1) if b + 1 < n_blk else None
        out_ref, c0, cols = others[b]
        out_ref[:, cols] = _dot(hb, win_ref[:, c0 + cols.start:c0 + cols.stop])
        conv_act(b, pre)
        pre = nxt_pre


def _input_stage(x, mods_l, norm_g, w_in, layer, conv_w, *, seq_len, row0):
    n = x.shape[0]
    tile = 512
    halo = seq_len > tile
    tiles_per_seq = max(seq_len // tile, 1)
    n_tiles = n // tile
    h8 = tile // 8
    last8 = n // 8 - 1
    in_specs = [pl.BlockSpec((tile, D_MODEL), lambda i: (i, 0))]
    args = [x]
    if halo:
        in_specs += [pl.BlockSpec((8, D_MODEL), lambda i: (jnp.maximum(i * h8 - 1, 0), 0)),
                     pl.BlockSpec((8, D_MODEL), lambda i: (jnp.minimum((i + 1) * h8, last8), 0))]
        args += [x, x]
    in_specs += [_const_spec((8, N_MOD * D_MODEL)), _const_spec((1, D_MODEL)),
                 _layer_spec(w_in.shape, layer), _const_spec((3, QKV))]
    args += [mods_l, norm_g, w_in, conv_w]
    widths = (QKV, A_WIDTH, 128, 2 * B_WIDTH, C_WIDTH)
    dtypes = (BF16, F32, F32, F32, F32)
    return pl.pallas_call(
        functools.partial(_in_kernel, seq_len=seq_len, tile=tile, row0=row0, halo=halo),
        grid=(n_tiles,),
        in_specs=in_specs,
        out_specs=[pl.BlockSpec((tile, c), lambda i: (i, 0)) for c in widths],
        out_shape=[jax.ShapeDtypeStruct((n, c), t) for c, t in zip(widths, dtypes)],
        scratch_shapes=[pltpu.VMEM((D_MODEL, IN_PAD), BF16)],
        compiler_params=_params(1),
        name="input_stage",
    )(*args)


N_LEVELS = int(math.log2(CHUNK))
WIDE = HEADS * CHUNK


def _delta_constants():
    r = np.arange(CHUNK)[:, None]
    c = (np.arange(WIDE) % CHUNK)[None, :]
    lvl = np.zeros((2, N_LEVELS, CHUNK, WIDE), np.float32)
    for d in range(2):
        rr, cc = (r, c) if d == 0 else (c, r)
        for i in range(N_LEVELS):
            s = 1 << i
            lvl[d, i] = (rr // (2 * s) == cc // (2 * s)) & ((rr // s) % 2 == 1) & ((cc // s) % 2 == 0)
    eye = (r == c).astype(np.float32)
    tri = np.concatenate([np.tril(np.ones((CHUNK, CHUNK))), np.triu(np.ones((CHUNK, CHUNK)))], 0)
    return (jnp.asarray(lvl.reshape(2 * N_LEVELS, CHUNK, WIDE)), jnp.asarray(eye),
            jnp.asarray(tri, BF16))


def _pair(a, b, lt_half):
    return jnp.where(lt_half, a, b)


def _block_diag(blocks, zero):
    n = len(blocks)
    rows = [jnp.concatenate([blocks[i] if j == i else zero for j in range(n)], axis=1)
            for i in range(n)]
    return jnp.concatenate(rows, axis=0)


def _delta_kernel(*refs, n_sub, n_chunks, has_init, n_par, cb, layer):
    it = iter(refs)
    qkv_ref, ba_ref, alog_ref, dtb_ref, lvl_ref, eye_ref, tri_ref = (next(it) for _ in range(7))
    s0_ref = next(it) if has_init else None
    sp_ref = next(it) if (not has_init and layer > 0) else None
    o_ref = next(it)
    st_ref = None if has_init else next(it)
    (s_ref, gc_ref, beta_ref, gct_ref, u_ref, w_ref, qd_ref, qkd_ref, kdt_ref, gt_ref) = it
    n_total = n_sub * n_chunks
    lane = lax.broadcasted_iota(jnp.int32, (CHUNK, 128), 1)
    lt_half = lane < CHUNK
    lt_half_row = lt_half[0:1]
    rw = lax.broadcasted_iota(jnp.int32, (CHUNK, WIDE), 0)
    cw = lax.broadcasted_iota(jnp.int32, (CHUNK, WIDE), 1) & (CHUNK - 1)
    incl = (rw >= cw, rw <= cw)
    a_row = -jnp.exp(alog_ref[...])
    dtb_row = dtb_ref[...]

    def gate_body(bi, carry):
        span = cb * CHUNK
        r0 = pl.multiple_of(bi * span, span)
        ba = ba_ref[pl.ds(r0, span), :]
        beta_ref[pl.ds(r0, span), :] = _sigmoid(ba)
        al = pltpu.roll(ba, 128 - 2 * HEADS, 1)
        lane_b = lax.broadcasted_iota(jnp.int32, (span, 128), 1)
        g = jnp.where(lane_b < 2 * HEADS, a_row * _softplus(al + dtb_row), 0.0)
        g_hi = g.astype(BF16).astype(F32)
        r1 = g - g_hi
        g_mid = r1.astype(BF16).astype(F32)
        g_lo = (r1 - g_mid).astype(BF16).astype(F32)
        g3 = (g_hi + pltpu.roll(g_mid, 8, 1) + pltpu.roll(g_lo, 16, 1)).astype(BF16)
        css = [_dot(tri_ref[...], g3[i * CHUNK:(i + 1) * CHUNK]) for i in range(cb)]
        css = [cs + pltpu.roll(cs, 128 - 8, 1) + pltpu.roll(cs, 128 - 16, 1) for cs in css]
        gcs = [jnp.where(lane < HEADS, cs[0:CHUNK], cs[CHUNK:]) for cs in css]
        ats = [jnp.concatenate([gc, gc], axis=0).T[0:8, :] for gc in gcs]
        for i in range(cb):
            gc_ref[pl.ds(pl.multiple_of(r0 + i * CHUNK, CHUNK), CHUNK), :] = gcs[i]
            gct_ref[bi * cb + i] = ats[i]
        return carry

    lax.fori_loop(0, n_total // cb, gate_body, 0)

    zero_blk = jnp.zeros((CHUNK, HEAD_DIM), BF16)
    hs = [slice(h * HEAD_DIM, (h + 1) * HEAD_DIM) for h in range(HEADS)]

    def head_mm(x, y):
        xb = x.astype(BF16)
        yb = y.astype(BF16)
        zero = jnp.zeros((CHUNK, 128), BF16)
        outs = []
        for pr in range(HEADS // 2):
            ys = yb[:, pr * 128:(pr + 1) * 128]
            bd = jnp.concatenate([jnp.where(lt_half, ys, zero), jnp.where(lt_half, zero, ys)], axis=0)
            outs.append(_dot(xb[:, pr * 128:(pr + 1) * 128], bd))
        return jnp.concatenate(outs, axis=1)

    def pre_body(bi, carry):
        groups = []
        for i in range(cb):
            c = bi * cb + i
            r0 = pl.multiple_of(c * CHUNK, CHUNK)
            rows = pl.ds(r0, CHUNK)
            qb = qkv_ref[rows, 0:QK]
            kb16 = qkv_ref[rows, QK:2 * QK]
            qf = qb.astype(F32)
            kf = kb16.astype(F32)
            vf = qkv_ref[rows, 2 * QK:].astype(F32)
            kt = jnp.concatenate(
                [jnp.concatenate([kf[:, 0:128], kf[:, 128:256]], axis=0).T,
                 jnp.concatenate([kf[:, 256:384], kf[:, 384:512]], axis=0).T], axis=1)
            ktbd = _group_rows(kt.astype(BF16), HEADS)
            qkk = _dot(jnp.concatenate([qb, kb16], axis=0), ktbd)
            cols = gc_ref[rows, :]
            bcols = beta_ref[rows, :]
            at = gct_ref[c]
            for d in range(2):
                j0 = d * HEADS
                bg = [jnp.broadcast_to(cols[:, j0 + h:j0 + h + 1], (CHUNK, 128)) for h in range(HEADS)]
                bb = [jnp.broadcast_to(bcols[:, j0 + h:j0 + h + 1], (CHUNK, 128)) for h in range(HEADS)]
                gcw = jnp.concatenate([_pair(bg[0], bg[1], lt_half), _pair(bg[2], bg[3], lt_half)], 1)
                bw = jnp.concatenate([_pair(bb[0], bb[1], lt_half), _pair(bb[2], bb[3], lt_half)], 1)
                gr = jnp.concatenate(
                    [_pair(at[j0:j0 + 1], at[j0 + 1:j0 + 2], lt_half_row),
                     _pair(at[j0 + 2:j0 + 3], at[j0 + 3:j0 + 4], lt_half_row)], axis=1)
                decay = jnp.exp(jnp.where(incl[d], gcw - gr, NEG_BIG))
                m = bw * qkk[CHUNK:] * decay
                qkd_ref[d, rows, :] = (qkk[0:CHUNK] * decay).astype(BF16)
                en = jnp.concatenate([jnp.exp(x) for x in bg], axis=1)
                bn = jnp.concatenate(bb, axis=1)
                qd_ref[d, rows, :] = (qf * en).astype(BF16)
                vb = (vf * bn).astype(BF16)
                kbe = (kf * (bn * en)).astype(BF16)
                e0 = CHUNK - 1 if d == 0 else 0
                bt = [jnp.broadcast_to(at[j0 + h:j0 + h + 1, e0:e0 + 1], (1, 128)) for h in range(HEADS)]
                tw = jnp.concatenate([_pair(bt[0], bt[1], lt_half_row),
                                      _pair(bt[2], bt[3], lt_half_row)], axis=1)
                kdt_ref[d, pl.ds(pl.multiple_of(c * HEAD_DIM, HEAD_DIM), HEAD_DIM), :] = (
                    kt * jnp.exp(tw - gr)).astype(BF16)
                gt_ref[d, c] = jnp.broadcast_to(
                    jnp.concatenate([jnp.exp(x) for x in bt], axis=1), (8, A_WIDTH))
                groups.append((d, rows, m, vb, kbe))
        ts = [eye_ref[...] - g[2] * lvl_ref[g[0] * N_LEVELS] for g in groups]
        for lv in range(1, N_LEVELS):
            xs = [head_mm(t, g[2] * lvl_ref[g[0] * N_LEVELS + lv]) for t, g in zip(ts, groups)]
            ts = [t - head_mm(x, t) for t, x in zip(ts, xs)]
        for t, (d, rows, _, vb, kbe) in zip(ts, groups):
            tb = t.astype(BF16)
            uw = []
            for pr in range(HEADS // 2):
                h0, h1 = hs[2 * pr], hs[2 * pr + 1]
                rhs = jnp.concatenate(
                    [jnp.concatenate([vb[:, h0], zero_blk, kbe[:, h0], zero_blk], axis=1),
                     jnp.concatenate([zero_blk, vb[:, h1], zero_blk, kbe[:, h1]], axis=1)], axis=0)
                uw.append(_dot(tb[:, pr * 128:(pr + 1) * 128], rhs))
            half = A_WIDTH // 2
            u_ref[d, rows, :] = jnp.concatenate([uw[0][:, 0:half], uw[1][:, 0:half]], axis=1)
            w_ref[d, rows, :] = jnp.concatenate([uw[0][:, half:], uw[1][:, half:]],
                                                axis=1).astype(BF16)
        return carry

    lax.fori_loop(0, n_total // cb, pre_body, 0)

    o_ref[...] = jnp.zeros(o_ref.shape, F32)
    zero_s = jnp.zeros((HEAD_DIM, HEAD_DIM), BF16)

    def scan_step(sub0, c):
        groups = [(p, d) for p in range(n_par) for d in range(2)]
        cidx = [(sub0 + p) * n_chunks + (c if d == 0 else n_chunks - 1 - c) for p, d in groups]
        rows = [pl.ds(pl.multiple_of(ci * CHUNK, CHUNK), CHUNK) for ci in cidx]
        ss = [s_ref[p, d] for p, d in groups]
        wqs = []
        for (p, d), r, s in zip(groups, rows, ss):
            sb = s.astype(BF16)
            lhs = jnp.concatenate([w_ref[d, r, :], qd_ref[d, r, :]], axis=0)
            halves = []
            for hp in range(HEADS // 2):
                bd = _block_diag([sb[:, hs[2 * hp]], sb[:, hs[2 * hp + 1]]], zero_s)
                halves.append(_dot(lhs[:, hp * 2 * HEAD_DIM:(hp + 1) * 2 * HEAD_DIM], bd))
            wqs.append(jnp.concatenate(halves, axis=1))
        boths = []
        for (p, d), r, ci, wq in zip(groups, rows, cidx, wqs):
            v_new = (u_ref[d, r, :] - wq[0:CHUNK]).astype(BF16)
            vbd = _block_diag([v_new[:, s] for s in hs], zero_blk)
            kd = kdt_ref[d, pl.ds(pl.multiple_of(ci * HEAD_DIM, HEAD_DIM), HEAD_DIM), :]
            lhs2 = jnp.concatenate([qkd_ref[d, r, :], kd], axis=0)
            boths.append(_dot(lhs2, vbd))
        for (p, d), r, ci, s, wq, both in zip(groups, rows, cidx, ss, wqs, boths):
            o_ref[r, :] += wq[CHUNK:] + both[0:CHUNK]
            s_ref[p, d] = s * gt_ref[d, ci][0:1] + both[CHUNK:]

    def sub_body(sb_i, carry):
        sub0 = sb_i * n_par
        for p in range(n_par):
            for d in range(2):
                if has_init:
                    s_ref[p, d] = jnp.concatenate(
                        [s0_ref[d * HEADS + h] for h in range(HEADS)], axis=1)
                else:
                    s_ref[p, d] = jnp.zeros((HEAD_DIM, A_WIDTH), F32)

        def step(c, carry2):
            scan_step(sub0, c)
            return carry2

        lax.fori_loop(0, n_chunks, step, 0)
        if st_ref is not None:
            for p in range(n_par):
                for d in range(2):
                    s = s_ref[p, d]
                    for h in range(HEADS):
                        j = d * HEADS + h
                        st_ref[sub0 + p, layer, j] = s[:, hs[h]]
                        for earlier in range(layer):
                            st_ref[sub0 + p, earlier, j] = sp_ref[sub0 + p, earlier, j]
        return carry

    lax.fori_loop(0, n_sub // n_par, sub_body, 0)


def _delta_stage(qkv, ba, alog_row, dtb_row, s0, consts, *, seq_len, layer=0, st_prev=None):
    n = qkv.shape[0]
    block = max(seq_len, DELTA_BLOCK)
    n_blocks = n // block
    n_sub = block // seq_len
    n_chunks = seq_len // CHUNK
    n_total = block // CHUNK
    has_init = s0 is not None
    n_par = min(n_sub, 4)
    lvl, eye, tri = consts
    in_specs = [pl.BlockSpec((block, QKV), lambda i: (i, 0)),
                pl.BlockSpec((block, 128), lambda i: (i, 0)),
                _const_spec((1, 128)), _const_spec((1, 128)),
                _const_spec(lvl.shape), _const_spec(eye.shape), _const_spec(tri.shape)]
    args = [qkv, ba, alog_row, dtb_row, lvl, eye, tri]
    o_spec = pl.BlockSpec((block, A_WIDTH), lambda i: (i, 0))
    o_shape = jax.ShapeDtypeStruct((n, A_WIDTH), F32)
    st_shape = (2 * HEADS, HEAD_DIM, HEAD_DIM)
    if has_init:
        assert n_sub == 1
        in_specs.append(pl.BlockSpec((None,) + st_shape, lambda i: (i, 0, 0, 0)))
        args.append(s0)
        out_specs, out_shape = o_spec, o_shape
    else:
        if layer > 0:
            in_specs.append(pl.BlockSpec((n_sub, layer) + st_shape, lambda i: (i, 0, 0, 0, 0)))
            args.append(st_prev)
        out_specs = [o_spec,
                     pl.BlockSpec((n_sub, layer + 1) + st_shape, lambda i: (i, 0, 0, 0, 0))]
        out_shape = [o_shape, jax.ShapeDtypeStruct((n // seq_len, layer + 1) + st_shape, F32)]
    scratch = [pltpu.VMEM((n_par, 2, HEAD_DIM, A_WIDTH), F32),
               pltpu.VMEM((block, 128), F32),
               pltpu.VMEM((block, 128), F32),
               pltpu.VMEM((n_total, 8, 128), F32),
               pltpu.VMEM((2, block, A_WIDTH), F32),
               pltpu.VMEM((2, block, A_WIDTH), BF16),
               pltpu.VMEM((2, block, A_WIDTH), BF16),
               pltpu.VMEM((2, block, WIDE), BF16),
               pltpu.VMEM((2, n_total * HEAD_DIM, WIDE), BF16),
               pltpu.VMEM((2, n_total, 8, A_WIDTH), F32)]
    return pl.pallas_call(
        functools.partial(_delta_kernel, n_sub=n_sub, n_chunks=n_chunks, has_init=has_init,
                          n_par=n_par, cb=4, layer=layer),
        grid=(n_blocks,),
        in_specs=in_specs,
        out_specs=out_specs,
        out_shape=out_shape,
        scratch_shapes=scratch,
        compiler_params=_params(1),
        name="delta_stage",
    )(*args)


def _group_rows(y, n_groups):
    half = y.shape[1] // 2
    gw = half // 2
    lane = lax.broadcasted_iota(jnp.int32, (y.shape[0], half), 1)
    zero = jnp.zeros((y.shape[0], half), y.dtype)
    blocks = []
    for g in range(n_groups):
        part = y[:, (g // 2) * half:(g // 2 + 1) * half]
        keep = (lane < gw) if g % 2 == 0 else (lane >= gw)
        part = jnp.where(keep, part, jnp.zeros_like(part))
        blocks.append(jnp.concatenate([part, zero] if g < 2 else [zero, part], axis=1))
    return jnp.concatenate(blocks, axis=0)


def _mix_delta(o_ref, z_ref, dg_ref, mix_ref):
    dg = dg_ref[...]
    for h in range(HEADS):
        cs = slice(h * HEAD_DIM, (h + 1) * HEAD_DIM)
        y = _rms_rows(o_ref[:, cs]) * dg * _silu(z_ref[:, cs])
        mix_ref[:, cs] = y.astype(BF16)


def _mix_sgu(uv_ref, sg_ref, ws_ref, bs_ref, seg_ref, mix_ref, tile):
    u = _gelu_tanh(uv_ref[:, 0:B_WIDTH])
    v = _gelu_tanh(uv_ref[:, B_WIDTH:])
    vv = v * v
    vv_hi = vv.astype(BF16)
    vv_lo = (vv - vv_hi.astype(F32)).astype(BF16)
    ms = _dot(jnp.concatenate([vv_hi, vv_lo], axis=1), seg_ref[...])
    vn = (v * lax.rsqrt(ms + EPS) * sg_ref[...]).astype(BF16)
    for c in range(tile // SGU_CHUNK):
        rs = slice(c * SGU_CHUNK, (c + 1) * SGU_CHUNK)
        s = _dot(ws_ref[...], _group_rows(vn[rs], B_GROUPS)) + bs_ref[...]
        mix_ref[rs, A_WIDTH:A_WIDTH + B_WIDTH] = (u[rs] * s).astype(BF16)


def _mix_pool(p_ref, band_ref, icnt_ref, wp_ref, ps_ref, mix_ref, tile):
    for c in range(tile // POOL_TILE):
        rs = slice(c * POOL_TILE, (c + 1) * POOL_TILE)
        x = p_ref[rs, :]
        x_hi = x.astype(BF16)
        x_lo = (x - x_hi.astype(F32)).astype(BF16)
        n_win = len(POOL_WINDOWS)
        rhs = jnp.concatenate([_group_rows(x_hi, n_win), _group_rows(x_lo, n_win)], axis=0)
        wsum = _dot(band_ref[...], rhs)
        diff = wsum * icnt_ref[...] - x
        y = _bdot(diff, wp_ref[...]) * ps_ref[...]
        mix_ref[rs, A_WIDTH + B_WIDTH:] = y.astype(BF16)


def _out_kernel(x_ref, o_ref, z_ref, uv_ref, p_ref, dg_ref, sg_ref, ws_ref, bs_ref, seg_ref,
                band_ref, icnt_ref, wp_ref, ps_ref, mod_ref, g2_ref, wof_ref, wgu_ref, wd_ref,
                nf_ref, y_ref, mix_ref, wo_ref, *, seq_len, tile, row0, final_norm):
    i = pl.program_id(0)

    @pl.when(i == 0)
    def _prepare_weights():
        rows = 256
        for r in range(0, D_MODEL, rows):
            wo_ref[r:r + rows, :] = wof_ref[r:r + rows, :].astype(BF16)

    tiles_per_seq = max(seq_len // tile, 1)
    row = row0 + i // tiles_per_seq if row0 else 0
    m = mod_ref[pl.ds(row, 1), :]
    gate1 = m[:, 2 * D_MODEL:3 * D_MODEL]
    shift2 = m[:, 3 * D_MODEL:4 * D_MODEL]
    scale2 = m[:, 4 * D_MODEL:5 * D_MODEL]
    gate2 = m[:, 5 * D_MODEL:6 * D_MODEL]
    _mix_delta(o_ref, z_ref, dg_ref, mix_ref)
    _mix_sgu(uv_ref, sg_ref, ws_ref, bs_ref, seg_ref, mix_ref, tile)
    _mix_pool(p_ref, band_ref, icnt_ref, wp_ref, ps_ref, mix_ref, tile)
    x1 = x_ref[...] + gate1 * _dot(mix_ref[...], wo_ref[...])
    hb = _modulated_norm(x1, g2_ref[...], shift2, scale2).astype(BF16)
    fc = FF_HIDDEN // FF_SPLIT
    ff = jnp.zeros((tile, D_MODEL), F32)
    for c in range(FF_SPLIT):
        gate = _dot(hb, wgu_ref[:, c * fc:(c + 1) * fc])
        up = _dot(hb, wgu_ref[:, FF_HIDDEN + c * fc:FF_HIDDEN + (c + 1) * fc])
        act = (_silu(gate) * up).astype(BF16)
        ff = ff + _dot(act, wd_ref[c * fc:(c + 1) * fc, :])
    x2 = x1 + gate2 * ff
    if final_norm:
        x2 = _rms_rows(x2) * nf_ref[...]
    y_ref[...] = x2


def _output_stage(x, o, z, uv, p, lw, pool_consts, mods_l, norm2_g, dense, layer, norm_f, *,
                  seq_len, row0, final_norm):
    n = x.shape[0]
    tile = 512
    band, icnt = pool_consts
    row = lambda c: pl.BlockSpec((tile, c), lambda i: (i, 0))
    small = [lw["delta_g"], lw["sgu_g"], lw["w_spatial"], lw["b_spatial"], lw["seg"], band, icnt,
             lw["w_pool"], lw["pool_scale"], mods_l, norm2_g]
    stacked = [dense["out"], dense["gu"], dense["down"]]
    consts = small + stacked + [norm_f]
    in_specs = ([row(D_MODEL), row(A_WIDTH), row(A_WIDTH), row(2 * B_WIDTH), row(C_WIDTH)]
                + [_const_spec(a.shape) for a in small]
                + [_layer_spec(a.shape, layer) for a in stacked] + [_const_spec(norm_f.shape)])
    return pl.pallas_call(
        functools.partial(_out_kernel, seq_len=seq_len, tile=tile, row0=row0,
                          final_norm=final_norm),
        grid=(n // tile,),
        in_specs=in_specs,
        out_specs=row(D_MODEL),
        out_shape=jax.ShapeDtypeStruct((n, D_MODEL), F32),
        scratch_shapes=[pltpu.VMEM((tile, D_MODEL), BF16),
                        pltpu.VMEM((D_MODEL, D_MODEL), BF16)],
        compiler_params=_params(1),
        name="output_stage",
    )(x, o, z, uv, p, *consts)


def _pool_constants(seg_len):
    pos = np.arange(POOL_TILE)
    seg = pos // seg_len
    band = np.zeros((len(POOL_WINDOWS), POOL_TILE, POOL_TILE), np.float32)
    icnt = np.zeros((POOL_TILE, C_WIDTH), np.float32)
    for g, win in enumerate(POOL_WINDOWS):
        lo = pos - win // 2
        hi = pos + win - win // 2
        inside = (pos[None, :] >= lo[:, None]) & (pos[None, :] < hi[:, None]) \
            & (seg[None, :] == seg[:, None])
        band[g] = inside
        icnt[:, g * C_GC:(g + 1) * C_GC] = (1.0 / inside.sum(axis=1))[:, None]
    band_cat = np.concatenate([band[g] for g in range(len(POOL_WINDOWS))] * 2, axis=1)
    return jnp.asarray(band_cat, BF16), jnp.asarray(icnt)


def _grid_pos_tables(rows, d):
    quarter = d // 4
    f = np.float32
    omega = (f(1.0) / (f(10000.0) ** (np.arange(quarter, dtype=f) / f(quarter)))).astype(f)
    r = np.arange(rows, dtype=f)[:, None] * omega
    cl = np.arange(GRID_W, dtype=f)[:, None] * omega
    row_emb = np.concatenate([np.sin(r), np.cos(r)], axis=-1).astype(f)
    col_emb = np.concatenate([np.sin(cl), np.cos(cl)], axis=-1).astype(f)
    return jnp.asarray(row_emb), jnp.asarray(col_emb)


def _pad_lanes(a, width):
    return jnp.pad(a, ((0, 0), (0, width - a.shape[1])))


def _dense_weights(w_in, w_out, w_gu, w_down):
    return {"in": w_in, "out": w_out, "gu": w_gu.astype(BF16), "down": w_down.astype(BF16)}


def _layer_weights(l, w_pool, w_spatial, b_spatial):
    wp = jnp.zeros((C_WIDTH, C_WIDTH), F32)
    for g in range(len(POOL_WINDOWS)):
        wp = wp.at[g * C_GC:(g + 1) * C_GC, g * C_GC:(g + 1) * C_GC].set(w_pool[l, g])
    grp = np.arange(B_WIDTH) // B_GC
    seg1 = (grp[:, None] == grp[None, :]).astype(np.float32) / B_GC
    seg = jnp.asarray(np.concatenate([seg1, seg1], axis=0), BF16)
    bs = jnp.repeat(b_spatial[l].T, B_GC, axis=1)
    ws_cat = jnp.concatenate([w_spatial[l, g] for g in range(B_GROUPS)], axis=1)
    return wp.astype(BF16), seg, ws_cat.astype(BF16), bs


def kernel(x_prompt, x_sample, state_delta, c, c_ctx, w_in, conv_w, a_log, dt_bias, delta_norm_g,
           sgu_norm_g, w_spatial, b_spatial, w_pool, pool_scale, w_out, norm1_g, norm2_g, w_mod,
           b_mod, w_gu, w_down, norm_f):
    batch, seq, d = x_prompt.shape
    dec_batch, dec_seq, _ = x_sample.shape
    cvec = jnp.concatenate([c_ctx[None, :], c, jnp.zeros((8 - 1 - dec_batch, d), F32)], axis=0)
    mods = _modulation(cvec, w_mod, b_mod)
    row_emb, col_emb = _grid_pos_tables(dec_seq // GRID_W, d)
    xs = _add_pos(x_sample, row_emb, col_emb).reshape(dec_batch * dec_seq, d)
    xc = x_prompt.reshape(batch * seq, d)
    pool_ctx = _pool_constants(min(seq, POOL_TILE))
    pool_lat = _pool_constants(GRID_W)
    nf = norm_f.reshape(1, d)
    delta_consts = _delta_constants()
    dense = _dense_weights(w_in, w_out, w_gu, w_down)
    ctx_states = None
    for l in range(DEPTH):
        wp, seg, ws, bs = _layer_weights(l, w_pool, w_spatial, b_spatial)
        lw = {"delta_g": delta_norm_g[l].reshape(1, HEAD_DIM),
              "sgu_g": sgu_norm_g[l].reshape(1, B_WIDTH), "w_spatial": ws, "b_spatial": bs,
              "seg": seg, "w_pool": wp, "pool_scale": pool_scale[l].reshape(1, C_WIDTH)}
        alog_row = _pad_lanes(a_log[l].reshape(1, 2 * HEADS), 128)
        dtb_row = _pad_lanes(dt_bias[l].reshape(1, 2 * HEADS), 128)
        n1 = norm1_g[l].reshape(1, d)
        n2 = norm2_g[l].reshape(1, d)
        last = l == DEPTH - 1
        s0_lat = state_delta[:, l].reshape(dec_batch, 2 * HEADS, HEAD_DIM, HEAD_DIM)
        streams = (("ctx", xc, seq, 0, None, pool_ctx), ("lat", xs, dec_seq, 1, s0_lat, pool_lat))
        outs = []
        for name, x, seq_len, row0, s0, pool_consts in streams:
            qkv, z, ba, uv, p = _input_stage(x, mods[l], n1, dense["in"], l, conv_w[l],
                                             seq_len=seq_len, row0=row0)
            if s0 is None:
                o, ctx_states = _delta_stage(qkv, ba, alog_row, dtb_row, None, delta_consts,
                                             seq_len=seq_len, layer=l, st_prev=ctx_states)
            else:
                o = _delta_stage(qkv, ba, alog_row, dtb_row, s0, delta_consts, seq_len=seq_len)
            outs.append(_output_stage(x, o, z, uv, p, lw, pool_consts, mods[l], n2, dense, l, nf,
                                      seq_len=seq_len, row0=row0, final_norm=last))
        xc, xs = outs
    y_prompt = xc.reshape(batch, seq, d)
    y_sample = xs.reshape(dec_batch, dec_seq, d)
    new_state = ctx_states.reshape(batch, DEPTH, 2, HEADS, HEAD_DIM, HEAD_DIM)
    return (y_prompt, y_sample, new_state)
```

```python
import functools
import math

import numpy as np
import jax
import jax.numpy as jnp
from jax import lax
from jax.experimental import pallas as pl
from jax.experimental.pallas import tpu as pltpu

F32 = jnp.float32
BF16 = jnp.bfloat16
HIGHEST = lax.Precision.HIGHEST

D_MODEL = 1024
DEPTH = 2
GRID_W = 64
HEADS = 4
HEAD_DIM = 128
QK = HEADS * HEAD_DIM
A_WIDTH = HEADS * HEAD_DIM
QKV = 2 * QK + A_WIDTH
CHUNK = 64
B_WIDTH = 256
B_GROUPS = 4
B_GC = 64
SGU_CHUNK = 128
C_WIDTH = 256
POOL_WINDOWS = (2, 4, 8, 16)
C_GC = 64
FF_HIDDEN = 2816
FF_SPLIT = 2
N_MOD = 6
EPS = 1e-6
IN_COLS = 2 * QK + 2 * A_WIDTH + 4 * HEADS + 2 * B_WIDTH + C_WIDTH
IN_Z = QKV
IN_UV = IN_Z + A_WIDTH
IN_P = IN_UV + 2 * B_WIDTH
IN_BA = IN_P + C_WIDTH
IN_PAD = IN_BA + 128
NEG_BIG = -1e30

DELTA_BLOCK = 1024
POOL_TILE = 256
VMEM_LIMIT = 56 * 1024 * 1024


def _dot(a, b, precision=None):
    return jnp.dot(a, b, preferred_element_type=F32, precision=precision)


def _bdot(a, b):
    return jnp.dot(a.astype(BF16), b.astype(BF16), preferred_element_type=F32)


def _sigmoid(x):
    return 1.0 / (1.0 + jnp.exp(-x))


def _silu(x):
    return x * _sigmoid(x)


def _softplus(x):
    return jnp.maximum(x, 0.0) + jnp.log1p(jnp.exp(-jnp.abs(x)))


def _gelu_tanh(x):
    c = math.sqrt(2.0 / math.pi)
    return 0.5 * x * (1.0 + jnp.tanh(c * (x + 0.044715 * (x * x * x))))


def _rms_rows(x):
    return x * lax.rsqrt(jnp.mean(x * x, axis=-1, keepdims=True) + EPS)


def _params(n_grid):
    return pltpu.CompilerParams(dimension_semantics=("arbitrary",) * n_grid,
                                vmem_limit_bytes=VMEM_LIMIT)


def _const_spec(shape):
    nd = len(shape)
    return pl.BlockSpec(shape, lambda *_: (0,) * nd, pipeline_mode=pl.Buffered(1))


def _layer_spec(shape, layer):
    nd = len(shape) - 1
    return pl.BlockSpec((None,) + tuple(shape[1:]), lambda *_: (layer,) + (0,) * nd,
                        pipeline_mode=pl.Buffered(1))


def _mod_kernel(c_ref, w_ref, b_ref, o_ref):
    a = _silu(c_ref[...])
    o_ref[0] = _bdot(a, w_ref[0]) + b_ref[0]


def _modulation(cvec, w_mod, b_mod):
    tn = 1536
    n_out = N_MOD * D_MODEL
    return pl.pallas_call(
        _mod_kernel,
        grid=(DEPTH, n_out // tn),
        in_specs=[pl.BlockSpec((8, D_MODEL), lambda l, j: (0, 0)),
                  pl.BlockSpec((1, D_MODEL, tn), lambda l, j: (l, 0, j)),
                  pl.BlockSpec((1, 1, tn), lambda l, j: (l, 0, j))],
        out_specs=pl.BlockSpec((1, 8, tn), lambda l, j: (l, 0, j)),
        out_shape=jax.ShapeDtypeStruct((DEPTH, 8, n_out), F32),
        compiler_params=_params(2),
        name="modulation",
    )(cvec, w_mod, b_mod.reshape(DEPTH, 1, n_out))


POS_ROWS = 8


def _pos_kernel(x_ref, r_ref, c_ref, o_ref):
    half = D_MODEL // 2
    col = c_ref[...]
    for r in range(POS_ROWS):
        rs = slice(r * GRID_W, (r + 1) * GRID_W)
        o_ref[rs, 0:half] = x_ref[rs, 0:half] + r_ref[r:r + 1, :]
        o_ref[rs, half:] = x_ref[rs, half:] + col


def _add_pos(x, row_emb, col_emb):
    b, n, d = x.shape
    tm = POS_ROWS * GRID_W
    return pl.pallas_call(
        _pos_kernel,
        grid=(b, n // tm),
        in_specs=[pl.BlockSpec((None, tm, d), lambda i, j: (i, j, 0)),
                  pl.BlockSpec((POS_ROWS, d // 2), lambda i, j: (j, 0)),
                  pl.BlockSpec((GRID_W, d // 2), lambda i, j: (0, 0))],
        out_specs=pl.BlockSpec((None, tm, d), lambda i, j: (i, j, 0)),
        out_shape=jax.ShapeDtypeStruct(x.shape, F32),
        compiler_params=_params(2),
        name="add_pos",
    )(x, row_emb, col_emb)


def _modulated_norm(x, gain, shift, scale):
    return _rms_rows(x) * gain * (1.0 + scale) + shift


def _pack_w_in(wf_ref, wb_ref):
    n_logit = 4 * HEADS
    rows = 256
    lane = lax.broadcasted_iota(jnp.int32, (rows, 128), 1)
    for r in range(0, D_MODEL, rows):
        rs = slice(r, r + rows)
        wb_ref[rs, 0:IN_UV] = wf_ref[rs, 0:IN_UV].astype(BF16)
        tail = wf_ref[rs, IN_UV:IN_COLS]
        wb_ref[rs, IN_UV:IN_BA] = tail[:, n_logit:].astype(BF16)
        wb_ref[rs, IN_BA:IN_PAD] = jnp.where(lane < n_logit, tail[:, 0:128], 0.0).astype(BF16)


def _in_kernel(*refs, seq_len, tile, row0, halo):
    if halo:
        (x_ref, xp_ref, xn_ref, mod_ref, g_ref, wf_ref,
         cw_ref, qkv_ref, z_ref, ba_ref, uv_ref, p_ref, win_ref) = refs
    else:
        (x_ref, mod_ref, g_ref, wf_ref,
         cw_ref, qkv_ref, z_ref, ba_ref, uv_ref, p_ref, win_ref) = refs
    i = pl.program_id(0)

    @pl.when(i == 0)
    def _prepare_weights():
        _pack_w_in(wf_ref, win_ref)

    tiles_per_seq = max(seq_len // tile, 1)
    row = row0 + i // tiles_per_seq if row0 else 0
    m = mod_ref[pl.ds(row, 1), :]
    shift = m[:, 0:D_MODEL]
    scale = m[:, D_MODEL:2 * D_MODEL]
    gain = g_ref[...]
    hf = _modulated_norm(x_ref[...], gain, shift, scale)
    hb = hf.astype(BF16)
    if halo:
        hp = _modulated_norm(xp_ref[...], gain, shift, scale)
        hn = _modulated_norm(xn_ref[...], gain, shift, scale)
        hq = jnp.concatenate([hf, hp, hn], axis=0).astype(BF16)
    else:
        hq = hb
    ridx = lax.broadcasted_iota(jnp.int32, (tile, 1), 0)
    if halo:
        pos_in_seq = (i % tiles_per_seq) * tile + ridx
    else:
        pos_in_seq = ridx % seq_len
    first_row = ridx == 0
    last_row = ridx == tile - 1
    seq_start = pos_in_seq == 0
    seq_end = pos_in_seq == seq_len - 1
    cw = cw_ref[...]
    blk = 2 * HEAD_DIM

    def project(b):
        return _dot(hq, win_ref[:, b * blk:(b + 1) * blk])

    def conv_act(b, pre):
        cols = slice(b * blk, (b + 1) * blk)
        cur = pre[0:tile]
        prev = pltpu.roll(cur, 1, 0)
        nxt = pltpu.roll(cur, tile - 1, 0)
        if halo:
            prev = jnp.where(first_row, pre[tile + 7:tile + 8], prev)
            nxt = jnp.where(last_row, pre[tile + 8:tile + 9], nxt)
        prev = jnp.where(seq_start, 0.0, prev)
        nxt = jnp.where(seq_end, 0.0, nxt)
        act = _silu(prev * cw[0:1, cols] + cur * cw[1:2, cols] + nxt * cw[2:3, cols])
        if b * blk >= 2 * QK:
            qkv_ref[:, cols] = act.astype(BF16)
            return
        scale = HEAD_DIM ** -0.5 if b * blk < QK else 1.0
        for h in range(blk // HEAD_DIM):
            a = act[:, h * HEAD_DIM:(h + 1) * HEAD_DIM]
            inv = lax.rsqrt(jnp.sum(a * a, axis=-1, keepdims=True) + EPS) * scale
            c0 = b * blk + h * HEAD_DIM
            qkv_ref[:, c0:c0 + HEAD_DIM] = (a * inv).astype(BF16)

    others = []
    for out_ref, c0 in ((z_ref, IN_Z), (uv_ref, IN_UV), (p_ref, IN_P), (ba_ref, IN_BA)):
        width = out_ref.shape[1]
        others += [(out_ref, c0, slice(c, min(c + blk, width))) for c in range(0, width, blk)]
    n_blk = QKV // blk
    assert len(others) == n_blk
    pre = project(0)
    for b in range(n_blk):
        nxt_pre = project(b + 1) if b + 1 < n_blk else None
        out_ref, c0, cols = others[b]
        out_ref[:, cols] = _dot(hb, win_ref[:, c0 + cols.start:c0 + cols.stop])
        conv_act(b, pre)
        pre = nxt_pre


def _input_stage(x, mods_l, norm_g, w_in, layer, conv_w, *, seq_len, row0):
    n = x.shape[0]
    tile = 512
    halo = seq_len > tile
    tiles_per_seq = max(seq_len // tile, 1)
    n_tiles = n // tile
    h8 = tile // 8
    last8 = n // 8 - 1
    in_specs = [pl.BlockSpec((tile, D_MODEL), lambda i: (i, 0))]
    args = [x]
    if halo:
        in_specs += [pl.BlockSpec((8, D_MODEL), lambda i: (jnp.maximum(i * h8 - 1, 0), 0)),
                     pl.BlockSpec((8, D_MODEL), lambda i: (jnp.minimum((i + 1) * h8, last8), 0))]
        args += [x, x]
    in_specs += [_const_spec((8, N_MOD * D_MODEL)), _const_spec((1, D_MODEL)),
                 _layer_spec(w_in.shape, layer), _const_spec((3, QKV))]
    args += [mods_l, norm_g, w_in, conv_w]
    widths = (QKV, A_WIDTH, 128, 2 * B_WIDTH, C_WIDTH)
    dtypes = (BF16, F32, F32, F32, F32)
    return pl.pallas_call(
        functools.partial(_in_kernel, seq_len=seq_len, tile=tile, row0=row0, halo=halo),
        grid=(n_tiles,),
        in_specs=in_specs,
        out_specs=[pl.BlockSpec((tile, c), lambda i: (i, 0)) for c in widths],
        out_shape=[jax.ShapeDtypeStruct((n, c), t) for c, t in zip(widths, dtypes)],
        scratch_shapes=[pltpu.VMEM((D_MODEL, IN_PAD), BF16)],
        compiler_params=_params(1),
        name="input_stage",
    )(*args)


N_LEVELS = int(math.log2(CHUNK))
WIDE = HEADS * CHUNK


def _delta_constants():
    r = np.arange(CHUNK)[:, None]
    c = (np.arange(WIDE) % CHUNK)[None, :]
    lvl = np.zeros((2, N_LEVELS, CHUNK, WIDE), np.float32)
    for d in range(2):
        rr, cc = (r, c) if d == 0 else (c, r)
        for i in range(N_LEVELS):
            s = 1 << i
            lvl[d, i] = (rr // (2 * s) == cc // (2 * s)) & ((rr // s) % 2 == 1) & ((cc // s) % 2 == 0)
    eye = (r == c).astype(np.float32)
    tri = np.concatenate([np.tril(np.ones((CHUNK, CHUNK))), np.triu(np.ones((CHUNK, CHUNK)))], 0)
    return (jnp.asarray(lvl.reshape(2 * N_LEVELS, CHUNK, WIDE)), jnp.asarray(eye),
            jnp.asarray(tri, BF16))


def _pair(a, b, lt_half):
    return jnp.where(lt_half, a, b)


def _block_diag(blocks, zero):
    n = len(blocks)
    rows = [jnp.concatenate([blocks[i] if j == i else zero for j in range(n)], axis=1)
            for i in range(n)]
    return jnp.concatenate(rows, axis=0)


def _delta_kernel(*refs, n_sub, n_chunks, has_init, n_par, cb, layer, n_cast):
    it = iter(refs)
    qkv_ref, ba_ref, alog_ref, dtb_ref, lvl_ref, eye_ref, tri_ref = (next(it) for _ in range(7))
    s0_ref = next(it) if has_init else None
    sp_ref = next(it) if (not has_init and layer > 0) else None
    cast_in = [next(it) for _ in range(n_cast)]
    o_ref = next(it)
    st_ref = None if has_init else next(it)
    for src_ref in cast_in:
        dst_ref = next(it)
        dst_ref[...] = src_ref[...].astype(BF16)
    (s_ref, gc_ref, beta_ref, gct_ref, u_ref, w_ref, qd_ref, qkd_ref, kdt_ref, gt_ref) = it
    n_total = n_sub * n_chunks
    lane = lax.broadcasted_iota(jnp.int32, (CHUNK, 128), 1)
    lt_half = lane < CHUNK
    lt_half_row = lt_half[0:1]
    rw = lax.broadcasted_iota(jnp.int32, (CHUNK, WIDE), 0)
    cw = lax.broadcasted_iota(jnp.int32, (CHUNK, WIDE), 1) & (CHUNK - 1)
    incl = (rw >= cw, rw <= cw)
    a_row = -jnp.exp(alog_ref[...])
    dtb_row = dtb_ref[...]

    def gate_body(bi, carry):
        span = cb * CHUNK
        r0 = pl.multiple_of(bi * span, span)
        ba = ba_ref[pl.ds(r0, span), :]
        beta_ref[pl.ds(r0, span), :] = _sigmoid(ba)
        al = pltpu.roll(ba, 128 - 2 * HEADS, 1)
        lane_b = lax.broadcasted_iota(jnp.int32, (span, 128), 1)
        g = jnp.where(lane_b < 2 * HEADS, a_row * _softplus(al + dtb_row), 0.0)
        g_hi = g.astype(BF16).astype(F32)
        r1 = g - g_hi
        g_mid = r1.astype(BF16).astype(F32)
        g_lo = (r1 - g_mid).astype(BF16).astype(F32)
        g3 = (g_hi + pltpu.roll(g_mid, 8, 1) + pltpu.roll(g_lo, 16, 1)).astype(BF16)
        css = [_dot(tri_ref[...], g3[i * CHUNK:(i + 1) * CHUNK]) for i in range(cb)]
        css = [cs + pltpu.roll(cs, 128 - 8, 1) + pltpu.roll(cs, 128 - 16, 1) for cs in css]
        gcs = [jnp.where(lane < HEADS, cs[0:CHUNK], cs[CHUNK:]) for cs in css]
        ats = [jnp.concatenate([gc, gc], axis=0).T[0:8, :] for gc in gcs]
        for i in range(cb):
            gc_ref[pl.ds(pl.multiple_of(r0 + i * CHUNK, CHUNK), CHUNK), :] = gcs[i]
            gct_ref[bi * cb + i] = ats[i]
        return carry

    lax.fori_loop(0, n_total // cb, gate_body, 0)

    zero_blk = jnp.zeros((CHUNK, HEAD_DIM), BF16)
    hs = [slice(h * HEAD_DIM, (h + 1) * HEAD_DIM) for h in range(HEADS)]

    def head_mm(x, y):
        xb = x.astype(BF16)
        yb = y.astype(BF16)
        zero = jnp.zeros((CHUNK, 128), BF16)
        outs = []
        for pr in range(HEADS // 2):
            ys = yb[:, pr * 128:(pr + 1) * 128]
            bd = jnp.concatenate([jnp.where(lt_half, ys, zero), jnp.where(lt_half, zero, ys)], axis=0)
            outs.append(_dot(xb[:, pr * 128:(pr + 1) * 128], bd))
        return jnp.concatenate(outs, axis=1)

    def pre_body(bi, carry):
        groups = []
        for i in range(cb):
            c = bi * cb + i
            r0 = pl.multiple_of(c * CHUNK, CHUNK)
            rows = pl.ds(r0, CHUNK)
            qb = qkv_ref[rows, 0:QK]
            kb16 = qkv_ref[rows, QK:2 * QK]
            qf = qb.astype(F32)
            kf = kb16.astype(F32)
            vf = qkv_ref[rows, 2 * QK:].astype(F32)
            kt = jnp.concatenate(
                [jnp.concatenate([kf[:, 0:128], kf[:, 128:256]], axis=0).T,
                 jnp.concatenate([kf[:, 256:384], kf[:, 384:512]], axis=0).T], axis=1)
            ktbd = _group_rows(kt.astype(BF16), HEADS)
            qkk = _dot(jnp.concatenate([qb, kb16], axis=0), ktbd)
            cols = gc_ref[rows, :]
            bcols = beta_ref[rows, :]
            at = gct_ref[c]
            for d in range(2):
                j0 = d * HEADS
                bg = [jnp.broadcast_to(cols[:, j0 + h:j0 + h + 1], (CHUNK, 128)) for h in range(HEADS)]
                bb = [jnp.broadcast_to(bcols[:, j0 + h:j0 + h + 1], (CHUNK, 128)) for h in range(HEADS)]
                gcw = jnp.concatenate([_pair(bg[0], bg[1], lt_half), _pair(bg[2], bg[3], lt_half)], 1)
                bw = jnp.concatenate([_pair(bb[0], bb[1], lt_half), _pair(bb[2], bb[3], lt_half)], 1)
                gr = jnp.concatenate(
                    [_pair(at[j0:j0 + 1], at[j0 + 1:j0 + 2], lt_half_row),
                     _pair(at[j0 + 2:j0 + 3], at[j0 + 3:j0 + 4], lt_half_row)], axis=1)
                decay = jnp.exp(jnp.where(incl[d], gcw - gr, NEG_BIG))
                m = bw * qkk[CHUNK:] * decay
                qkd_ref[d, rows, :] = (qkk[0:CHUNK] * decay).astype(BF16)
                en = jnp.concatenate([jnp.exp(x) for x in bg], axis=1)
                bn = jnp.concatenate(bb, axis=1)
                qd_ref[d, rows, :] = (qf * en).astype(BF16)
                vb = (vf * bn).astype(BF16)
                kbe = (kf * (bn * en)).astype(BF16)
                e0 = CHUNK - 1 if d == 0 else 0
                bt = [jnp.broadcast_to(at[j0 + h:j0 + h + 1, e0:e0 + 1], (1, 128)) for h in range(HEADS)]
                tw = jnp.concatenate([_pair(bt[0], bt[1], lt_half_row),
                                      _pair(bt[2], bt[3], lt_half_row)], axis=1)
                kdt_ref[d, pl.ds(pl.multiple_of(c * HEAD_DIM, HEAD_DIM), HEAD_DIM), :] = (
                    kt * jnp.exp(tw - gr)).astype(BF16)
                gt_ref[d, c] = jnp.broadcast_to(
                    jnp.concatenate([jnp.exp(x) for x in bt], axis=1), (8, A_WIDTH))
                groups.append((d, rows, m, vb, kbe))
        ts = [eye_ref[...] - g[2] * lvl_ref[g[0] * N_LEVELS] for g in groups]
        for lv in range(1, N_LEVELS):
            xs = [head_mm(t, g[2] * lvl_ref[g[0] * N_LEVELS + lv]) for t, g in zip(ts, groups)]
            ts = [t - head_mm(x, t) for t, x in zip(ts, xs)]
        for t, (d, rows, _, vb, kbe) in zip(ts, groups):
            tb = t.astype(BF16)
            uw = []
            for pr in range(HEADS // 2):
                h0, h1 = hs[2 * pr], hs[2 * pr + 1]
                rhs = jnp.concatenate(
                    [jnp.concatenate([vb[:, h0], zero_blk, kbe[:, h0], zero_blk], axis=1),
                     jnp.concatenate([zero_blk, vb[:, h1], zero_blk, kbe[:, h1]], axis=1)], axis=0)
                uw.append(_dot(tb[:, pr * 128:(pr + 1) * 128], rhs))
            half = A_WIDTH // 2
            u_ref[d, rows, :] = jnp.concatenate([uw[0][:, 0:half], uw[1][:, 0:half]], axis=1)
            w_ref[d, rows, :] = jnp.concatenate([uw[0][:, half:], uw[1][:, half:]],
                                                axis=1).astype(BF16)
        return carry

    lax.fori_loop(0, n_total // cb, pre_body, 0)

    o_ref[...] = jnp.zeros(o_ref.shape, F32)
    zero_s = jnp.zeros((HEAD_DIM, HEAD_DIM), BF16)

    def scan_step(sub0, c):
        groups = [(p, d) for p in range(n_par) for d in range(2)]
        cidx = [(sub0 + p) * n_chunks + (c if d == 0 else n_chunks - 1 - c) for p, d in groups]
        rows = [pl.ds(pl.multiple_of(ci * CHUNK, CHUNK), CHUNK) for ci in cidx]
        ss = [s_ref[p, d] for p, d in groups]
        wqs = []
        for (p, d), r, s in zip(groups, rows, ss):
            sb = s.astype(BF16)
            lhs = jnp.concatenate([w_ref[d, r, :], qd_ref[d, r, :]], axis=0)
            halves = []
            for hp in range(HEADS // 2):
                bd = _block_diag([sb[:, hs[2 * hp]], sb[:, hs[2 * hp + 1]]], zero_s)
                halves.append(_dot(lhs[:, hp * 2 * HEAD_DIM:(hp + 1) * 2 * HEAD_DIM], bd))
            wqs.append(jnp.concatenate(halves, axis=1))
        boths = []
        for (p, d), r, ci, wq in zip(groups, rows, cidx, wqs):
            v_new = (u_ref[d, r, :] - wq[0:CHUNK]).astype(BF16)
            vbd = _block_diag([v_new[:, s] for s in hs], zero_blk)
            kd = kdt_ref[d, pl.ds(pl.multiple_of(ci * HEAD_DIM, HEAD_DIM), HEAD_DIM), :]
            lhs2 = jnp.concatenate([qkd_ref[d, r, :], kd], axis=0)
            boths.append(_dot(lhs2, vbd))
        for (p, d), r, ci, s, wq, both in zip(groups, rows, cidx, ss, wqs, boths):
            o_ref[r, :] += wq[CHUNK:] + both[0:CHUNK]
            s_ref[p, d] = s * gt_ref[d, ci][0:1] + both[CHUNK:]

    def sub_body(sb_i, carry):
        sub0 = sb_i * n_par
        for p in range(n_par):
            for d in range(2):
                if has_init:
                    s_ref[p, d] = jnp.concatenate(
                        [s0_ref[d, h] for h in range(HEADS)], axis=1)
                else:
                    s_ref[p, d] = jnp.zeros((HEAD_DIM, A_WIDTH), F32)

        def step(c, carry2):
            scan_step(sub0, c)
            return carry2

        lax.fori_loop(0, n_chunks, step, 0)
        if st_ref is not None:
            for p in range(n_par):
                for d in range(2):
                    s = s_ref[p, d]
                    for h in range(HEADS):
                        st_ref[sub0 + p, layer, d, h] = s[:, hs[h]]
                        for earlier in range(layer):
                            st_ref[sub0 + p, earlier, d, h] = sp_ref[sub0 + p, earlier, d, h]
        return carry

    lax.fori_loop(0, n_sub // n_par, sub_body, 0)


def _delta_stage(qkv, ba, alog_row, dtb_row, s0, consts, *, seq_len, layer=0, st_prev=None,
                 cast_weights=()):
    n = qkv.shape[0]
    block = max(seq_len, DELTA_BLOCK)
    n_blocks = n // block
    n_sub = block // seq_len
    n_chunks = seq_len // CHUNK
    n_total = block // CHUNK
    has_init = s0 is not None
    n_par = min(n_sub, 4)
    lvl, eye, tri = consts
    in_specs = [pl.BlockSpec((block, QKV), lambda i: (i, 0)),
                pl.BlockSpec((block, 128), lambda i: (i, 0)),
                _const_spec((1, 128)), _const_spec((1, 128)),
                _const_spec(lvl.shape), _const_spec(eye.shape), _const_spec(tri.shape)]
    args = [qkv, ba, alog_row, dtb_row, lvl, eye, tri]
    o_spec = pl.BlockSpec((block, A_WIDTH), lambda i: (i, 0))
    o_shape = jax.ShapeDtypeStruct((n, A_WIDTH), F32)
    st_shape = (2, HEADS, HEAD_DIM, HEAD_DIM)
    if has_init:
        assert n_sub == 1
        in_specs.append(pl.BlockSpec((None,) + st_shape, lambda i: (i, 0, 0, 0, 0)))
        args.append(s0)
        out_specs, out_shape = o_spec, o_shape
    else:
        if layer > 0:
            in_specs.append(pl.BlockSpec((n_sub, layer) + st_shape, lambda i: (i, 0, 0, 0, 0, 0)))
            args.append(st_prev)
        out_specs = [o_spec,
                     pl.BlockSpec((n_sub, layer + 1) + st_shape, lambda i: (i, 0, 0, 0, 0, 0))]
        out_shape = [o_shape, jax.ShapeDtypeStruct((n // seq_len, layer + 1) + st_shape, F32)]
    scratch = [pltpu.VMEM((n_par, 2, HEAD_DIM, A_WIDTH), F32),
               pltpu.VMEM((block, 128), F32),
               pltpu.VMEM((block, 128), F32),
               pltpu.VMEM((n_total, 8, 128), F32),
               pltpu.VMEM((2, block, A_WIDTH), F32),
               pltpu.VMEM((2, block, A_WIDTH), BF16),
               pltpu.VMEM((2, block, A_WIDTH), BF16),
               pltpu.VMEM((2, block, WIDE), BF16),
               pltpu.VMEM((2, n_total * HEAD_DIM, WIDE), BF16),
               pltpu.VMEM((2, n_total, 8, A_WIDTH), F32)]
    for wt in cast_weights:
        rows = wt.shape[1] // n_blocks
        assert rows * n_blocks == wt.shape[1] and rows % 16 == 0
        in_specs.append(pl.BlockSpec((None, rows, wt.shape[2]), lambda i: (layer, i, 0)))
        args.append(wt)
        out_specs = list(out_specs) if isinstance(out_specs, list) else [out_specs]
        out_shape = list(out_shape) if isinstance(out_shape, list) else [out_shape]
        out_specs.append(pl.BlockSpec((rows, wt.shape[2]), lambda i: (i, 0)))
        out_shape.append(jax.ShapeDtypeStruct(wt.shape[1:], BF16))
    return pl.pallas_call(
        functools.partial(_delta_kernel, n_sub=n_sub, n_chunks=n_chunks, has_init=has_init,
                          n_par=n_par, cb=4, layer=layer, n_cast=len(cast_weights)),
        grid=(n_blocks,),
        in_specs=in_specs,
        out_specs=out_specs,
        out_shape=out_shape,
        scratch_shapes=scratch,
        compiler_params=_params(1),
        name="delta_stage",
    )(*args)


def _group_rows(y, n_groups):
    half = y.shape[1] // 2
    gw = half // 2
    lane = lax.broadcasted_iota(jnp.int32, (y.shape[0], half), 1)
    zero = jnp.zeros((y.shape[0], half), y.dtype)
    blocks = []
    for g in range(n_groups):
        part = y[:, (g // 2) * half:(g // 2 + 1) * half]
        keep = (lane < gw) if g % 2 == 0 else (lane >= gw)
        part = jnp.where(keep, part, jnp.zeros_like(part))
        blocks.append(jnp.concatenate([part, zero] if g < 2 else [zero, part], axis=1))
    return jnp.concatenate(blocks, axis=0)


def _mix_delta(o_ref, z_ref, dg_ref, mix_ref):
    dg = dg_ref[...]
    for h in range(HEADS):
        cs = slice(h * HEAD_DIM, (h + 1) * HEAD_DIM)
        y = _rms_rows(o_ref[:, cs]) * dg * _silu(z_ref[:, cs])
        mix_ref[:, cs] = y.astype(BF16)


def _mix_sgu(uv_ref, sg_ref, ws_ref, bs_ref, seg_ref, mix_ref, tile):
    u = _gelu_tanh(uv_ref[:, 0:B_WIDTH])
    v = _gelu_tanh(uv_ref[:, B_WIDTH:])
    vv = v * v
    vv_hi = vv.astype(BF16)
    vv_lo = (vv - vv_hi.astype(F32)).astype(BF16)
    ms = _dot(jnp.concatenate([vv_hi, vv_lo], axis=1), seg_ref[...])
    vn = (v * lax.rsqrt(ms + EPS) * sg_ref[...]).astype(BF16)
    for c in range(tile // SGU_CHUNK):
        rs = slice(c * SGU_CHUNK, (c + 1) * SGU_CHUNK)
        s = _dot(ws_ref[...], _group_rows(vn[rs], B_GROUPS)) + bs_ref[...]
        mix_ref[rs, A_WIDTH:A_WIDTH + B_WIDTH] = (u[rs] * s).astype(BF16)


def _mix_pool(p_ref, band_ref, icnt_ref, wp_ref, ps_ref, mix_ref, tile):
    for c in range(tile // POOL_TILE):
        rs = slice(c * POOL_TILE, (c + 1) * POOL_TILE)
        x = p_ref[rs, :]
        x_hi = x.astype(BF16)
        x_lo = (x - x_hi.astype(F32)).astype(BF16)
        n_win = len(POOL_WINDOWS)
        rhs = jnp.concatenate([_group_rows(x_hi, n_win), _group_rows(x_lo, n_win)], axis=0)
        wsum = _dot(band_ref[...], rhs)
        diff = wsum * icnt_ref[...] - x
        y = _bdot(diff, wp_ref[...]) * ps_ref[...]
        mix_ref[rs, A_WIDTH + B_WIDTH:] = y.astype(BF16)


def _out_kernel(x_ref, o_ref, z_ref, uv_ref, p_ref, dg_ref, sg_ref, ws_ref, bs_ref, seg_ref,
                band_ref, icnt_ref, wp_ref, ps_ref, mod_ref, g2_ref, wof_ref, wgu_ref, wd_ref,
                nf_ref, y_ref, mix_ref, wo_ref, *, seq_len, tile, row0, final_norm):
    i = pl.program_id(0)

    @pl.when(i == 0)
    def _prepare_weights():
        rows = 256
        for r in range(0, D_MODEL, rows):
            wo_ref[r:r + rows, :] = wof_ref[r:r + rows, :].astype(BF16)

    tiles_per_seq = max(seq_len // tile, 1)
    row = row0 + i // tiles_per_seq if row0 else 0
    m = mod_ref[pl.ds(row, 1), :]
    gate1 = m[:, 2 * D_MODEL:3 * D_MODEL]
    shift2 = m[:, 3 * D_MODEL:4 * D_MODEL]
    scale2 = m[:, 4 * D_MODEL:5 * D_MODEL]
    gate2 = m[:, 5 * D_MODEL:6 * D_MODEL]
    _mix_delta(o_ref, z_ref, dg_ref, mix_ref)
    _mix_sgu(uv_ref, sg_ref, ws_ref, bs_ref, seg_ref, mix_ref, tile)
    _mix_pool(p_ref, band_ref, icnt_ref, wp_ref, ps_ref, mix_ref, tile)
    x1 = x_ref[...] + gate1 * _dot(mix_ref[...], wo_ref[...])
    hb = _modulated_norm(x1, g2_ref[...], shift2, scale2).astype(BF16)
    fc = FF_HIDDEN // FF_SPLIT
    ff = jnp.zeros((tile, D_MODEL), F32)
    for c in range(FF_SPLIT):
        gate = _dot(hb, wgu_ref[:, c * fc:(c + 1) * fc])
        up = _dot(hb, wgu_ref[:, FF_HIDDEN + c * fc:FF_HIDDEN + (c + 1) * fc])
        act = (_silu(gate) * up).astype(BF16)
        ff = ff + _dot(act, wd_ref[c * fc:(c + 1) * fc, :])
    x2 = x1 + gate2 * ff
    if final_norm:
        x2 = _rms_rows(x2) * nf_ref[...]
    y_ref[...] = x2


def _output_stage(x, o, z, uv, p, lw, pool_consts, mods_l, norm2_g, w_out, layer, w_gu_b,
                  w_down_b, norm_f, *, seq_len, row0, final_norm):
    n = x.shape[0]
    tile = 512
    band, icnt = pool_consts
    row = lambda c: pl.BlockSpec((tile, c), lambda i: (i, 0))
    small = [lw["delta_g"], lw["sgu_g"], lw["w_spatial"], lw["b_spatial"], lw["seg"], band, icnt,
             lw["w_pool"], lw["pool_scale"], mods_l, norm2_g]
    consts = small + [w_out, w_gu_b, w_down_b, norm_f]
    in_specs = ([row(D_MODEL), row(A_WIDTH), row(A_WIDTH), row(2 * B_WIDTH), row(C_WIDTH)]
                + [_const_spec(a.shape) for a in small]
                + [_layer_spec(w_out.shape, layer), _const_spec(w_gu_b.shape),
                   _const_spec(w_down_b.shape), _const_spec(norm_f.shape)])
    return pl.pallas_call(
        functools.partial(_out_kernel, seq_len=seq_len, tile=tile, row0=row0,
                          final_norm=final_norm),
        grid=(n // tile,),
        in_specs=in_specs,
        out_specs=row(D_MODEL),
        out_shape=jax.ShapeDtypeStruct((n, D_MODEL), F32),
        scratch_shapes=[pltpu.VMEM((tile, D_MODEL), BF16),
                        pltpu.VMEM((D_MODEL, D_MODEL), BF16)],
        compiler_params=_params(1),
        name="output_stage",
    )(x, o, z, uv, p, *consts)


def _pool_constants(seg_len):
    pos = np.arange(POOL_TILE)
    seg = pos // seg_len
    band = np.zeros((len(POOL_WINDOWS), POOL_TILE, POOL_TILE), np.float32)
    icnt = np.zeros((POOL_TILE, C_WIDTH), np.float32)
    for g, win in enumerate(POOL_WINDOWS):
        lo = pos - win // 2
        hi = pos + win - win // 2
        inside = (pos[None, :] >= lo[:, None]) & (pos[None, :] < hi[:, None]) \
            & (seg[None, :] == seg[:, None])
        band[g] = inside
        icnt[:, g * C_GC:(g + 1) * C_GC] = (1.0 / inside.sum(axis=1))[:, None]
    band_cat = np.concatenate([band[g] for g in range(len(POOL_WINDOWS))] * 2, axis=1)
    return jnp.asarray(band_cat, BF16), jnp.asarray(icnt)


def _grid_pos_tables(rows, d):
    quarter = d // 4
    f = np.float32
    omega = (f(1.0) / (f(10000.0) ** (np.arange(quarter, dtype=f) / f(quarter)))).astype(f)
    r = np.arange(rows, dtype=f)[:, None] * omega
    cl = np.arange(GRID_W, dtype=f)[:, None] * omega
    row_emb = np.concatenate([np.sin(r), np.cos(r)], axis=-1).astype(f)
    col_emb = np.concatenate([np.sin(cl), np.cos(cl)], axis=-1).astype(f)
    return jnp.asarray(row_emb), jnp.asarray(col_emb)


def _pad_lanes(a, width):
    return jnp.pad(a, ((0, 0), (0, width - a.shape[1])))


def _layer_weights(l, w_pool, w_spatial, b_spatial):
    wp = jnp.zeros((C_WIDTH, C_WIDTH), F32)
    for g in range(len(POOL_WINDOWS)):
        wp = wp.at[g * C_GC:(g + 1) * C_GC, g * C_GC:(g + 1) * C_GC].set(w_pool[l, g])
    grp = np.arange(B_WIDTH) // B_GC
    seg1 = (grp[:, None] == grp[None, :]).astype(np.float32) / B_GC
    seg = jnp.asarray(np.concatenate([seg1, seg1], axis=0), BF16)
    bs = jnp.repeat(b_spatial[l].T, B_GC, axis=1)
    ws_cat = jnp.concatenate([w_spatial[l, g] for g in range(B_GROUPS)], axis=1)
    return wp.astype(BF16), seg, ws_cat.astype(BF16), bs


def kernel(x_prompt, x_sample, state_delta, c, c_ctx, w_in, conv_w, a_log, dt_bias, delta_norm_g,
           sgu_norm_g, w_spatial, b_spatial, w_pool, pool_scale, w_out, norm1_g, norm2_g, w_mod,
           b_mod, w_gu, w_down, norm_f):
    batch, seq, d = x_prompt.shape
    dec_batch, dec_seq, _ = x_sample.shape
    cvec = jnp.concatenate([c_ctx[None, :], c, jnp.zeros((8 - 1 - dec_batch, d), F32)], axis=0)
    mods = _modulation(cvec, w_mod, b_mod)
    row_emb, col_emb = _grid_pos_tables(dec_seq // GRID_W, d)
    xs = _add_pos(x_sample, row_emb, col_emb).reshape(dec_batch * dec_seq, d)
    xc = x_prompt.reshape(batch * seq, d)
    pool_ctx = _pool_constants(min(seq, POOL_TILE))
    pool_lat = _pool_constants(GRID_W)
    nf = norm_f.reshape(1, d)
    delta_consts = _delta_constants()
    ctx_states = None
    for l in range(DEPTH):
        wp, seg, ws, bs = _layer_weights(l, w_pool, w_spatial, b_spatial)
        lw = {"delta_g": delta_norm_g[l].reshape(1, HEAD_DIM),
              "sgu_g": sgu_norm_g[l].reshape(1, B_WIDTH), "w_spatial": ws, "b_spatial": bs,
              "seg": seg, "w_pool": wp, "pool_scale": pool_scale[l].reshape(1, C_WIDTH)}
        alog_row = _pad_lanes(a_log[l].reshape(1, 2 * HEADS), 128)
        dtb_row = _pad_lanes(dt_bias[l].reshape(1, 2 * HEADS), 128)
        n1 = norm1_g[l].reshape(1, d)
        n2 = norm2_g[l].reshape(1, d)
        last = l == DEPTH - 1
        s0_lat = state_delta[:, l]
        streams = (("ctx", xc, seq, 0, None, pool_ctx), ("lat", xs, dec_seq, 1, s0_lat, pool_lat))
        outs = []
        for name, x, seq_len, row0, s0, pool_consts in streams:
            qkv, z, ba, uv, p = _input_stage(x, mods[l], n1, w_in, l, conv_w[l],
                                             seq_len=seq_len, row0=row0)
            if s0 is None:
                o, ctx_states, w_gu_b, w_down_b = _delta_stage(
                    qkv, ba, alog_row, dtb_row, None, delta_consts, seq_len=seq_len, layer=l,
                    st_prev=ctx_states, cast_weights=(w_gu, w_down))
            else:
                o = _delta_stage(qkv, ba, alog_row, dtb_row, s0, delta_consts, seq_len=seq_len)
            outs.append(_output_stage(x, o, z, uv, p, lw, pool_consts, mods[l], n2, w_out, l,
                                      w_gu_b, w_down_b, nf, seq_len=seq_len, row0=row0,
                                      final_norm=last))
        xc, xs = outs
    y_prompt = xc.reshape(batch, seq, d)
    y_sample = xs.reshape(dec_batch, dec_seq, d)
    return (y_prompt, y_sample, ctx_states)
```

```python
import functools
import math

import numpy as np
import jax
import jax.numpy as jnp
from jax import lax
from jax.experimental import pallas as pl
from jax.experimental.pallas import tpu as pltpu

F32 = jnp.float32
BF16 = jnp.bfloat16
HIGHEST = lax.Precision.HIGHEST

D_MODEL = 1024
DEPTH = 2
GRID_W = 64
HEADS = 4
HEAD_DIM = 128
QK = HEADS * HEAD_DIM
A_WIDTH = HEADS * HEAD_DIM
QKV = 2 * QK + A_WIDTH
CHUNK = 64
B_WIDTH = 256
B_GROUPS = 4
B_GC = 64
SGU_CHUNK = 128
C_WIDTH = 256
POOL_WINDOWS = (2, 4, 8, 16)
C_GC = 64
FF_HIDDEN = 2816
FF_SPLIT = 2
N_MOD = 6
EPS = 1e-6
IN_COLS = 2 * QK + 2 * A_WIDTH + 4 * HEADS + 2 * B_WIDTH + C_WIDTH
IN_Z = QKV
IN_UV = IN_Z + A_WIDTH
IN_P = IN_UV + 2 * B_WIDTH
IN_BA = IN_P + C_WIDTH
IN_PAD = IN_BA + 128
NEG_BIG = -1e30

DELTA_BLOCK = 1024
POOL_TILE = 256
VMEM_LIMIT = 56 * 1024 * 1024


def _dot(a, b, precision=None):
    return jnp.dot(a, b, preferred_element_type=F32, precision=precision)


def _bdot(a, b):
    return jnp.dot(a.astype(BF16), b.astype(BF16), preferred_element_type=F32)


def _sigmoid(x):
    return 0.5 * jnp.tanh(0.5 * x) + 0.5


def _silu(x):
    h = 0.5 * x
    return h + h * jnp.tanh(h)


def _softplus(x):
    return jnp.maximum(x, 0.0) + jnp.log1p(jnp.exp(-jnp.abs(x)))


def _gelu_tanh(x):
    c = math.sqrt(2.0 / math.pi)
    return 0.5 * x * (1.0 + jnp.tanh(c * (x + 0.044715 * (x * x * x))))


def _rms_rows(x):
    return x * lax.rsqrt(jnp.mean(x * x, axis=-1, keepdims=True) + EPS)


def _params(n_grid):
    return pltpu.CompilerParams(dimension_semantics=("arbitrary",) * n_grid,
                                vmem_limit_bytes=VMEM_LIMIT)


def _const_spec(shape):
    nd = len(shape)
    return pl.BlockSpec(shape, lambda *_: (0,) * nd, pipeline_mode=pl.Buffered(1))


def _layer_spec(shape, layer):
    nd = len(shape) - 1
    return pl.BlockSpec((None,) + tuple(shape[1:]), lambda *_: (layer,) + (0,) * nd,
                        pipeline_mode=pl.Buffered(1))


def _mod_kernel(c_ref, w_ref, b_ref, o_ref):
    a = _silu(c_ref[...])
    o_ref[0] = _bdot(a, w_ref[0]) + b_ref[0]


def _modulation(cvec, w_mod, b_mod):
    tn = 1536
    n_out = N_MOD * D_MODEL
    return pl.pallas_call(
        _mod_kernel,
        grid=(DEPTH, n_out // tn),
        in_specs=[pl.BlockSpec((8, D_MODEL), lambda l, j: (0, 0)),
                  pl.BlockSpec((1, D_MODEL, tn), lambda l, j: (l, 0, j)),
                  pl.BlockSpec((1, 1, tn), lambda l, j: (l, 0, j))],
        out_specs=pl.BlockSpec((1, 8, tn), lambda l, j: (l, 0, j)),
        out_shape=jax.ShapeDtypeStruct((DEPTH, 8, n_out), F32),
        compiler_params=_params(2),
        name="modulation",
    )(cvec, w_mod, b_mod.reshape(DEPTH, 1, n_out))


POS_ROWS = 8


def _pos_kernel(x_ref, r_ref, c_ref, o_ref):
    half = D_MODEL // 2
    col = c_ref[...]
    for r in range(POS_ROWS):
        rs = slice(r * GRID_W, (r + 1) * GRID_W)
        o_ref[rs, 0:half] = x_ref[rs, 0:half] + r_ref[r:r + 1, :]
        o_ref[rs, half:] = x_ref[rs, half:] + col


def _add_pos(x, row_emb, col_emb):
    b, n, d = x.shape
    tm = POS_ROWS * GRID_W
    return pl.pallas_call(
        _pos_kernel,
        grid=(b, n // tm),
        in_specs=[pl.BlockSpec((None, tm, d), lambda i, j: (i, j, 0)),
                  pl.BlockSpec((POS_ROWS, d // 2), lambda i, j: (j, 0)),
                  pl.BlockSpec((GRID_W, d // 2), lambda i, j: (0, 0))],
        out_specs=pl.BlockSpec((None, tm, d), lambda i, j: (i, j, 0)),
        out_shape=jax.ShapeDtypeStruct(x.shape, F32),
        compiler_params=_params(2),
        name="add_pos",
    )(x, row_emb, col_emb)


def _modulated_norm(x, gain, shift, scale):
    return _rms_rows(x) * gain * (1.0 + scale) + shift


def _pack_w_in(ws_ref, wb_ref):
    n_logit = 4 * HEADS
    rows = 256
    lane = lax.broadcasted_iota(jnp.int32, (rows, 128), 1)
    for r in range(0, D_MODEL, rows):
        rs = slice(r, r + rows)
        wb_ref[rs, 0:IN_UV] = ws_ref[rs, 0:IN_UV]
        tail = ws_ref[rs, IN_UV:IN_PAD]
        wb_ref[rs, IN_UV:IN_BA] = tail[:, n_logit:n_logit + IN_BA - IN_UV]
        wb_ref[rs, IN_BA:IN_PAD] = jnp.where(lane < n_logit, tail[:, 0:128],
                                             jnp.zeros((rows, 128), BF16))


def _in_kernel(*refs, seq_len, tile, row0, halo):
    if halo:
        (x_ref, xp_ref, xn_ref, mod_ref, g_ref, wf_ref,
         cw_ref, qkv_ref, z_ref, ba_ref, uv_ref, p_ref, win_ref) = refs
    else:
        (x_ref, mod_ref, g_ref, wf_ref,
         cw_ref, qkv_ref, z_ref, ba_ref, uv_ref, p_ref, win_ref) = refs
    i = pl.program_id(0)

    @pl.when(i == 0)
    def _prepare_weights():
        _pack_w_in(wf_ref, win_ref)

    tiles_per_seq = max(seq_len // tile, 1)
    row = row0 + i // tiles_per_seq if row0 else 0
    m = mod_ref[pl.ds(row, 1), :]
    shift = m[:, 0:D_MODEL]
    scale = m[:, D_MODEL:2 * D_MODEL]
    gain = g_ref[...]
    hf = _modulated_norm(x_ref[...], gain, shift, scale)
    hb = hf.astype(BF16)
    if halo:
        hp = _modulated_norm(xp_ref[...], gain, shift, scale)
        hn = _modulated_norm(xn_ref[...], gain, shift, scale)
        hq = jnp.concatenate([hf, hp, hn], axis=0).astype(BF16)
    else:
        hq = hb
    ridx = lax.broadcasted_iota(jnp.int32, (tile, 1), 0)
    if halo:
        pos_in_seq = (i % tiles_per_seq) * tile + ridx
    else:
        pos_in_seq = ridx % seq_len
    first_row = ridx == 0
    last_row = ridx == tile - 1
    seq_start = pos_in_seq == 0
    seq_end = pos_in_seq == seq_len - 1
    cw = cw_ref[...]
    blk = 2 * HEAD_DIM

    def project(b):
        return _dot(hq, win_ref[:, b * blk:(b + 1) * blk])

    def conv_act(b, pre):
        cols = slice(b * blk, (b + 1) * blk)
        cur = pre[0:tile]
        prev = pltpu.roll(cur, 1, 0)
        nxt = pltpu.roll(cur, tile - 1, 0)
        if halo:
            prev = jnp.where(first_row, pre[tile + 7:tile + 8], prev)
            nxt = jnp.where(last_row, pre[tile + 8:tile + 9], nxt)
        prev = jnp.where(seq_start, 0.0, prev)
        nxt = jnp.where(seq_end, 0.0, nxt)
        act = _silu(prev * cw[0:1, cols] + cur * cw[1:2, cols] + nxt * cw[2:3, cols])
        if b * blk >= 2 * QK:
            qkv_ref[:, cols] = act.astype(BF16)
            return
        scale = HEAD_DIM ** -0.5 if b * blk < QK else 1.0
        for h in range(blk // HEAD_DIM):
            a = act[:, h * HEAD_DIM:(h + 1) * HEAD_DIM]
            inv = lax.rsqrt(jnp.sum(a * a, axis=-1, keepdims=True) + EPS) * scale
            c0 = b * blk + h * HEAD_DIM
            qkv_ref[:, c0:c0 + HEAD_DIM] = (a * inv).astype(BF16)

    others = []
    for out_ref, c0 in ((z_ref, IN_Z), (uv_ref, IN_UV), (p_ref, IN_P), (ba_ref, IN_BA)):
        width = out_ref.shape[1]
        others += [(out_ref, c0, slice(c, min(c + blk, width))) for c in range(0, width, blk)]
    n_blk = QKV // blk
    assert len(others) == n_blk
    pre = project(0)
    for b in range(n_blk):
        nxt_pre = project(b + 1) if b + 1 < n_blk else None
        out_ref, c0, cols = others[b]
        out_ref[:, cols] = _dot(hb, win_ref[:, c0 + cols.start:c0 + cols.stop])
        conv_act(b, pre)
        pre = nxt_pre


def _input_stage(x, mods_l, norm_g, w_in, layer, conv_w, *, seq_len, row0):
    n = x.shape[0]
    tile = 512
    halo = seq_len > tile
    tiles_per_seq = max(seq_len // tile, 1)
    n_tiles = n // tile
    h8 = tile // 8
    last8 = n // 8 - 1
    in_specs = [pl.BlockSpec((tile, D_MODEL), lambda i: (i, 0))]
    args = [x]
    if halo:
        in_specs += [pl.BlockSpec((8, D_MODEL), lambda i: (jnp.maximum(i * h8 - 1, 0), 0)),
                     pl.BlockSpec((8, D_MODEL), lambda i: (jnp.minimum((i + 1) * h8, last8), 0))]
        args += [x, x]
    in_specs += [_const_spec((8, N_MOD * D_MODEL)), _const_spec((1, D_MODEL)),
                 _layer_spec(w_in.shape, layer), _const_spec((3, QKV))]
    args += [mods_l, norm_g, w_in, conv_w]
    widths = (QKV, A_WIDTH, 128, 2 * B_WIDTH, C_WIDTH)
    dtypes = (BF16, F32, F32, F32, F32)
    return pl.pallas_call(
        functools.partial(_in_kernel, seq_len=seq_len, tile=tile, row0=row0, halo=halo),
        grid=(n_tiles,),
        in_specs=in_specs,
        out_specs=[pl.BlockSpec((tile, c), lambda i: (i, 0)) for c in widths],
        out_shape=[jax.ShapeDtypeStruct((n, c), t) for c, t in zip(widths, dtypes)],
        scratch_shapes=[pltpu.VMEM((D_MODEL, IN_PAD), BF16)],
        compiler_params=_params(1),
        name="input_stage",
    )(*args)


N_LEVELS = int(math.log2(CHUNK))
WIDE = HEADS * CHUNK


def _delta_constants():
    r = np.arange(CHUNK)[:, None]
    c = (np.arange(WIDE) % CHUNK)[None, :]
    lvl = np.zeros((2, N_LEVELS, CHUNK, WIDE), np.float32)
    for d in range(2):
        rr, cc = (r, c) if d == 0 else (c, r)
        for i in range(N_LEVELS):
            s = 1 << i
            lvl[d, i] = (rr // (2 * s) == cc // (2 * s)) & ((rr // s) % 2 == 1) & ((cc // s) % 2 == 0)
    eye = (r == c).astype(np.float32)
    tri = np.concatenate([np.tril(np.ones((CHUNK, CHUNK))), np.triu(np.ones((CHUNK, CHUNK)))], 0)
    return (jnp.asarray(lvl.reshape(2 * N_LEVELS, CHUNK, WIDE)), jnp.asarray(eye),
            jnp.asarray(tri, BF16))


def _pair(a, b, lt_half):
    return jnp.where(lt_half, a, b)


def _block_diag(blocks, zero):
    n = len(blocks)
    rows = [jnp.concatenate([blocks[i] if j == i else zero for j in range(n)], axis=1)
            for i in range(n)]
    return jnp.concatenate(rows, axis=0)


def _delta_kernel(*refs, n_sub, n_chunks, has_init, n_par, cb, layer, n_cast):
    it = iter(refs)
    qkv_ref, ba_ref, alog_ref, dtb_ref, lvl_ref, eye_ref, tri_ref = (next(it) for _ in range(7))
    s0_ref = next(it) if has_init else None
    sp_ref = next(it) if (not has_init and layer > 0) else None
    cast_in = [next(it) for _ in range(n_cast)]
    o_ref = next(it)
    st_ref = None if has_init else next(it)
    for src_ref in cast_in:
        dst_ref = next(it)
        dst_ref[...] = src_ref[...].astype(BF16)
    (s_ref, gc_ref, beta_ref, gct_ref, u_ref, w_ref, qd_ref, qkd_ref, kdt_ref, gt_ref) = it
    n_total = n_sub * n_chunks
    lane = lax.broadcasted_iota(jnp.int32, (CHUNK, 128), 1)
    lt_half = lane < CHUNK
    lt_half_row = lt_half[0:1]
    rw = lax.broadcasted_iota(jnp.int32, (CHUNK, WIDE), 0)
    cw = lax.broadcasted_iota(jnp.int32, (CHUNK, WIDE), 1) & (CHUNK - 1)
    incl = (rw >= cw, rw <= cw)
    a_row = -jnp.exp(alog_ref[...])
    dtb_row = dtb_ref[...]

    def gate_body(bi, carry):
        span = cb * CHUNK
        r0 = pl.multiple_of(bi * span, span)
        ba = ba_ref[pl.ds(r0, span), :]
        beta_ref[pl.ds(r0, span), :] = _sigmoid(ba)
        al = pltpu.roll(ba, 128 - 2 * HEADS, 1)
        lane_b = lax.broadcasted_iota(jnp.int32, (span, 128), 1)
        g = jnp.where(lane_b < 2 * HEADS, a_row * _softplus(al + dtb_row), 0.0)
        g_hi = g.astype(BF16).astype(F32)
        r1 = g - g_hi
        g_mid = r1.astype(BF16).astype(F32)
        g_lo = (r1 - g_mid).astype(BF16).astype(F32)
        g3 = (g_hi + pltpu.roll(g_mid, 8, 1) + pltpu.roll(g_lo, 16, 1)).astype(BF16)
        css = [_dot(tri_ref[...], g3[i * CHUNK:(i + 1) * CHUNK]) for i in range(cb)]
        css = [cs + pltpu.roll(cs, 128 - 8, 1) + pltpu.roll(cs, 128 - 16, 1) for cs in css]
        gcs = [jnp.where(lane < HEADS, cs[0:CHUNK], cs[CHUNK:]) for cs in css]
        ats = [jnp.concatenate([gc, gc], axis=0).T[0:8, :] for gc in gcs]
        for i in range(cb):
            gc_ref[pl.ds(pl.multiple_of(r0 + i * CHUNK, CHUNK), CHUNK), :] = gcs[i]
            gct_ref[bi * cb + i] = ats[i]
        return carry

    lax.fori_loop(0, n_total // cb, gate_body, 0)

    zero_blk = jnp.zeros((CHUNK, HEAD_DIM), BF16)
    hs = [slice(h * HEAD_DIM, (h + 1) * HEAD_DIM) for h in range(HEADS)]

    def head_mm(x, y):
        xb = x.astype(BF16)
        yb = y.astype(BF16)
        zero = jnp.zeros((CHUNK, 128), BF16)
        outs = []
        for pr in range(HEADS // 2):
            ys = yb[:, pr * 128:(pr + 1) * 128]
            bd = jnp.concatenate([jnp.where(lt_half, ys, zero), jnp.where(lt_half, zero, ys)], axis=0)
            outs.append(_dot(xb[:, pr * 128:(pr + 1) * 128], bd))
        return jnp.concatenate(outs, axis=1)

    def pre_body(bi, carry):
        groups = []
        for i in range(cb):
            c = bi * cb + i
            r0 = pl.multiple_of(c * CHUNK, CHUNK)
            rows = pl.ds(r0, CHUNK)
            qb = qkv_ref[rows, 0:QK]
            kb16 = qkv_ref[rows, QK:2 * QK]
            qf = qb.astype(F32)
            kf = kb16.astype(F32)
            vf = qkv_ref[rows, 2 * QK:].astype(F32)
            kt = jnp.concatenate(
                [jnp.concatenate([kf[:, 0:128], kf[:, 128:256]], axis=0).T,
                 jnp.concatenate([kf[:, 256:384], kf[:, 384:512]], axis=0).T], axis=1)
            ktbd = _group_rows(kt.astype(BF16), HEADS)
            qkk = _dot(jnp.concatenate([qb, kb16], axis=0), ktbd)
            cols = gc_ref[rows, :]
            bcols = beta_ref[rows, :]
            at = gct_ref[c]
            for d in range(2):
                j0 = d * HEADS
                bg = [jnp.broadcast_to(cols[:, j0 + h:j0 + h + 1], (CHUNK, 128)) for h in range(HEADS)]
                bb = [jnp.broadcast_to(bcols[:, j0 + h:j0 + h + 1], (CHUNK, 128)) for h in range(HEADS)]
                gcw = jnp.concatenate([_pair(bg[0], bg[1], lt_half), _pair(bg[2], bg[3], lt_half)], 1)
                bw = jnp.concatenate([_pair(bb[0], bb[1], lt_half), _pair(bb[2], bb[3], lt_half)], 1)
                gr = jnp.concatenate(
                    [_pair(at[j0:j0 + 1], at[j0 + 1:j0 + 2], lt_half_row),
                     _pair(at[j0 + 2:j0 + 3], at[j0 + 3:j0 + 4], lt_half_row)], axis=1)
                decay = jnp.exp(jnp.where(incl[d], gcw - gr, NEG_BIG))
                m = bw * qkk[CHUNK:] * decay
                qkd_ref[d, rows, :] = (qkk[0:CHUNK] * decay).astype(BF16)
                en = jnp.concatenate([jnp.exp(x) for x in bg], axis=1)
                bn = jnp.concatenate(bb, axis=1)
                qd_ref[d, rows, :] = (qf * en).astype(BF16)
                vb = (vf * bn).astype(BF16)
                kbe = (kf * (bn * en)).astype(BF16)
                e0 = CHUNK - 1 if d == 0 else 0
                bt = [jnp.broadcast_to(at[j0 + h:j0 + h + 1, e0:e0 + 1], (1, 128)) for h in range(HEADS)]
                tw = jnp.concatenate([_pair(bt[0], bt[1], lt_half_row),
                                      _pair(bt[2], bt[3], lt_half_row)], axis=1)
                kdt_ref[d, pl.ds(pl.multiple_of(c * HEAD_DIM, HEAD_DIM), HEAD_DIM), :] = (
                    kt * jnp.exp(tw - gr)).astype(BF16)
                gt_ref[d, c] = jnp.broadcast_to(
                    jnp.concatenate([jnp.exp(x) for x in bt], axis=1), (8, A_WIDTH))
                groups.append((d, rows, m, vb, kbe))
        ts = [eye_ref[...] - g[2] * lvl_ref[g[0] * N_LEVELS] for g in groups]
        for lv in range(1, N_LEVELS):
            xs = [head_mm(t, g[2] * lvl_ref[g[0] * N_LEVELS + lv]) for t, g in zip(ts, groups)]
            ts = [t - head_mm(x, t) for t, x in zip(ts, xs)]
        for t, (d, rows, _, vb, kbe) in zip(ts, groups):
            tb = t.astype(BF16)
            uw = []
            for pr in range(HEADS // 2):
                h0, h1 = hs[2 * pr], hs[2 * pr + 1]
                rhs = jnp.concatenate(
                    [jnp.concatenate([vb[:, h0], zero_blk, kbe[:, h0], zero_blk], axis=1),
                     jnp.concatenate([zero_blk, vb[:, h1], zero_blk, kbe[:, h1]], axis=1)], axis=0)
                uw.append(_dot(tb[:, pr * 128:(pr + 1) * 128], rhs))
            half = A_WIDTH // 2
            u_ref[d, rows, :] = jnp.concatenate([uw[0][:, 0:half], uw[1][:, 0:half]], axis=1)
            w_ref[d, rows, :] = jnp.concatenate([uw[0][:, half:], uw[1][:, half:]],
                                                axis=1).astype(BF16)
        return carry

    lax.fori_loop(0, n_total // cb, pre_body, 0)

    o_ref[...] = jnp.zeros(o_ref.shape, F32)
    zero_s = jnp.zeros((HEAD_DIM, HEAD_DIM), BF16)

    def scan_step(sub0, c):
        groups = [(p, d) for p in range(n_par) for d in range(2)]
        cidx = [(sub0 + p) * n_chunks + (c if d == 0 else n_chunks - 1 - c) for p, d in groups]
        rows = [pl.ds(pl.multiple_of(ci * CHUNK, CHUNK), CHUNK) for ci in cidx]
        ss = [s_ref[p, d] for p, d in groups]
        wqs = []
        for (p, d), r, s in zip(groups, rows, ss):
            sb = s.astype(BF16)
            lhs = jnp.concatenate([w_ref[d, r, :], qd_ref[d, r, :]], axis=0)
            halves = []
            for hp in range(HEADS // 2):
                bd = _block_diag([sb[:, hs[2 * hp]], sb[:, hs[2 * hp + 1]]], zero_s)
                halves.append(_dot(lhs[:, hp * 2 * HEAD_DIM:(hp + 1) * 2 * HEAD_DIM], bd))
            wqs.append(jnp.concatenate(halves, axis=1))
        boths = []
        for (p, d), r, ci, wq in zip(groups, rows, cidx, wqs):
            v_new = (u_ref[d, r, :] - wq[0:CHUNK]).astype(BF16)
            vbd = _block_diag([v_new[:, s] for s in hs], zero_blk)
            kd = kdt_ref[d, pl.ds(pl.multiple_of(ci * HEAD_DIM, HEAD_DIM), HEAD_DIM), :]
            lhs2 = jnp.concatenate([qkd_ref[d, r, :], kd], axis=0)
            boths.append(_dot(lhs2, vbd))
        for (p, d), r, ci, s, wq, both in zip(groups, rows, cidx, ss, wqs, boths):
            o_ref[r, :] += wq[CHUNK:] + both[0:CHUNK]
            s_ref[p, d] = s * gt_ref[d, ci][0:1] + both[CHUNK:]

    def sub_body(sb_i, carry):
        sub0 = sb_i * n_par
        for p in range(n_par):
            for d in range(2):
                if has_init:
                    s_ref[p, d] = jnp.concatenate(
                        [s0_ref[d, h] for h in range(HEADS)], axis=1)
                else:
                    s_ref[p, d] = jnp.zeros((HEAD_DIM, A_WIDTH), F32)

        def step(c, carry2):
            scan_step(sub0, c)
            return carry2

        lax.fori_loop(0, n_chunks, step, 0)
        if st_ref is not None:
            for p in range(n_par):
                for d in range(2):
                    s = s_ref[p, d]
                    for h in range(HEADS):
                        st_ref[sub0 + p, layer, d, h] = s[:, hs[h]]
                        for earlier in range(layer):
                            st_ref[sub0 + p, earlier, d, h] = sp_ref[sub0 + p, earlier, d, h]
        return carry

    lax.fori_loop(0, n_sub // n_par, sub_body, 0)


def _delta_stage(qkv, ba, alog_row, dtb_row, s0, consts, *, seq_len, layer=0, st_prev=None,
                 cast_weights=()):
    n = qkv.shape[0]
    block = max(seq_len, DELTA_BLOCK)
    n_blocks = n // block
    n_sub = block // seq_len
    n_chunks = seq_len // CHUNK
    n_total = block // CHUNK
    has_init = s0 is not None
    n_par = min(n_sub, 4)
    lvl, eye, tri = consts
    in_specs = [pl.BlockSpec((block, QKV), lambda i: (i, 0)),
                pl.BlockSpec((block, 128), lambda i: (i, 0)),
                _const_spec((1, 128)), _const_spec((1, 128)),
                _const_spec(lvl.shape), _const_spec(eye.shape), _const_spec(tri.shape)]
    args = [qkv, ba, alog_row, dtb_row, lvl, eye, tri]
    o_spec = pl.BlockSpec((block, A_WIDTH), lambda i: (i, 0))
    o_shape = jax.ShapeDtypeStruct((n, A_WIDTH), F32)
    st_shape = (2, HEADS, HEAD_DIM, HEAD_DIM)
    if has_init:
        assert n_sub == 1
        in_specs.append(pl.BlockSpec((None,) + st_shape, lambda i: (i, 0, 0, 0, 0)))
        args.append(s0)
        out_specs, out_shape = o_spec, o_shape
    else:
        if layer > 0:
            in_specs.append(pl.BlockSpec((n_sub, layer) + st_shape, lambda i: (i, 0, 0, 0, 0, 0)))
            args.append(st_prev)
        out_specs = [o_spec,
                     pl.BlockSpec((n_sub, layer + 1) + st_shape, lambda i: (i, 0, 0, 0, 0, 0))]
        out_shape = [o_shape, jax.ShapeDtypeStruct((n // seq_len, layer + 1) + st_shape, F32)]
    scratch = [pltpu.VMEM((n_par, 2, HEAD_DIM, A_WIDTH), F32),
               pltpu.VMEM((block, 128), F32),
               pltpu.VMEM((block, 128), F32),
               pltpu.VMEM((n_total, 8, 128), F32),
               pltpu.VMEM((2, block, A_WIDTH), F32),
               pltpu.VMEM((2, block, A_WIDTH), BF16),
               pltpu.VMEM((2, block, A_WIDTH), BF16),
               pltpu.VMEM((2, block, WIDE), BF16),
               pltpu.VMEM((2, n_total * HEAD_DIM, WIDE), BF16),
               pltpu.VMEM((2, n_total, 8, A_WIDTH), F32)]
    for wt in cast_weights:
        rows = wt.shape[1] // n_blocks
        assert rows * n_blocks == wt.shape[1] and rows % 16 == 0
        in_specs.append(pl.BlockSpec((None, rows, wt.shape[2]), lambda i: (layer, i, 0)))
        args.append(wt)
        out_specs = list(out_specs) if isinstance(out_specs, list) else [out_specs]
        out_shape = list(out_shape) if isinstance(out_shape, list) else [out_shape]
        out_specs.append(pl.BlockSpec((rows, wt.shape[2]), lambda i: (i, 0)))
        out_shape.append(jax.ShapeDtypeStruct(wt.shape[1:], BF16))
    return pl.pallas_call(
        functools.partial(_delta_kernel, n_sub=n_sub, n_chunks=n_chunks, has_init=has_init,
                          n_par=n_par, cb=4, layer=layer, n_cast=len(cast_weights)),
        grid=(n_blocks,),
        in_specs=in_specs,
        out_specs=out_specs,
        out_shape=out_shape,
        scratch_shapes=scratch,
        compiler_params=_params(1),
        name="delta_stage",
    )(*args)


def _group_rows(y, n_groups):
    half = y.shape[1] // 2
    gw = half // 2
    lane = lax.broadcasted_iota(jnp.int32, (y.shape[0], half), 1)
    zero = jnp.zeros((y.shape[0], half), y.dtype)
    blocks = []
    for g in range(n_groups):
        part = y[:, (g // 2) * half:(g // 2 + 1) * half]
        keep = (lane < gw) if g % 2 == 0 else (lane >= gw)
        part = jnp.where(keep, part, jnp.zeros_like(part))
        blocks.append(jnp.concatenate([part, zero] if g < 2 else [zero, part], axis=1))
    return jnp.concatenate(blocks, axis=0)


def _mix_delta(o_ref, z_ref, dg_ref, mix_ref):
    dg = dg_ref[...]
    for h in range(HEADS):
        cs = slice(h * HEAD_DIM, (h + 1) * HEAD_DIM)
        y = _rms_rows(o_ref[:, cs]) * dg * _silu(z_ref[:, cs])
        mix_ref[:, cs] = y.astype(BF16)


def _mix_sgu(uv_ref, sg_ref, ws_ref, bs_ref, seg_ref, mix_ref, tile):
    u = _gelu_tanh(uv_ref[:, 0:B_WIDTH])
    v = _gelu_tanh(uv_ref[:, B_WIDTH:])
    vv = v * v
    vv_hi = vv.astype(BF16)
    vv_lo = (vv - vv_hi.astype(F32)).astype(BF16)
    ms = _dot(jnp.concatenate([vv_hi, vv_lo], axis=1), seg_ref[...])
    vn = (v * lax.rsqrt(ms + EPS) * sg_ref[...]).astype(BF16)
    for c in range(tile // SGU_CHUNK):
        rs = slice(c * SGU_CHUNK, (c + 1) * SGU_CHUNK)
        s = _dot(ws_ref[...], _group_rows(vn[rs], B_GROUPS)) + bs_ref[...]
        mix_ref[rs, A_WIDTH:A_WIDTH + B_WIDTH] = (u[rs] * s).astype(BF16)


def _mix_pool(p_ref, band_ref, icnt_ref, wp_ref, ps_ref, mix_ref, tile):
    for c in range(tile // POOL_TILE):
        rs = slice(c * POOL_TILE, (c + 1) * POOL_TILE)
        x = p_ref[rs, :]
        x_hi = x.astype(BF16)
        x_lo = (x - x_hi.astype(F32)).astype(BF16)
        n_win = len(POOL_WINDOWS)
        rhs = jnp.concatenate([_group_rows(x_hi, n_win), _group_rows(x_lo, n_win)], axis=0)
        wsum = _dot(band_ref[...], rhs)
        diff = wsum * icnt_ref[...] - x
        y = _bdot(diff, wp_ref[...]) * ps_ref[...]
        mix_ref[rs, A_WIDTH + B_WIDTH:] = y.astype(BF16)


def _out_kernel(x_ref, o_ref, z_ref, uv_ref, p_ref, dg_ref, sg_ref, ws_ref, bs_ref, seg_ref,
                band_ref, icnt_ref, wp_ref, ps_ref, mod_ref, g2_ref, wof_ref, wgu_ref, wd_ref,
                nf_ref, y_ref, mix_ref, wo_ref, *, seq_len, tile, row0, final_norm):
    i = pl.program_id(0)

    @pl.when(i == 0)
    def _prepare_weights():
        rows = 256
        for r in range(0, D_MODEL, rows):
            wo_ref[r:r + rows, :] = wof_ref[r:r + rows, :].astype(BF16)

    tiles_per_seq = max(seq_len // tile, 1)
    row = row0 + i // tiles_per_seq if row0 else 0
    m = mod_ref[pl.ds(row, 1), :]
    gate1 = m[:, 2 * D_MODEL:3 * D_MODEL]
    shift2 = m[:, 3 * D_MODEL:4 * D_MODEL]
    scale2 = m[:, 4 * D_MODEL:5 * D_MODEL]
    gate2 = m[:, 5 * D_MODEL:6 * D_MODEL]
    _mix_delta(o_ref, z_ref, dg_ref, mix_ref)
    _mix_sgu(uv_ref, sg_ref, ws_ref, bs_ref, seg_ref, mix_ref, tile)
    _mix_pool(p_ref, band_ref, icnt_ref, wp_ref, ps_ref, mix_ref, tile)
    x1 = x_ref[...] + gate1 * _dot(mix_ref[...], wo_ref[...])
    hb = _modulated_norm(x1, g2_ref[...], shift2, scale2).astype(BF16)
    fc = FF_HIDDEN // FF_SPLIT
    ff = jnp.zeros((tile, D_MODEL), F32)
    for c in range(FF_SPLIT):
        gate = _dot(hb, wgu_ref[:, c * fc:(c + 1) * fc])
        up = _dot(hb, wgu_ref[:, FF_HIDDEN + c * fc:FF_HIDDEN + (c + 1) * fc])
        act = (_silu(gate) * up).astype(BF16)
        ff = ff + _dot(act, wd_ref[c * fc:(c + 1) * fc, :])
    x2 = x1 + gate2 * ff
    if final_norm:
        x2 = _rms_rows(x2) * nf_ref[...]
    y_ref[...] = x2


def _output_stage(x, o, z, uv, p, lw, pool_consts, mods_l, norm2_g, w_out, layer, w_gu_b,
                  w_down_b, norm_f, *, seq_len, row0, final_norm):
    n = x.shape[0]
    tile = 512
    band, icnt = pool_consts
    row = lambda c: pl.BlockSpec((tile, c), lambda i: (i, 0))
    small = [lw["delta_g"], lw["sgu_g"], lw["w_spatial"], lw["b_spatial"], lw["seg"], band, icnt,
             lw["w_pool"], lw["pool_scale"], mods_l, norm2_g]
    consts = small + [w_out, w_gu_b, w_down_b, norm_f]
    in_specs = ([row(D_MODEL), row(A_WIDTH), row(A_WIDTH), row(2 * B_WIDTH), row(C_WIDTH)]
                + [_const_spec(a.shape) for a in small]
                + [_layer_spec(w_out.shape, layer), _const_spec(w_gu_b.shape),
                   _const_spec(w_down_b.shape), _const_spec(norm_f.shape)])
    return pl.pallas_call(
        functools.partial(_out_kernel, seq_len=seq_len, tile=tile, row0=row0,
                          final_norm=final_norm),
        grid=(n // tile,),
        in_specs=in_specs,
        out_specs=row(D_MODEL),
        out_shape=jax.ShapeDtypeStruct((n, D_MODEL), F32),
        scratch_shapes=[pltpu.VMEM((tile, D_MODEL), BF16),
                        pltpu.VMEM((D_MODEL, D_MODEL), BF16)],
        compiler_params=_params(1),
        name="output_stage",
    )(x, o, z, uv, p, *consts)


def _pool_constants(seg_len):
    pos = np.arange(POOL_TILE)
    seg = pos // seg_len
    band = np.zeros((len(POOL_WINDOWS), POOL_TILE, POOL_TILE), np.float32)
    icnt = np.zeros((POOL_TILE, C_WIDTH), np.float32)
    for g, win in enumerate(POOL_WINDOWS):
        lo = pos - win // 2
        hi = pos + win - win // 2
        inside = (pos[None, :] >= lo[:, None]) & (pos[None, :] < hi[:, None]) \
            & (seg[None, :] == seg[:, None])
        band[g] = inside
        icnt[:, g * C_GC:(g + 1) * C_GC] = (1.0 / inside.sum(axis=1))[:, None]
    band_cat = np.concatenate([band[g] for g in range(len(POOL_WINDOWS))] * 2, axis=1)
    return jnp.asarray(band_cat, BF16), jnp.asarray(icnt)


def _grid_pos_tables(rows, d):
    quarter = d // 4
    f = np.float32
    omega = (f(1.0) / (f(10000.0) ** (np.arange(quarter, dtype=f) / f(quarter)))).astype(f)
    r = np.arange(rows, dtype=f)[:, None] * omega
    cl = np.arange(GRID_W, dtype=f)[:, None] * omega
    row_emb = np.concatenate([np.sin(r), np.cos(r)], axis=-1).astype(f)
    col_emb = np.concatenate([np.sin(cl), np.cos(cl)], axis=-1).astype(f)
    return jnp.asarray(row_emb), jnp.asarray(col_emb)


def _pad_lanes(a, width):
    return jnp.pad(a, ((0, 0), (0, width - a.shape[1])))


def _layer_weights(l, w_pool, w_spatial, b_spatial):
    wp = jnp.zeros((C_WIDTH, C_WIDTH), F32)
    for g in range(len(POOL_WINDOWS)):
        wp = wp.at[g * C_GC:(g + 1) * C_GC, g * C_GC:(g + 1) * C_GC].set(w_pool[l, g])
    grp = np.arange(B_WIDTH) // B_GC
    seg1 = (grp[:, None] == grp[None, :]).astype(np.float32) / B_GC
    seg = jnp.asarray(np.concatenate([seg1, seg1], axis=0), BF16)
    bs = jnp.repeat(b_spatial[l].T, B_GC, axis=1)
    ws_cat = jnp.concatenate([w_spatial[l, g] for g in range(B_GROUPS)], axis=1)
    return wp.astype(BF16), seg, ws_cat.astype(BF16), bs


def kernel(x_prompt, x_sample, state_delta, c, c_ctx, w_in, conv_w, a_log, dt_bias, delta_norm_g,
           sgu_norm_g, w_spatial, b_spatial, w_pool, pool_scale, w_out, norm1_g, norm2_g, w_mod,
           b_mod, w_gu, w_down, norm_f):
    batch, seq, d = x_prompt.shape
    dec_batch, dec_seq, _ = x_sample.shape
    cvec = jnp.concatenate([c_ctx[None, :], c, jnp.zeros((8 - 1 - dec_batch, d), F32)], axis=0)
    mods = _modulation(cvec, w_mod, b_mod)
    row_emb, col_emb = _grid_pos_tables(dec_seq // GRID_W, d)
    xs = _add_pos(x_sample, row_emb, col_emb).reshape(dec_batch * dec_seq, d)
    xc = x_prompt.reshape(batch * seq, d)
    pool_ctx = _pool_constants(min(seq, POOL_TILE))
    pool_lat = _pool_constants(GRID_W)
    nf = norm_f.reshape(1, d)
    delta_consts = _delta_constants()
    w_in_b = jnp.pad(w_in, ((0, 0), (0, 0), (0, IN_PAD - IN_COLS))).astype(BF16)
    ctx_states = None
    for l in range(DEPTH):
        wp, seg, ws, bs = _layer_weights(l, w_pool, w_spatial, b_spatial)
        lw = {"delta_g": delta_norm_g[l].reshape(1, HEAD_DIM),
              "sgu_g": sgu_norm_g[l].reshape(1, B_WIDTH), "w_spatial": ws, "b_spatial": bs,
              "seg": seg, "w_pool": wp, "pool_scale": pool_scale[l].reshape(1, C_WIDTH)}
        alog_row = _pad_lanes(a_log[l].reshape(1, 2 * HEADS), 128)
        dtb_row = _pad_lanes(dt_bias[l].reshape(1, 2 * HEADS), 128)
        n1 = norm1_g[l].reshape(1, d)
        n2 = norm2_g[l].reshape(1, d)
        last = l == DEPTH - 1
        s0_lat = state_delta[:, l]
        streams = (("ctx", xc, seq, 0, None, pool_ctx), ("lat", xs, dec_seq, 1, s0_lat, pool_lat))
        outs = []
        for name, x, seq_len, row0, s0, pool_consts in streams:
            qkv, z, ba, uv, p = _input_stage(x, mods[l], n1, w_in_b, l, conv_w[l],
                                             seq_len=seq_len, row0=row0)
            if s0 is None:
                o, ctx_states, w_gu_b, w_down_b = _delta_stage(
                    qkv, ba, alog_row, dtb_row, None, delta_consts, seq_len=seq_len, layer=l,
                    st_prev=ctx_states, cast_weights=(w_gu, w_down))
            else:
                o = _delta_stage(qkv, ba, alog_row, dtb_row, s0, delta_consts, seq_len=seq_len)
            outs.append(_output_stage(x, o, z, uv, p, lw, pool_consts, mods[l], n2, w_out, l,
                                      w_gu_b, w_down_b, nf, seq_len=seq_len, row0=row0,
                                      final_norm=last))
        xc, xs = outs
    y_prompt = xc.reshape(batch, seq, d)
    y_sample = xs.reshape(dec_batch, dec_seq, d)
    return (y_prompt, y_sample, ctx_states)
```

```python
import functools
import math

import numpy as np
import jax
import jax.numpy as jnp
from jax import lax
from jax.experimental import pallas as pl
from jax.experimental.pallas import tpu as pltpu

F32 = jnp.float32
BF16 = jnp.bfloat16
HIGHEST = lax.Precision.HIGHEST

D_MODEL = 1024
DEPTH = 2
GRID_W = 64
HEADS = 4
HEAD_DIM = 128
QK = HEADS * HEAD_DIM
A_WIDTH = HEADS * HEAD_DIM
QKV = 2 * QK + A_WIDTH
CHUNK = 64
B_WIDTH = 256
B_GROUPS = 4
B_GC = 64
SGU_CHUNK = 128
C_WIDTH = 256
POOL_WINDOWS = (2, 4, 8, 16)
C_GC = 64
FF_HIDDEN = 2816
FF_SPLIT = 2
N_MOD = 6
EPS = 1e-6
IN_COLS = 2 * QK + 2 * A_WIDTH + 4 * HEADS + 2 * B_WIDTH + C_WIDTH
IN_Z = QKV
IN_UV = IN_Z + A_WIDTH
IN_P = IN_UV + 2 * B_WIDTH
IN_BA = IN_P + C_WIDTH
IN_PAD = IN_BA + 128
NEG_BIG = -1e30

DELTA_BLOCK = 1024
POOL_TILE = 256
VMEM_LIMIT = 58 * 1024 * 1024


def _dot(a, b, precision=None):
    return jnp.dot(a, b, preferred_element_type=F32, precision=precision)


def _bdot(a, b):
    return jnp.dot(a.astype(BF16), b.astype(BF16), preferred_element_type=F32)


def _sigmoid(x):
    return 0.5 * jnp.tanh(0.5 * x) + 0.5


def _silu(x):
    h = 0.5 * x
    return h + h * jnp.tanh(h)


def _softplus(x):
    return jnp.maximum(x, 0.0) + jnp.log1p(jnp.exp(-jnp.abs(x)))


def _gelu_tanh(x):
    c = math.sqrt(2.0 / math.pi)
    return 0.5 * x * (1.0 + jnp.tanh(c * (x + 0.044715 * (x * x * x))))


def _rms_rows(x):
    return x * lax.rsqrt(jnp.mean(x * x, axis=-1, keepdims=True) + EPS)


def _params(n_grid):
    return pltpu.CompilerParams(dimension_semantics=("arbitrary",) * n_grid,
                                vmem_limit_bytes=VMEM_LIMIT)


def _const_spec(shape):
    nd = len(shape)
    return pl.BlockSpec(shape, lambda *_: (0,) * nd, pipeline_mode=pl.Buffered(1))


def _layer_spec(shape, layer):
    nd = len(shape) - 1
    return pl.BlockSpec((None,) + tuple(shape[1:]), lambda *_: (layer,) + (0,) * nd,
                        pipeline_mode=pl.Buffered(1))


def _mod_kernel(c_ref, w_ref, b_ref, o_ref):
    a = _silu(c_ref[...])
    o_ref[0] = _bdot(a, w_ref[0]) + b_ref[0]


def _modulation(cvec, w_mod, b_mod):
    tn = 1536
    n_out = N_MOD * D_MODEL
    return pl.pallas_call(
        _mod_kernel,
        grid=(DEPTH, n_out // tn),
        in_specs=[pl.BlockSpec((8, D_MODEL), lambda l, j: (0, 0)),
                  pl.BlockSpec((1, D_MODEL, tn), lambda l, j: (l, 0, j)),
                  pl.BlockSpec((1, 1, tn), lambda l, j: (l, 0, j))],
        out_specs=pl.BlockSpec((1, 8, tn), lambda l, j: (l, 0, j)),
        out_shape=jax.ShapeDtypeStruct((DEPTH, 8, n_out), F32),
        compiler_params=_params(2),
        name="modulation",
    )(cvec, w_mod, b_mod.reshape(DEPTH, 1, n_out))


POS_ROWS = 8


def _pos_kernel(x_ref, r_ref, c_ref, o_ref):
    half = D_MODEL // 2
    col = c_ref[...]
    for r in range(POS_ROWS):
        rs = slice(r * GRID_W, (r + 1) * GRID_W)
        o_ref[rs, 0:half] = x_ref[rs, 0:half] + r_ref[r:r + 1, :]
        o_ref[rs, half:] = x_ref[rs, half:] + col


def _add_pos(x, row_emb, col_emb):
    b, n, d = x.shape
    tm = POS_ROWS * GRID_W
    return pl.pallas_call(
        _pos_kernel,
        grid=(b, n // tm),
        in_specs=[pl.BlockSpec((None, tm, d), lambda i, j: (i, j, 0)),
                  pl.BlockSpec((POS_ROWS, d // 2), lambda i, j: (j, 0)),
                  pl.BlockSpec((GRID_W, d // 2), lambda i, j: (0, 0))],
        out_specs=pl.BlockSpec((None, tm, d), lambda i, j: (i, j, 0)),
        out_shape=jax.ShapeDtypeStruct(x.shape, F32),
        compiler_params=_params(2),
        name="add_pos",
    )(x, row_emb, col_emb)


def _modulated_norm(x, gain, shift, scale):
    return _rms_rows(x) * gain * (1.0 + scale) + shift


def _pack_w_in(ws_ref, wb_ref):
    n_logit = 4 * HEADS
    rows = 256
    lane = lax.broadcasted_iota(jnp.int32, (rows, 128), 1)
    for r in range(0, D_MODEL, rows):
        rs = slice(r, r + rows)
        wb_ref[rs, 0:IN_UV] = ws_ref[rs, 0:IN_UV]
        tail = ws_ref[rs, IN_UV:IN_COLS]
        wb_ref[rs, IN_UV:IN_BA] = tail[:, n_logit:]
        wb_ref[rs, IN_BA:IN_PAD] = jnp.where(lane < n_logit, tail[:, 0:128],
                                             jnp.zeros((rows, 128), BF16))


def _in_kernel(*refs, seq_len, tile, row0, halo):
    if halo:
        (x_ref, xp_ref, xn_ref, mod_ref, g_ref, wf_ref,
         cw_ref, qkv_ref, z_ref, ba_ref, uv_ref, p_ref, win_ref) = refs
    else:
        (x_ref, mod_ref, g_ref, wf_ref,
         cw_ref, qkv_ref, z_ref, ba_ref, uv_ref, p_ref, win_ref) = refs
    i = pl.program_id(0)

    @pl.when(i == 0)
    def _prepare_weights():
        _pack_w_in(wf_ref, win_ref)

    tiles_per_seq = max(seq_len // tile, 1)
    row = row0 + i // tiles_per_seq if row0 else 0
    m = mod_ref[pl.ds(row, 1), :]
    shift = m[:, 0:D_MODEL]
    scale = m[:, D_MODEL:2 * D_MODEL]
    gain = g_ref[...]
    hf = _modulated_norm(x_ref[...], gain, shift, scale)
    hb = hf.astype(BF16)
    if halo:
        hp = _modulated_norm(xp_ref[...], gain, shift, scale)
        hn = _modulated_norm(xn_ref[...], gain, shift, scale)
        hq = jnp.concatenate([hf, hp, hn], axis=0).astype(BF16)
    else:
        hq = hb
    ridx = lax.broadcasted_iota(jnp.int32, (tile, 1), 0)
    if halo:
        pos_in_seq = (i % tiles_per_seq) * tile + ridx
    else:
        pos_in_seq = ridx % seq_len
    first_row = ridx == 0
    last_row = ridx == tile - 1
    seq_start = pos_in_seq == 0
    seq_end = pos_in_seq == seq_len - 1
    cw = cw_ref[...]
    blk = 2 * HEAD_DIM

    def project(b):
        return _dot(hq, win_ref[:, b * blk:(b + 1) * blk])

    def conv_act(b, pre):
        cols = slice(b * blk, (b + 1) * blk)
        cur = pre[0:tile]
        prev = pltpu.roll(cur, 1, 0)
        nxt = pltpu.roll(cur, tile - 1, 0)
        if halo:
            prev = jnp.where(first_row, pre[tile + 7:tile + 8], prev)
            nxt = jnp.where(last_row, pre[tile + 8:tile + 9], nxt)
        prev = jnp.where(seq_start, 0.0, prev)
        nxt = jnp.where(seq_end, 0.0, nxt)
        act = _silu(prev * cw[0:1, cols] + cur * cw[1:2, cols] + nxt * cw[2:3, cols])
        if b * blk >= 2 * QK:
            qkv_ref[:, cols] = act.astype(BF16)
            return
        scale = HEAD_DIM ** -0.5 if b * blk < QK else 1.0
        for h in range(blk // HEAD_DIM):
            a = act[:, h * HEAD_DIM:(h + 1) * HEAD_DIM]
            inv = lax.rsqrt(jnp.sum(a * a, axis=-1, keepdims=True) + EPS) * scale
            c0 = b * blk + h * HEAD_DIM
            qkv_ref[:, c0:c0 + HEAD_DIM] = (a * inv).astype(BF16)

    others = []
    for out_ref, c0 in ((z_ref, IN_Z), (uv_ref, IN_UV), (p_ref, IN_P), (ba_ref, IN_BA)):
        width = out_ref.shape[1]
        others += [(out_ref, c0, slice(c, min(c + blk, width))) for c in range(0, width, blk)]
    n_blk = QKV // blk
    assert len(others) == n_blk
    pre = project(0)
    for b in range(n_blk):
        nxt_pre = project(b + 1) if b + 1 < n_blk else None
        out_ref, c0, cols = others[b]
        out_ref[:, cols] = _dot(hb, win_ref[:, c0 + cols.start:c0 + cols.stop])
        conv_act(b, pre)
        pre = nxt_pre


def _input_stage(x, mods_l, norm_g, w_in, layer, conv_w, *, seq_len, row0):
    n = x.shape[0]
    tile = 512
    halo = seq_len > tile
    tiles_per_seq = max(seq_len // tile, 1)
    n_tiles = n // tile
    h8 = tile // 8
    last8 = n // 8 - 1
    in_specs = [pl.BlockSpec((tile, D_MODEL), lambda i: (i, 0))]
    args = [x]
    if halo:
        in_specs += [pl.BlockSpec((8, D_MODEL), lambda i: (jnp.maximum(i * h8 - 1, 0), 0)),
                     pl.BlockSpec((8, D_MODEL), lambda i: (jnp.minimum((i + 1) * h8, last8), 0))]
        args += [x, x]
    in_specs += [_const_spec((8, N_MOD * D_MODEL)), _const_spec((1, D_MODEL)),
                 _layer_spec(w_in.shape, layer), _const_spec((3, QKV))]
    args += [mods_l, norm_g, w_in, conv_w]
    widths = (QKV, A_WIDTH, 128, 2 * B_WIDTH, C_WIDTH)
    dtypes = (BF16, F32, F32, F32, F32)
    return pl.pallas_call(
        functools.partial(_in_kernel, seq_len=seq_len, tile=tile, row0=row0, halo=halo),
        grid=(n_tiles,),
        in_specs=in_specs,
        out_specs=[pl.BlockSpec((tile, c), lambda i: (i, 0)) for c in widths],
        out_shape=[jax.ShapeDtypeStruct((n, c), t) for c, t in zip(widths, dtypes)],
        scratch_shapes=[pltpu.VMEM((D_MODEL, IN_PAD), BF16)],
        compiler_params=_params(1),
        name="input_stage",
    )(*args)


N_LEVELS = int(math.log2(CHUNK))
WIDE = HEADS * CHUNK


def _delta_constants():
    r = np.arange(CHUNK)[:, None]
    c = (np.arange(WIDE) % CHUNK)[None, :]
    lvl = np.zeros((2, N_LEVELS, CHUNK, WIDE), np.float32)
    for d in range(2):
        rr, cc = (r, c) if d == 0 else (c, r)
        for i in range(N_LEVELS):
            s = 1 << i
            lvl[d, i] = (rr // (2 * s) == cc // (2 * s)) & ((rr // s) % 2 == 1) & ((cc // s) % 2 == 0)
    eye = (r == c).astype(np.float32)
    tri = np.concatenate([np.tril(np.ones((CHUNK, CHUNK))), np.triu(np.ones((CHUNK, CHUNK)))], 0)
    tri = np.concatenate([tri] * 3, axis=1)
    return (jnp.asarray(lvl.reshape(2 * N_LEVELS, CHUNK, WIDE)), jnp.asarray(eye),
            jnp.asarray(tri, BF16))


def _pair(a, b, lt_half):
    return jnp.where(lt_half, a, b)


def _block_diag(blocks, zero):
    n = len(blocks)
    rows = [jnp.concatenate([blocks[i] if j == i else zero for j in range(n)], axis=1)
            for i in range(n)]
    return jnp.concatenate(rows, axis=0)


def _delta_kernel(*refs, n_sub, n_chunks, has_init, n_par, cb, layer, n_cast):
    it = iter(refs)
    qkv_ref, ba_ref, alog_ref, dtb_ref, lvl_ref, eye_ref, tri_ref = (next(it) for _ in range(7))
    s0_ref = next(it) if has_init else None
    sp_ref = next(it) if (not has_init and layer > 0) else None
    cast_in = [next(it) for _ in range(n_cast)]
    o_ref = next(it)
    st_ref = None if has_init else next(it)
    for src_ref in cast_in:
        dst_ref = next(it)
        dst_ref[...] = src_ref[...].astype(BF16)
    (s_ref, gc_ref, beta_ref, gct_ref, u_ref, w_ref, qd_ref, qkd_ref, kdt_ref, gt_ref) = it
    n_total = n_sub * n_chunks
    lane = lax.broadcasted_iota(jnp.int32, (CHUNK, 128), 1)
    lt_half = lane < CHUNK
    lt_half_row = lt_half[0:1]
    rw = lax.broadcasted_iota(jnp.int32, (CHUNK, WIDE), 0)
    cw = lax.broadcasted_iota(jnp.int32, (CHUNK, WIDE), 1) & (CHUNK - 1)
    incl = (rw >= cw, rw <= cw)
    a_row = -jnp.exp(alog_ref[...])
    dtb_row = dtb_ref[...]

    def gate_body(bi, carry):
        span = cb * CHUNK
        r0 = pl.multiple_of(bi * span, span)
        ba = ba_ref[pl.ds(r0, span), :]
        beta_ref[pl.ds(r0, span), :] = _sigmoid(ba)
        al = pltpu.roll(ba, 128 - 2 * HEADS, 1)
        lane_b = lax.broadcasted_iota(jnp.int32, (span, 128), 1)
        g = jnp.where(lane_b < 2 * HEADS, a_row * _softplus(al + dtb_row), 0.0)
        g_hi = g.astype(BF16)
        r1 = g - g_hi.astype(F32)
        g_mid = r1.astype(BF16)
        g_lo = (r1 - g_mid.astype(F32)).astype(BF16)
        css = []
        for i in range(cb):
            rs = slice(i * CHUNK, (i + 1) * CHUNK)
            terms = jnp.concatenate([g_hi[rs], g_mid[rs], g_lo[rs]], axis=0)
            css.append(_dot(tri_ref[...], terms))
        gcs = [jnp.where(lane < HEADS, cs[0:CHUNK], cs[CHUNK:]) for cs in css]
        ats = [jnp.concatenate([gc, gc], axis=0).T[0:8, :] for gc in gcs]
        for i in range(cb):
            gc_ref[pl.ds(pl.multiple_of(r0 + i * CHUNK, CHUNK), CHUNK), :] = gcs[i]
            gct_ref[bi * cb + i] = ats[i]
        return carry

    lax.fori_loop(0, n_total // cb, gate_body, 0)

    zero_blk = jnp.zeros((CHUNK, HEAD_DIM), BF16)
    hs = [slice(h * HEAD_DIM, (h + 1) * HEAD_DIM) for h in range(HEADS)]

    def head_mm(x, y):
        xb = x.astype(BF16)
        yb = y.astype(BF16)
        zero = jnp.zeros((CHUNK, 128), BF16)
        outs = []
        for pr in range(HEADS // 2):
            ys = yb[:, pr * 128:(pr + 1) * 128]
            bd = jnp.concatenate([jnp.where(lt_half, ys, zero), jnp.where(lt_half, zero, ys)], axis=0)
            outs.append(_dot(xb[:, pr * 128:(pr + 1) * 128], bd))
        return jnp.concatenate(outs, axis=1)

    def pre_body(bi, carry):
        groups = []
        for i in range(cb):
            c = bi * cb + i
            r0 = pl.multiple_of(c * CHUNK, CHUNK)
            rows = pl.ds(r0, CHUNK)
            qb = qkv_ref[rows, 0:QK]
            kb16 = qkv_ref[rows, QK:2 * QK]
            qf = qb.astype(F32)
            kf = kb16.astype(F32)
            vf = qkv_ref[rows, 2 * QK:].astype(F32)
            kt = jnp.concatenate(
                [jnp.concatenate([kf[:, 0:128], kf[:, 128:256]], axis=0).T,
                 jnp.concatenate([kf[:, 256:384], kf[:, 384:512]], axis=0).T], axis=1)
            ktbd = _group_rows(kt.astype(BF16), HEADS)
            qkk = _dot(jnp.concatenate([qb, kb16], axis=0), ktbd)
            cols = gc_ref[rows, :]
            bcols = beta_ref[rows, :]
            at = gct_ref[c]
            for d in range(2):
                j0 = d * HEADS
                bg = [jnp.broadcast_to(cols[:, j0 + h:j0 + h + 1], (CHUNK, 128)) for h in range(HEADS)]
                bb = [jnp.broadcast_to(bcols[:, j0 + h:j0 + h + 1], (CHUNK, 128)) for h in range(HEADS)]
                gcw = jnp.concatenate([_pair(bg[0], bg[1], lt_half), _pair(bg[2], bg[3], lt_half)], 1)
                bw = jnp.concatenate([_pair(bb[0], bb[1], lt_half), _pair(bb[2], bb[3], lt_half)], 1)
                gr = jnp.concatenate(
                    [_pair(at[j0:j0 + 1], at[j0 + 1:j0 + 2], lt_half_row),
                     _pair(at[j0 + 2:j0 + 3], at[j0 + 3:j0 + 4], lt_half_row)], axis=1)
                decay = jnp.exp(jnp.where(incl[d], gcw - gr, NEG_BIG))
                m = bw * qkk[CHUNK:] * decay
                qkd_ref[d, rows, :] = (qkk[0:CHUNK] * decay).astype(BF16)
                en = jnp.concatenate([jnp.exp(x) for x in bg], axis=1)
                bn = jnp.concatenate(bb, axis=1)
                qd_ref[d, rows, :] = (qf * en).astype(BF16)
                vb = (vf * bn).astype(BF16)
                kbe = (kf * (bn * en)).astype(BF16)
                e0 = CHUNK - 1 if d == 0 else 0
                bt = [jnp.broadcast_to(at[j0 + h:j0 + h + 1, e0:e0 + 1], (1, 128)) for h in range(HEADS)]
                tw = jnp.concatenate([_pair(bt[0], bt[1], lt_half_row),
                                      _pair(bt[2], bt[3], lt_half_row)], axis=1)
                kdt_ref[d, pl.ds(pl.multiple_of(c * HEAD_DIM, HEAD_DIM), HEAD_DIM), :] = (
                    kt * jnp.exp(tw - gr)).astype(BF16)
                gt_ref[d, c] = jnp.broadcast_to(
                    jnp.concatenate([jnp.exp(x) for x in bt], axis=1), (8, A_WIDTH))
                groups.append((d, rows, m, vb, kbe))
        ts = [eye_ref[...] - g[2] * lvl_ref[g[0] * N_LEVELS] for g in groups]
        for lv in range(1, N_LEVELS):
            xs = [head_mm(t, g[2] * lvl_ref[g[0] * N_LEVELS + lv]) for t, g in zip(ts, groups)]
            ts = [t - head_mm(x, t) for t, x in zip(ts, xs)]
        for t, (d, rows, _, vb, kbe) in zip(ts, groups):
            tb = t.astype(BF16)
            uw = []
            for pr in range(HEADS // 2):
                h0, h1 = hs[2 * pr], hs[2 * pr + 1]
                rhs = jnp.concatenate(
                    [jnp.concatenate([vb[:, h0], zero_blk, kbe[:, h0], zero_blk], axis=1),
                     jnp.concatenate([zero_blk, vb[:, h1], zero_blk, kbe[:, h1]], axis=1)], axis=0)
                uw.append(_dot(tb[:, pr * 128:(pr + 1) * 128], rhs))
            half = A_WIDTH // 2
            u_ref[d, rows, :] = jnp.concatenate([uw[0][:, 0:half], uw[1][:, 0:half]], axis=1)
            w_ref[d, rows, :] = jnp.concatenate([uw[0][:, half:], uw[1][:, half:]],
                                                axis=1).astype(BF16)
        return carry

    lax.fori_loop(0, n_total // cb, pre_body, 0)

    o_ref[...] = jnp.zeros(o_ref.shape, F32)
    zero_s = jnp.zeros((HEAD_DIM, HEAD_DIM), BF16)

    def scan_step(sub0, c):
        groups = [(p, d) for p in range(n_par) for d in range(2)]
        cidx = [(sub0 + p) * n_chunks + (c if d == 0 else n_chunks - 1 - c) for p, d in groups]
        rows = [pl.ds(pl.multiple_of(ci * CHUNK, CHUNK), CHUNK) for ci in cidx]
        ss = [s_ref[p, d] for p, d in groups]
        wqs = []
        for (p, d), r, s in zip(groups, rows, ss):
            sb = s.astype(BF16)
            lhs = jnp.concatenate([w_ref[d, r, :], qd_ref[d, r, :]], axis=0)
            halves = []
            for hp in range(HEADS // 2):
                bd = _block_diag([sb[:, hs[2 * hp]], sb[:, hs[2 * hp + 1]]], zero_s)
                halves.append(_dot(lhs[:, hp * 2 * HEAD_DIM:(hp + 1) * 2 * HEAD_DIM], bd))
            wqs.append(jnp.concatenate(halves, axis=1))
        boths = []
        for (p, d), r, ci, wq in zip(groups, rows, cidx, wqs):
            v_new = (u_ref[d, r, :] - wq[0:CHUNK]).astype(BF16)
            vbd = _block_diag([v_new[:, s] for s in hs], zero_blk)
            kd = kdt_ref[d, pl.ds(pl.multiple_of(ci * HEAD_DIM, HEAD_DIM), HEAD_DIM), :]
            lhs2 = jnp.concatenate([qkd_ref[d, r, :], kd], axis=0)
            boths.append(_dot(lhs2, vbd))
        for (p, d), r, ci, s, wq, both in zip(groups, rows, cidx, ss, wqs, boths):
            o_ref[r, :] += wq[CHUNK:] + both[0:CHUNK]
            s_ref[p, d] = s * gt_ref[d, ci][0:1] + both[CHUNK:]

    def sub_body(sb_i, carry):
        sub0 = sb_i * n_par
        for p in range(n_par):
            for d in range(2):
                if has_init:
                    s_ref[p, d] = jnp.concatenate(
                        [s0_ref[d, h] for h in range(HEADS)], axis=1)
                else:
                    s_ref[p, d] = jnp.zeros((HEAD_DIM, A_WIDTH), F32)

        def step(c, carry2):
            scan_step(sub0, c)
            return carry2

        lax.fori_loop(0, n_chunks, step, 0)
        if st_ref is not None:
            for p in range(n_par):
                for d in range(2):
                    s = s_ref[p, d]
                    for h in range(HEADS):
                        st_ref[sub0 + p, layer, d, h] = s[:, hs[h]]
                        for earlier in range(layer):
                            st_ref[sub0 + p, earlier, d, h] = sp_ref[sub0 + p, earlier, d, h]
        return carry

    lax.fori_loop(0, n_sub // n_par, sub_body, 0)


def _delta_stage(qkv, ba, alog_row, dtb_row, s0, consts, *, seq_len, layer=0, st_prev=None,
                 cast_weights=()):
    n = qkv.shape[0]
    block = max(seq_len, DELTA_BLOCK)
    n_blocks = n // block
    n_sub = block // seq_len
    n_chunks = seq_len // CHUNK
    n_total = block // CHUNK
    has_init = s0 is not None
    n_par = min(n_sub, 4)
    lvl, eye, tri = consts
    in_specs = [pl.BlockSpec((block, QKV), lambda i: (i, 0)),
                pl.BlockSpec((block, 128), lambda i: (i, 0)),
                _const_spec((1, 128)), _const_spec((1, 128)),
                _const_spec(lvl.shape), _const_spec(eye.shape), _const_spec(tri.shape)]
    args = [qkv, ba, alog_row, dtb_row, lvl, eye, tri]
    o_spec = pl.BlockSpec((block, A_WIDTH), lambda i: (i, 0))
    o_shape = jax.ShapeDtypeStruct((n, A_WIDTH), F32)
    st_shape = (2, HEADS, HEAD_DIM, HEAD_DIM)
    if has_init:
        assert n_sub == 1
        in_specs.append(pl.BlockSpec((None,) + st_shape, lambda i: (i, 0, 0, 0, 0)))
        args.append(s0)
        out_specs, out_shape = o_spec, o_shape
    else:
        if layer > 0:
            in_specs.append(pl.BlockSpec((n_sub, layer) + st_shape, lambda i: (i, 0, 0, 0, 0, 0)))
            args.append(st_prev)
        out_specs = [o_spec,
                     pl.BlockSpec((n_sub, layer + 1) + st_shape, lambda i: (i, 0, 0, 0, 0, 0))]
        out_shape = [o_shape, jax.ShapeDtypeStruct((n // seq_len, layer + 1) + st_shape, F32)]
    scratch = [pltpu.VMEM((n_par, 2, HEAD_DIM, A_WIDTH), F32),
               pltpu.VMEM((block, 128), F32),
               pltpu.VMEM((block, 128), F32),
               pltpu.VMEM((n_total, 8, 128), F32),
               pltpu.VMEM((2, block, A_WIDTH), F32),
               pltpu.VMEM((2, block, A_WIDTH), BF16),
               pltpu.VMEM((2, block, A_WIDTH), BF16),
               pltpu.VMEM((2, block, WIDE), BF16),
               pltpu.VMEM((2, n_total * HEAD_DIM, WIDE), BF16),
               pltpu.VMEM((2, n_total, 8, A_WIDTH), F32)]
    for wt in cast_weights:
        rows = wt.shape[1] // n_blocks
        assert rows * n_blocks == wt.shape[1] and rows % 16 == 0
        in_specs.append(pl.BlockSpec((None, rows, wt.shape[2]), lambda i: (layer, i, 0)))
        args.append(wt)
        out_specs = list(out_specs) if isinstance(out_specs, list) else [out_specs]
        out_shape = list(out_shape) if isinstance(out_shape, list) else [out_shape]
        out_specs.append(pl.BlockSpec((rows, wt.shape[2]), lambda i: (i, 0)))
        out_shape.append(jax.ShapeDtypeStruct(wt.shape[1:], BF16))
    return pl.pallas_call(
        functools.partial(_delta_kernel, n_sub=n_sub, n_chunks=n_chunks, has_init=has_init,
                          n_par=n_par, cb=4, layer=layer, n_cast=len(cast_weights)),
        grid=(n_blocks,),
        in_specs=in_specs,
        out_specs=out_specs,
        out_shape=out_shape,
        scratch_shapes=scratch,
        compiler_params=_params(1),
        name="delta_stage",
    )(*args)


def _group_rows(y, n_groups):
    half = y.shape[1] // 2
    gw = half // 2
    lane = lax.broadcasted_iota(jnp.int32, (y.shape[0], half), 1)
    zero = jnp.zeros((y.shape[0], half), y.dtype)
    blocks = []
    for g in range(n_groups):
        part = y[:, (g // 2) * half:(g // 2 + 1) * half]
        keep = (lane < gw) if g % 2 == 0 else (lane >= gw)
        part = jnp.where(keep, part, jnp.zeros_like(part))
        blocks.append(jnp.concatenate([part, zero] if g < 2 else [zero, part], axis=1))
    return jnp.concatenate(blocks, axis=0)


def _mix_delta(o_ref, z_ref, dg_ref, mix_ref):
    dg = dg_ref[...]
    for h in range(HEADS):
        cs = slice(h * HEAD_DIM, (h + 1) * HEAD_DIM)
        y = _rms_rows(o_ref[:, cs]) * dg * _silu(z_ref[:, cs])
        mix_ref[:, cs] = y.astype(BF16)


def _mix_sgu(uv_ref, sg_ref, ws_ref, bs_ref, seg_ref, mix_ref, tile):
    u = _gelu_tanh(uv_ref[:, 0:B_WIDTH])
    v = _gelu_tanh(uv_ref[:, B_WIDTH:])
    vv = v * v
    vv_hi = vv.astype(BF16)
    vv_lo = (vv - vv_hi.astype(F32)).astype(BF16)
    ms = _dot(jnp.concatenate([vv_hi, vv_lo], axis=1), seg_ref[...])
    vn = (v * lax.rsqrt(ms + EPS) * sg_ref[...]).astype(BF16)
    for c in range(tile // SGU_CHUNK):
        rs = slice(c * SGU_CHUNK, (c + 1) * SGU_CHUNK)
        s = _dot(ws_ref[...], _group_rows(vn[rs], B_GROUPS)) + bs_ref[...]
        mix_ref[rs, A_WIDTH:A_WIDTH + B_WIDTH] = (u[rs] * s).astype(BF16)


def _mix_pool(p_ref, band_ref, icnt_ref, wp_ref, ps_ref, mix_ref, tile):
    for c in range(tile // POOL_TILE):
        rs = slice(c * POOL_TILE, (c + 1) * POOL_TILE)
        x = p_ref[rs, :]
        x_hi = x.astype(BF16)
        x_lo = (x - x_hi.astype(F32)).astype(BF16)
        n_win = len(POOL_WINDOWS)
        rhs = jnp.concatenate([_group_rows(x_hi, n_win), _group_rows(x_lo, n_win)], axis=0)
        wsum = _dot(band_ref[...], rhs)
        diff = wsum * icnt_ref[...] - x
        y = _bdot(diff, wp_ref[...]) * ps_ref[...]
        mix_ref[rs, A_WIDTH + B_WIDTH:] = y.astype(BF16)


def _out_kernel(x_ref, o_ref, z_ref, uv_ref, p_ref, dg_ref, sg_ref, ws_ref, bs_ref, seg_ref,
                band_ref, icnt_ref, wp_ref, ps_ref, mod_ref, g2_ref, wof_ref, wgu_ref, wd_ref,
                nf_ref, y_ref, mix_ref, wo_ref, *, seq_len, tile, row0, final_norm):
    i = pl.program_id(0)

    @pl.when(i == 0)
    def _prepare_weights():
        rows = 256
        for r in range(0, D_MODEL, rows):
            wo_ref[r:r + rows, :] = wof_ref[r:r + rows, :].astype(BF16)

    tiles_per_seq = max(seq_len // tile, 1)
    row = row0 + i // tiles_per_seq if row0 else 0
    m = mod_ref[pl.ds(row, 1), :]
    gate1 = m[:, 2 * D_MODEL:3 * D_MODEL]
    shift2 = m[:, 3 * D_MODEL:4 * D_MODEL]
    scale2 = m[:, 4 * D_MODEL:5 * D_MODEL]
    gate2 = m[:, 5 * D_MODEL:6 * D_MODEL]
    _mix_delta(o_ref, z_ref, dg_ref, mix_ref)
    _mix_sgu(uv_ref, sg_ref, ws_ref, bs_ref, seg_ref, mix_ref, tile)
    _mix_pool(p_ref, band_ref, icnt_ref, wp_ref, ps_ref, mix_ref, tile)
    x1 = x_ref[...] + gate1 * _dot(mix_ref[...], wo_ref[...])
    hb = _modulated_norm(x1, g2_ref[...], shift2, scale2).astype(BF16)
    fc = FF_HIDDEN // FF_SPLIT
    ff = jnp.zeros((tile, D_MODEL), F32)
    for c in range(FF_SPLIT):
        gate = _dot(hb, wgu_ref[:, c * fc:(c + 1) * fc])
        up = _dot(hb, wgu_ref[:, FF_HIDDEN + c * fc:FF_HIDDEN + (c + 1) * fc])
        act = (_silu(gate) * up).astype(BF16)
        ff = ff + _dot(act, wd_ref[c * fc:(c + 1) * fc, :])
    x2 = x1 + gate2 * ff
    if final_norm:
        x2 = _rms_rows(x2) * nf_ref[...]
    y_ref[...] = x2


def _output_stage(x, o, z, uv, p, lw, pool_consts, mods_l, norm2_g, w_out, layer, w_gu_b,
                  w_down_b, norm_f, *, seq_len, row0, final_norm):
    n = x.shape[0]
    tile = 512
    band, icnt = pool_consts
    row = lambda c: pl.BlockSpec((tile, c), lambda i: (i, 0))
    small = [lw["delta_g"], lw["sgu_g"], lw["w_spatial"], lw["b_spatial"], lw["seg"], band, icnt,
             lw["w_pool"], lw["pool_scale"], mods_l, norm2_g]
    consts = small + [w_out, w_gu_b, w_down_b, norm_f]
    in_specs = ([row(D_MODEL), row(A_WIDTH), row(A_WIDTH), row(2 * B_WIDTH), row(C_WIDTH)]
                + [_const_spec(a.shape) for a in small]
                + [_layer_spec(w_out.shape, layer), _const_spec(w_gu_b.shape),
                   _const_spec(w_down_b.shape), _const_spec(norm_f.shape)])
    return pl.pallas_call(
        functools.partial(_out_kernel, seq_len=seq_len, tile=tile, row0=row0,
                          final_norm=final_norm),
        grid=(n // tile,),
        in_specs=in_specs,
        out_specs=row(D_MODEL),
        out_shape=jax.ShapeDtypeStruct((n, D_MODEL), F32),
        scratch_shapes=[pltpu.VMEM((tile, D_MODEL), BF16),
                        pltpu.VMEM((D_MODEL, D_MODEL), BF16)],
        compiler_params=_params(1),
        name="output_stage",
    )(x, o, z, uv, p, *consts)


def _pool_constants(seg_len):
    pos = np.arange(POOL_TILE)
    seg = pos // seg_len
    band = np.zeros((len(POOL_WINDOWS), POOL_TILE, POOL_TILE), np.float32)
    icnt = np.zeros((POOL_TILE, C_WIDTH), np.float32)
    for g, win in enumerate(POOL_WINDOWS):
        lo = pos - win // 2
        hi = pos + win - win // 2
        inside = (pos[None, :] >= lo[:, None]) & (pos[None, :] < hi[:, None]) \
            & (seg[None, :] == seg[:, None])
        band[g] = inside
        icnt[:, g * C_GC:(g + 1) * C_GC] = (1.0 / inside.sum(axis=1))[:, None]
    band_cat = np.concatenate([band[g] for g in range(len(POOL_WINDOWS))] * 2, axis=1)
    return jnp.asarray(band_cat, BF16), jnp.asarray(icnt)


def _grid_pos_tables(rows, d):
    quarter = d // 4
    f = np.float32
    omega = (f(1.0) / (f(10000.0) ** (np.arange(quarter, dtype=f) / f(quarter)))).astype(f)
    r = np.arange(rows, dtype=f)[:, None] * omega
    cl = np.arange(GRID_W, dtype=f)[:, None] * omega
    row_emb = np.concatenate([np.sin(r), np.cos(r)], axis=-1).astype(f)
    col_emb = np.concatenate([np.sin(cl), np.cos(cl)], axis=-1).astype(f)
    return jnp.asarray(row_emb), jnp.asarray(col_emb)


def _pad_lanes(a, width):
    return jnp.pad(a, ((0, 0), (0, width - a.shape[1])))


def _layer_weights(l, w_pool, w_spatial, b_spatial):
    wp = jnp.zeros((C_WIDTH, C_WIDTH), F32)
    for g in range(len(POOL_WINDOWS)):
        wp = wp.at[g * C_GC:(g + 1) * C_GC, g * C_GC:(g + 1) * C_GC].set(w_pool[l, g])
    grp = np.arange(B_WIDTH) // B_GC
    seg1 = (grp[:, None] == grp[None, :]).astype(np.float32) / B_GC
    seg = jnp.asarray(np.concatenate([seg1, seg1], axis=0), BF16)
    bs = jnp.repeat(b_spatial[l].T, B_GC, axis=1)
    ws_cat = jnp.concatenate([w_spatial[l, g] for g in range(B_GROUPS)], axis=1)
    return wp.astype(BF16), seg, ws_cat.astype(BF16), bs


def kernel(x_prompt, x_sample, state_delta, c, c_ctx, w_in, conv_w, a_log, dt_bias, delta_norm_g,
           sgu_norm_g, w_spatial, b_spatial, w_pool, pool_scale, w_out, norm1_g, norm2_g, w_mod,
           b_mod, w_gu, w_down, norm_f):
    batch, seq, d = x_prompt.shape
    dec_batch, dec_seq, _ = x_sample.shape
    cvec = jnp.concatenate([c_ctx[None, :], c, jnp.zeros((8 - 1 - dec_batch, d), F32)], axis=0)
    mods = _modulation(cvec, w_mod, b_mod)
    row_emb, col_emb = _grid_pos_tables(dec_seq // GRID_W, d)
    xs = _add_pos(x_sample, row_emb, col_emb).reshape(dec_batch * dec_seq, d)
    xc = x_prompt.reshape(batch * seq, d)
    pool_ctx = _pool_constants(min(seq, POOL_TILE))
    pool_lat = _pool_constants(GRID_W)
    nf = norm_f.reshape(1, d)
    delta_consts = _delta_constants()
    w_in_b = w_in.astype(BF16)
    ctx_states = None
    for l in range(DEPTH):
        wp, seg, ws, bs = _layer_weights(l, w_pool, w_spatial, b_spatial)
        lw = {"delta_g": delta_norm_g[l].reshape(1, HEAD_DIM),
              "sgu_g": sgu_norm_g[l].reshape(1, B_WIDTH), "w_spatial": ws, "b_spatial": bs,
              "seg": seg, "w_pool": wp, "pool_scale": pool_scale[l].reshape(1, C_WIDTH)}
        alog_row = _pad_lanes(a_log[l].reshape(1, 2 * HEADS), 128)
        dtb_row = _pad_lanes(dt_bias[l].reshape(1, 2 * HEADS), 128)
        n1 = norm1_g[l].reshape(1, d)
        n2 = norm2_g[l].reshape(1, d)
        last = l == DEPTH - 1
        s0_lat = state_delta[:, l]
        streams = (("ctx", xc, seq, 0, None, pool_ctx), ("lat", xs, dec_seq, 1, s0_lat, pool_lat))
        outs = []
        for name, x, seq_len, row0, s0, pool_consts in streams:
            qkv, z, ba, uv, p = _input_stage(x, mods[l], n1, w_in_b, l, conv_w[l],
                                             seq_len=seq_len, row0=row0)
            if s0 is None:
                o, ctx_states, w_gu_b, w_down_b = _delta_stage(
                    qkv, ba, alog_row, dtb_row, None, delta_consts, seq_len=seq_len, layer=l,
                    st_prev=ctx_states, cast_weights=(w_gu, w_down))
            else:
                o = _delta_stage(qkv, ba, alog_row, dtb_row, s0, delta_consts, seq_len=seq_len)
            outs.append(_output_stage(x, o, z, uv, p, lw, pool_consts, mods[l], n2, w_out, l,
                                      w_gu_b, w_down_b, nf, seq_len=seq_len, row0=row0,
                                      final_norm=last))
        xc, xs = outs
    y_prompt = xc.reshape(batch, seq, d)
    y_sample = xs.reshape(dec_batch, dec_seq, d)
    return (y_prompt, y_sample, ctx_states)
```

```python
import functools
import math

import numpy as np
import jax
import jax.numpy as jnp
from jax import lax
from jax.experimental import pallas as pl
from jax.experimental.pallas import tpu as pltpu

F32 = jnp.float32
BF16 = jnp.bfloat16
HIGHEST = lax.Precision.HIGHEST

D_MODEL = 1024
DEPTH = 2
GRID_W = 64
HEADS = 4
HEAD_DIM = 128
QK = HEADS * HEAD_DIM
A_WIDTH = HEADS * HEAD_DIM
QKV = 2 * QK + A_WIDTH
CHUNK = 64
B_WIDTH = 256
B_GROUPS = 4
B_GC = 64
SGU_CHUNK = 128
C_WIDTH = 256
POOL_WINDOWS = (2, 4, 8, 16)
C_GC = 64
FF_HIDDEN = 2816
FF_SPLIT = 2
N_MOD = 6
EPS = 1e-6
IN_COLS = 2 * QK + 2 * A_WIDTH + 4 * HEADS + 2 * B_WIDTH + C_WIDTH
IN_Z = QKV
IN_UV = IN_Z + A_WIDTH
IN_P = IN_UV + 2 * B_WIDTH
IN_BA = IN_P + C_WIDTH
IN_PAD = IN_BA + 128
NEG_BIG = -1e30

DELTA_BLOCK = 1024
POOL_TILE = 256
VMEM_LIMIT = 58 * 1024 * 1024


def _dot(a, b, precision=None):
    return jnp.dot(a, b, preferred_element_type=F32, precision=precision)


def _bdot(a, b):
    return jnp.dot(a.astype(BF16), b.astype(BF16), preferred_element_type=F32)


def _sigmoid(x):
    return 0.5 * jnp.tanh(0.5 * x) + 0.5


def _silu(x):
    h = 0.5 * x
    return h + h * jnp.tanh(h)


def _softplus(x):
    return jnp.maximum(x, 0.0) + jnp.log1p(jnp.exp(-jnp.abs(x)))


def _gelu_tanh(x):
    c = math.sqrt(2.0 / math.pi)
    return 0.5 * x * (1.0 + jnp.tanh(c * (x + 0.044715 * (x * x * x))))


def _rms_rows(x):
    return x * lax.rsqrt(jnp.mean(x * x, axis=-1, keepdims=True) + EPS)


def _params(n_grid):
    return pltpu.CompilerParams(dimension_semantics=("arbitrary",) * n_grid,
                                vmem_limit_bytes=VMEM_LIMIT)


def _const_spec(shape):
    nd = len(shape)
    return pl.BlockSpec(shape, lambda *_: (0,) * nd, pipeline_mode=pl.Buffered(1))


def _layer_spec(shape, layer):
    nd = len(shape) - 1
    return pl.BlockSpec((None,) + tuple(shape[1:]), lambda *_: (layer,) + (0,) * nd,
                        pipeline_mode=pl.Buffered(1))


def _mod_kernel(c_ref, w_ref, b_ref, o_ref):
    a = _silu(c_ref[...])
    o_ref[0] = _bdot(a, w_ref[0]) + b_ref[0]


def _modulation(cvec, w_mod, b_mod):
    tn = 1536
    n_out = N_MOD * D_MODEL
    return pl.pallas_call(
        _mod_kernel,
        grid=(DEPTH, n_out // tn),
        in_specs=[pl.BlockSpec((8, D_MODEL), lambda l, j: (0, 0)),
                  pl.BlockSpec((1, D_MODEL, tn), lambda l, j: (l, 0, j)),
                  pl.BlockSpec((1, 1, tn), lambda l, j: (l, 0, j))],
        out_specs=pl.BlockSpec((1, 8, tn), lambda l, j: (l, 0, j)),
        out_shape=jax.ShapeDtypeStruct((DEPTH, 8, n_out), F32),
        compiler_params=_params(2),
        name="modulation",
    )(cvec, w_mod, b_mod.reshape(DEPTH, 1, n_out))


POS_ROWS = 8


def _pos_kernel(x_ref, r_ref, c_ref, o_ref):
    half = D_MODEL // 2
    col = c_ref[...]
    for r in range(POS_ROWS):
        rs = slice(r * GRID_W, (r + 1) * GRID_W)
        o_ref[rs, 0:half] = x_ref[rs, 0:half] + r_ref[r:r + 1, :]
        o_ref[rs, half:] = x_ref[rs, half:] + col


def _add_pos(x, row_emb, col_emb):
    b, n, d = x.shape
    tm = POS_ROWS * GRID_W
    return pl.pallas_call(
        _pos_kernel,
        grid=(b, n // tm),
        in_specs=[pl.BlockSpec((None, tm, d), lambda i, j: (i, j, 0)),
                  pl.BlockSpec((POS_ROWS, d // 2), lambda i, j: (j, 0)),
                  pl.BlockSpec((GRID_W, d // 2), lambda i, j: (0, 0))],
        out_specs=pl.BlockSpec((None, tm, d), lambda i, j: (i, j, 0)),
        out_shape=jax.ShapeDtypeStruct(x.shape, F32),
        compiler_params=_params(2),
        name="add_pos",
    )(x, row_emb, col_emb)


def _modulated_norm(x, gain, shift, scale):
    return _rms_rows(x) * gain * (1.0 + scale) + shift


def _pack_w_in(ws_ref, wb_ref):
    n_logit = 4 * HEADS
    rows = 256
    lane = lax.broadcasted_iota(jnp.int32, (rows, 128), 1)
    for r in range(0, D_MODEL, rows):
        rs = slice(r, r + rows)
        wb_ref[rs, 0:IN_UV] = ws_ref[rs, 0:IN_UV]
        tail = ws_ref[rs, IN_UV:IN_COLS]
        wb_ref[rs, IN_UV:IN_BA] = tail[:, n_logit:]
        wb_ref[rs, IN_BA:IN_PAD] = jnp.where(lane < n_logit, tail[:, 0:128],
                                             jnp.zeros((rows, 128), BF16))


def _in_kernel(*refs, seq_len, tile, row0, halo):
    if halo:
        (x_ref, xp_ref, xn_ref, mod_ref, g_ref, wf_ref,
         cw_ref, qkv_ref, z_ref, ba_ref, uv_ref, p_ref, win_ref) = refs
    else:
        (x_ref, mod_ref, g_ref, wf_ref,
         cw_ref, qkv_ref, z_ref, ba_ref, uv_ref, p_ref, win_ref) = refs
    i = pl.program_id(0)

    @pl.when(i == 0)
    def _prepare_weights():
        _pack_w_in(wf_ref, win_ref)

    tiles_per_seq = max(seq_len // tile, 1)
    row = row0 + i // tiles_per_seq if row0 else 0
    m = mod_ref[pl.ds(row, 1), :]
    shift = m[:, 0:D_MODEL]
    scale = m[:, D_MODEL:2 * D_MODEL]
    gain = g_ref[...]
    hf = _modulated_norm(x_ref[...], gain, shift, scale)
    hb = hf.astype(BF16)
    if halo:
        hp = _modulated_norm(xp_ref[...], gain, shift, scale)
        hn = _modulated_norm(xn_ref[...], gain, shift, scale)
        hq = jnp.concatenate([hf, hp, hn], axis=0).astype(BF16)
    else:
        hq = hb
    ridx = lax.broadcasted_iota(jnp.int32, (tile, 1), 0)
    if halo:
        pos_in_seq = (i % tiles_per_seq) * tile + ridx
    else:
        pos_in_seq = ridx % seq_len
    first_row = ridx == 0
    last_row = ridx == tile - 1
    seq_start = pos_in_seq == 0
    seq_end = pos_in_seq == seq_len - 1
    cw = cw_ref[...]
    blk = 2 * HEAD_DIM

    def project(b):
        return _dot(hq, win_ref[:, b * blk:(b + 1) * blk])

    def conv_act(b, pre):
        cols = slice(b * blk, (b + 1) * blk)
        cur = pre[0:tile]
        prev = pltpu.roll(cur, 1, 0)
        nxt = pltpu.roll(cur, tile - 1, 0)
        if halo:
            prev = jnp.where(first_row, pre[tile + 7:tile + 8], prev)
            nxt = jnp.where(last_row, pre[tile + 8:tile + 9], nxt)
        prev = jnp.where(seq_start, 0.0, prev)
        nxt = jnp.where(seq_end, 0.0, nxt)
        act = _silu(prev * cw[0:1, cols] + cur * cw[1:2, cols] + nxt * cw[2:3, cols])
        if b * blk >= 2 * QK:
            qkv_ref[:, cols] = act.astype(BF16)
            return
        scale = HEAD_DIM ** -0.5 if b * blk < QK else 1.0
        for h in range(blk // HEAD_DIM):
            a = act[:, h * HEAD_DIM:(h + 1) * HEAD_DIM]
            inv = lax.rsqrt(jnp.sum(a * a, axis=-1, keepdims=True) + EPS) * scale
            c0 = b * blk + h * HEAD_DIM
            qkv_ref[:, c0:c0 + HEAD_DIM] = (a * inv).astype(BF16)

    others = []
    for out_ref, c0 in ((z_ref, IN_Z), (uv_ref, IN_UV), (p_ref, IN_P), (ba_ref, IN_BA)):
        width = out_ref.shape[1]
        others += [(out_ref, c0, slice(c, min(c + blk, width))) for c in range(0, width, blk)]
    n_blk = QKV // blk
    assert len(others) == n_blk
    pre = project(0)
    for b in range(n_blk):
        nxt_pre = project(b + 1) if b + 1 < n_blk else None
        out_ref, c0, cols = others[b]
        out_ref[:, cols] = _dot(hb, win_ref[:, c0 + cols.start:c0 + cols.stop])
        conv_act(b, pre)
        pre = nxt_pre


def _input_stage(x, mods_l, norm_g, w_in, layer, conv_w, *, seq_len, row0):
    n = x.shape[0]
    tile = 512
    halo = seq_len > tile
    tiles_per_seq = max(seq_len // tile, 1)
    n_tiles = n // tile
    h8 = tile // 8
    last8 = n // 8 - 1
    in_specs = [pl.BlockSpec((tile, D_MODEL), lambda i: (i, 0))]
    args = [x]
    if halo:
        in_specs += [pl.BlockSpec((8, D_MODEL), lambda i: (jnp.maximum(i * h8 - 1, 0), 0)),
                     pl.BlockSpec((8, D_MODEL), lambda i: (jnp.minimum((i + 1) * h8, last8), 0))]
        args += [x, x]
    in_specs += [_const_spec((8, N_MOD * D_MODEL)), _const_spec((1, D_MODEL)),
                 _layer_spec(w_in.shape, layer), _const_spec((3, QKV))]
    args += [mods_l, norm_g, w_in, conv_w]
    widths = (QKV, A_WIDTH, 128, 2 * B_WIDTH, C_WIDTH)
    dtypes = (BF16, F32, F32, F32, F32)
    return pl.pallas_call(
        functools.partial(_in_kernel, seq_len=seq_len, tile=tile, row0=row0, halo=halo),
        grid=(n_tiles,),
        in_specs=in_specs,
        out_specs=[pl.BlockSpec((tile, c), lambda i: (i, 0)) for c in widths],
        out_shape=[jax.ShapeDtypeStruct((n, c), t) for c, t in zip(widths, dtypes)],
        scratch_shapes=[pltpu.VMEM((D_MODEL, IN_PAD), BF16)],
        compiler_params=_params(1),
        name="input_stage",
    )(*args)


N_LEVELS = int(math.log2(CHUNK))
WIDE = HEADS * CHUNK


def _delta_constants():
    r = np.arange(CHUNK)[:, None]
    c = (np.arange(WIDE) % CHUNK)[None, :]
    lvl = np.zeros((2, N_LEVELS, CHUNK, WIDE), np.float32)
    for d in range(2):
        rr, cc = (r, c) if d == 0 else (c, r)
        for i in range(N_LEVELS):
            s = 1 << i
            lvl[d, i] = (rr // (2 * s) == cc // (2 * s)) & ((rr // s) % 2 == 1) & ((cc // s) % 2 == 0)
    eye = (r == c).astype(np.float32)
    tri = np.concatenate([np.tril(np.ones((CHUNK, CHUNK))), np.triu(np.ones((CHUNK, CHUNK)))], 0)
    tri = np.concatenate([tri] * 3, axis=1)
    return (jnp.asarray(lvl.reshape(2 * N_LEVELS, CHUNK, WIDE)), jnp.asarray(eye),
            jnp.asarray(tri, BF16))


def _pair(a, b, lt_half):
    return jnp.where(lt_half, a, b)


def _block_diag(blocks, zero):
    n = len(blocks)
    rows = [jnp.concatenate([blocks[i] if j == i else zero for j in range(n)], axis=1)
            for i in range(n)]
    return jnp.concatenate(rows, axis=0)


def _delta_kernel(*refs, n_sub, n_chunks, has_init, n_par, cb, layer, n_cast):
    it = iter(refs)
    qkv_ref, ba_ref, alog_ref, dtb_ref, lvl_ref, eye_ref, tri_ref = (next(it) for _ in range(7))
    s0_ref = next(it) if has_init else None
    sp_ref = next(it) if (not has_init and layer > 0) else None
    cast_in = [next(it) for _ in range(n_cast)]
    o_ref = next(it)
    st_ref = None if has_init else next(it)
    for src_ref in cast_in:
        dst_ref = next(it)
        dst_ref[...] = src_ref[...].astype(BF16)
    (s_ref, gc_ref, beta_ref, gct_ref, u_ref, w_ref, qd_ref, qkd_ref, kdt_ref, gt_ref) = it
    n_total = n_sub * n_chunks
    lane = lax.broadcasted_iota(jnp.int32, (CHUNK, 128), 1)
    lt_half = lane < CHUNK
    lt_half_row = lt_half[0:1]
    rw = lax.broadcasted_iota(jnp.int32, (CHUNK, WIDE), 0)
    cw = lax.broadcasted_iota(jnp.int32, (CHUNK, WIDE), 1) & (CHUNK - 1)
    incl = (rw >= cw, rw <= cw)
    a_row = -jnp.exp(alog_ref[...])
    dtb_row = dtb_ref[...]

    def gate_body(bi, carry):
        span = cb * CHUNK
        r0 = pl.multiple_of(bi * span, span)
        ba = ba_ref[pl.ds(r0, span), :]
        beta_ref[pl.ds(r0, span), :] = _sigmoid(ba)
        al = pltpu.roll(ba, 128 - 2 * HEADS, 1)
        lane_b = lax.broadcasted_iota(jnp.int32, (span, 128), 1)
        g = jnp.where(lane_b < 2 * HEADS, a_row * _softplus(al + dtb_row), 0.0)
        g_hi = g.astype(BF16)
        r1 = g - g_hi.astype(F32)
        g_mid = r1.astype(BF16)
        g_lo = (r1 - g_mid.astype(F32)).astype(BF16)
        css = []
        for i in range(cb):
            rs = slice(i * CHUNK, (i + 1) * CHUNK)
            terms = jnp.concatenate([g_hi[rs], g_mid[rs], g_lo[rs]], axis=0)
            css.append(_dot(tri_ref[...], terms))
        gcs = [jnp.where(lane < HEADS, cs[0:CHUNK], cs[CHUNK:]) for cs in css]
        ats = [jnp.concatenate([gc, gc], axis=0).T[0:8, :] for gc in gcs]
        for i in range(cb):
            gc_ref[pl.ds(pl.multiple_of(r0 + i * CHUNK, CHUNK), CHUNK), :] = gcs[i]
            gct_ref[bi * cb + i] = ats[i]
        return carry

    n_batches = n_total // cb
    gate_body(0, 0)

    zero_blk = jnp.zeros((CHUNK, HEAD_DIM), BF16)
    hs = [slice(h * HEAD_DIM, (h + 1) * HEAD_DIM) for h in range(HEADS)]

    def head_mm(x, y):
        xb = x.astype(BF16)
        yb = y.astype(BF16)
        zero = jnp.zeros((CHUNK, 128), BF16)
        outs = []
        for pr in range(HEADS // 2):
            ys = yb[:, pr * 128:(pr + 1) * 128]
            bd = jnp.concatenate([jnp.where(lt_half, ys, zero), jnp.where(lt_half, zero, ys)], axis=0)
            outs.append(_dot(xb[:, pr * 128:(pr + 1) * 128], bd))
        return jnp.concatenate(outs, axis=1)

    def pre_body(bi, carry):
        groups = []
        for i in range(cb):
            c = bi * cb + i
            r0 = pl.multiple_of(c * CHUNK, CHUNK)
            rows = pl.ds(r0, CHUNK)
            qb = qkv_ref[rows, 0:QK]
            kb16 = qkv_ref[rows, QK:2 * QK]
            qf = qb.astype(F32)
            kf = kb16.astype(F32)
            vf = qkv_ref[rows, 2 * QK:].astype(F32)
            kt = jnp.concatenate(
                [jnp.concatenate([kf[:, 0:128], kf[:, 128:256]], axis=0).T,
                 jnp.concatenate([kf[:, 256:384], kf[:, 384:512]], axis=0).T], axis=1)
            ktbd = _group_rows(kt.astype(BF16), HEADS)
            qkk = _dot(jnp.concatenate([qb, kb16], axis=0), ktbd)
            cols = gc_ref[rows, :]
            bcols = beta_ref[rows, :]
            at = gct_ref[c]
            for d in range(2):
                j0 = d * HEADS
                bg = [jnp.broadcast_to(cols[:, j0 + h:j0 + h + 1], (CHUNK, 128)) for h in range(HEADS)]
                bb = [jnp.broadcast_to(bcols[:, j0 + h:j0 + h + 1], (CHUNK, 128)) for h in range(HEADS)]
                gcw = jnp.concatenate([_pair(bg[0], bg[1], lt_half), _pair(bg[2], bg[3], lt_half)], 1)
                bw = jnp.concatenate([_pair(bb[0], bb[1], lt_half), _pair(bb[2], bb[3], lt_half)], 1)
                gr = jnp.concatenate(
                    [_pair(at[j0:j0 + 1], at[j0 + 1:j0 + 2], lt_half_row),
                     _pair(at[j0 + 2:j0 + 3], at[j0 + 3:j0 + 4], lt_half_row)], axis=1)
                decay = jnp.exp(jnp.where(incl[d], gcw - gr, NEG_BIG))
                m = bw * qkk[CHUNK:] * decay
                qkd_ref[d, rows, :] = (qkk[0:CHUNK] * decay).astype(BF16)
                en = jnp.concatenate([jnp.exp(x) for x in bg], axis=1)
                bn = jnp.concatenate(bb, axis=1)
                qd_ref[d, rows, :] = (qf * en).astype(BF16)
                vb = (vf * bn).astype(BF16)
                kbe = (kf * (bn * en)).astype(BF16)
                e0 = CHUNK - 1 if d == 0 else 0
                bt = [jnp.broadcast_to(at[j0 + h:j0 + h + 1, e0:e0 + 1], (1, 128)) for h in range(HEADS)]
                tw = jnp.concatenate([_pair(bt[0], bt[1], lt_half_row),
                                      _pair(bt[2], bt[3], lt_half_row)], axis=1)
                kdt_ref[d, pl.ds(pl.multiple_of(c * HEAD_DIM, HEAD_DIM), HEAD_DIM), :] = (
                    kt * jnp.exp(tw - gr)).astype(BF16)
                gt_ref[d, c] = jnp.broadcast_to(
                    jnp.concatenate([jnp.exp(x) for x in bt], axis=1), (8, A_WIDTH))
                groups.append((d, rows, m, vb, kbe))
        gate_body(jnp.minimum(bi + 1, n_batches - 1), 0)
        ts = [eye_ref[...] - g[2] * lvl_ref[g[0] * N_LEVELS] for g in groups]
        for lv in range(1, N_LEVELS):
            xs = [head_mm(t, g[2] * lvl_ref[g[0] * N_LEVELS + lv]) for t, g in zip(ts, groups)]
            ts = [t - head_mm(x, t) for t, x in zip(ts, xs)]
        for t, (d, rows, _, vb, kbe) in zip(ts, groups):
            tb = t.astype(BF16)
            uw = []
            for pr in range(HEADS // 2):
                h0, h1 = hs[2 * pr], hs[2 * pr + 1]
                rhs = jnp.concatenate(
                    [jnp.concatenate([vb[:, h0], zero_blk, kbe[:, h0], zero_blk], axis=1),
                     jnp.concatenate([zero_blk, vb[:, h1], zero_blk, kbe[:, h1]], axis=1)], axis=0)
                uw.append(_dot(tb[:, pr * 128:(pr + 1) * 128], rhs))
            half = A_WIDTH // 2
            u_ref[d, rows, :] = jnp.concatenate([uw[0][:, 0:half], uw[1][:, 0:half]], axis=1)
            w_ref[d, rows, :] = jnp.concatenate([uw[0][:, half:], uw[1][:, half:]],
                                                axis=1).astype(BF16)
        return carry

    lax.fori_loop(0, n_batches, pre_body, 0)

    o_ref[...] = jnp.zeros(o_ref.shape, F32)
    zero_s = jnp.zeros((HEAD_DIM, HEAD_DIM), BF16)

    def scan_step(sub0, c):
        groups = [(p, d) for p in range(n_par) for d in range(2)]
        cidx = [(sub0 + p) * n_chunks + (c if d == 0 else n_chunks - 1 - c) for p, d in groups]
        rows = [pl.ds(pl.multiple_of(ci * CHUNK, CHUNK), CHUNK) for ci in cidx]
        ss = [s_ref[p, d] for p, d in groups]
        wqs = []
        for (p, d), r, s in zip(groups, rows, ss):
            sb = s.astype(BF16)
            lhs = jnp.concatenate([w_ref[d, r, :], qd_ref[d, r, :]], axis=0)
            halves = []
            for hp in range(HEADS // 2):
                bd = _block_diag([sb[:, hs[2 * hp]], sb[:, hs[2 * hp + 1]]], zero_s)
                halves.append(_dot(lhs[:, hp * 2 * HEAD_DIM:(hp + 1) * 2 * HEAD_DIM], bd))
            wqs.append(jnp.concatenate(halves, axis=1))
        boths = []
        for (p, d), r, ci, wq in zip(groups, rows, cidx, wqs):
            v_new = (u_ref[d, r, :] - wq[0:CHUNK]).astype(BF16)
            vbd = _block_diag([v_new[:, s] for s in hs], zero_blk)
            kd = kdt_ref[d, pl.ds(pl.multiple_of(ci * HEAD_DIM, HEAD_DIM), HEAD_DIM), :]
            lhs2 = jnp.concatenate([qkd_ref[d, r, :], kd], axis=0)
            boths.append(_dot(lhs2, vbd))
        for (p, d), r, ci, s, wq, both in zip(groups, rows, cidx, ss, wqs, boths):
            o_ref[r, :] += wq[CHUNK:] + both[0:CHUNK]
            s_ref[p, d] = s * gt_ref[d, ci][0:1] + both[CHUNK:]

    def sub_body(sb_i, carry):
        sub0 = sb_i * n_par
        for p in range(n_par):
            for d in range(2):
                if has_init:
                    s_ref[p, d] = jnp.concatenate(
                        [s0_ref[d, h] for h in range(HEADS)], axis=1)
                else:
                    s_ref[p, d] = jnp.zeros((HEAD_DIM, A_WIDTH), F32)

        def step(c, carry2):
            scan_step(sub0, c)
            return carry2

        lax.fori_loop(0, n_chunks, step, 0)
        if st_ref is not None:
            for p in range(n_par):
                for d in range(2):
                    s = s_ref[p, d]
                    for h in range(HEADS):
                        st_ref[sub0 + p, layer, d, h] = s[:, hs[h]]
                        for earlier in range(layer):
                            st_ref[sub0 + p, earlier, d, h] = sp_ref[sub0 + p, earlier, d, h]
        return carry

    lax.fori_loop(0, n_sub // n_par, sub_body, 0)


def _delta_stage(qkv, ba, alog_row, dtb_row, s0, consts, *, seq_len, layer=0, st_prev=None,
                 cast_weights=()):
    n = qkv.shape[0]
    block = max(seq_len, DELTA_BLOCK)
    n_blocks = n // block
    n_sub = block // seq_len
    n_chunks = seq_len // CHUNK
    n_total = block // CHUNK
    has_init = s0 is not None
    n_par = min(n_sub, 4)
    lvl, eye, tri = consts
    in_specs = [pl.BlockSpec((block, QKV), lambda i: (i, 0)),
                pl.BlockSpec((block, 128), lambda i: (i, 0)),
                _const_spec((1, 128)), _const_spec((1, 128)),
                _const_spec(lvl.shape), _const_spec(eye.shape), _const_spec(tri.shape)]
    args = [qkv, ba, alog_row, dtb_row, lvl, eye, tri]
    o_spec = pl.BlockSpec((block, A_WIDTH), lambda i: (i, 0))
    o_shape = jax.ShapeDtypeStruct((n, A_WIDTH), F32)
    st_shape = (2, HEADS, HEAD_DIM, HEAD_DIM)
    if has_init:
        assert n_sub == 1
        in_specs.append(pl.BlockSpec((None,) + st_shape, lambda i: (i, 0, 0, 0, 0)))
        args.append(s0)
        out_specs, out_shape = o_spec, o_shape
    else:
        if layer > 0:
            in_specs.append(pl.BlockSpec((n_sub, layer) + st_shape, lambda i: (i, 0, 0, 0, 0, 0)))
            args.append(st_prev)
        out_specs = [o_spec,
                     pl.BlockSpec((n_sub, layer + 1) + st_shape, lambda i: (i, 0, 0, 0, 0, 0))]
        out_shape = [o_shape, jax.ShapeDtypeStruct((n // seq_len, layer + 1) + st_shape, F32)]
    scratch = [pltpu.VMEM((n_par, 2, HEAD_DIM, A_WIDTH), F32),
               pltpu.VMEM((block, 128), F32),
               pltpu.VMEM((block, 128), F32),
               pltpu.VMEM((n_total, 8, 128), F32),
               pltpu.VMEM((2, block, A_WIDTH), F32),
               pltpu.VMEM((2, block, A_WIDTH), BF16),
               pltpu.VMEM((2, block, A_WIDTH), BF16),
               pltpu.VMEM((2, block, WIDE), BF16),
               pltpu.VMEM((2, n_total * HEAD_DIM, WIDE), BF16),
               pltpu.VMEM((2, n_total, 8, A_WIDTH), F32)]
    for wt in cast_weights:
        rows = wt.shape[1] // n_blocks
        assert rows * n_blocks == wt.shape[1] and rows % 16 == 0
        in_specs.append(pl.BlockSpec((None, rows, wt.shape[2]), lambda i: (layer, i, 0)))
        args.append(wt)
        out_specs = list(out_specs) if isinstance(out_specs, list) else [out_specs]
        out_shape = list(out_shape) if isinstance(out_shape, list) else [out_shape]
        out_specs.append(pl.BlockSpec((rows, wt.shape[2]), lambda i: (i, 0)))
        out_shape.append(jax.ShapeDtypeStruct(wt.shape[1:], BF16))
    return pl.pallas_call(
        functools.partial(_delta_kernel, n_sub=n_sub, n_chunks=n_chunks, has_init=has_init,
                          n_par=n_par, cb=4, layer=layer, n_cast=len(cast_weights)),
        grid=(n_blocks,),
        in_specs=in_specs,
        out_specs=out_specs,
        out_shape=out_shape,
        scratch_shapes=scratch,
        compiler_params=_params(1),
        name="delta_stage",
    )(*args)


def _group_rows(y, n_groups):
    half = y.shape[1] // 2
    gw = half // 2
    lane = lax.broadcasted_iota(jnp.int32, (y.shape[0], half), 1)
    zero = jnp.zeros((y.shape[0], half), y.dtype)
    blocks = []
    for g in range(n_groups):
        part = y[:, (g // 2) * half:(g // 2 + 1) * half]
        keep = (lane < gw) if g % 2 == 0 else (lane >= gw)
        part = jnp.where(keep, part, jnp.zeros_like(part))
        blocks.append(jnp.concatenate([part, zero] if g < 2 else [zero, part], axis=1))
    return jnp.concatenate(blocks, axis=0)


def _mix_delta(o_ref, z_ref, dg_ref, mix_ref):
    dg = dg_ref[...]
    for h in range(HEADS):
        cs = slice(h * HEAD_DIM, (h + 1) * HEAD_DIM)
        y = _rms_rows(o_ref[:, cs]) * dg * _silu(z_ref[:, cs])
        mix_ref[:, cs] = y.astype(BF16)


def _mix_sgu(uv_ref, sg_ref, ws_ref, bs_ref, seg_ref, mix_ref, tile):
    u = _gelu_tanh(uv_ref[:, 0:B_WIDTH])
    v = _gelu_tanh(uv_ref[:, B_WIDTH:])
    vv = v * v
    vv_hi = vv.astype(BF16)
    vv_lo = (vv - vv_hi.astype(F32)).astype(BF16)
    ms = _dot(jnp.concatenate([vv_hi, vv_lo], axis=1), seg_ref[...])
    vn = (v * lax.rsqrt(ms + EPS) * sg_ref[...]).astype(BF16)
    for c in range(tile // SGU_CHUNK):
        rs = slice(c * SGU_CHUNK, (c + 1) * SGU_CHUNK)
        s = _dot(ws_ref[...], _group_rows(vn[rs], B_GROUPS)) + bs_ref[...]
        mix_ref[rs, A_WIDTH:A_WIDTH + B_WIDTH] = (u[rs] * s).astype(BF16)


def _mix_pool(p_ref, band_ref, icnt_ref, wp_ref, ps_ref, mix_ref, tile):
    for c in range(tile // POOL_TILE):
        rs = slice(c * POOL_TILE, (c + 1) * POOL_TILE)
        x = p_ref[rs, :]
        x_hi = x.astype(BF16)
        x_lo = (x - x_hi.astype(F32)).astype(BF16)
        n_win = len(POOL_WINDOWS)
        rhs = jnp.concatenate([_group_rows(x_hi, n_win), _group_rows(x_lo, n_win)], axis=0)
        wsum = _dot(band_ref[...], rhs)
        diff = wsum * icnt_ref[...] - x
        y = _bdot(diff, wp_ref[...]) * ps_ref[...]
        mix_ref[rs, A_WIDTH + B_WIDTH:] = y.astype(BF16)


def _out_kernel(x_ref, o_ref, z_ref, uv_ref, p_ref, dg_ref, sg_ref, ws_ref, bs_ref, seg_ref,
                band_ref, icnt_ref, wp_ref, ps_ref, mod_ref, g2_ref, wof_ref, wgu_ref, wd_ref,
                nf_ref, y_ref, mix_ref, wo_ref, *, seq_len, tile, row0, final_norm):
    i = pl.program_id(0)

    @pl.when(i == 0)
    def _prepare_weights():
        rows = 256
        for r in range(0, D_MODEL, rows):
            wo_ref[r:r + rows, :] = wof_ref[r:r + rows, :].astype(BF16)

    tiles_per_seq = max(seq_len // tile, 1)
    row = row0 + i // tiles_per_seq if row0 else 0
    m = mod_ref[pl.ds(row, 1), :]
    gate1 = m[:, 2 * D_MODEL:3 * D_MODEL]
    shift2 = m[:, 3 * D_MODEL:4 * D_MODEL]
    scale2 = m[:, 4 * D_MODEL:5 * D_MODEL]
    gate2 = m[:, 5 * D_MODEL:6 * D_MODEL]
    _mix_delta(o_ref, z_ref, dg_ref, mix_ref)
    _mix_sgu(uv_ref, sg_ref, ws_ref, bs_ref, seg_ref, mix_ref, tile)
    _mix_pool(p_ref, band_ref, icnt_ref, wp_ref, ps_ref, mix_ref, tile)
    x1 = x_ref[...] + gate1 * _dot(mix_ref[...], wo_ref[...])
    hb = _modulated_norm(x1, g2_ref[...], shift2, scale2).astype(BF16)
    fc = FF_HIDDEN // FF_SPLIT
    ff = jnp.zeros((tile, D_MODEL), F32)
    for c in range(FF_SPLIT):
        gate = _dot(hb, wgu_ref[:, c * fc:(c + 1) * fc])
        up = _dot(hb, wgu_ref[:, FF_HIDDEN + c * fc:FF_HIDDEN + (c + 1) * fc])
        act = (_silu(gate) * up).astype(BF16)
        ff = ff + _dot(act, wd_ref[c * fc:(c + 1) * fc, :])
    x2 = x1 + gate2 * ff
    if final_norm:
        x2 = _rms_rows(x2) * nf_ref[...]
    y_ref[...] = x2


def _output_stage(x, o, z, uv, p, lw, pool_consts, mods_l, norm2_g, w_out, layer, w_gu_b,
                  w_down_b, norm_f, *, seq_len, row0, final_norm):
    n = x.shape[0]
    tile = 512
    band, icnt = pool_consts
    row = lambda c: pl.BlockSpec((tile, c), lambda i: (i, 0))
    small = [lw["delta_g"], lw["sgu_g"], lw["w_spatial"], lw["b_spatial"], lw["seg"], band, icnt,
             lw["w_pool"], lw["pool_scale"], mods_l, norm2_g]
    consts = small + [w_out, w_gu_b, w_down_b, norm_f]
    in_specs = ([row(D_MODEL), row(A_WIDTH), row(A_WIDTH), row(2 * B_WIDTH), row(C_WIDTH)]
                + [_const_spec(a.shape) for a in small]
                + [_layer_spec(w_out.shape, layer), _const_spec(w_gu_b.shape),
                   _const_spec(w_down_b.shape), _const_spec(norm_f.shape)])
    return pl.pallas_call(
        functools.partial(_out_kernel, seq_len=seq_len, tile=tile, row0=row0,
                          final_norm=final_norm),
        grid=(n // tile,),
        in_specs=in_specs,
        out_specs=row(D_MODEL),
        out_shape=jax.ShapeDtypeStruct((n, D_MODEL), F32),
        scratch_shapes=[pltpu.VMEM((tile, D_MODEL), BF16),
                        pltpu.VMEM((D_MODEL, D_MODEL), BF16)],
        compiler_params=_params(1),
        name="output_stage",
    )(x, o, z, uv, p, *consts)


def _pool_constants(seg_len):
    pos = np.arange(POOL_TILE)
    seg = pos // seg_len
    band = np.zeros((len(POOL_WINDOWS), POOL_TILE, POOL_TILE), np.float32)
    icnt = np.zeros((POOL_TILE, C_WIDTH), np.float32)
    for g, win in enumerate(POOL_WINDOWS):
        lo = pos - win // 2
        hi = pos + win - win // 2
        inside = (pos[None, :] >= lo[:, None]) & (pos[None, :] < hi[:, None]) \
            & (seg[None, :] == seg[:, None])
        band[g] = inside
        icnt[:, g * C_GC:(g + 1) * C_GC] = (1.0 / inside.sum(axis=1))[:, None]
    band_cat = np.concatenate([band[g] for g in range(len(POOL_WINDOWS))] * 2, axis=1)
    return jnp.asarray(band_cat, BF16), jnp.asarray(icnt)


def _grid_pos_tables(rows, d):
    quarter = d // 4
    f = np.float32
    omega = (f(1.0) / (f(10000.0) ** (np.arange(quarter, dtype=f) / f(quarter)))).astype(f)
    r = np.arange(rows, dtype=f)[:, None] * omega
    cl = np.arange(GRID_W, dtype=f)[:, None] * omega
    row_emb = np.concatenate([np.sin(r), np.cos(r)], axis=-1).astype(f)
    col_emb = np.concatenate([np.sin(cl), np.cos(cl)], axis=-1).astype(f)
    return jnp.asarray(row_emb), jnp.asarray(col_emb)


def _pad_lanes(a, width):
    return jnp.pad(a, ((0, 0), (0, width - a.shape[1])))


def _layer_weights(l, w_pool, w_spatial, b_spatial):
    wp = jnp.zeros((C_WIDTH, C_WIDTH), F32)
    for g in range(len(POOL_WINDOWS)):
        wp = wp.at[g * C_GC:(g + 1) * C_GC, g * C_GC:(g + 1) * C_GC].set(w_pool[l, g])
    grp = np.arange(B_WIDTH) // B_GC
    seg1 = (grp[:, None] == grp[None, :]).astype(np.float32) / B_GC
    seg = jnp.asarray(np.concatenate([seg1, seg1], axis=0), BF16)
    bs = jnp.repeat(b_spatial[l].T, B_GC, axis=1)
    ws_cat = jnp.concatenate([w_spatial[l, g] for g in range(B_GROUPS)], axis=1)
    return wp.astype(BF16), seg, ws_cat.astype(BF16), bs


def kernel(x_prompt, x_sample, state_delta, c, c_ctx, w_in, conv_w, a_log, dt_bias, delta_norm_g,
           sgu_norm_g, w_spatial, b_spatial, w_pool, pool_scale, w_out, norm1_g, norm2_g, w_mod,
           b_mod, w_gu, w_down, norm_f):
    batch, seq, d = x_prompt.shape
    dec_batch, dec_seq, _ = x_sample.shape
    cvec = jnp.concatenate([c_ctx[None, :], c, jnp.zeros((8 - 1 - dec_batch, d), F32)], axis=0)
    mods = _modulation(cvec, w_mod, b_mod)
    row_emb, col_emb = _grid_pos_tables(dec_seq // GRID_W, d)
    xs = _add_pos(x_sample, row_emb, col_emb).reshape(dec_batch * dec_seq, d)
    xc = x_prompt.reshape(batch * seq, d)
    pool_ctx = _pool_constants(min(seq, POOL_TILE))
    pool_lat = _pool_constants(GRID_W)
    nf = norm_f.reshape(1, d)
    delta_consts = _delta_constants()
    w_in_b = w_in.astype(BF16)
    ctx_states = None
    for l in range(DEPTH):
        wp, seg, ws, bs = _layer_weights(l, w_pool, w_spatial, b_spatial)
        lw = {"delta_g": delta_norm_g[l].reshape(1, HEAD_DIM),
              "sgu_g": sgu_norm_g[l].reshape(1, B_WIDTH), "w_spatial": ws, "b_spatial": bs,
              "seg": seg, "w_pool": wp, "pool_scale": pool_scale[l].reshape(1, C_WIDTH)}
        alog_row = _pad_lanes(a_log[l].reshape(1, 2 * HEADS), 128)
        dtb_row = _pad_lanes(dt_bias[l].reshape(1, 2 * HEADS), 128)
        n1 = norm1_g[l].reshape(1, d)
        n2 = norm2_g[l].reshape(1, d)
        last = l == DEPTH - 1
        s0_lat = state_delta[:, l]
        streams = (("ctx", xc, seq, 0, None, pool_ctx), ("lat", xs, dec_seq, 1, s0_lat, pool_lat))
        outs = []
        for name, x, seq_len, row0, s0, pool_consts in streams:
            qkv, z, ba, uv, p = _input_stage(x, mods[l], n1, w_in_b, l, conv_w[l],
                                             seq_len=seq_len, row0=row0)
            if s0 is None:
                o, ctx_states, w_gu_b, w_down_b = _delta_stage(
                    qkv, ba, alog_row, dtb_row, None, delta_consts, seq_len=seq_len, layer=l,
                    st_prev=ctx_states, cast_weights=(w_gu, w_down))
            else:
                o = _delta_stage(qkv, ba, alog_row, dtb_row, s0, delta_consts, seq_len=seq_len)
            outs.append(_output_stage(x, o, z, uv, p, lw, pool_consts, mods[l], n2, w_out, l,
                                      w_gu_b, w_down_b, nf, seq_len=seq_len, row0=row0,
                                      final_norm=last))
        xc, xs = outs
    y_prompt = xc.reshape(batch, seq, d)
    y_sample = xs.reshape(dec_batch, dec_seq, d)
    return (y_prompt, y_sample, ctx_states)
```

```python
import functools
import math

import numpy as np
import jax
import jax.numpy as jnp
from jax import lax
from jax.experimental import pallas as pl
from jax.experimental.pallas import tpu as pltpu

F32 = jnp.float32
BF16 = jnp.bfloat16
HIGHEST = lax.Precision.HIGHEST

D_MODEL = 1024
DEPTH = 2
GRID_W = 64
HEADS = 4
HEAD_DIM = 128
QK = HEADS * HEAD_DIM
A_WIDTH = HEADS * HEAD_DIM
QKV = 2 * QK + A_WIDTH
CHUNK = 64
B_WIDTH = 256
B_GROUPS = 4
B_GC = 64
SGU_CHUNK = 128
C_WIDTH = 256
POOL_WINDOWS = (2, 4, 8, 16)
C_GC = 64
FF_HIDDEN = 2816
FF_SPLIT = 1
N_MOD = 6
EPS = 1e-6
IN_COLS = 2 * QK + 2 * A_WIDTH + 4 * HEADS + 2 * B_WIDTH + C_WIDTH
IN_Z = QKV
IN_UV = IN_Z + A_WIDTH
IN_P = IN_UV + 2 * B_WIDTH
IN_BA = IN_P + C_WIDTH
IN_PAD = IN_BA + 128
NEG_BIG = -1e30

DELTA_BLOCK = 1024
POOL_TILE = 256
VMEM_LIMIT = 58 * 1024 * 1024


def _dot(a, b, precision=None):
    return jnp.dot(a, b, preferred_element_type=F32, precision=precision)


def _bdot(a, b):
    return jnp.dot(a.astype(BF16), b.astype(BF16), preferred_element_type=F32)


def _sigmoid(x):
    return 0.5 * jnp.tanh(0.5 * x) + 0.5


def _silu(x):
    h = 0.5 * x
    return h + h * jnp.tanh(h)


def _softplus(x):
    return jnp.maximum(x, 0.0) + jnp.log1p(jnp.exp(-jnp.abs(x)))


def _gelu_tanh(x):
    c = math.sqrt(2.0 / math.pi)
    return 0.5 * x * (1.0 + jnp.tanh(c * (x + 0.044715 * (x * x * x))))


def _rms_rows(x):
    return x * lax.rsqrt(jnp.mean(x * x, axis=-1, keepdims=True) + EPS)


def _params(n_grid):
    return pltpu.CompilerParams(dimension_semantics=("arbitrary",) * n_grid,
                                vmem_limit_bytes=VMEM_LIMIT)


def _const_spec(shape):
    nd = len(shape)
    return pl.BlockSpec(shape, lambda *_: (0,) * nd, pipeline_mode=pl.Buffered(1))


def _layer_spec(shape, layer):
    nd = len(shape) - 1
    return pl.BlockSpec((None,) + tuple(shape[1:]), lambda *_: (layer,) + (0,) * nd,
                        pipeline_mode=pl.Buffered(1))


def _mod_kernel(c_ref, w_ref, b_ref, o_ref):
    a = _silu(c_ref[...])
    o_ref[0] = _bdot(a, w_ref[0]) + b_ref[0]


def _modulation(cvec, w_mod, b_mod):
    tn = 1536
    n_out = N_MOD * D_MODEL
    return pl.pallas_call(
        _mod_kernel,
        grid=(DEPTH, n_out // tn),
        in_specs=[pl.BlockSpec((8, D_MODEL), lambda l, j: (0, 0)),
                  pl.BlockSpec((1, D_MODEL, tn), lambda l, j: (l, 0, j)),
                  pl.BlockSpec((1, 1, tn), lambda l, j: (l, 0, j))],
        out_specs=pl.BlockSpec((1, 8, tn), lambda l, j: (l, 0, j)),
        out_shape=jax.ShapeDtypeStruct((DEPTH, 8, n_out), F32),
        compiler_params=_params(2),
        name="modulation",
    )(cvec, w_mod, b_mod.reshape(DEPTH, 1, n_out))


POS_ROWS = 8


def _pos_kernel(x_ref, r_ref, c_ref, o_ref):
    half = D_MODEL // 2
    col = c_ref[...]
    for r in range(POS_ROWS):
        rs = slice(r * GRID_W, (r + 1) * GRID_W)
        o_ref[rs, 0:half] = x_ref[rs, 0:half] + r_ref[r:r + 1, :]
        o_ref[rs, half:] = x_ref[rs, half:] + col


def _add_pos(x, row_emb, col_emb):
    b, n, d = x.shape
    tm = POS_ROWS * GRID_W
    return pl.pallas_call(
        _pos_kernel,
        grid=(b, n // tm),
        in_specs=[pl.BlockSpec((None, tm, d), lambda i, j: (i, j, 0)),
                  pl.BlockSpec((POS_ROWS, d // 2), lambda i, j: (j, 0)),
                  pl.BlockSpec((GRID_W, d // 2), lambda i, j: (0, 0))],
        out_specs=pl.BlockSpec((None, tm, d), lambda i, j: (i, j, 0)),
        out_shape=jax.ShapeDtypeStruct(x.shape, F32),
        compiler_params=_params(2),
        name="add_pos",
    )(x, row_emb, col_emb)


def _modulated_norm(x, gain, shift, scale):
    return _rms_rows(x) * gain * (1.0 + scale) + shift


def _pack_w_in(ws_ref, wb_ref):
    n_logit = 4 * HEADS
    rows = 256
    lane = lax.broadcasted_iota(jnp.int32, (rows, 128), 1)
    for r in range(0, D_MODEL, rows):
        rs = slice(r, r + rows)
        wb_ref[rs, 0:IN_UV] = ws_ref[rs, 0:IN_UV]
        tail = ws_ref[rs, IN_UV:IN_COLS]
        wb_ref[rs, IN_UV:IN_BA] = tail[:, n_logit:]
        wb_ref[rs, IN_BA:IN_PAD] = jnp.where(lane < n_logit, tail[:, 0:128],
                                             jnp.zeros((rows, 128), BF16))


def _in_kernel(*refs, seq_len, tile, row0, halo):
    if halo:
        (x_ref, xp_ref, xn_ref, mod_ref, g_ref, wf_ref,
         cw_ref, qkv_ref, z_ref, ba_ref, uv_ref, p_ref, win_ref) = refs
    else:
        (x_ref, mod_ref, g_ref, wf_ref,
         cw_ref, qkv_ref, z_ref, ba_ref, uv_ref, p_ref, win_ref) = refs
    i = pl.program_id(0)

    @pl.when(i == 0)
    def _prepare_weights():
        _pack_w_in(wf_ref, win_ref)

    tiles_per_seq = max(seq_len // tile, 1)
    row = row0 + i // tiles_per_seq if row0 else 0
    m = mod_ref[pl.ds(row, 1), :]
    shift = m[:, 0:D_MODEL]
    scale = m[:, D_MODEL:2 * D_MODEL]
    gain = g_ref[...]
    hf = _modulated_norm(x_ref[...], gain, shift, scale)
    hb = hf.astype(BF16)
    if halo:
        hp = _modulated_norm(xp_ref[...], gain, shift, scale)
        hn = _modulated_norm(xn_ref[...], gain, shift, scale)
        hq = jnp.concatenate([hf, hp, hn], axis=0).astype(BF16)
    else:
        hq = hb
    ridx = lax.broadcasted_iota(jnp.int32, (tile, 1), 0)
    if halo:
        pos_in_seq = (i % tiles_per_seq) * tile + ridx
    else:
        pos_in_seq = ridx % seq_len
    first_row = ridx == 0
    last_row = ridx == tile - 1
    seq_start = pos_in_seq == 0
    seq_end = pos_in_seq == seq_len - 1
    cw = cw_ref[...]
    blk = 2 * HEAD_DIM

    def project(b):
        return _dot(hq, win_ref[:, b * blk:(b + 1) * blk])

    def conv_act(b, pre):
        cols = slice(b * blk, (b + 1) * blk)
        cur = pre[0:tile]
        prev = pltpu.roll(cur, 1, 0)
        nxt = pltpu.roll(cur, tile - 1, 0)
        if halo:
            prev = jnp.where(first_row, pre[tile + 7:tile + 8], prev)
            nxt = jnp.where(last_row, pre[tile + 8:tile + 9], nxt)
        prev = jnp.where(seq_start, 0.0, prev)
        nxt = jnp.where(seq_end, 0.0, nxt)
        act = _silu(prev * cw[0:1, cols] + cur * cw[1:2, cols] + nxt * cw[2:3, cols])
        if b * blk >= 2 * QK:
            qkv_ref[:, cols] = act.astype(BF16)
            return
        scale = HEAD_DIM ** -0.5 if b * blk < QK else 1.0
        for h in range(blk // HEAD_DIM):
            a = act[:, h * HEAD_DIM:(h + 1) * HEAD_DIM]
            inv = lax.rsqrt(jnp.sum(a * a, axis=-1, keepdims=True) + EPS) * scale
            c0 = b * blk + h * HEAD_DIM
            qkv_ref[:, c0:c0 + HEAD_DIM] = (a * inv).astype(BF16)

    others = []
    for out_ref, c0 in ((z_ref, IN_Z), (uv_ref, IN_UV), (p_ref, IN_P), (ba_ref, IN_BA)):
        width = out_ref.shape[1]
        others += [(out_ref, c0, slice(c, min(c + blk, width))) for c in range(0, width, blk)]
    n_blk = QKV // blk
    assert len(others) == n_blk
    pre = project(0)
    for b in range(n_blk):
        nxt_pre = project(b + 1) if b + 1 < n_blk else None
        out_ref, c0, cols = others[b]
        out_ref[:, cols] = _dot(hb, win_ref[:, c0 + cols.start:c0 + cols.stop])
        conv_act(b, pre)
        pre = nxt_pre


def _input_stage(x, mods_l, norm_g, w_in, layer, conv_w, *, seq_len, row0):
    n = x.shape[0]
    tile = 512
    halo = seq_len > tile
    tiles_per_seq = max(seq_len // tile, 1)
    n_tiles = n // tile
    h8 = tile // 8
    last8 = n // 8 - 1
    in_specs = [pl.BlockSpec((tile, D_MODEL), lambda i: (i, 0))]
    args = [x]
    if halo:
        in_specs += [pl.BlockSpec((8, D_MODEL), lambda i: (jnp.maximum(i * h8 - 1, 0), 0)),
                     pl.BlockSpec((8, D_MODEL), lambda i: (jnp.minimum((i + 1) * h8, last8), 0))]
        args += [x, x]
    in_specs += [_const_spec((8, N_MOD * D_MODEL)), _const_spec((1, D_MODEL)),
                 _layer_spec(w_in.shape, layer), _const_spec((3, QKV))]
    args += [mods_l, norm_g, w_in, conv_w]
    widths = (QKV, A_WIDTH, 128, 2 * B_WIDTH, C_WIDTH)
    dtypes = (BF16, F32, F32, F32, F32)
    return pl.pallas_call(
        functools.partial(_in_kernel, seq_len=seq_len, tile=tile, row0=row0, halo=halo),
        grid=(n_tiles,),
        in_specs=in_specs,
        out_specs=[pl.BlockSpec((tile, c), lambda i: (i, 0)) for c in widths],
        out_shape=[jax.ShapeDtypeStruct((n, c), t) for c, t in zip(widths, dtypes)],
        scratch_shapes=[pltpu.VMEM((D_MODEL, IN_PAD), BF16)],
        compiler_params=_params(1),
        name="input_stage",
    )(*args)


N_LEVELS = int(math.log2(CHUNK))
WIDE = HEADS * CHUNK


def _delta_constants():
    r = np.arange(CHUNK)[:, None]
    c = (np.arange(WIDE) % CHUNK)[None, :]
    lvl = np.zeros((2, N_LEVELS, CHUNK, WIDE), np.float32)
    for d in range(2):
        rr, cc = (r, c) if d == 0 else (c, r)
        for i in range(N_LEVELS):
            s = 1 << i
            lvl[d, i] = (rr // (2 * s) == cc // (2 * s)) & ((rr // s) % 2 == 1) & ((cc // s) % 2 == 0)
    eye = (r == c).astype(np.float32)
    tri = np.concatenate([np.tril(np.ones((CHUNK, CHUNK))), np.triu(np.ones((CHUNK, CHUNK)))], 0)
    tri = np.concatenate([tri] * 3, axis=1)
    return (jnp.asarray(lvl.reshape(2 * N_LEVELS, CHUNK, WIDE)), jnp.asarray(eye),
            jnp.asarray(tri, BF16))


def _pair(a, b, lt_half):
    return jnp.where(lt_half, a, b)


def _block_diag(blocks, zero):
    n = len(blocks)
    rows = [jnp.concatenate([blocks[i] if j == i else zero for j in range(n)], axis=1)
            for i in range(n)]
    return jnp.concatenate(rows, axis=0)


def _delta_kernel(*refs, n_sub, n_chunks, has_init, n_par, cb, layer, n_cast):
    it = iter(refs)
    qkv_ref, ba_ref, alog_ref, dtb_ref, lvl_ref, eye_ref, tri_ref = (next(it) for _ in range(7))
    s0_ref = next(it) if has_init else None
    sp_ref = next(it) if (not has_init and layer > 0) else None
    cast_in = [next(it) for _ in range(n_cast)]
    o_ref = next(it)
    st_ref = None if has_init else next(it)
    for src_ref in cast_in:
        dst_ref = next(it)
        dst_ref[...] = src_ref[...].astype(BF16)
    (s_ref, gc_ref, beta_ref, gct_ref, u_ref, w_ref, qd_ref, qkd_ref, kdt_ref, gt_ref) = it
    n_total = n_sub * n_chunks
    lane = lax.broadcasted_iota(jnp.int32, (CHUNK, 128), 1)
    lt_half = lane < CHUNK
    lt_half_row = lt_half[0:1]
    rw = lax.broadcasted_iota(jnp.int32, (CHUNK, WIDE), 0)
    cw = lax.broadcasted_iota(jnp.int32, (CHUNK, WIDE), 1) & (CHUNK - 1)
    incl = (rw >= cw, rw <= cw)
    a_row = -jnp.exp(alog_ref[...])
    dtb_row = dtb_ref[...]

    def gate_body(bi, carry):
        span = cb * CHUNK
        r0 = pl.multiple_of(bi * span, span)
        ba = ba_ref[pl.ds(r0, span), :]
        beta_ref[pl.ds(r0, span), :] = _sigmoid(ba)
        al = pltpu.roll(ba, 128 - 2 * HEADS, 1)
        lane_b = lax.broadcasted_iota(jnp.int32, (span, 128), 1)
        g = jnp.where(lane_b < 2 * HEADS, a_row * _softplus(al + dtb_row), 0.0)
        g_hi = g.astype(BF16)
        r1 = g - g_hi.astype(F32)
        g_mid = r1.astype(BF16)
        g_lo = (r1 - g_mid.astype(F32)).astype(BF16)
        css = []
        for i in range(cb):
            rs = slice(i * CHUNK, (i + 1) * CHUNK)
            terms = jnp.concatenate([g_hi[rs], g_mid[rs], g_lo[rs]], axis=0)
            css.append(_dot(tri_ref[...], terms))
        gcs = [jnp.where(lane < HEADS, cs[0:CHUNK], cs[CHUNK:]) for cs in css]
        ats = [jnp.concatenate([gc, gc], axis=0).T[0:8, :] for gc in gcs]
        for i in range(cb):
            gc_ref[pl.ds(pl.multiple_of(r0 + i * CHUNK, CHUNK), CHUNK), :] = gcs[i]
            gct_ref[bi * cb + i] = ats[i]
        return carry

    n_batches = n_total // cb
    gate_body(0, 0)

    zero_blk = jnp.zeros((CHUNK, HEAD_DIM), BF16)
    hs = [slice(h * HEAD_DIM, (h + 1) * HEAD_DIM) for h in range(HEADS)]

    def head_mm(x, y):
        xb = x.astype(BF16)
        yb = y.astype(BF16)
        zero = jnp.zeros((CHUNK, 128), BF16)
        outs = []
        for pr in range(HEADS // 2):
            ys = yb[:, pr * 128:(pr + 1) * 128]
            bd = jnp.concatenate([jnp.where(lt_half, ys, zero), jnp.where(lt_half, zero, ys)], axis=0)
            outs.append(_dot(xb[:, pr * 128:(pr + 1) * 128], bd))
        return jnp.concatenate(outs, axis=1)

    def pre_body(bi, carry):
        groups = []
        for i in range(cb):
            c = bi * cb + i
            r0 = pl.multiple_of(c * CHUNK, CHUNK)
            rows = pl.ds(r0, CHUNK)
            qb = qkv_ref[rows, 0:QK]
            kb16 = qkv_ref[rows, QK:2 * QK]
            qf = qb.astype(F32)
            kf = kb16.astype(F32)
            vf = qkv_ref[rows, 2 * QK:].astype(F32)
            kt = jnp.concatenate(
                [jnp.concatenate([kf[:, 0:128], kf[:, 128:256]], axis=0).T,
                 jnp.concatenate([kf[:, 256:384], kf[:, 384:512]], axis=0).T], axis=1)
            ktbd = _group_rows(kt.astype(BF16), HEADS)
            qkk = _dot(jnp.concatenate([qb, kb16], axis=0), ktbd)
            cols = gc_ref[rows, :]
            bcols = beta_ref[rows, :]
            at = gct_ref[c]
            for d in range(2):
                j0 = d * HEADS
                bg = [jnp.broadcast_to(cols[:, j0 + h:j0 + h + 1], (CHUNK, 128)) for h in range(HEADS)]
                bb = [jnp.broadcast_to(bcols[:, j0 + h:j0 + h + 1], (CHUNK, 128)) for h in range(HEADS)]
                gcw = jnp.concatenate([_pair(bg[0], bg[1], lt_half), _pair(bg[2], bg[3], lt_half)], 1)
                bw = jnp.concatenate([_pair(bb[0], bb[1], lt_half), _pair(bb[2], bb[3], lt_half)], 1)
                gr = jnp.concatenate(
                    [_pair(at[j0:j0 + 1], at[j0 + 1:j0 + 2], lt_half_row),
                     _pair(at[j0 + 2:j0 + 3], at[j0 + 3:j0 + 4], lt_half_row)], axis=1)
                decay = jnp.exp(jnp.where(incl[d], gcw - gr, NEG_BIG))
                m = bw * qkk[CHUNK:] * decay
                qkd_ref[d, rows, :] = (qkk[0:CHUNK] * decay).astype(BF16)
                en = jnp.concatenate([jnp.exp(x) for x in bg], axis=1)
                bn = jnp.concatenate(bb, axis=1)
                qd_ref[d, rows, :] = (qf * en).astype(BF16)
                vb = (vf * bn).astype(BF16)
                kbe = (kf * (bn * en)).astype(BF16)
                e0 = CHUNK - 1 if d == 0 else 0
                bt = [jnp.broadcast_to(at[j0 + h:j0 + h + 1, e0:e0 + 1], (1, 128)) for h in range(HEADS)]
                tw = jnp.concatenate([_pair(bt[0], bt[1], lt_half_row),
                                      _pair(bt[2], bt[3], lt_half_row)], axis=1)
                kdt_ref[d, pl.ds(pl.multiple_of(c * HEAD_DIM, HEAD_DIM), HEAD_DIM), :] = (
                    kt * jnp.exp(tw - gr)).astype(BF16)
                gt_ref[d, c] = jnp.broadcast_to(
                    jnp.concatenate([jnp.exp(x) for x in bt], axis=1), (8, A_WIDTH))
                groups.append((d, rows, m, vb, kbe))
        gate_body(jnp.minimum(bi + 1, n_batches - 1), 0)
        ts = [eye_ref[...] - g[2] * lvl_ref[g[0] * N_LEVELS] for g in groups]
        for lv in range(1, N_LEVELS):
            xs = [head_mm(t, g[2] * lvl_ref[g[0] * N_LEVELS + lv]) for t, g in zip(ts, groups)]
            ts = [t - head_mm(x, t) for t, x in zip(ts, xs)]
        for t, (d, rows, _, vb, kbe) in zip(ts, groups):
            tb = t.astype(BF16)
            uw = []
            for pr in range(HEADS // 2):
                h0, h1 = hs[2 * pr], hs[2 * pr + 1]
                rhs = jnp.concatenate(
                    [jnp.concatenate([vb[:, h0], zero_blk, kbe[:, h0], zero_blk], axis=1),
                     jnp.concatenate([zero_blk, vb[:, h1], zero_blk, kbe[:, h1]], axis=1)], axis=0)
                uw.append(_dot(tb[:, pr * 128:(pr + 1) * 128], rhs))
            half = A_WIDTH // 2
            u_ref[d, rows, :] = jnp.concatenate([uw[0][:, 0:half], uw[1][:, 0:half]], axis=1)
            w_ref[d, rows, :] = jnp.concatenate([uw[0][:, half:], uw[1][:, half:]],
                                                axis=1).astype(BF16)
        return carry

    lax.fori_loop(0, n_batches, pre_body, 0)

    o_ref[...] = jnp.zeros(o_ref.shape, F32)
    zero_s = jnp.zeros((HEAD_DIM, HEAD_DIM), BF16)

    def scan_step(sub0, c):
        groups = [(p, d) for p in range(n_par) for d in range(2)]
        cidx = [(sub0 + p) * n_chunks + (c if d == 0 else n_chunks - 1 - c) for p, d in groups]
        rows = [pl.ds(pl.multiple_of(ci * CHUNK, CHUNK), CHUNK) for ci in cidx]
        ss = [s_ref[p, d] for p, d in groups]
        wqs = []
        for (p, d), r, s in zip(groups, rows, ss):
            sb = s.astype(BF16)
            lhs = jnp.concatenate([w_ref[d, r, :], qd_ref[d, r, :]], axis=0)
            halves = []
            for hp in range(HEADS // 2):
                bd = _block_diag([sb[:, hs[2 * hp]], sb[:, hs[2 * hp + 1]]], zero_s)
                halves.append(_dot(lhs[:, hp * 2 * HEAD_DIM:(hp + 1) * 2 * HEAD_DIM], bd))
            wqs.append(jnp.concatenate(halves, axis=1))
        boths = []
        for (p, d), r, ci, wq in zip(groups, rows, cidx, wqs):
            v_new = (u_ref[d, r, :] - wq[0:CHUNK]).astype(BF16)
            vbd = _block_diag([v_new[:, s] for s in hs], zero_blk)
            kd = kdt_ref[d, pl.ds(pl.multiple_of(ci * HEAD_DIM, HEAD_DIM), HEAD_DIM), :]
            lhs2 = jnp.concatenate([qkd_ref[d, r, :], kd], axis=0)
            boths.append(_dot(lhs2, vbd))
        for (p, d), r, ci, s, wq, both in zip(groups, rows, cidx, ss, wqs, boths):
            o_ref[r, :] += wq[CHUNK:] + both[0:CHUNK]
            s_ref[p, d] = s * gt_ref[d, ci][0:1] + both[CHUNK:]

    def sub_body(sb_i, carry):
        sub0 = sb_i * n_par
        for p in range(n_par):
            for d in range(2):
                if has_init:
                    s_ref[p, d] = jnp.concatenate(
                        [s0_ref[d, h] for h in range(HEADS)], axis=1)
                else:
                    s_ref[p, d] = jnp.zeros((HEAD_DIM, A_WIDTH), F32)

        def step(c, carry2):
            scan_step(sub0, c)
            return carry2

        lax.fori_loop(0, n_chunks, step, 0)
        if st_ref is not None:
            for p in range(n_par):
                for d in range(2):
                    s = s_ref[p, d]
                    for h in range(HEADS):
                        st_ref[sub0 + p, layer, d, h] = s[:, hs[h]]
                        for earlier in range(layer):
                            st_ref[sub0 + p, earlier, d, h] = sp_ref[sub0 + p, earlier, d, h]
        return carry

    lax.fori_loop(0, n_sub // n_par, sub_body, 0)


def _delta_stage(qkv, ba, alog_row, dtb_row, s0, consts, *, seq_len, layer=0, st_prev=None,
                 cast_weights=()):
    n = qkv.shape[0]
    block = max(seq_len, DELTA_BLOCK)
    n_blocks = n // block
    n_sub = block // seq_len
    n_chunks = seq_len // CHUNK
    n_total = block // CHUNK
    has_init = s0 is not None
    n_par = min(n_sub, 4)
    lvl, eye, tri = consts
    in_specs = [pl.BlockSpec((block, QKV), lambda i: (i, 0)),
                pl.BlockSpec((block, 128), lambda i: (i, 0)),
                _const_spec((1, 128)), _const_spec((1, 128)),
                _const_spec(lvl.shape), _const_spec(eye.shape), _const_spec(tri.shape)]
    args = [qkv, ba, alog_row, dtb_row, lvl, eye, tri]
    o_spec = pl.BlockSpec((block, A_WIDTH), lambda i: (i, 0))
    o_shape = jax.ShapeDtypeStruct((n, A_WIDTH), F32)
    st_shape = (2, HEADS, HEAD_DIM, HEAD_DIM)
    if has_init:
        assert n_sub == 1
        in_specs.append(pl.BlockSpec((None,) + st_shape, lambda i: (i, 0, 0, 0, 0)))
        args.append(s0)
        out_specs, out_shape = o_spec, o_shape
    else:
        if layer > 0:
            in_specs.append(pl.BlockSpec((n_sub, layer) + st_shape, lambda i: (i, 0, 0, 0, 0, 0)))
            args.append(st_prev)
        out_specs = [o_spec,
                     pl.BlockSpec((n_sub, layer + 1) + st_shape, lambda i: (i, 0, 0, 0, 0, 0))]
        out_shape = [o_shape, jax.ShapeDtypeStruct((n // seq_len, layer + 1) + st_shape, F32)]
    scratch = [pltpu.VMEM((n_par, 2, HEAD_DIM, A_WIDTH), F32),
               pltpu.VMEM((block, 128), F32),
               pltpu.VMEM((block, 128), F32),
               pltpu.VMEM((n_total, 8, 128), F32),
               pltpu.VMEM((2, block, A_WIDTH), F32),
               pltpu.VMEM((2, block, A_WIDTH), BF16),
               pltpu.VMEM((2, block, A_WIDTH), BF16),
               pltpu.VMEM((2, block, WIDE), BF16),
               pltpu.VMEM((2, n_total * HEAD_DIM, WIDE), BF16),
               pltpu.VMEM((2, n_total, 8, A_WIDTH), F32)]
    for wt in cast_weights:
        rows = wt.shape[1] // n_blocks
        assert rows * n_blocks == wt.shape[1] and rows % 16 == 0
        in_specs.append(pl.BlockSpec((None, rows, wt.shape[2]), lambda i: (layer, i, 0)))
        args.append(wt)
        out_specs = list(out_specs) if isinstance(out_specs, list) else [out_specs]
        out_shape = list(out_shape) if isinstance(out_shape, list) else [out_shape]
        out_specs.append(pl.BlockSpec((rows, wt.shape[2]), lambda i: (i, 0)))
        out_shape.append(jax.ShapeDtypeStruct(wt.shape[1:], BF16))
    return pl.pallas_call(
        functools.partial(_delta_kernel, n_sub=n_sub, n_chunks=n_chunks, has_init=has_init,
                          n_par=n_par, cb=4, layer=layer, n_cast=len(cast_weights)),
        grid=(n_blocks,),
        in_specs=in_specs,
        out_specs=out_specs,
        out_shape=out_shape,
        scratch_shapes=scratch,
        compiler_params=_params(1),
        name="delta_stage",
    )(*args)


def _group_rows(y, n_groups):
    half = y.shape[1] // 2
    gw = half // 2
    lane = lax.broadcasted_iota(jnp.int32, (y.shape[0], half), 1)
    zero = jnp.zeros((y.shape[0], half), y.dtype)
    blocks = []
    for g in range(n_groups):
        part = y[:, (g // 2) * half:(g // 2 + 1) * half]
        keep = (lane < gw) if g % 2 == 0 else (lane >= gw)
        part = jnp.where(keep, part, jnp.zeros_like(part))
        blocks.append(jnp.concatenate([part, zero] if g < 2 else [zero, part], axis=1))
    return jnp.concatenate(blocks, axis=0)


def _mix_delta(o_ref, z_ref, dg_ref, mix_ref):
    dg = dg_ref[...]
    for h in range(HEADS):
        cs = slice(h * HEAD_DIM, (h + 1) * HEAD_DIM)
        y = _rms_rows(o_ref[:, cs]) * dg * _silu(z_ref[:, cs])
        mix_ref[:, cs] = y.astype(BF16)


def _mix_sgu(uv_ref, sg_ref, ws_ref, bs_ref, seg_ref, mix_ref, tile):
    u = _gelu_tanh(uv_ref[:, 0:B_WIDTH])
    v = _gelu_tanh(uv_ref[:, B_WIDTH:])
    vv = v * v
    vv_hi = vv.astype(BF16)
    vv_lo = (vv - vv_hi.astype(F32)).astype(BF16)
    ms = _dot(jnp.concatenate([vv_hi, vv_lo], axis=1), seg_ref[...])
    vn = (v * lax.rsqrt(ms + EPS) * sg_ref[...]).astype(BF16)
    for c in range(tile // SGU_CHUNK):
        rs = slice(c * SGU_CHUNK, (c + 1) * SGU_CHUNK)
        s = _dot(ws_ref[...], _group_rows(vn[rs], B_GROUPS)) + bs_ref[...]
        mix_ref[rs, A_WIDTH:A_WIDTH + B_WIDTH] = (u[rs] * s).astype(BF16)


def _mix_pool(p_ref, band_ref, icnt_ref, wp_ref, ps_ref, mix_ref, tile):
    for c in range(tile // POOL_TILE):
        rs = slice(c * POOL_TILE, (c + 1) * POOL_TILE)
        x = p_ref[rs, :]
        x_hi = x.astype(BF16)
        x_lo = (x - x_hi.astype(F32)).astype(BF16)
        n_win = len(POOL_WINDOWS)
        rhs = jnp.concatenate([_group_rows(x_hi, n_win), _group_rows(x_lo, n_win)], axis=0)
        wsum = _dot(band_ref[...], rhs)
        diff = wsum * icnt_ref[...] - x
        y = _bdot(diff, wp_ref[...]) * ps_ref[...]
        mix_ref[rs, A_WIDTH + B_WIDTH:] = y.astype(BF16)


def _out_kernel(x_ref, o_ref, z_ref, uv_ref, p_ref, dg_ref, sg_ref, ws_ref, bs_ref, seg_ref,
                band_ref, icnt_ref, wp_ref, ps_ref, mod_ref, g2_ref, wof_ref, wgu_ref, wd_ref,
                nf_ref, y_ref, mix_ref, wo_ref, *, seq_len, tile, row0, final_norm):
    i = pl.program_id(0)

    @pl.when(i == 0)
    def _prepare_weights():
        rows = 256
        for r in range(0, D_MODEL, rows):
            wo_ref[r:r + rows, :] = wof_ref[r:r + rows, :].astype(BF16)

    tiles_per_seq = max(seq_len // tile, 1)
    row = row0 + i // tiles_per_seq if row0 else 0
    m = mod_ref[pl.ds(row, 1), :]
    gate1 = m[:, 2 * D_MODEL:3 * D_MODEL]
    shift2 = m[:, 3 * D_MODEL:4 * D_MODEL]
    scale2 = m[:, 4 * D_MODEL:5 * D_MODEL]
    gate2 = m[:, 5 * D_MODEL:6 * D_MODEL]
    _mix_delta(o_ref, z_ref, dg_ref, mix_ref)
    _mix_sgu(uv_ref, sg_ref, ws_ref, bs_ref, seg_ref, mix_ref, tile)
    _mix_pool(p_ref, band_ref, icnt_ref, wp_ref, ps_ref, mix_ref, tile)
    x1 = x_ref[...] + gate1 * _dot(mix_ref[...], wo_ref[...])
    hb = _modulated_norm(x1, g2_ref[...], shift2, scale2).astype(BF16)
    fc = FF_HIDDEN // FF_SPLIT
    ff = jnp.zeros((tile, D_MODEL), F32)
    for c in range(FF_SPLIT):
        gate = _dot(hb, wgu_ref[:, c * fc:(c + 1) * fc])
        up = _dot(hb, wgu_ref[:, FF_HIDDEN + c * fc:FF_HIDDEN + (c + 1) * fc])
        act = (_silu(gate) * up).astype(BF16)
        ff = ff + _dot(act, wd_ref[c * fc:(c + 1) * fc, :])
    x2 = x1 + gate2 * ff
    if final_norm:
        x2 = _rms_rows(x2) * nf_ref[...]
    y_ref[...] = x2


def _output_stage(x, o, z, uv, p, lw, pool_consts, mods_l, norm2_g, w_out, layer, w_gu_b,
                  w_down_b, norm_f, *, seq_len, row0, final_norm):
    n = x.shape[0]
    tile = 512
    band, icnt = pool_consts
    row = lambda c: pl.BlockSpec((tile, c), lambda i: (i, 0))
    small = [lw["delta_g"], lw["sgu_g"], lw["w_spatial"], lw["b_spatial"], lw["seg"], band, icnt,
             lw["w_pool"], lw["pool_scale"], mods_l, norm2_g]
    consts = small + [w_out, w_gu_b, w_down_b, norm_f]
    in_specs = ([row(D_MODEL), row(A_WIDTH), row(A_WIDTH), row(2 * B_WIDTH), row(C_WIDTH)]
                + [_const_spec(a.shape) for a in small]
                + [_layer_spec(w_out.shape, layer), _const_spec(w_gu_b.shape),
                   _const_spec(w_down_b.shape), _const_spec(norm_f.shape)])
    return pl.pallas_call(
        functools.partial(_out_kernel, seq_len=seq_len, tile=tile, row0=row0,
                          final_norm=final_norm),
        grid=(n // tile,),
        in_specs=in_specs,
        out_specs=row(D_MODEL),
        out_shape=jax.ShapeDtypeStruct((n, D_MODEL), F32),
        scratch_shapes=[pltpu.VMEM((tile, D_MODEL), BF16),
                        pltpu.VMEM((D_MODEL, D_MODEL), BF16)],
        compiler_params=_params(1),
        name="output_stage",
    )(x, o, z, uv, p, *consts)


def _pool_constants(seg_len):
    pos = np.arange(POOL_TILE)
    seg = pos // seg_len
    band = np.zeros((len(POOL_WINDOWS), POOL_TILE, POOL_TILE), np.float32)
    icnt = np.zeros((POOL_TILE, C_WIDTH), np.float32)
    for g, win in enumerate(POOL_WINDOWS):
        lo = pos - win // 2
        hi = pos + win - win // 2
        inside = (pos[None, :] >= lo[:, None]) & (pos[None, :] < hi[:, None]) \
            & (seg[None, :] == seg[:, None])
        band[g] = inside
        icnt[:, g * C_GC:(g + 1) * C_GC] = (1.0 / inside.sum(axis=1))[:, None]
    band_cat = np.concatenate([band[g] for g in range(len(POOL_WINDOWS))] * 2, axis=1)
    return jnp.asarray(band_cat, BF16), jnp.asarray(icnt)


def _grid_pos_tables(rows, d):
    quarter = d // 4
    f = np.float32
    omega = (f(1.0) / (f(10000.0) ** (np.arange(quarter, dtype=f) / f(quarter)))).astype(f)
    r = np.arange(rows, dtype=f)[:, None] * omega
    cl = np.arange(GRID_W, dtype=f)[:, None] * omega
    row_emb = np.concatenate([np.sin(r), np.cos(r)], axis=-1).astype(f)
    col_emb = np.concatenate([np.sin(cl), np.cos(cl)], axis=-1).astype(f)
    return jnp.asarray(row_emb), jnp.asarray(col_emb)


def _pad_lanes(a, width):
    return jnp.pad(a, ((0, 0), (0, width - a.shape[1])))


def _layer_weights(l, w_pool, w_spatial, b_spatial):
    wp = jnp.zeros((C_WIDTH, C_WIDTH), F32)
    for g in range(len(POOL_WINDOWS)):
        wp = wp.at[g * C_GC:(g + 1) * C_GC, g * C_GC:(g + 1) * C_GC].set(w_pool[l, g])
    grp = np.arange(B_WIDTH) // B_GC
    seg1 = (grp[:, None] == grp[None, :]).astype(np.float32) / B_GC
    seg = jnp.asarray(np.concatenate([seg1, seg1], axis=0), BF16)
    bs = jnp.repeat(b_spatial[l].T, B_GC, axis=1)
    ws_cat = jnp.concatenate([w_spatial[l, g] for g in range(B_GROUPS)], axis=1)
    return wp.astype(BF16), seg, ws_cat.astype(BF16), bs


def kernel(x_prompt, x_sample, state_delta, c, c_ctx, w_in, conv_w, a_log, dt_bias, delta_norm_g,
           sgu_norm_g, w_spatial, b_spatial, w_pool, pool_scale, w_out, norm1_g, norm2_g, w_mod,
           b_mod, w_gu, w_down, norm_f):
    batch, seq, d = x_prompt.shape
    dec_batch, dec_seq, _ = x_sample.shape
    cvec = jnp.concatenate([c_ctx[None, :], c, jnp.zeros((8 - 1 - dec_batch, d), F32)], axis=0)
    mods = _modulation(cvec, w_mod, b_mod)
    row_emb, col_emb = _grid_pos_tables(dec_seq // GRID_W, d)
    xs = _add_pos(x_sample, row_emb, col_emb).reshape(dec_batch * dec_seq, d)
    xc = x_prompt.reshape(batch * seq, d)
    pool_ctx = _pool_constants(min(seq, POOL_TILE))
    pool_lat = _pool_constants(GRID_W)
    nf = norm_f.reshape(1, d)
    delta_consts = _delta_constants()
    w_in_b = w_in.astype(BF16)
    ctx_states = None
    for l in range(DEPTH):
        wp, seg, ws, bs = _layer_weights(l, w_pool, w_spatial, b_spatial)
        lw = {"delta_g": delta_norm_g[l].reshape(1, HEAD_DIM),
              "sgu_g": sgu_norm_g[l].reshape(1, B_WIDTH), "w_spatial": ws, "b_spatial": bs,
              "seg": seg, "w_pool": wp, "pool_scale": pool_scale[l].reshape(1, C_WIDTH)}
        alog_row = _pad_lanes(a_log[l].reshape(1, 2 * HEADS), 128)
        dtb_row = _pad_lanes(dt_bias[l].reshape(1, 2 * HEADS), 128)
        n1 = norm1_g[l].reshape(1, d)
        n2 = norm2_g[l].reshape(1, d)
        last = l == DEPTH - 1
        s0_lat = state_delta[:, l]
        streams = (("ctx", xc, seq, 0, None, pool_ctx), ("lat", xs, dec_seq, 1, s0_lat, pool_lat))
        outs = []
        for name, x, seq_len, row0, s0, pool_consts in streams:
            qkv, z, ba, uv, p = _input_stage(x, mods[l], n1, w_in_b, l, conv_w[l],
                                             seq_len=seq_len, row0=row0)
            if s0 is None:
                o, ctx_states, w_gu_b, w_down_b = _delta_stage(
                    qkv, ba, alog_row, dtb_row, None, delta_consts, seq_len=seq_len, layer=l,
                    st_prev=ctx_states, cast_weights=(w_gu, w_down))
            else:
                o = _delta_stage(qkv, ba, alog_row, dtb_row, s0, delta_consts, seq_len=seq_len)
            outs.append(_output_stage(x, o, z, uv, p, lw, pool_consts, mods[l], n2, w_out, l,
                                      w_gu_b, w_down_b, nf, seq_len=seq_len, row0=row0,
                                      final_norm=last))
        xc, xs = outs
    y_prompt = xc.reshape(batch, seq, d)
    y_sample = xs.reshape(dec_batch, dec_seq, d)
    return (y_prompt, y_sample, ctx_states)
```

```python
import functools
import math

import numpy as np
import jax
import jax.numpy as jnp
from jax import lax
from jax.experimental import pallas as pl
from jax.experimental.pallas import tpu as pltpu

F32 = jnp.float32
BF16 = jnp.bfloat16

LANES = 128
D_MODEL = 1024
DEPTH = 2
GRID_W = 64
HEADS = 4
HEAD_DIM = 128
QK = HEADS * HEAD_DIM
A_WIDTH = HEADS * HEAD_DIM
QKV = 2 * QK + A_WIDTH
CHUNK = 64
B_WIDTH = 256
B_GROUPS = 4
B_GC = 64
SGU_CHUNK = 128
C_WIDTH = 256
POOL_WINDOWS = (2, 4, 8, 16)
C_GC = 64
FF_HIDDEN = 2816
FF_SPLIT = 1
N_MOD = 6
EPS = 1e-6
IN_COLS = 2 * QK + 2 * A_WIDTH + 4 * HEADS + 2 * B_WIDTH + C_WIDTH
IN_Z = QKV
IN_UV = IN_Z + A_WIDTH
IN_P = IN_UV + 2 * B_WIDTH
IN_BA = IN_P + C_WIDTH
IN_PAD = IN_BA + LANES
NEG_BIG = -1e30

DELTA_BLOCK = 1024
DELTA_BATCH = 4
DELTA_PAR = 4
TOKEN_TILE = 512
POOL_TILE = 256
VMEM_LIMIT = 58 * 1024 * 1024


def _dot(a, b):
    return jnp.dot(a, b, preferred_element_type=F32)


def _bdot(a, b):
    return jnp.dot(a.astype(BF16), b.astype(BF16), preferred_element_type=F32)


def _sigmoid(x):
    return 0.5 * jnp.tanh(0.5 * x) + 0.5


def _silu(x):
    h = 0.5 * x
    return h + h * jnp.tanh(h)


def _softplus(x):
    return jnp.maximum(x, 0.0) + jnp.log1p(jnp.exp(-jnp.abs(x)))


def _gelu_tanh(x):
    c = math.sqrt(2.0 / math.pi)
    return 0.5 * x * (1.0 + jnp.tanh(c * (x + 0.044715 * (x * x * x))))


def _rms_rows(x):
    return x * lax.rsqrt(jnp.mean(x * x, axis=-1, keepdims=True) + EPS)


def _params(n_grid):
    return pltpu.CompilerParams(dimension_semantics=("arbitrary",) * n_grid,
                                vmem_limit_bytes=VMEM_LIMIT)


def _const_spec(shape):
    nd = len(shape)
    return pl.BlockSpec(shape, lambda *_: (0,) * nd, pipeline_mode=pl.Buffered(1))


def _layer_spec(shape, layer):
    nd = len(shape) - 1
    return pl.BlockSpec((None,) + tuple(shape[1:]), lambda *_: (layer,) + (0,) * nd,
                        pipeline_mode=pl.Buffered(1))


def _mod_kernel(c_ref, w_ref, b_ref, o_ref):
    a = _silu(c_ref[...])
    o_ref[0] = _bdot(a, w_ref[0]) + b_ref[0]


def _modulation(cvec, w_mod, b_mod):
    tn = 3072
    n_out = N_MOD * D_MODEL
    return pl.pallas_call(
        _mod_kernel,
        grid=(DEPTH, n_out // tn),
        in_specs=[pl.BlockSpec((8, D_MODEL), lambda l, j: (0, 0)),
                  pl.BlockSpec((1, D_MODEL, tn), lambda l, j: (l, 0, j)),
                  pl.BlockSpec((1, 1, tn), lambda l, j: (l, 0, j))],
        out_specs=pl.BlockSpec((1, 8, tn), lambda l, j: (l, 0, j)),
        out_shape=jax.ShapeDtypeStruct((DEPTH, 8, n_out), F32),
        compiler_params=_params(2),
        name="modulation",
    )(cvec, w_mod, b_mod.reshape(DEPTH, 1, n_out))


POS_ROWS = 8


def _pos_kernel(x_ref, r_ref, c_ref, o_ref):
    half = D_MODEL // 2
    col = c_ref[...]
    for r in range(POS_ROWS):
        rs = slice(r * GRID_W, (r + 1) * GRID_W)
        o_ref[rs, 0:half] = x_ref[rs, 0:half] + r_ref[r:r + 1, :]
        o_ref[rs, half:] = x_ref[rs, half:] + col


def _add_pos(x, row_emb, col_emb):
    b, n, d = x.shape
    tm = POS_ROWS * GRID_W
    return pl.pallas_call(
        _pos_kernel,
        grid=(b, n // tm),
        in_specs=[pl.BlockSpec((None, tm, d), lambda i, j: (i, j, 0)),
                  pl.BlockSpec((POS_ROWS, d // 2), lambda i, j: (j, 0)),
                  pl.BlockSpec((GRID_W, d // 2), lambda i, j: (0, 0))],
        out_specs=pl.BlockSpec((None, tm, d), lambda i, j: (i, j, 0)),
        out_shape=jax.ShapeDtypeStruct(x.shape, F32),
        compiler_params=_params(2),
        name="add_pos",
    )(x, row_emb, col_emb)


def _modulated_norm(x, gain, shift, scale):
    return _rms_rows(x) * gain * (1.0 + scale) + shift


def _pack_w_in(ws_ref, wb_ref):
    n_logit = 4 * HEADS
    rows = 256
    lane = lax.broadcasted_iota(jnp.int32, (rows, LANES), 1)
    for r in range(0, D_MODEL, rows):
        rs = slice(r, r + rows)
        wb_ref[rs, 0:IN_UV] = ws_ref[rs, 0:IN_UV]
        tail = ws_ref[rs, IN_UV:IN_COLS]
        wb_ref[rs, IN_UV:IN_BA] = tail[:, n_logit:]
        wb_ref[rs, IN_BA:IN_PAD] = jnp.where(lane < n_logit, tail[:, 0:LANES],
                                             jnp.zeros((rows, LANES), BF16))


def _in_kernel(*refs, seq_len, tile, row0, halo):
    if halo:
        (x_ref, xp_ref, xn_ref, mod_ref, g_ref, wf_ref,
         cw_ref, qkv_ref, z_ref, ba_ref, uv_ref, p_ref, win_ref) = refs
    else:
        (x_ref, mod_ref, g_ref, wf_ref,
         cw_ref, qkv_ref, z_ref, ba_ref, uv_ref, p_ref, win_ref) = refs
    i = pl.program_id(0)

    @pl.when(i == 0)
    def _prepare_weights():
        _pack_w_in(wf_ref, win_ref)

    tiles_per_seq = max(seq_len // tile, 1)
    row = row0 + i // tiles_per_seq if row0 else 0
    m = mod_ref[pl.ds(row, 1), :]
    shift = m[:, 0:D_MODEL]
    scale = m[:, D_MODEL:2 * D_MODEL]
    gain = g_ref[...]
    hf = _modulated_norm(x_ref[...], gain, shift, scale)
    hb = hf.astype(BF16)
    if halo:
        hp = _modulated_norm(xp_ref[...], gain, shift, scale)
        hn = _modulated_norm(xn_ref[...], gain, shift, scale)
        hq = jnp.concatenate([hf, hp, hn], axis=0).astype(BF16)
    else:
        hq = hb
    ridx = lax.broadcasted_iota(jnp.int32, (tile, 1), 0)
    if halo:
        pos_in_seq = (i % tiles_per_seq) * tile + ridx
    else:
        pos_in_seq = ridx % seq_len
    first_row = ridx == 0
    last_row = ridx == tile - 1
    seq_start = pos_in_seq == 0
    seq_end = pos_in_seq == seq_len - 1
    cw = cw_ref[...]
    blk = 2 * HEAD_DIM

    def project(b):
        return _dot(hq, win_ref[:, b * blk:(b + 1) * blk])

    def conv_act(b, pre):
        cols = slice(b * blk, (b + 1) * blk)
        cur = pre[0:tile]
        prev = pltpu.roll(cur, 1, 0)
        nxt = pltpu.roll(cur, tile - 1, 0)
        if halo:
            prev = jnp.where(first_row, pre[tile + 7:tile + 8], prev)
            nxt = jnp.where(last_row, pre[tile + 8:tile + 9], nxt)
        prev = jnp.where(seq_start, 0.0, prev)
        nxt = jnp.where(seq_end, 0.0, nxt)
        act = _silu(prev * cw[0:1, cols] + cur * cw[1:2, cols] + nxt * cw[2:3, cols])
        if b * blk >= 2 * QK:
            qkv_ref[:, cols] = act.astype(BF16)
            return
        scale = HEAD_DIM ** -0.5 if b * blk < QK else 1.0
        for h in range(blk // HEAD_DIM):
            a = act[:, h * HEAD_DIM:(h + 1) * HEAD_DIM]
            inv = lax.rsqrt(jnp.sum(a * a, axis=-1, keepdims=True) + EPS) * scale
            c0 = b * blk + h * HEAD_DIM
            qkv_ref[:, c0:c0 + HEAD_DIM] = (a * inv).astype(BF16)

    others = []
    for out_ref, c0 in ((z_ref, IN_Z), (uv_ref, IN_UV), (p_ref, IN_P), (ba_ref, IN_BA)):
        width = out_ref.shape[1]
        others += [(out_ref, c0, slice(c, min(c + blk, width))) for c in range(0, width, blk)]
    n_blk = QKV // blk
    assert len(others) == n_blk
    pre = project(0)
    for b in range(n_blk):
        nxt_pre = project(b + 1) if b + 1 < n_blk else None
        out_ref, c0, cols = others[b]
        out_ref[:, cols] = _dot(hb, win_ref[:, c0 + cols.start:c0 + cols.stop])
        conv_act(b, pre)
        pre = nxt_pre


def _input_stage(x, mods_l, norm_g, w_in, layer, conv_w, *, seq_len, row0):
    n = x.shape[0]
    tile = TOKEN_TILE
    halo = seq_len > tile
    tiles_per_seq = max(seq_len // tile, 1)
    n_tiles = n // tile
    h8 = tile // 8
    last8 = n // 8 - 1
    in_specs = [pl.BlockSpec((tile, D_MODEL), lambda i: (i, 0))]
    args = [x]
    if halo:
        in_specs += [pl.BlockSpec((8, D_MODEL), lambda i: (jnp.maximum(i * h8 - 1, 0), 0)),
                     pl.BlockSpec((8, D_MODEL), lambda i: (jnp.minimum((i + 1) * h8, last8), 0))]
        args += [x, x]
    in_specs += [_const_spec((8, N_MOD * D_MODEL)), _const_spec((1, D_MODEL)),
                 _layer_spec(w_in.shape, layer), _const_spec((3, QKV))]
    args += [mods_l, norm_g, w_in, conv_w]
    widths = (QKV, A_WIDTH, LANES, 2 * B_WIDTH, C_WIDTH)
    dtypes = (BF16, F32, F32, F32, F32)
    return pl.pallas_call(
        functools.partial(_in_kernel, seq_len=seq_len, tile=tile, row0=row0, halo=halo),
        grid=(n_tiles,),
        in_specs=in_specs,
        out_specs=[pl.BlockSpec((tile, c), lambda i: (i, 0)) for c in widths],
        out_shape=[jax.ShapeDtypeStruct((n, c), t) for c, t in zip(widths, dtypes)],
        scratch_shapes=[pltpu.VMEM((D_MODEL, IN_PAD), BF16)],
        compiler_params=_params(1),
        name="input_stage",
    )(*args)


N_LEVELS = int(math.log2(CHUNK))
WIDE = HEADS * CHUNK


def _delta_constants():
    r = np.arange(CHUNK)[:, None]
    c = (np.arange(WIDE) % CHUNK)[None, :]
    lvl = np.zeros((2, N_LEVELS, CHUNK, WIDE), np.float32)
    for d in range(2):
        rr, cc = (r, c) if d == 0 else (c, r)
        for i in range(N_LEVELS):
            s = 1 << i
            lvl[d, i] = (rr // (2 * s) == cc // (2 * s)) & ((rr // s) % 2 == 1) & ((cc // s) % 2 == 0)
    eye = (r == c).astype(np.float32)
    tri = np.concatenate([np.tril(np.ones((CHUNK, CHUNK))), np.triu(np.ones((CHUNK, CHUNK)))], 0)
    tri = np.concatenate([tri] * 3, axis=1)
    return (jnp.asarray(lvl.reshape(2 * N_LEVELS, CHUNK, WIDE)), jnp.asarray(eye),
            jnp.asarray(tri, BF16))


def _pair(a, b, lt_half):
    return jnp.where(lt_half, a, b)


def _block_diag(blocks, zero):
    n = len(blocks)
    rows = [jnp.concatenate([blocks[i] if j == i else zero for j in range(n)], axis=1)
            for i in range(n)]
    return jnp.concatenate(rows, axis=0)


def _delta_kernel(*refs, n_sub, n_chunks, has_init, n_par, cb, layer, n_cast):
    it = iter(refs)
    qkv_ref, ba_ref, alog_ref, dtb_ref, lvl_ref, eye_ref, tri_ref = (next(it) for _ in range(7))
    s0_ref = next(it) if has_init else None
    sp_ref = next(it) if (not has_init and layer > 0) else None
    cast_in = [next(it) for _ in range(n_cast)]
    o_ref = next(it)
    st_ref = None if has_init else next(it)
    for src_ref in cast_in:
        dst_ref = next(it)
        dst_ref[...] = src_ref[...].astype(BF16)
    (s_ref, gc_ref, beta_ref, gct_ref, u_ref, w_ref, qd_ref, qkd_ref, kdt_ref, gt_ref) = it
    n_total = n_sub * n_chunks
    lane = lax.broadcasted_iota(jnp.int32, (CHUNK, LANES), 1)
    lt_half = lane < CHUNK
    lt_half_row = lt_half[0:1]
    rw = lax.broadcasted_iota(jnp.int32, (CHUNK, WIDE), 0)
    cw = lax.broadcasted_iota(jnp.int32, (CHUNK, WIDE), 1) & (CHUNK - 1)
    incl = (rw >= cw, rw <= cw)
    a_row = -jnp.exp(alog_ref[...])
    dtb_row = dtb_ref[...]

    def gate_body(bi, carry):
        span = cb * CHUNK
        r0 = pl.multiple_of(bi * span, span)
        ba = ba_ref[pl.ds(r0, span), :]
        beta_ref[pl.ds(r0, span), :] = _sigmoid(ba)
        al = pltpu.roll(ba, LANES - 2 * HEADS, 1)
        lane_b = lax.broadcasted_iota(jnp.int32, (span, LANES), 1)
        g = jnp.where(lane_b < 2 * HEADS, a_row * _softplus(al + dtb_row), 0.0)
        g_hi = g.astype(BF16)
        r1 = g - g_hi.astype(F32)
        g_mid = r1.astype(BF16)
        g_lo = (r1 - g_mid.astype(F32)).astype(BF16)
        css = []
        for i in range(cb):
            rs = slice(i * CHUNK, (i + 1) * CHUNK)
            terms = jnp.concatenate([g_hi[rs], g_mid[rs], g_lo[rs]], axis=0)
            css.append(_dot(tri_ref[...], terms))
        gcs = [jnp.where(lane < HEADS, cs[0:CHUNK], cs[CHUNK:]) for cs in css]
        ats = [jnp.concatenate([gc, gc], axis=0).T[0:8, :] for gc in gcs]
        for i in range(cb):
            gc_ref[pl.ds(pl.multiple_of(r0 + i * CHUNK, CHUNK), CHUNK), :] = gcs[i]
            gct_ref[bi * cb + i] = ats[i]
        return carry

    n_batches = n_total // cb
    gate_body(0, 0)

    zero_blk = jnp.zeros((CHUNK, HEAD_DIM), BF16)
    hs = [slice(h * HEAD_DIM, (h + 1) * HEAD_DIM) for h in range(HEADS)]

    def head_mm(x, y):
        xb = x.astype(BF16)
        yb = y.astype(BF16)
        zero = jnp.zeros((CHUNK, LANES), BF16)
        outs = []
        for pr in range(HEADS // 2):
            ys = yb[:, pr * LANES:(pr + 1) * LANES]
            bd = jnp.concatenate([jnp.where(lt_half, ys, zero), jnp.where(lt_half, zero, ys)], axis=0)
            outs.append(_dot(xb[:, pr * LANES:(pr + 1) * LANES], bd))
        return jnp.concatenate(outs, axis=1)

    def pre_body(bi, carry):
        groups = []
        for i in range(cb):
            c = bi * cb + i
            r0 = pl.multiple_of(c * CHUNK, CHUNK)
            rows = pl.ds(r0, CHUNK)
            qb = qkv_ref[rows, 0:QK]
            kb16 = qkv_ref[rows, QK:2 * QK]
            qf = qb.astype(F32)
            kf = kb16.astype(F32)
            vf = qkv_ref[rows, 2 * QK:].astype(F32)
            kt = jnp.concatenate(
                [jnp.concatenate([kf[:, hs[0]], kf[:, hs[1]]], axis=0).T,
                 jnp.concatenate([kf[:, hs[2]], kf[:, hs[3]]], axis=0).T], axis=1)
            ktbd = _group_rows(kt.astype(BF16), HEADS)
            qkk = _dot(jnp.concatenate([qb, kb16], axis=0), ktbd)
            cols = gc_ref[rows, :]
            bcols = beta_ref[rows, :]
            at = gct_ref[c]
            for d in range(2):
                j0 = d * HEADS
                bg = [jnp.broadcast_to(cols[:, j0 + h:j0 + h + 1], (CHUNK, LANES)) for h in range(HEADS)]
                bb = [jnp.broadcast_to(bcols[:, j0 + h:j0 + h + 1], (CHUNK, LANES)) for h in range(HEADS)]
                gcw = jnp.concatenate([_pair(bg[0], bg[1], lt_half), _pair(bg[2], bg[3], lt_half)], 1)
                bw = jnp.concatenate([_pair(bb[0], bb[1], lt_half), _pair(bb[2], bb[3], lt_half)], 1)
                gr = jnp.concatenate(
                    [_pair(at[j0:j0 + 1], at[j0 + 1:j0 + 2], lt_half_row),
                     _pair(at[j0 + 2:j0 + 3], at[j0 + 3:j0 + 4], lt_half_row)], axis=1)
                decay = jnp.exp(jnp.where(incl[d], gcw - gr, NEG_BIG))
                m = bw * qkk[CHUNK:] * decay
                qkd_ref[d, rows, :] = (qkk[0:CHUNK] * decay).astype(BF16)
                en = jnp.concatenate([jnp.exp(x) for x in bg], axis=1)
                bn = jnp.concatenate(bb, axis=1)
                qd_ref[d, rows, :] = (qf * en).astype(BF16)
                vb = (vf * bn).astype(BF16)
                kbe = (kf * (bn * en)).astype(BF16)
                e0 = CHUNK - 1 if d == 0 else 0
                bt = [jnp.broadcast_to(at[j0 + h:j0 + h + 1, e0:e0 + 1], (1, LANES)) for h in range(HEADS)]
                tw = jnp.concatenate([_pair(bt[0], bt[1], lt_half_row),
                                      _pair(bt[2], bt[3], lt_half_row)], axis=1)
                kdt_ref[d, pl.ds(pl.multiple_of(c * HEAD_DIM, HEAD_DIM), HEAD_DIM), :] = (
                    kt * jnp.exp(tw - gr)).astype(BF16)
                gt_ref[d, c] = jnp.broadcast_to(
                    jnp.concatenate([jnp.exp(x) for x in bt], axis=1), (8, A_WIDTH))
                groups.append((d, rows, m, vb, kbe))
        gate_body(jnp.minimum(bi + 1, n_batches - 1), 0)
        ts = [eye_ref[...] - g[2] * lvl_ref[g[0] * N_LEVELS] for g in groups]
        for lv in range(1, N_LEVELS):
            xs = [head_mm(t, g[2] * lvl_ref[g[0] * N_LEVELS + lv]) for t, g in zip(ts, groups)]
            ts = [t - head_mm(x, t) for t, x in zip(ts, xs)]
        for t, (d, rows, _, vb, kbe) in zip(ts, groups):
            tb = t.astype(BF16)
            uw = []
            for pr in range(HEADS // 2):
                h0, h1 = hs[2 * pr], hs[2 * pr + 1]
                rhs = jnp.concatenate(
                    [jnp.concatenate([vb[:, h0], zero_blk, kbe[:, h0], zero_blk], axis=1),
                     jnp.concatenate([zero_blk, vb[:, h1], zero_blk, kbe[:, h1]], axis=1)], axis=0)
                uw.append(_dot(tb[:, pr * LANES:(pr + 1) * LANES], rhs))
            half = A_WIDTH // 2
            u_ref[d, rows, :] = jnp.concatenate([uw[0][:, 0:half], uw[1][:, 0:half]], axis=1)
            w_ref[d, rows, :] = jnp.concatenate([uw[0][:, half:], uw[1][:, half:]],
                                                axis=1).astype(BF16)
        return carry

    lax.fori_loop(0, n_batches, pre_body, 0)

    o_ref[...] = jnp.zeros(o_ref.shape, F32)
    zero_s = jnp.zeros((HEAD_DIM, HEAD_DIM), BF16)

    def scan_step(sub0, c):
        groups = [(p, d) for p in range(n_par) for d in range(2)]
        cidx = [(sub0 + p) * n_chunks + (c if d == 0 else n_chunks - 1 - c) for p, d in groups]
        rows = [pl.ds(pl.multiple_of(ci * CHUNK, CHUNK), CHUNK) for ci in cidx]
        ss = [s_ref[p, d] for p, d in groups]
        wqs = []
        for (p, d), r, s in zip(groups, rows, ss):
            sb = s.astype(BF16)
            lhs = jnp.concatenate([w_ref[d, r, :], qd_ref[d, r, :]], axis=0)
            halves = []
            for hp in range(HEADS // 2):
                bd = _block_diag([sb[:, hs[2 * hp]], sb[:, hs[2 * hp + 1]]], zero_s)
                halves.append(_dot(lhs[:, hp * 2 * HEAD_DIM:(hp + 1) * 2 * HEAD_DIM], bd))
            wqs.append(jnp.concatenate(halves, axis=1))
        boths = []
        for (p, d), r, ci, wq in zip(groups, rows, cidx, wqs):
            v_new = (u_ref[d, r, :] - wq[0:CHUNK]).astype(BF16)
            vbd = _block_diag([v_new[:, s] for s in hs], zero_blk)
            kd = kdt_ref[d, pl.ds(pl.multiple_of(ci * HEAD_DIM, HEAD_DIM), HEAD_DIM), :]
            lhs2 = jnp.concatenate([qkd_ref[d, r, :], kd], axis=0)
            boths.append(_dot(lhs2, vbd))
        for (p, d), r, ci, s, wq, both in zip(groups, rows, cidx, ss, wqs, boths):
            o_ref[r, :] += wq[CHUNK:] + both[0:CHUNK]
            s_ref[p, d] = s * gt_ref[d, ci][0:1] + both[CHUNK:]

    def sub_body(sb_i, carry):
        sub0 = sb_i * n_par
        for p in range(n_par):
            for d in range(2):
                if has_init:
                    s_ref[p, d] = jnp.concatenate(
                        [s0_ref[d, h] for h in range(HEADS)], axis=1)
                else:
                    s_ref[p, d] = jnp.zeros((HEAD_DIM, A_WIDTH), F32)

        def step(c, carry2):
            scan_step(sub0, c)
            return carry2

        lax.fori_loop(0, n_chunks, step, 0)
        if st_ref is not None:
            for p in range(n_par):
                for d in range(2):
                    s = s_ref[p, d]
                    for h in range(HEADS):
                        st_ref[sub0 + p, layer, d, h] = s[:, hs[h]]
                        for earlier in range(layer):
                            st_ref[sub0 + p, earlier, d, h] = sp_ref[sub0 + p, earlier, d, h]
        return carry

    lax.fori_loop(0, n_sub // n_par, sub_body, 0)


def _delta_stage(qkv, ba, alog_row, dtb_row, s0, consts, *, seq_len, layer=0, st_prev=None,
                 cast_weights=()):
    n = qkv.shape[0]
    block = max(seq_len, DELTA_BLOCK)
    n_blocks = n // block
    n_sub = block // seq_len
    n_chunks = seq_len // CHUNK
    n_total = block // CHUNK
    has_init = s0 is not None
    n_par = min(n_sub, DELTA_PAR)
    lvl, eye, tri = consts
    in_specs = [pl.BlockSpec((block, QKV), lambda i: (i, 0)),
                pl.BlockSpec((block, LANES), lambda i: (i, 0)),
                _const_spec((1, LANES)), _const_spec((1, LANES)),
                _const_spec(lvl.shape), _const_spec(eye.shape), _const_spec(tri.shape)]
    args = [qkv, ba, alog_row, dtb_row, lvl, eye, tri]
    o_spec = pl.BlockSpec((block, A_WIDTH), lambda i: (i, 0))
    o_shape = jax.ShapeDtypeStruct((n, A_WIDTH), F32)
    st_shape = (2, HEADS, HEAD_DIM, HEAD_DIM)
    if has_init:
        assert n_sub == 1
        in_specs.append(pl.BlockSpec((None,) + st_shape, lambda i: (i, 0, 0, 0, 0)))
        args.append(s0)
        out_specs, out_shape = o_spec, o_shape
    else:
        if layer > 0:
            in_specs.append(pl.BlockSpec((n_sub, layer) + st_shape, lambda i: (i, 0, 0, 0, 0, 0)))
            args.append(st_prev)
        out_specs = [o_spec,
                     pl.BlockSpec((n_sub, layer + 1) + st_shape, lambda i: (i, 0, 0, 0, 0, 0))]
        out_shape = [o_shape, jax.ShapeDtypeStruct((n // seq_len, layer + 1) + st_shape, F32)]
    scratch = [pltpu.VMEM((n_par, 2, HEAD_DIM, A_WIDTH), F32),
               pltpu.VMEM((block, LANES), F32),
               pltpu.VMEM((block, LANES), F32),
               pltpu.VMEM((n_total, 8, LANES), F32),
               pltpu.VMEM((2, block, A_WIDTH), F32),
               pltpu.VMEM((2, block, A_WIDTH), BF16),
               pltpu.VMEM((2, block, A_WIDTH), BF16),
               pltpu.VMEM((2, block, WIDE), BF16),
               pltpu.VMEM((2, n_total * HEAD_DIM, WIDE), BF16),
               pltpu.VMEM((2, n_total, 8, A_WIDTH), F32)]
    for wt in cast_weights:
        rows = wt.shape[1] // n_blocks
        assert rows * n_blocks == wt.shape[1] and rows % 16 == 0
        in_specs.append(pl.BlockSpec((None, rows, wt.shape[2]), lambda i: (layer, i, 0)))
        args.append(wt)
        out_specs = list(out_specs) if isinstance(out_specs, list) else [out_specs]
        out_shape = list(out_shape) if isinstance(out_shape, list) else [out_shape]
        out_specs.append(pl.BlockSpec((rows, wt.shape[2]), lambda i: (i, 0)))
        out_shape.append(jax.ShapeDtypeStruct(wt.shape[1:], BF16))
    return pl.pallas_call(
        functools.partial(_delta_kernel, n_sub=n_sub, n_chunks=n_chunks, has_init=has_init,
                          n_par=n_par, cb=DELTA_BATCH, layer=layer, n_cast=len(cast_weights)),
        grid=(n_blocks,),
        in_specs=in_specs,
        out_specs=out_specs,
        out_shape=out_shape,
        scratch_shapes=scratch,
        compiler_params=_params(1),
        name="delta_stage",
    )(*args)


def _group_rows(y, n_groups):
    half = y.shape[1] // 2
    gw = half // 2
    lane = lax.broadcasted_iota(jnp.int32, (y.shape[0], half), 1)
    zero = jnp.zeros((y.shape[0], half), y.dtype)
    blocks = []
    for g in range(n_groups):
        part = y[:, (g // 2) * half:(g // 2 + 1) * half]
        keep = (lane < gw) if g % 2 == 0 else (lane >= gw)
        part = jnp.where(keep, part, jnp.zeros_like(part))
        blocks.append(jnp.concatenate([part, zero] if g < 2 else [zero, part], axis=1))
    return jnp.concatenate(blocks, axis=0)


def _mix_delta(o_ref, z_ref, dg_ref, mix_ref):
    dg = dg_ref[...]
    for h in range(HEADS):
        cs = slice(h * HEAD_DIM, (h + 1) * HEAD_DIM)
        y = _rms_rows(o_ref[:, cs]) * dg * _silu(z_ref[:, cs])
        mix_ref[:, cs] = y.astype(BF16)


def _mix_sgu(uv_ref, sg_ref, ws_ref, bs_ref, seg_ref, mix_ref, tile):
    u = _gelu_tanh(uv_ref[:, 0:B_WIDTH])
    v = _gelu_tanh(uv_ref[:, B_WIDTH:])
    vv = v * v
    vv_hi = vv.astype(BF16)
    vv_lo = (vv - vv_hi.astype(F32)).astype(BF16)
    ms = _dot(jnp.concatenate([vv_hi, vv_lo], axis=1), seg_ref[...])
    vn = (v * lax.rsqrt(ms + EPS) * sg_ref[...]).astype(BF16)
    for c in range(tile // SGU_CHUNK):
        rs = slice(c * SGU_CHUNK, (c + 1) * SGU_CHUNK)
        s = _dot(ws_ref[...], _group_rows(vn[rs], B_GROUPS)) + bs_ref[...]
        mix_ref[rs, A_WIDTH:A_WIDTH + B_WIDTH] = (u[rs] * s).astype(BF16)


def _mix_pool(p_ref, band_ref, icnt_ref, wp_ref, ps_ref, mix_ref, tile):
    for c in range(tile // POOL_TILE):
        rs = slice(c * POOL_TILE, (c + 1) * POOL_TILE)
        x = p_ref[rs, :]
        x_hi = x.astype(BF16)
        x_lo = (x - x_hi.astype(F32)).astype(BF16)
        n_win = len(POOL_WINDOWS)
        rhs = jnp.concatenate([_group_rows(x_hi, n_win), _group_rows(x_lo, n_win)], axis=0)
        wsum = _dot(band_ref[...], rhs)
        diff = wsum * icnt_ref[...] - x
        y = _bdot(diff, wp_ref[...]) * ps_ref[...]
        mix_ref[rs, A_WIDTH + B_WIDTH:] = y.astype(BF16)


def _out_kernel(x_ref, o_ref, z_ref, uv_ref, p_ref, dg_ref, sg_ref, ws_ref, bs_ref, seg_ref,
                band_ref, icnt_ref, wp_ref, ps_ref, mod_ref, g2_ref, wof_ref, wgu_ref, wd_ref,
                nf_ref, y_ref, mix_ref, wo_ref, *, seq_len, tile, row0, final_norm):
    i = pl.program_id(0)

    @pl.when(i == 0)
    def _prepare_weights():
        rows = 256
        for r in range(0, D_MODEL, rows):
            wo_ref[r:r + rows, :] = wof_ref[r:r + rows, :].astype(BF16)

    tiles_per_seq = max(seq_len // tile, 1)
    row = row0 + i // tiles_per_seq if row0 else 0
    m = mod_ref[pl.ds(row, 1), :]
    gate1 = m[:, 2 * D_MODEL:3 * D_MODEL]
    shift2 = m[:, 3 * D_MODEL:4 * D_MODEL]
    scale2 = m[:, 4 * D_MODEL:5 * D_MODEL]
    gate2 = m[:, 5 * D_MODEL:6 * D_MODEL]
    _mix_delta(o_ref, z_ref, dg_ref, mix_ref)
    _mix_sgu(uv_ref, sg_ref, ws_ref, bs_ref, seg_ref, mix_ref, tile)
    _mix_pool(p_ref, band_ref, icnt_ref, wp_ref, ps_ref, mix_ref, tile)
    x1 = x_ref[...] + gate1 * _dot(mix_ref[...], wo_ref[...])
    hb = _modulated_norm(x1, g2_ref[...], shift2, scale2).astype(BF16)
    fc = FF_HIDDEN // FF_SPLIT
    ff = jnp.zeros((tile, D_MODEL), F32)
    for c in range(FF_SPLIT):
        gate = _dot(hb, wgu_ref[:, c * fc:(c + 1) * fc])
        up = _dot(hb, wgu_ref[:, FF_HIDDEN + c * fc:FF_HIDDEN + (c + 1) * fc])
        act = (_silu(gate) * up).astype(BF16)
        ff = ff + _dot(act, wd_ref[c * fc:(c + 1) * fc, :])
    x2 = x1 + gate2 * ff
    if final_norm:
        x2 = _rms_rows(x2) * nf_ref[...]
    y_ref[...] = x2


def _output_stage(x, o, z, uv, p, lw, pool_consts, mods_l, norm2_g, w_out, layer, w_gu_b,
                  w_down_b, norm_f, *, seq_len, row0, final_norm):
    n = x.shape[0]
    tile = TOKEN_TILE
    band, icnt = pool_consts
    row = lambda c: pl.BlockSpec((tile, c), lambda i: (i, 0))
    small = [lw["delta_g"], lw["sgu_g"], lw["w_spatial"], lw["b_spatial"], lw["seg"], band, icnt,
             lw["w_pool"], lw["pool_scale"], mods_l, norm2_g]
    consts = small + [w_out, w_gu_b, w_down_b, norm_f]
    in_specs = ([row(D_MODEL), row(A_WIDTH), row(A_WIDTH), row(2 * B_WIDTH), row(C_WIDTH)]
                + [_const_spec(a.shape) for a in small]
                + [_layer_spec(w_out.shape, layer), _const_spec(w_gu_b.shape),
                   _const_spec(w_down_b.shape), _const_spec(norm_f.shape)])
    return pl.pallas_call(
        functools.partial(_out_kernel, seq_len=seq_len, tile=tile, row0=row0,
                          final_norm=final_norm),
        grid=(n // tile,),
        in_specs=in_specs,
        out_specs=row(D_MODEL),
        out_shape=jax.ShapeDtypeStruct((n, D_MODEL), F32),
        scratch_shapes=[pltpu.VMEM((tile, D_MODEL), BF16),
                        pltpu.VMEM((D_MODEL, D_MODEL), BF16)],
        compiler_params=_params(1),
        name="output_stage",
    )(x, o, z, uv, p, *consts)


def _pool_constants(seg_len):
    pos = np.arange(POOL_TILE)
    seg = pos // seg_len
    band = np.zeros((len(POOL_WINDOWS), POOL_TILE, POOL_TILE), np.float32)
    icnt = np.zeros((POOL_TILE, C_WIDTH), np.float32)
    for g, win in enumerate(POOL_WINDOWS):
        lo = pos - win // 2
        hi = pos + win - win // 2
        inside = (pos[None, :] >= lo[:, None]) & (pos[None, :] < hi[:, None]) \
            & (seg[None, :] == seg[:, None])
        band[g] = inside
        icnt[:, g * C_GC:(g + 1) * C_GC] = (1.0 / inside.sum(axis=1))[:, None]
    band_cat = np.concatenate([band[g] for g in range(len(POOL_WINDOWS))] * 2, axis=1)
    return jnp.asarray(band_cat, BF16), jnp.asarray(icnt)


def _grid_pos_tables(rows, d):
    quarter = d // 4
    f = np.float32
    omega = (f(1.0) / (f(10000.0) ** (np.arange(quarter, dtype=f) / f(quarter)))).astype(f)
    r = np.arange(rows, dtype=f)[:, None] * omega
    cl = np.arange(GRID_W, dtype=f)[:, None] * omega
    row_emb = np.concatenate([np.sin(r), np.cos(r)], axis=-1).astype(f)
    col_emb = np.concatenate([np.sin(cl), np.cos(cl)], axis=-1).astype(f)
    return jnp.asarray(row_emb), jnp.asarray(col_emb)


def _pad_lanes(a, width):
    return jnp.pad(a, ((0, 0), (0, width - a.shape[1])))


def _layer_weights(l, w_pool, w_spatial, b_spatial):
    wp = jnp.zeros((C_WIDTH, C_WIDTH), F32)
    for g in range(len(POOL_WINDOWS)):
        wp = wp.at[g * C_GC:(g + 1) * C_GC, g * C_GC:(g + 1) * C_GC].set(w_pool[l, g])
    grp = np.arange(B_WIDTH) // B_GC
    seg1 = (grp[:, None] == grp[None, :]).astype(np.float32) / B_GC
    seg = jnp.asarray(np.concatenate([seg1, seg1], axis=0), BF16)
    bs = jnp.repeat(b_spatial[l].T, B_GC, axis=1)
    ws_cat = jnp.concatenate([w_spatial[l, g] for g in range(B_GROUPS)], axis=1)
    return wp.astype(BF16), seg, ws_cat.astype(BF16), bs


def kernel(x_prompt, x_sample, state_delta, c, c_ctx, w_in, conv_w, a_log, dt_bias, delta_norm_g,
           sgu_norm_g, w_spatial, b_spatial, w_pool, pool_scale, w_out, norm1_g, norm2_g, w_mod,
           b_mod, w_gu, w_down, norm_f):
    batch, seq, d = x_prompt.shape
    dec_batch, dec_seq, _ = x_sample.shape
    cvec = jnp.concatenate([c_ctx[None, :], c, jnp.zeros((8 - 1 - dec_batch, d), F32)], axis=0)
    mods = _modulation(cvec, w_mod, b_mod)
    row_emb, col_emb = _grid_pos_tables(dec_seq // GRID_W, d)
    xs = _add_pos(x_sample, row_emb, col_emb).reshape(dec_batch * dec_seq, d)
    xc = x_prompt.reshape(batch * seq, d)
    pool_ctx = _pool_constants(min(seq, POOL_TILE))
    pool_lat = _pool_constants(GRID_W)
    nf = norm_f.reshape(1, d)
    delta_consts = _delta_constants()
    w_in_b = w_in.astype(BF16)
    ctx_states = None
    for l in range(DEPTH):
        wp, seg, ws, bs = _layer_weights(l, w_pool, w_spatial, b_spatial)
        lw = {"delta_g": delta_norm_g[l].reshape(1, HEAD_DIM),
              "sgu_g": sgu_norm_g[l].reshape(1, B_WIDTH), "w_spatial": ws, "b_spatial": bs,
              "seg": seg, "w_pool": wp, "pool_scale": pool_scale[l].reshape(1, C_WIDTH)}
        alog_row = _pad_lanes(a_log[l].reshape(1, 2 * HEADS), LANES)
        dtb_row = _pad_lanes(dt_bias[l].reshape(1, 2 * HEADS), LANES)
        n1 = norm1_g[l].reshape(1, d)
        n2 = norm2_g[l].reshape(1, d)
        last = l == DEPTH - 1
        s0_lat = state_delta[:, l]
        streams = (("ctx", xc, seq, 0, None, pool_ctx), ("lat", xs, dec_seq, 1, s0_lat, pool_lat))
        outs = []
        for name, x, seq_len, row0, s0, pool_consts in streams:
            qkv, z, ba, uv, p = _input_stage(x, mods[l], n1, w_in_b, l, conv_w[l],
                                             seq_len=seq_len, row0=row0)
            if s0 is None:
                o, ctx_states, w_gu_b, w_down_b = _delta_stage(
                    qkv, ba, alog_row, dtb_row, None, delta_consts, seq_len=seq_len, layer=l,
                    st_prev=ctx_states, cast_weights=(w_gu, w_down))
            else:
                o = _delta_stage(qkv, ba, alog_row, dtb_row, s0, delta_consts, seq_len=seq_len)
            outs.append(_output_stage(x, o, z, uv, p, lw, pool_consts, mods[l], n2, w_out, l,
                                      w_gu_b, w_down_b, nf, seq_len=seq_len, row0=row0,
                                      final_norm=last))
        xc, xs = outs
    y_prompt = xc.reshape(batch, seq, d)
    y_sample = xs.reshape(dec_batch, dec_seq, d)
    return (y_prompt, y_sample, ctx_states)
```

```python
import functools
import math

import numpy as np
import jax
import jax.numpy as jnp
from jax import lax
from jax.experimental import pallas as pl
from jax.experimental.pallas import tpu as pltpu

F32 = jnp.float32
BF16 = jnp.bfloat16

LANES = 128
D_MODEL = 1024
DEPTH = 2
GRID_W = 64
HEADS = 4
HEAD_DIM = 128
QK = HEADS * HEAD_DIM
A_WIDTH = HEADS * HEAD_DIM
QKV = 2 * QK + A_WIDTH
CHUNK = 64
B_WIDTH = 256
B_GROUPS = 4
B_GC = 64
SGU_CHUNK = 128
C_WIDTH = 256
POOL_WINDOWS = (2, 4, 8, 16)
C_GC = 64
FF_HIDDEN = 2816
FF_SPLIT = 1
N_MOD = 6
EPS = 1e-6
IN_COLS = 2 * QK + 2 * A_WIDTH + 4 * HEADS + 2 * B_WIDTH + C_WIDTH
IN_Z = QKV
IN_UV = IN_Z + A_WIDTH
IN_P = IN_UV + 2 * B_WIDTH
IN_BA = IN_P + C_WIDTH
IN_PAD = IN_BA + LANES
NEG_BIG = -1e30

DELTA_BLOCK = 1024
DELTA_BATCH = 4
DELTA_PAR = 4
TOKEN_TILE = 512
POOL_TILE = 256
VMEM_LIMIT = 58 * 1024 * 1024


def _dot(a, b):
    return jnp.dot(a, b, preferred_element_type=F32)


def _bdot(a, b):
    return jnp.dot(a.astype(BF16), b.astype(BF16), preferred_element_type=F32)


def _sigmoid(x):
    return 0.5 * jnp.tanh(0.5 * x) + 0.5


def _silu(x):
    h = 0.5 * x
    return h + h * jnp.tanh(h)


def _softplus(x):
    return jnp.maximum(x, 0.0) + jnp.log1p(jnp.exp(-jnp.abs(x)))


def _gelu_tanh(x):
    c = math.sqrt(2.0 / math.pi)
    return 0.5 * x * (1.0 + jnp.tanh(c * (x + 0.044715 * (x * x * x))))


def _rms_rows(x):
    return x * lax.rsqrt(jnp.mean(x * x, axis=-1, keepdims=True) + EPS)


def _params(n_grid):
    return pltpu.CompilerParams(dimension_semantics=("arbitrary",) * n_grid,
                                vmem_limit_bytes=VMEM_LIMIT)


def _const_spec(shape):
    nd = len(shape)
    return pl.BlockSpec(shape, lambda *_: (0,) * nd, pipeline_mode=pl.Buffered(1))


def _layer_spec(shape, layer):
    nd = len(shape) - 1
    return pl.BlockSpec((None,) + tuple(shape[1:]), lambda *_: (layer,) + (0,) * nd,
                        pipeline_mode=pl.Buffered(1))


def _mod_kernel(c_ref, w_ref, b_ref, o_ref):
    a = _silu(c_ref[...])
    o_ref[0] = _bdot(a, w_ref[0]) + b_ref[0]


def _modulation(cvec, w_mod, b_mod):
    tn = 3072
    n_out = N_MOD * D_MODEL
    return pl.pallas_call(
        _mod_kernel,
        grid=(DEPTH, n_out // tn),
        in_specs=[pl.BlockSpec((8, D_MODEL), lambda l, j: (0, 0)),
                  pl.BlockSpec((1, D_MODEL, tn), lambda l, j: (l, 0, j)),
                  pl.BlockSpec((1, 1, tn), lambda l, j: (l, 0, j))],
        out_specs=pl.BlockSpec((1, 8, tn), lambda l, j: (l, 0, j)),
        out_shape=jax.ShapeDtypeStruct((DEPTH, 8, n_out), F32),
        compiler_params=_params(2),
        name="modulation",
    )(cvec, w_mod, b_mod.reshape(DEPTH, 1, n_out))


POS_ROWS = 8


def _pos_kernel(x_ref, r_ref, c_ref, o_ref):
    half = D_MODEL // 2
    col = c_ref[...]
    for r in range(POS_ROWS):
        rs = slice(r * GRID_W, (r + 1) * GRID_W)
        o_ref[rs, 0:half] = x_ref[rs, 0:half] + r_ref[r:r + 1, :]
        o_ref[rs, half:] = x_ref[rs, half:] + col


def _add_pos(x, row_emb, col_emb):
    b, n, d = x.shape
    tm = POS_ROWS * GRID_W
    return pl.pallas_call(
        _pos_kernel,
        grid=(b, n // tm),
        in_specs=[pl.BlockSpec((None, tm, d), lambda i, j: (i, j, 0)),
                  pl.BlockSpec((POS_ROWS, d // 2), lambda i, j: (j, 0)),
                  pl.BlockSpec((GRID_W, d // 2), lambda i, j: (0, 0))],
        out_specs=pl.BlockSpec((None, tm, d), lambda i, j: (i, j, 0)),
        out_shape=jax.ShapeDtypeStruct(x.shape, F32),
        compiler_params=_params(2),
        name="add_pos",
    )(x, row_emb, col_emb)


def _modulated_norm(x, gain, shift, scale):
    return _rms_rows(x) * gain * (1.0 + scale) + shift


def _pack_w_in(ws_ref, wb_ref):
    n_logit = 4 * HEADS
    rows = 256
    lane = lax.broadcasted_iota(jnp.int32, (rows, LANES), 1)
    for r in range(0, D_MODEL, rows):
        rs = slice(r, r + rows)
        wb_ref[rs, 0:IN_UV] = ws_ref[rs, 0:IN_UV]
        tail = ws_ref[rs, IN_UV:IN_COLS]
        wb_ref[rs, IN_UV:IN_BA] = tail[:, n_logit:]
        wb_ref[rs, IN_BA:IN_PAD] = jnp.where(lane < n_logit, tail[:, 0:LANES],
                                             jnp.zeros((rows, LANES), BF16))


def _in_kernel(*refs, seq_len, tile, row0, halo):
    if halo:
        (x_ref, xp_ref, xn_ref, mod_ref, g_ref, wf_ref,
         cw_ref, qkv_ref, z_ref, ba_ref, uv_ref, p_ref, win_ref) = refs
    else:
        (x_ref, mod_ref, g_ref, wf_ref,
         cw_ref, qkv_ref, z_ref, ba_ref, uv_ref, p_ref, win_ref) = refs
    i = pl.program_id(0)

    @pl.when(i == 0)
    def _prepare_weights():
        _pack_w_in(wf_ref, win_ref)

    tiles_per_seq = max(seq_len // tile, 1)
    row = row0 + i // tiles_per_seq if row0 else 0
    m = mod_ref[pl.ds(row, 1), :]
    shift = m[:, 0:D_MODEL]
    scale = m[:, D_MODEL:2 * D_MODEL]
    gain = g_ref[...]
    hf = _modulated_norm(x_ref[...], gain, shift, scale)
    hb = hf.astype(BF16)
    if halo:
        hp = _modulated_norm(xp_ref[...], gain, shift, scale)
        hn = _modulated_norm(xn_ref[...], gain, shift, scale)
        hq = jnp.concatenate([hf, hp, hn], axis=0).astype(BF16)
    else:
        hq = hb
    first_tile = i % tiles_per_seq == 0
    last_tile = i % tiles_per_seq == tiles_per_seq - 1
    sub8 = lax.broadcasted_iota(jnp.int32, (8, 1), 0)

    def set_row(arr, row, value):
        s0 = row - row % 8
        slab = jnp.where(sub8 == row % 8, value, arr[s0:s0 + 8])
        parts = [p for p in (arr[0:s0], slab, arr[s0 + 8:]) if p.shape[0]]
        return jnp.concatenate(parts, axis=0)

    cw = cw_ref[...]
    blk = 2 * HEAD_DIM

    def project(b):
        return _dot(hq, win_ref[:, b * blk:(b + 1) * blk])

    def conv_act(b, pre):
        cols = slice(b * blk, (b + 1) * blk)
        cur = pre[0:tile]
        prev = pltpu.roll(cur, 1, 0)
        nxt = pltpu.roll(cur, tile - 1, 0)
        if halo:
            prev = set_row(prev, 0, jnp.where(first_tile, 0.0, pre[tile + 7:tile + 8]))
            nxt = set_row(nxt, tile - 1, jnp.where(last_tile, 0.0, pre[tile + 8:tile + 9]))
        else:
            for start in range(0, tile, seq_len):
                prev = set_row(prev, start, 0.0)
                nxt = set_row(nxt, start + seq_len - 1, 0.0)
        act = _silu(prev * cw[0:1, cols] + cur * cw[1:2, cols] + nxt * cw[2:3, cols])
        if b * blk >= 2 * QK:
            qkv_ref[:, cols] = act.astype(BF16)
            return
        scale = HEAD_DIM ** -0.5 if b * blk < QK else 1.0
        for h in range(blk // HEAD_DIM):
            a = act[:, h * HEAD_DIM:(h + 1) * HEAD_DIM]
            inv = lax.rsqrt(jnp.sum(a * a, axis=-1, keepdims=True) + EPS) * scale
            c0 = b * blk + h * HEAD_DIM
            qkv_ref[:, c0:c0 + HEAD_DIM] = (a * inv).astype(BF16)

    others = []
    for out_ref, c0 in ((z_ref, IN_Z), (uv_ref, IN_UV), (p_ref, IN_P), (ba_ref, IN_BA)):
        width = out_ref.shape[1]
        others += [(out_ref, c0, slice(c, min(c + blk, width))) for c in range(0, width, blk)]
    n_blk = QKV // blk
    assert len(others) == n_blk
    pre = project(0)
    for b in range(n_blk):
        nxt_pre = project(b + 1) if b + 1 < n_blk else None
        out_ref, c0, cols = others[b]
        out_ref[:, cols] = _dot(hb, win_ref[:, c0 + cols.start:c0 + cols.stop])
        conv_act(b, pre)
        pre = nxt_pre


def _input_stage(x, mods_l, norm_g, w_in, layer, conv_w, *, seq_len, row0):
    n = x.shape[0]
    tile = TOKEN_TILE
    halo = seq_len > tile
    tiles_per_seq = max(seq_len // tile, 1)
    n_tiles = n // tile
    h8 = tile // 8
    last8 = n // 8 - 1
    in_specs = [pl.BlockSpec((tile, D_MODEL), lambda i: (i, 0))]
    args = [x]
    if halo:
        in_specs += [pl.BlockSpec((8, D_MODEL), lambda i: (jnp.maximum(i * h8 - 1, 0), 0)),
                     pl.BlockSpec((8, D_MODEL), lambda i: (jnp.minimum((i + 1) * h8, last8), 0))]
        args += [x, x]
    in_specs += [_const_spec((8, N_MOD * D_MODEL)), _const_spec((1, D_MODEL)),
                 _layer_spec(w_in.shape, layer), _const_spec((3, QKV))]
    args += [mods_l, norm_g, w_in, conv_w]
    widths = (QKV, A_WIDTH, LANES, 2 * B_WIDTH, C_WIDTH)
    dtypes = (BF16, F32, F32, F32, F32)
    return pl.pallas_call(
        functools.partial(_in_kernel, seq_len=seq_len, tile=tile, row0=row0, halo=halo),
        grid=(n_tiles,),
        in_specs=in_specs,
        out_specs=[pl.BlockSpec((tile, c), lambda i: (i, 0)) for c in widths],
        out_shape=[jax.ShapeDtypeStruct((n, c), t) for c, t in zip(widths, dtypes)],
        scratch_shapes=[pltpu.VMEM((D_MODEL, IN_PAD), BF16)],
        compiler_params=_params(1),
        name="input_stage",
    )(*args)


N_LEVELS = int(math.log2(CHUNK))
WIDE = HEADS * CHUNK


def _delta_constants():
    r = np.arange(CHUNK)[:, None]
    c = (np.arange(WIDE) % CHUNK)[None, :]
    lvl = np.zeros((2, N_LEVELS, CHUNK, WIDE), np.float32)
    for d in range(2):
        rr, cc = (r, c) if d == 0 else (c, r)
        for i in range(N_LEVELS):
            s = 1 << i
            lvl[d, i] = (rr // (2 * s) == cc // (2 * s)) & ((rr // s) % 2 == 1) & ((cc // s) % 2 == 0)
    eye = (r == c).astype(np.float32)
    tri = np.concatenate([np.tril(np.ones((CHUNK, CHUNK))), np.triu(np.ones((CHUNK, CHUNK)))], 0)
    tri = np.concatenate([tri] * 3, axis=1)
    return (jnp.asarray(lvl.reshape(2 * N_LEVELS, CHUNK, WIDE)), jnp.asarray(eye),
            jnp.asarray(tri, BF16))


def _pair(a, b, lt_half):
    return jnp.where(lt_half, a, b)


def _block_diag(blocks, zero):
    n = len(blocks)
    rows = [jnp.concatenate([blocks[i] if j == i else zero for j in range(n)], axis=1)
            for i in range(n)]
    return jnp.concatenate(rows, axis=0)


def _delta_kernel(*refs, n_sub, n_chunks, has_init, n_par, cb, layer, n_cast):
    it = iter(refs)
    qkv_ref, ba_ref, alog_ref, dtb_ref, lvl_ref, eye_ref, tri_ref = (next(it) for _ in range(7))
    s0_ref = next(it) if has_init else None
    sp_ref = next(it) if (not has_init and layer > 0) else None
    cast_in = [next(it) for _ in range(n_cast)]
    o_ref = next(it)
    st_ref = None if has_init else next(it)
    for src_ref in cast_in:
        dst_ref = next(it)
        dst_ref[...] = src_ref[...].astype(BF16)
    (s_ref, gc_ref, beta_ref, gct_ref, u_ref, w_ref, qd_ref, qkd_ref, kdt_ref, gt_ref) = it
    n_total = n_sub * n_chunks
    lane = lax.broadcasted_iota(jnp.int32, (CHUNK, LANES), 1)
    lt_half = lane < CHUNK
    lt_half_row = lt_half[0:1]
    rw = lax.broadcasted_iota(jnp.int32, (CHUNK, WIDE), 0)
    cw = lax.broadcasted_iota(jnp.int32, (CHUNK, WIDE), 1) & (CHUNK - 1)
    incl = (rw >= cw, rw <= cw)
    a_row = -jnp.exp(alog_ref[...])
    dtb_row = dtb_ref[...]

    def gate_body(bi, carry):
        span = cb * CHUNK
        r0 = pl.multiple_of(bi * span, span)
        ba = ba_ref[pl.ds(r0, span), :]
        beta_ref[pl.ds(r0, span), :] = _sigmoid(ba)
        al = pltpu.roll(ba, LANES - 2 * HEADS, 1)
        lane_b = lax.broadcasted_iota(jnp.int32, (span, LANES), 1)
        g = jnp.where(lane_b < 2 * HEADS, a_row * _softplus(al + dtb_row), 0.0)
        g_hi = g.astype(BF16)
        r1 = g - g_hi.astype(F32)
        g_mid = r1.astype(BF16)
        g_lo = (r1 - g_mid.astype(F32)).astype(BF16)
        css = []
        for i in range(cb):
            rs = slice(i * CHUNK, (i + 1) * CHUNK)
            terms = jnp.concatenate([g_hi[rs], g_mid[rs], g_lo[rs]], axis=0)
            css.append(_dot(tri_ref[...], terms))
        gcs = [jnp.where(lane < HEADS, cs[0:CHUNK], cs[CHUNK:]) for cs in css]
        ats = [jnp.concatenate([gc, gc], axis=0).T[0:8, :] for gc in gcs]
        for i in range(cb):
            gc_ref[pl.ds(pl.multiple_of(r0 + i * CHUNK, CHUNK), CHUNK), :] = gcs[i]
            gct_ref[bi * cb + i] = ats[i]
        return carry

    n_batches = n_total // cb
    gate_body(0, 0)

    zero_blk = jnp.zeros((CHUNK, HEAD_DIM), BF16)
    hs = [slice(h * HEAD_DIM, (h + 1) * HEAD_DIM) for h in range(HEADS)]

    def head_mm(x, y):
        xb = x.astype(BF16)
        yb = y.astype(BF16)
        zero = jnp.zeros((CHUNK, LANES), BF16)
        outs = []
        for pr in range(HEADS // 2):
            ys = yb[:, pr * LANES:(pr + 1) * LANES]
            bd = jnp.concatenate([jnp.where(lt_half, ys, zero), jnp.where(lt_half, zero, ys)], axis=0)
            outs.append(_dot(xb[:, pr * LANES:(pr + 1) * LANES], bd))
        return jnp.concatenate(outs, axis=1)

    def pre_body(bi, carry):
        groups = []
        for i in range(cb):
            c = bi * cb + i
            r0 = pl.multiple_of(c * CHUNK, CHUNK)
            rows = pl.ds(r0, CHUNK)
            qb = qkv_ref[rows, 0:QK]
            kb16 = qkv_ref[rows, QK:2 * QK]
            qf = qb.astype(F32)
            kf = kb16.astype(F32)
            vf = qkv_ref[rows, 2 * QK:].astype(F32)
            kt = jnp.concatenate(
                [jnp.concatenate([kf[:, hs[0]], kf[:, hs[1]]], axis=0).T,
                 jnp.concatenate([kf[:, hs[2]], kf[:, hs[3]]], axis=0).T], axis=1)
            ktbd = _group_rows(kt.astype(BF16), HEADS)
            qkk = _dot(jnp.concatenate([qb, kb16], axis=0), ktbd)
            cols = gc_ref[rows, :]
            bcols = beta_ref[rows, :]
            at = gct_ref[c]
            for d in range(2):
                j0 = d * HEADS
                bg = [jnp.broadcast_to(cols[:, j0 + h:j0 + h + 1], (CHUNK, LANES)) for h in range(HEADS)]
                bb = [jnp.broadcast_to(bcols[:, j0 + h:j0 + h + 1], (CHUNK, LANES)) for h in range(HEADS)]
                gcw = jnp.concatenate([_pair(bg[0], bg[1], lt_half), _pair(bg[2], bg[3], lt_half)], 1)
                bw = jnp.concatenate([_pair(bb[0], bb[1], lt_half), _pair(bb[2], bb[3], lt_half)], 1)
                gr = jnp.concatenate(
                    [_pair(at[j0:j0 + 1], at[j0 + 1:j0 + 2], lt_half_row),
                     _pair(at[j0 + 2:j0 + 3], at[j0 + 3:j0 + 4], lt_half_row)], axis=1)
                decay = jnp.exp(jnp.where(incl[d], gcw - gr, NEG_BIG))
                m = bw * qkk[CHUNK:] * decay
                qkd_ref[d, rows, :] = (qkk[0:CHUNK] * decay).astype(BF16)
                en = jnp.concatenate([jnp.exp(x) for x in bg], axis=1)
                bn = jnp.concatenate(bb, axis=1)
                qd_ref[d, rows, :] = (qf * en).astype(BF16)
                vb = (vf * bn).astype(BF16)
                kbe = (kf * (bn * en)).astype(BF16)
                e0 = CHUNK - 1 if d == 0 else 0
                bt = [jnp.broadcast_to(at[j0 + h:j0 + h + 1, e0:e0 + 1], (1, LANES)) for h in range(HEADS)]
                tw = jnp.concatenate([_pair(bt[0], bt[1], lt_half_row),
                                      _pair(bt[2], bt[3], lt_half_row)], axis=1)
                kdt_ref[d, pl.ds(pl.multiple_of(c * HEAD_DIM, HEAD_DIM), HEAD_DIM), :] = (
                    kt * jnp.exp(tw - gr)).astype(BF16)
                gt_ref[d, c] = jnp.broadcast_to(
                    jnp.concatenate([jnp.exp(x) for x in bt], axis=1), (8, A_WIDTH))
                groups.append((d, rows, m, vb, kbe))
        gate_body(jnp.minimum(bi + 1, n_batches - 1), 0)
        ts = [eye_ref[...] - g[2] * lvl_ref[g[0] * N_LEVELS] for g in groups]
        for lv in range(1, N_LEVELS):
            xs = [head_mm(t, g[2] * lvl_ref[g[0] * N_LEVELS + lv]) for t, g in zip(ts, groups)]
            ts = [t - head_mm(x, t) for t, x in zip(ts, xs)]
        for t, (d, rows, _, vb, kbe) in zip(ts, groups):
            tb = t.astype(BF16)
            uw = []
            for pr in range(HEADS // 2):
                h0, h1 = hs[2 * pr], hs[2 * pr + 1]
                rhs = jnp.concatenate(
                    [jnp.concatenate([vb[:, h0], zero_blk, kbe[:, h0], zero_blk], axis=1),
                     jnp.concatenate([zero_blk, vb[:, h1], zero_blk, kbe[:, h1]], axis=1)], axis=0)
                uw.append(_dot(tb[:, pr * LANES:(pr + 1) * LANES], rhs))
            half = A_WIDTH // 2
            u_ref[d, rows, :] = jnp.concatenate([uw[0][:, 0:half], uw[1][:, 0:half]], axis=1)
            w_ref[d, rows, :] = jnp.concatenate([uw[0][:, half:], uw[1][:, half:]],
                                                axis=1).astype(BF16)
        return carry

    lax.fori_loop(0, n_batches, pre_body, 0)

    o_ref[...] = jnp.zeros(o_ref.shape, F32)
    zero_s = jnp.zeros((HEAD_DIM, HEAD_DIM), BF16)

    def scan_step(sub0, c):
        groups = [(p, d) for p in range(n_par) for d in range(2)]
        cidx = [(sub0 + p) * n_chunks + (c if d == 0 else n_chunks - 1 - c) for p, d in groups]
        rows = [pl.ds(pl.multiple_of(ci * CHUNK, CHUNK), CHUNK) for ci in cidx]
        ss = [s_ref[p, d] for p, d in groups]
        wqs = []
        for (p, d), r, s in zip(groups, rows, ss):
            sb = s.astype(BF16)
            lhs = jnp.concatenate([w_ref[d, r, :], qd_ref[d, r, :]], axis=0)
            halves = []
            for hp in range(HEADS // 2):
                bd = _block_diag([sb[:, hs[2 * hp]], sb[:, hs[2 * hp + 1]]], zero_s)
                halves.append(_dot(lhs[:, hp * 2 * HEAD_DIM:(hp + 1) * 2 * HEAD_DIM], bd))
            wqs.append(jnp.concatenate(halves, axis=1))
        boths = []
        for (p, d), r, ci, wq in zip(groups, rows, cidx, wqs):
            v_new = (u_ref[d, r, :] - wq[0:CHUNK]).astype(BF16)
            vbd = _block_diag([v_new[:, s] for s in hs], zero_blk)
            kd = kdt_ref[d, pl.ds(pl.multiple_of(ci * HEAD_DIM, HEAD_DIM), HEAD_DIM), :]
            lhs2 = jnp.concatenate([qkd_ref[d, r, :], kd], axis=0)
            boths.append(_dot(lhs2, vbd))
        for (p, d), r, ci, s, wq, both in zip(groups, rows, cidx, ss, wqs, boths):
            o_ref[r, :] += wq[CHUNK:] + both[0:CHUNK]
            s_ref[p, d] = s * gt_ref[d, ci][0:1] + both[CHUNK:]

    def sub_body(sb_i, carry):
        sub0 = sb_i * n_par
        for p in range(n_par):
            for d in range(2):
                if has_init:
                    s_ref[p, d] = jnp.concatenate(
                        [s0_ref[d, h] for h in range(HEADS)], axis=1)
                else:
                    s_ref[p, d] = jnp.zeros((HEAD_DIM, A_WIDTH), F32)

        def step(c, carry2):
            scan_step(sub0, c)
            return carry2

        lax.fori_loop(0, n_chunks, step, 0)
        if st_ref is not None:
            for p in range(n_par):
                for d in range(2):
                    s = s_ref[p, d]
                    for h in range(HEADS):
                        st_ref[sub0 + p, layer, d, h] = s[:, hs[h]]
                        for earlier in range(layer):
                            st_ref[sub0 + p, earlier, d, h] = sp_ref[sub0 + p, earlier, d, h]
        return carry

    lax.fori_loop(0, n_sub // n_par, sub_body, 0)


def _delta_stage(qkv, ba, alog_row, dtb_row, s0, consts, *, seq_len, layer=0, st_prev=None,
                 cast_weights=()):
    n = qkv.shape[0]
    block = max(seq_len, DELTA_BLOCK)
    n_blocks = n // block
    n_sub = block // seq_len
    n_chunks = seq_len // CHUNK
    n_total = block // CHUNK
    has_init = s0 is not None
    n_par = min(n_sub, DELTA_PAR)
    lvl, eye, tri = consts
    in_specs = [pl.BlockSpec((block, QKV), lambda i: (i, 0)),
                pl.BlockSpec((block, LANES), lambda i: (i, 0)),
                _const_spec((1, LANES)), _const_spec((1, LANES)),
                _const_spec(lvl.shape), _const_spec(eye.shape), _const_spec(tri.shape)]
    args = [qkv, ba, alog_row, dtb_row, lvl, eye, tri]
    o_spec = pl.BlockSpec((block, A_WIDTH), lambda i: (i, 0))
    o_shape = jax.ShapeDtypeStruct((n, A_WIDTH), F32)
    st_shape = (2, HEADS, HEAD_DIM, HEAD_DIM)
    if has_init:
        assert n_sub == 1
        in_specs.append(pl.BlockSpec((None,) + st_shape, lambda i: (i, 0, 0, 0, 0)))
        args.append(s0)
        out_specs, out_shape = o_spec, o_shape
    else:
        if layer > 0:
            in_specs.append(pl.BlockSpec((n_sub, layer) + st_shape, lambda i: (i, 0, 0, 0, 0, 0)))
            args.append(st_prev)
        out_specs = [o_spec,
                     pl.BlockSpec((n_sub, layer + 1) + st_shape, lambda i: (i, 0, 0, 0, 0, 0))]
        out_shape = [o_shape, jax.ShapeDtypeStruct((n // seq_len, layer + 1) + st_shape, F32)]
    scratch = [pltpu.VMEM((n_par, 2, HEAD_DIM, A_WIDTH), F32),
               pltpu.VMEM((block, LANES), F32),
               pltpu.VMEM((block, LANES), F32),
               pltpu.VMEM((n_total, 8, LANES), F32),
               pltpu.VMEM((2, block, A_WIDTH), F32),
               pltpu.VMEM((2, block, A_WIDTH), BF16),
               pltpu.VMEM((2, block, A_WIDTH), BF16),
               pltpu.VMEM((2, block, WIDE), BF16),
               pltpu.VMEM((2, n_total * HEAD_DIM, WIDE), BF16),
               pltpu.VMEM((2, n_total, 8, A_WIDTH), F32)]
    for wt in cast_weights:
        rows = wt.shape[1] // n_blocks
        assert rows * n_blocks == wt.shape[1] and rows % 16 == 0
        in_specs.append(pl.BlockSpec((None, rows, wt.shape[2]), lambda i: (layer, i, 0)))
        args.append(wt)
        out_specs = list(out_specs) if isinstance(out_specs, list) else [out_specs]
        out_shape = list(out_shape) if isinstance(out_shape, list) else [out_shape]
        out_specs.append(pl.BlockSpec((rows, wt.shape[2]), lambda i: (i, 0)))
        out_shape.append(jax.ShapeDtypeStruct(wt.shape[1:], BF16))
    return pl.pallas_call(
        functools.partial(_delta_kernel, n_sub=n_sub, n_chunks=n_chunks, has_init=has_init,
                          n_par=n_par, cb=DELTA_BATCH, layer=layer, n_cast=len(cast_weights)),
        grid=(n_blocks,),
        in_specs=in_specs,
        out_specs=out_specs,
        out_shape=out_shape,
        scratch_shapes=scratch,
        compiler_params=_params(1),
        name="delta_stage",
    )(*args)


def _group_rows(y, n_groups):
    half = y.shape[1] // 2
    gw = half // 2
    lane = lax.broadcasted_iota(jnp.int32, (y.shape[0], half), 1)
    zero = jnp.zeros((y.shape[0], half), y.dtype)
    blocks = []
    for g in range(n_groups):
        part = y[:, (g // 2) * half:(g // 2 + 1) * half]
        keep = (lane < gw) if g % 2 == 0 else (lane >= gw)
        part = jnp.where(keep, part, jnp.zeros_like(part))
        blocks.append(jnp.concatenate([part, zero] if g < 2 else [zero, part], axis=1))
    return jnp.concatenate(blocks, axis=0)


def _mix_delta(o_ref, z_ref, dg_ref, mix_ref):
    dg = dg_ref[...]
    for h in range(HEADS):
        cs = slice(h * HEAD_DIM, (h + 1) * HEAD_DIM)
        y = _rms_rows(o_ref[:, cs]) * dg * _silu(z_ref[:, cs])
        mix_ref[:, cs] = y.astype(BF16)


def _mix_sgu(uv_ref, sg_ref, ws_ref, bs_ref, seg_ref, mix_ref, tile):
    u = _gelu_tanh(uv_ref[:, 0:B_WIDTH])
    v = _gelu_tanh(uv_ref[:, B_WIDTH:])
    vv = v * v
    vv_hi = vv.astype(BF16)
    vv_lo = (vv - vv_hi.astype(F32)).astype(BF16)
    ms = _dot(jnp.concatenate([vv_hi, vv_lo], axis=1), seg_ref[...])
    vn = (v * lax.rsqrt(ms + EPS) * sg_ref[...]).astype(BF16)
    for c in range(tile // SGU_CHUNK):
        rs = slice(c * SGU_CHUNK, (c + 1) * SGU_CHUNK)
        s = _dot(ws_ref[...], _group_rows(vn[rs], B_GROUPS)) + bs_ref[...]
        mix_ref[rs, A_WIDTH:A_WIDTH + B_WIDTH] = (u[rs] * s).astype(BF16)


def _mix_pool(p_ref, band_ref, icnt_ref, wp_ref, ps_ref, mix_ref, tile):
    for c in range(tile // POOL_TILE):
        rs = slice(c * POOL_TILE, (c + 1) * POOL_TILE)
        x = p_ref[rs, :]
        x_hi = x.astype(BF16)
        x_lo = (x - x_hi.astype(F32)).astype(BF16)
        n_win = len(POOL_WINDOWS)
        rhs = jnp.concatenate([_group_rows(x_hi, n_win), _group_rows(x_lo, n_win)], axis=0)
        wsum = _dot(band_ref[...], rhs)
        diff = wsum * icnt_ref[...] - x
        y = _bdot(diff, wp_ref[...]) * ps_ref[...]
        mix_ref[rs, A_WIDTH + B_WIDTH:] = y.astype(BF16)


def _out_kernel(x_ref, o_ref, z_ref, uv_ref, p_ref, dg_ref, sg_ref, ws_ref, bs_ref, seg_ref,
                band_ref, icnt_ref, wp_ref, ps_ref, mod_ref, g2_ref, wof_ref, wgu_ref, wd_ref,
                nf_ref, y_ref, mix_ref, wo_ref, *, seq_len, tile, row0, final_norm):
    i = pl.program_id(0)

    @pl.when(i == 0)
    def _prepare_weights():
        rows = 256
        for r in range(0, D_MODEL, rows):
            wo_ref[r:r + rows, :] = wof_ref[r:r + rows, :].astype(BF16)

    tiles_per_seq = max(seq_len // tile, 1)
    row = row0 + i // tiles_per_seq if row0 else 0
    m = mod_ref[pl.ds(row, 1), :]
    gate1 = m[:, 2 * D_MODEL:3 * D_MODEL]
    shift2 = m[:, 3 * D_MODEL:4 * D_MODEL]
    scale2 = m[:, 4 * D_MODEL:5 * D_MODEL]
    gate2 = m[:, 5 * D_MODEL:6 * D_MODEL]
    _mix_delta(o_ref, z_ref, dg_ref, mix_ref)
    _mix_sgu(uv_ref, sg_ref, ws_ref, bs_ref, seg_ref, mix_ref, tile)
    _mix_pool(p_ref, band_ref, icnt_ref, wp_ref, ps_ref, mix_ref, tile)
    x1 = x_ref[...] + gate1 * _dot(mix_ref[...], wo_ref[...])
    hb = _modulated_norm(x1, g2_ref[...], shift2, scale2).astype(BF16)
    fc = FF_HIDDEN // FF_SPLIT
    ff = jnp.zeros((tile, D_MODEL), F32)
    for c in range(FF_SPLIT):
        gate = _dot(hb, wgu_ref[:, c * fc:(c + 1) * fc])
        up = _dot(hb, wgu_ref[:, FF_HIDDEN + c * fc:FF_HIDDEN + (c + 1) * fc])
        act = (_silu(gate) * up).astype(BF16)
        ff = ff + _dot(act, wd_ref[c * fc:(c + 1) * fc, :])
    x2 = x1 + gate2 * ff
    if final_norm:
        x2 = _rms_rows(x2) * nf_ref[...]
    y_ref[...] = x2


def _output_stage(x, o, z, uv, p, lw, pool_consts, mods_l, norm2_g, w_out, layer, w_gu_b,
                  w_down_b, norm_f, *, seq_len, row0, final_norm):
    n = x.shape[0]
    tile = TOKEN_TILE
    band, icnt = pool_consts
    row = lambda c: pl.BlockSpec((tile, c), lambda i: (i, 0))
    small = [lw["delta_g"], lw["sgu_g"], lw["w_spatial"], lw["b_spatial"], lw["seg"], band, icnt,
             lw["w_pool"], lw["pool_scale"], mods_l, norm2_g]
    consts = small + [w_out, w_gu_b, w_down_b, norm_f]
    in_specs = ([row(D_MODEL), row(A_WIDTH), row(A_WIDTH), row(2 * B_WIDTH), row(C_WIDTH)]
                + [_const_spec(a.shape) for a in small]
                + [_layer_spec(w_out.shape, layer), _const_spec(w_gu_b.shape),
                   _const_spec(w_down_b.shape), _const_spec(norm_f.shape)])
    return pl.pallas_call(
        functools.partial(_out_kernel, seq_len=seq_len, tile=tile, row0=row0,
                          final_norm=final_norm),
        grid=(n // tile,),
        in_specs=in_specs,
        out_specs=row(D_MODEL),
        out_shape=jax.ShapeDtypeStruct((n, D_MODEL), F32),
        scratch_shapes=[pltpu.VMEM((tile, D_MODEL), BF16),
                        pltpu.VMEM((D_MODEL, D_MODEL), BF16)],
        compiler_params=_params(1),
        name="output_stage",
    )(x, o, z, uv, p, *consts)


def _pool_constants(seg_len):
    pos = np.arange(POOL_TILE)
    seg = pos // seg_len
    band = np.zeros((len(POOL_WINDOWS), POOL_TILE, POOL_TILE), np.float32)
    icnt = np.zeros((POOL_TILE, C_WIDTH), np.float32)
    for g, win in enumerate(POOL_WINDOWS):
        lo = pos - win // 2
        hi = pos + win - win // 2
        inside = (pos[None, :] >= lo[:, None]) & (pos[None, :] < hi[:, None]) \
            & (seg[None, :] == seg[:, None])
        band[g] = inside
        icnt[:, g * C_GC:(g + 1) * C_GC] = (1.0 / inside.sum(axis=1))[:, None]
    band_cat = np.concatenate([band[g] for g in range(len(POOL_WINDOWS))] * 2, axis=1)
    return jnp.asarray(band_cat, BF16), jnp.asarray(icnt)


def _grid_pos_tables(rows, d):
    quarter = d // 4
    f = np.float32
    omega = (f(1.0) / (f(10000.0) ** (np.arange(quarter, dtype=f) / f(quarter)))).astype(f)
    r = np.arange(rows, dtype=f)[:, None] * omega
    cl = np.arange(GRID_W, dtype=f)[:, None] * omega
    row_emb = np.concatenate([np.sin(r), np.cos(r)], axis=-1).astype(f)
    col_emb = np.concatenate([np.sin(cl), np.cos(cl)], axis=-1).astype(f)
    return jnp.asarray(row_emb), jnp.asarray(col_emb)


def _pad_lanes(a, width):
    return jnp.pad(a, ((0, 0), (0, width - a.shape[1])))


def _layer_weights(l, w_pool, w_spatial, b_spatial):
    wp = jnp.zeros((C_WIDTH, C_WIDTH), F32)
    for g in range(len(POOL_WINDOWS)):
        wp = wp.at[g * C_GC:(g + 1) * C_GC, g * C_GC:(g + 1) * C_GC].set(w_pool[l, g])
    grp = np.arange(B_WIDTH) // B_GC
    seg1 = (grp[:, None] == grp[None, :]).astype(np.float32) / B_GC
    seg = jnp.asarray(np.concatenate([seg1, seg1], axis=0), BF16)
    bs = jnp.repeat(b_spatial[l].T, B_GC, axis=1)
    ws_cat = jnp.concatenate([w_spatial[l, g] for g in range(B_GROUPS)], axis=1)
    return wp.astype(BF16), seg, ws_cat.astype(BF16), bs


def kernel(x_prompt, x_sample, state_delta, c, c_ctx, w_in, conv_w, a_log, dt_bias, delta_norm_g,
           sgu_norm_g, w_spatial, b_spatial, w_pool, pool_scale, w_out, norm1_g, norm2_g, w_mod,
           b_mod, w_gu, w_down, norm_f):
    batch, seq, d = x_prompt.shape
    dec_batch, dec_seq, _ = x_sample.shape
    cvec = jnp.concatenate([c_ctx[None, :], c, jnp.zeros((8 - 1 - dec_batch, d), F32)], axis=0)
    mods = _modulation(cvec, w_mod, b_mod)
    row_emb, col_emb = _grid_pos_tables(dec_seq // GRID_W, d)
    xs = _add_pos(x_sample, row_emb, col_emb).reshape(dec_batch * dec_seq, d)
    xc = x_prompt.reshape(batch * seq, d)
    pool_ctx = _pool_constants(min(seq, POOL_TILE))
    pool_lat = _pool_constants(GRID_W)
    nf = norm_f.reshape(1, d)
    delta_consts = _delta_constants()
    w_in_b = w_in.astype(BF16)
    ctx_states = None
    for l in range(DEPTH):
        wp, seg, ws, bs = _layer_weights(l, w_pool, w_spatial, b_spatial)
        lw = {"delta_g": delta_norm_g[l].reshape(1, HEAD_DIM),
              "sgu_g": sgu_norm_g[l].reshape(1, B_WIDTH), "w_spatial": ws, "b_spatial": bs,
              "seg": seg, "w_pool": wp, "pool_scale": pool_scale[l].reshape(1, C_WIDTH)}
        alog_row = _pad_lanes(a_log[l].reshape(1, 2 * HEADS), LANES)
        dtb_row = _pad_lanes(dt_bias[l].reshape(1, 2 * HEADS), LANES)
        n1 = norm1_g[l].reshape(1, d)
        n2 = norm2_g[l].reshape(1, d)
        last = l == DEPTH - 1
        s0_lat = state_delta[:, l]
        streams = (("ctx", xc, seq, 0, None, pool_ctx), ("lat", xs, dec_seq, 1, s0_lat, pool_lat))
        outs = []
        for name, x, seq_len, row0, s0, pool_consts in streams:
            qkv, z, ba, uv, p = _input_stage(x, mods[l], n1, w_in_b, l, conv_w[l],
                                             seq_len=seq_len, row0=row0)
            if s0 is None:
                o, ctx_states, w_gu_b, w_down_b = _delta_stage(
                    qkv, ba, alog_row, dtb_row, None, delta_consts, seq_len=seq_len, layer=l,
                    st_prev=ctx_states, cast_weights=(w_gu, w_down))
            else:
                o = _delta_stage(qkv, ba, alog_row, dtb_row, s0, delta_consts, seq_len=seq_len)
            outs.append(_output_stage(x, o, z, uv, p, lw, pool_consts, mods[l], n2, w_out, l,
                                      w_gu_b, w_down_b, nf, seq_len=seq_len, row0=row0,
                                      final_norm=last))
        xc, xs = outs
    y_prompt = xc.reshape(batch, seq, d)
    y_sample = xs.reshape(dec_batch, dec_seq, d)
    return (y_prompt, y_sample, ctx_states)
```

```python
import functools
import math

import numpy as np
import jax
import jax.numpy as jnp
from jax import lax
from jax.experimental import pallas as pl
from jax.experimental.pallas import tpu as pltpu

F32 = jnp.float32
BF16 = jnp.bfloat16

LANES = 128
D_MODEL = 1024
DEPTH = 2
GRID_W = 64
HEADS = 4
HEAD_DIM = 128
QK = HEADS * HEAD_DIM
A_WIDTH = HEADS * HEAD_DIM
QKV = 2 * QK + A_WIDTH
CHUNK = 64
B_WIDTH = 256
B_GROUPS = 4
B_GC = 64
SGU_CHUNK = 128
C_WIDTH = 256
POOL_WINDOWS = (2, 4, 8, 16)
C_GC = 64
FF_HIDDEN = 2816
FF_SPLIT = 1
N_MOD = 6
EPS = 1e-6
IN_COLS = 2 * QK + 2 * A_WIDTH + 4 * HEADS + 2 * B_WIDTH + C_WIDTH
IN_Z = QKV
IN_UV = IN_Z + A_WIDTH
IN_P = IN_UV + 2 * B_WIDTH
IN_BA = IN_P + C_WIDTH
IN_PAD = IN_BA + LANES
NEG_BIG = -1e30

DELTA_BLOCK = 1024
DELTA_BATCH = 4
DELTA_PAR = 4
TOKEN_TILE = 512
INPUT_TILE = 1024
POOL_TILE = 256
VMEM_LIMIT = 58 * 1024 * 1024


def _dot(a, b):
    return jnp.dot(a, b, preferred_element_type=F32)


def _bdot(a, b):
    return jnp.dot(a.astype(BF16), b.astype(BF16), preferred_element_type=F32)


def _sigmoid(x):
    return 0.5 * jnp.tanh(0.5 * x) + 0.5


def _silu(x):
    h = 0.5 * x
    return h + h * jnp.tanh(h)


def _softplus(x):
    return jnp.maximum(x, 0.0) + jnp.log1p(jnp.exp(-jnp.abs(x)))


def _gelu_tanh(x):
    c = math.sqrt(2.0 / math.pi)
    return 0.5 * x * (1.0 + jnp.tanh(c * (x + 0.044715 * (x * x * x))))


def _rms_rows(x):
    return x * lax.rsqrt(jnp.mean(x * x, axis=-1, keepdims=True) + EPS)


def _params(n_grid):
    return pltpu.CompilerParams(dimension_semantics=("arbitrary",) * n_grid,
                                vmem_limit_bytes=VMEM_LIMIT)


def _const_spec(shape):
    nd = len(shape)
    return pl.BlockSpec(shape, lambda *_: (0,) * nd, pipeline_mode=pl.Buffered(1))


def _layer_spec(shape, layer):
    nd = len(shape) - 1
    return pl.BlockSpec((None,) + tuple(shape[1:]), lambda *_: (layer,) + (0,) * nd,
                        pipeline_mode=pl.Buffered(1))


def _mod_kernel(c_ref, w_ref, b_ref, o_ref):
    a = _silu(c_ref[...])
    o_ref[0] = _bdot(a, w_ref[0]) + b_ref[0]


def _modulation(cvec, w_mod, b_mod):
    tn = 3072
    n_out = N_MOD * D_MODEL
    return pl.pallas_call(
        _mod_kernel,
        grid=(DEPTH, n_out // tn),
        in_specs=[pl.BlockSpec((8, D_MODEL), lambda l, j: (0, 0)),
                  pl.BlockSpec((1, D_MODEL, tn), lambda l, j: (l, 0, j)),
                  pl.BlockSpec((1, 1, tn), lambda l, j: (l, 0, j))],
        out_specs=pl.BlockSpec((1, 8, tn), lambda l, j: (l, 0, j)),
        out_shape=jax.ShapeDtypeStruct((DEPTH, 8, n_out), F32),
        compiler_params=_params(2),
        name="modulation",
    )(cvec, w_mod, b_mod.reshape(DEPTH, 1, n_out))


POS_ROWS = 8


def _pos_kernel(x_ref, r_ref, c_ref, o_ref):
    half = D_MODEL // 2
    col = c_ref[...]
    for r in range(POS_ROWS):
        rs = slice(r * GRID_W, (r + 1) * GRID_W)
        o_ref[rs, 0:half] = x_ref[rs, 0:half] + r_ref[r:r + 1, :]
        o_ref[rs, half:] = x_ref[rs, half:] + col


def _add_pos(x, row_emb, col_emb):
    b, n, d = x.shape
    tm = POS_ROWS * GRID_W
    return pl.pallas_call(
        _pos_kernel,
        grid=(b, n // tm),
        in_specs=[pl.BlockSpec((None, tm, d), lambda i, j: (i, j, 0)),
                  pl.BlockSpec((POS_ROWS, d // 2), lambda i, j: (j, 0)),
                  pl.BlockSpec((GRID_W, d // 2), lambda i, j: (0, 0))],
        out_specs=pl.BlockSpec((None, tm, d), lambda i, j: (i, j, 0)),
        out_shape=jax.ShapeDtypeStruct(x.shape, F32),
        compiler_params=_params(2),
        name="add_pos",
    )(x, row_emb, col_emb)


def _modulated_norm(x, gain, shift, scale):
    return _rms_rows(x) * gain * (1.0 + scale) + shift


def _pack_w_in(ws_ref, wb_ref):
    n_logit = 4 * HEADS
    rows = 256
    lane = lax.broadcasted_iota(jnp.int32, (rows, LANES), 1)
    for r in range(0, D_MODEL, rows):
        rs = slice(r, r + rows)
        wb_ref[rs, 0:IN_UV] = ws_ref[rs, 0:IN_UV]
        tail = ws_ref[rs, IN_UV:IN_COLS]
        wb_ref[rs, IN_UV:IN_BA] = tail[:, n_logit:]
        wb_ref[rs, IN_BA:IN_PAD] = jnp.where(lane < n_logit, tail[:, 0:LANES],
                                             jnp.zeros((rows, LANES), BF16))


def _in_kernel(*refs, seq_len, tile, row0, halo):
    if halo:
        (x_ref, xp_ref, xn_ref, mod_ref, g_ref, wf_ref,
         cw_ref, qkv_ref, z_ref, ba_ref, uv_ref, p_ref, win_ref) = refs
    else:
        (x_ref, mod_ref, g_ref, wf_ref,
         cw_ref, qkv_ref, z_ref, ba_ref, uv_ref, p_ref, win_ref) = refs
    i = pl.program_id(0)

    @pl.when(i == 0)
    def _prepare_weights():
        _pack_w_in(wf_ref, win_ref)

    tiles_per_seq = max(seq_len // tile, 1)
    row = row0 + i // tiles_per_seq if row0 else 0
    m = mod_ref[pl.ds(row, 1), :]
    shift = m[:, 0:D_MODEL]
    scale = m[:, D_MODEL:2 * D_MODEL]
    gain = g_ref[...]
    hf = _modulated_norm(x_ref[...], gain, shift, scale)
    hb = hf.astype(BF16)
    if halo:
        hp = _modulated_norm(xp_ref[...], gain, shift, scale)
        hn = _modulated_norm(xn_ref[...], gain, shift, scale)
        hq = jnp.concatenate([hf, hp, hn], axis=0).astype(BF16)
    else:
        hq = hb
    first_tile = i % tiles_per_seq == 0
    last_tile = i % tiles_per_seq == tiles_per_seq - 1
    sub8 = lax.broadcasted_iota(jnp.int32, (8, 1), 0)

    def set_row(arr, row, value):
        s0 = row - row % 8
        slab = jnp.where(sub8 == row % 8, value, arr[s0:s0 + 8])
        parts = [p for p in (arr[0:s0], slab, arr[s0 + 8:]) if p.shape[0]]
        return jnp.concatenate(parts, axis=0)

    cw = cw_ref[...]
    blk = 2 * HEAD_DIM

    def project(b):
        return _dot(hq, win_ref[:, b * blk:(b + 1) * blk])

    def conv_act(b, pre):
        cols = slice(b * blk, (b + 1) * blk)
        cur = pre[0:tile]
        prev = pltpu.roll(cur, 1, 0)
        nxt = pltpu.roll(cur, tile - 1, 0)
        if halo:
            prev = set_row(prev, 0, jnp.where(first_tile, 0.0, pre[tile + 7:tile + 8]))
            nxt = set_row(nxt, tile - 1, jnp.where(last_tile, 0.0, pre[tile + 8:tile + 9]))
        else:
            for start in range(0, tile, seq_len):
                prev = set_row(prev, start, 0.0)
                nxt = set_row(nxt, start + seq_len - 1, 0.0)
        act = _silu(prev * cw[0:1, cols] + cur * cw[1:2, cols] + nxt * cw[2:3, cols])
        if b * blk >= 2 * QK:
            qkv_ref[:, cols] = act.astype(BF16)
            return
        scale = HEAD_DIM ** -0.5 if b * blk < QK else 1.0
        for h in range(blk // HEAD_DIM):
            a = act[:, h * HEAD_DIM:(h + 1) * HEAD_DIM]
            inv = lax.rsqrt(jnp.sum(a * a, axis=-1, keepdims=True) + EPS) * scale
            c0 = b * blk + h * HEAD_DIM
            qkv_ref[:, c0:c0 + HEAD_DIM] = (a * inv).astype(BF16)

    others = []
    for out_ref, c0 in ((z_ref, IN_Z), (uv_ref, IN_UV), (p_ref, IN_P), (ba_ref, IN_BA)):
        width = out_ref.shape[1]
        others += [(out_ref, c0, slice(c, min(c + blk, width))) for c in range(0, width, blk)]
    n_blk = QKV // blk
    assert len(others) == n_blk
    pre = project(0)
    for b in range(n_blk):
        nxt_pre = project(b + 1) if b + 1 < n_blk else None
        out_ref, c0, cols = others[b]
        out_ref[:, cols] = _dot(hb, win_ref[:, c0 + cols.start:c0 + cols.stop])
        conv_act(b, pre)
        pre = nxt_pre


def _input_stage(x, mods_l, norm_g, w_in, layer, conv_w, *, seq_len, row0):
    n = x.shape[0]
    tile = INPUT_TILE
    halo = seq_len > tile
    tiles_per_seq = max(seq_len // tile, 1)
    n_tiles = n // tile
    h8 = tile // 8
    last8 = n // 8 - 1
    in_specs = [pl.BlockSpec((tile, D_MODEL), lambda i: (i, 0))]
    args = [x]
    if halo:
        in_specs += [pl.BlockSpec((8, D_MODEL), lambda i: (jnp.maximum(i * h8 - 1, 0), 0)),
                     pl.BlockSpec((8, D_MODEL), lambda i: (jnp.minimum((i + 1) * h8, last8), 0))]
        args += [x, x]
    in_specs += [_const_spec((8, N_MOD * D_MODEL)), _const_spec((1, D_MODEL)),
                 _layer_spec(w_in.shape, layer), _const_spec((3, QKV))]
    args += [mods_l, norm_g, w_in, conv_w]
    widths = (QKV, A_WIDTH, LANES, 2 * B_WIDTH, C_WIDTH)
    dtypes = (BF16, F32, F32, F32, F32)
    return pl.pallas_call(
        functools.partial(_in_kernel, seq_len=seq_len, tile=tile, row0=row0, halo=halo),
        grid=(n_tiles,),
        in_specs=in_specs,
        out_specs=[pl.BlockSpec((tile, c), lambda i: (i, 0)) for c in widths],
        out_shape=[jax.ShapeDtypeStruct((n, c), t) for c, t in zip(widths, dtypes)],
        scratch_shapes=[pltpu.VMEM((D_MODEL, IN_PAD), BF16)],
        compiler_params=_params(1),
        name="input_stage",
    )(*args)


N_LEVELS = int(math.log2(CHUNK))
WIDE = HEADS * CHUNK


def _delta_constants():
    r = np.arange(CHUNK)[:, None]
    c = (np.arange(WIDE) % CHUNK)[None, :]
    lvl = np.zeros((2, N_LEVELS, CHUNK, WIDE), np.float32)
    for d in range(2):
        rr, cc = (r, c) if d == 0 else (c, r)
        for i in range(N_LEVELS):
            s = 1 << i
            lvl[d, i] = (rr // (2 * s) == cc // (2 * s)) & ((rr // s) % 2 == 1) & ((cc // s) % 2 == 0)
    eye = (r == c).astype(np.float32)
    tri = np.concatenate([np.tril(np.ones((CHUNK, CHUNK))), np.triu(np.ones((CHUNK, CHUNK)))], 0)
    tri = np.concatenate([tri] * 3, axis=1)
    return (jnp.asarray(lvl.reshape(2 * N_LEVELS, CHUNK, WIDE)), jnp.asarray(eye),
            jnp.asarray(tri, BF16))


def _pair(a, b, lt_half):
    return jnp.where(lt_half, a, b)


def _block_diag(blocks, zero):
    n = len(blocks)
    rows = [jnp.concatenate([blocks[i] if j == i else zero for j in range(n)], axis=1)
            for i in range(n)]
    return jnp.concatenate(rows, axis=0)


def _delta_kernel(*refs, n_sub, n_chunks, has_init, n_par, cb, layer, n_cast):
    it = iter(refs)
    qkv_ref, ba_ref, alog_ref, dtb_ref, lvl_ref, eye_ref, tri_ref = (next(it) for _ in range(7))
    s0_ref = next(it) if has_init else None
    sp_ref = next(it) if (not has_init and layer > 0) else None
    cast_in = [next(it) for _ in range(n_cast)]
    o_ref = next(it)
    st_ref = None if has_init else next(it)
    for src_ref in cast_in:
        dst_ref = next(it)
        dst_ref[...] = src_ref[...].astype(BF16)
    (s_ref, gc_ref, beta_ref, gct_ref, u_ref, w_ref, qd_ref, qkd_ref, kdt_ref, gt_ref) = it
    n_total = n_sub * n_chunks
    lane = lax.broadcasted_iota(jnp.int32, (CHUNK, LANES), 1)
    lt_half = lane < CHUNK
    lt_half_row = lt_half[0:1]
    rw = lax.broadcasted_iota(jnp.int32, (CHUNK, WIDE), 0)
    cw = lax.broadcasted_iota(jnp.int32, (CHUNK, WIDE), 1) & (CHUNK - 1)
    incl = (rw >= cw, rw <= cw)
    a_row = -jnp.exp(alog_ref[...])
    dtb_row = dtb_ref[...]

    def gate_body(bi, carry):
        span = cb * CHUNK
        r0 = pl.multiple_of(bi * span, span)
        ba = ba_ref[pl.ds(r0, span), :]
        beta_ref[pl.ds(r0, span), :] = _sigmoid(ba)
        al = pltpu.roll(ba, LANES - 2 * HEADS, 1)
        lane_b = lax.broadcasted_iota(jnp.int32, (span, LANES), 1)
        g = jnp.where(lane_b < 2 * HEADS, a_row * _softplus(al + dtb_row), 0.0)
        g_hi = g.astype(BF16)
        r1 = g - g_hi.astype(F32)
        g_mid = r1.astype(BF16)
        g_lo = (r1 - g_mid.astype(F32)).astype(BF16)
        css = []
        for i in range(cb):
            rs = slice(i * CHUNK, (i + 1) * CHUNK)
            terms = jnp.concatenate([g_hi[rs], g_mid[rs], g_lo[rs]], axis=0)
            css.append(_dot(tri_ref[...], terms))
        gcs = [jnp.where(lane < HEADS, cs[0:CHUNK], cs[CHUNK:]) for cs in css]
        ats = [jnp.concatenate([gc, gc], axis=0).T[0:8, :] for gc in gcs]
        for i in range(cb):
            gc_ref[pl.ds(pl.multiple_of(r0 + i * CHUNK, CHUNK), CHUNK), :] = gcs[i]
            gct_ref[bi * cb + i] = ats[i]
        return carry

    n_batches = n_total // cb
    gate_body(0, 0)

    zero_blk = jnp.zeros((CHUNK, HEAD_DIM), BF16)
    hs = [slice(h * HEAD_DIM, (h + 1) * HEAD_DIM) for h in range(HEADS)]

    def head_mm(x, y):
        xb = x.astype(BF16)
        yb = y.astype(BF16)
        zero = jnp.zeros((CHUNK, LANES), BF16)
        outs = []
        for pr in range(HEADS // 2):
            ys = yb[:, pr * LANES:(pr + 1) * LANES]
            bd = jnp.concatenate([jnp.where(lt_half, ys, zero), jnp.where(lt_half, zero, ys)], axis=0)
            outs.append(_dot(xb[:, pr * LANES:(pr + 1) * LANES], bd))
        return jnp.concatenate(outs, axis=1)

    def pre_body(bi, carry):
        groups = []
        for i in range(cb):
            c = bi * cb + i
            r0 = pl.multiple_of(c * CHUNK, CHUNK)
            rows = pl.ds(r0, CHUNK)
            qb = qkv_ref[rows, 0:QK]
            kb16 = qkv_ref[rows, QK:2 * QK]
            qf = qb.astype(F32)
            kf = kb16.astype(F32)
            vf = qkv_ref[rows, 2 * QK:].astype(F32)
            kt = jnp.concatenate(
                [jnp.concatenate([kf[:, hs[0]], kf[:, hs[1]]], axis=0).T,
                 jnp.concatenate([kf[:, hs[2]], kf[:, hs[3]]], axis=0).T], axis=1)
            ktbd = _group_rows(kt.astype(BF16), HEADS)
            qkk = _dot(jnp.concatenate([qb, kb16], axis=0), ktbd)
            cols = gc_ref[rows, :]
            bcols = beta_ref[rows, :]
            at = gct_ref[c]
            for d in range(2):
                j0 = d * HEADS
                bg = [jnp.broadcast_to(cols[:, j0 + h:j0 + h + 1], (CHUNK, LANES)) for h in range(HEADS)]
                bb = [jnp.broadcast_to(bcols[:, j0 + h:j0 + h + 1], (CHUNK, LANES)) for h in range(HEADS)]
                gcw = jnp.concatenate([_pair(bg[0], bg[1], lt_half), _pair(bg[2], bg[3], lt_half)], 1)
                bw = jnp.concatenate([_pair(bb[0], bb[1], lt_half), _pair(bb[2], bb[3], lt_half)], 1)
                gr = jnp.concatenate(
                    [_pair(at[j0:j0 + 1], at[j0 + 1:j0 + 2], lt_half_row),
                     _pair(at[j0 + 2:j0 + 3], at[j0 + 3:j0 + 4], lt_half_row)], axis=1)
                decay = jnp.exp(jnp.where(incl[d], gcw - gr, NEG_BIG))
                m = bw * qkk[CHUNK:] * decay
                qkd_ref[d, rows, :] = (qkk[0:CHUNK] * decay).astype(BF16)
                en = jnp.concatenate([jnp.exp(x) for x in bg], axis=1)
                bn = jnp.concatenate(bb, axis=1)
                qd_ref[d, rows, :] = (qf * en).astype(BF16)
                vb = (vf * bn).astype(BF16)
                kbe = (kf * (bn * en)).astype(BF16)
                e0 = CHUNK - 1 if d == 0 else 0
                bt = [jnp.broadcast_to(at[j0 + h:j0 + h + 1, e0:e0 + 1], (1, LANES)) for h in range(HEADS)]
                tw = jnp.concatenate([_pair(bt[0], bt[1], lt_half_row),
                                      _pair(bt[2], bt[3], lt_half_row)], axis=1)
                kdt_ref[d, pl.ds(pl.multiple_of(c * HEAD_DIM, HEAD_DIM), HEAD_DIM), :] = (
                    kt * jnp.exp(tw - gr)).astype(BF16)
                gt_ref[d, c] = jnp.broadcast_to(
                    jnp.concatenate([jnp.exp(x) for x in bt], axis=1), (8, A_WIDTH))
                groups.append((d, rows, m, vb, kbe))
        gate_body(jnp.minimum(bi + 1, n_batches - 1), 0)
        ts = [eye_ref[...] - g[2] * lvl_ref[g[0] * N_LEVELS] for g in groups]
        for lv in range(1, N_LEVELS):
            xs = [head_mm(t, g[2] * lvl_ref[g[0] * N_LEVELS + lv]) for t, g in zip(ts, groups)]
            ts = [t - head_mm(x, t) for t, x in zip(ts, xs)]
        for t, (d, rows, _, vb, kbe) in zip(ts, groups):
            tb = t.astype(BF16)
            uw = []
            for pr in range(HEADS // 2):
                h0, h1 = hs[2 * pr], hs[2 * pr + 1]
                rhs = jnp.concatenate(
                    [jnp.concatenate([vb[:, h0], zero_blk, kbe[:, h0], zero_blk], axis=1),
                     jnp.concatenate([zero_blk, vb[:, h1], zero_blk, kbe[:, h1]], axis=1)], axis=0)
                uw.append(_dot(tb[:, pr * LANES:(pr + 1) * LANES], rhs))
            half = A_WIDTH // 2
            u_ref[d, rows, :] = jnp.concatenate([uw[0][:, 0:half], uw[1][:, 0:half]], axis=1)
            w_ref[d, rows, :] = jnp.concatenate([uw[0][:, half:], uw[1][:, half:]],
                                                axis=1).astype(BF16)
        return carry

    lax.fori_loop(0, n_batches, pre_body, 0)

    o_ref[...] = jnp.zeros(o_ref.shape, F32)
    zero_s = jnp.zeros((HEAD_DIM, HEAD_DIM), BF16)

    def scan_step(sub0, c):
        groups = [(p, d) for p in range(n_par) for d in range(2)]
        cidx = [(sub0 + p) * n_chunks + (c if d == 0 else n_chunks - 1 - c) for p, d in groups]
        rows = [pl.ds(pl.multiple_of(ci * CHUNK, CHUNK), CHUNK) for ci in cidx]
        ss = [s_ref[p, d] for p, d in groups]
        wqs = []
        for (p, d), r, s in zip(groups, rows, ss):
            sb = s.astype(BF16)
            lhs = jnp.concatenate([w_ref[d, r, :], qd_ref[d, r, :]], axis=0)
            halves = []
            for hp in range(HEADS // 2):
                bd = _block_diag([sb[:, hs[2 * hp]], sb[:, hs[2 * hp + 1]]], zero_s)
                halves.append(_dot(lhs[:, hp * 2 * HEAD_DIM:(hp + 1) * 2 * HEAD_DIM], bd))
            wqs.append(jnp.concatenate(halves, axis=1))
        boths = []
        for (p, d), r, ci, wq in zip(groups, rows, cidx, wqs):
            v_new = (u_ref[d, r, :] - wq[0:CHUNK]).astype(BF16)
            vbd = _block_diag([v_new[:, s] for s in hs], zero_blk)
            kd = kdt_ref[d, pl.ds(pl.multiple_of(ci * HEAD_DIM, HEAD_DIM), HEAD_DIM), :]
            lhs2 = jnp.concatenate([qkd_ref[d, r, :], kd], axis=0)
            boths.append(_dot(lhs2, vbd))
        for (p, d), r, ci, s, wq, both in zip(groups, rows, cidx, ss, wqs, boths):
            o_ref[r, :] += wq[CHUNK:] + both[0:CHUNK]
            s_ref[p, d] = s * gt_ref[d, ci][0:1] + both[CHUNK:]

    def sub_body(sb_i, carry):
        sub0 = sb_i * n_par
        for p in range(n_par):
            for d in range(2):
                if has_init:
                    s_ref[p, d] = jnp.concatenate(
                        [s0_ref[d, h] for h in range(HEADS)], axis=1)
                else:
                    s_ref[p, d] = jnp.zeros((HEAD_DIM, A_WIDTH), F32)

        def step(c, carry2):
            scan_step(sub0, c)
            return carry2

        lax.fori_loop(0, n_chunks, step, 0)
        if st_ref is not None:
            for p in range(n_par):
                for d in range(2):
                    s = s_ref[p, d]
                    for h in range(HEADS):
                        st_ref[sub0 + p, layer, d, h] = s[:, hs[h]]
                        for earlier in range(layer):
                            st_ref[sub0 + p, earlier, d, h] = sp_ref[sub0 + p, earlier, d, h]
        return carry

    lax.fori_loop(0, n_sub // n_par, sub_body, 0)


def _delta_stage(qkv, ba, alog_row, dtb_row, s0, consts, *, seq_len, layer=0, st_prev=None,
                 cast_weights=()):
    n = qkv.shape[0]
    block = max(seq_len, DELTA_BLOCK)
    n_blocks = n // block
    n_sub = block // seq_len
    n_chunks = seq_len // CHUNK
    n_total = block // CHUNK
    has_init = s0 is not None
    n_par = min(n_sub, DELTA_PAR)
    lvl, eye, tri = consts
    in_specs = [pl.BlockSpec((block, QKV), lambda i: (i, 0)),
                pl.BlockSpec((block, LANES), lambda i: (i, 0)),
                _const_spec((1, LANES)), _const_spec((1, LANES)),
                _const_spec(lvl.shape), _const_spec(eye.shape), _const_spec(tri.shape)]
    args = [qkv, ba, alog_row, dtb_row, lvl, eye, tri]
    o_spec = pl.BlockSpec((block, A_WIDTH), lambda i: (i, 0))
    o_shape = jax.ShapeDtypeStruct((n, A_WIDTH), F32)
    st_shape = (2, HEADS, HEAD_DIM, HEAD_DIM)
    if has_init:
        assert n_sub == 1
        in_specs.append(pl.BlockSpec((None,) + st_shape, lambda i: (i, 0, 0, 0, 0)))
        args.append(s0)
        out_specs, out_shape = o_spec, o_shape
    else:
        if layer > 0:
            in_specs.append(pl.BlockSpec((n_sub, layer) + st_shape, lambda i: (i, 0, 0, 0, 0, 0)))
            args.append(st_prev)
        out_specs = [o_spec,
                     pl.BlockSpec((n_sub, layer + 1) + st_shape, lambda i: (i, 0, 0, 0, 0, 0))]
        out_shape = [o_shape, jax.ShapeDtypeStruct((n // seq_len, layer + 1) + st_shape, F32)]
    scratch = [pltpu.VMEM((n_par, 2, HEAD_DIM, A_WIDTH), F32),
               pltpu.VMEM((block, LANES), F32),
               pltpu.VMEM((block, LANES), F32),
               pltpu.VMEM((n_total, 8, LANES), F32),
               pltpu.VMEM((2, block, A_WIDTH), F32),
               pltpu.VMEM((2, block, A_WIDTH), BF16),
               pltpu.VMEM((2, block, A_WIDTH), BF16),
               pltpu.VMEM((2, block, WIDE), BF16),
               pltpu.VMEM((2, n_total * HEAD_DIM, WIDE), BF16),
               pltpu.VMEM((2, n_total, 8, A_WIDTH), F32)]
    for wt in cast_weights:
        rows = wt.shape[1] // n_blocks
        assert rows * n_blocks == wt.shape[1] and rows % 16 == 0
        in_specs.append(pl.BlockSpec((None, rows, wt.shape[2]), lambda i: (layer, i, 0)))
        args.append(wt)
        out_specs = list(out_specs) if isinstance(out_specs, list) else [out_specs]
        out_shape = list(out_shape) if isinstance(out_shape, list) else [out_shape]
        out_specs.append(pl.BlockSpec((rows, wt.shape[2]), lambda i: (i, 0)))
        out_shape.append(jax.ShapeDtypeStruct(wt.shape[1:], BF16))
    return pl.pallas_call(
        functools.partial(_delta_kernel, n_sub=n_sub, n_chunks=n_chunks, has_init=has_init,
                          n_par=n_par, cb=DELTA_BATCH, layer=layer, n_cast=len(cast_weights)),
        grid=(n_blocks,),
        in_specs=in_specs,
        out_specs=out_specs,
        out_shape=out_shape,
        scratch_shapes=scratch,
        compiler_params=_params(1),
        name="delta_stage",
    )(*args)


def _group_rows(y, n_groups):
    half = y.shape[1] // 2
    gw = half // 2
    lane = lax.broadcasted_iota(jnp.int32, (y.shape[0], half), 1)
    zero = jnp.zeros((y.shape[0], half), y.dtype)
    blocks = []
    for g in range(n_groups):
        part = y[:, (g // 2) * half:(g // 2 + 1) * half]
        keep = (lane < gw) if g % 2 == 0 else (lane >= gw)
        part = jnp.where(keep, part, jnp.zeros_like(part))
        blocks.append(jnp.concatenate([part, zero] if g < 2 else [zero, part], axis=1))
    return jnp.concatenate(blocks, axis=0)


def _mix_delta(o_ref, z_ref, dg_ref, mix_ref):
    dg = dg_ref[...]
    for h in range(HEADS):
        cs = slice(h * HEAD_DIM, (h + 1) * HEAD_DIM)
        y = _rms_rows(o_ref[:, cs]) * dg * _silu(z_ref[:, cs])
        mix_ref[:, cs] = y.astype(BF16)


def _mix_sgu(uv_ref, sg_ref, ws_ref, bs_ref, seg_ref, mix_ref, tile):
    u = _gelu_tanh(uv_ref[:, 0:B_WIDTH])
    v = _gelu_tanh(uv_ref[:, B_WIDTH:])
    vv = v * v
    vv_hi = vv.astype(BF16)
    vv_lo = (vv - vv_hi.astype(F32)).astype(BF16)
    ms = _dot(jnp.concatenate([vv_hi, vv_lo], axis=1), seg_ref[...])
    vn = (v * lax.rsqrt(ms + EPS) * sg_ref[...]).astype(BF16)
    for c in range(tile // SGU_CHUNK):
        rs = slice(c * SGU_CHUNK, (c + 1) * SGU_CHUNK)
        s = _dot(ws_ref[...], _group_rows(vn[rs], B_GROUPS)) + bs_ref[...]
        mix_ref[rs, A_WIDTH:A_WIDTH + B_WIDTH] = (u[rs] * s).astype(BF16)


def _mix_pool(p_ref, band_ref, icnt_ref, wp_ref, ps_ref, mix_ref, tile):
    for c in range(tile // POOL_TILE):
        rs = slice(c * POOL_TILE, (c + 1) * POOL_TILE)
        x = p_ref[rs, :]
        x_hi = x.astype(BF16)
        x_lo = (x - x_hi.astype(F32)).astype(BF16)
        n_win = len(POOL_WINDOWS)
        rhs = jnp.concatenate([_group_rows(x_hi, n_win), _group_rows(x_lo, n_win)], axis=0)
        wsum = _dot(band_ref[...], rhs)
        diff = wsum * icnt_ref[...] - x
        y = _bdot(diff, wp_ref[...]) * ps_ref[...]
        mix_ref[rs, A_WIDTH + B_WIDTH:] = y.astype(BF16)


def _out_kernel(x_ref, o_ref, z_ref, uv_ref, p_ref, dg_ref, sg_ref, ws_ref, bs_ref, seg_ref,
                band_ref, icnt_ref, wp_ref, ps_ref, mod_ref, g2_ref, wof_ref, wgu_ref, wd_ref,
                nf_ref, y_ref, mix_ref, wo_ref, *, seq_len, tile, row0, final_norm):
    i = pl.program_id(0)

    @pl.when(i == 0)
    def _prepare_weights():
        rows = 256
        for r in range(0, D_MODEL, rows):
            wo_ref[r:r + rows, :] = wof_ref[r:r + rows, :].astype(BF16)

    tiles_per_seq = max(seq_len // tile, 1)
    row = row0 + i // tiles_per_seq if row0 else 0
    m = mod_ref[pl.ds(row, 1), :]
    gate1 = m[:, 2 * D_MODEL:3 * D_MODEL]
    shift2 = m[:, 3 * D_MODEL:4 * D_MODEL]
    scale2 = m[:, 4 * D_MODEL:5 * D_MODEL]
    gate2 = m[:, 5 * D_MODEL:6 * D_MODEL]
    _mix_delta(o_ref, z_ref, dg_ref, mix_ref)
    _mix_sgu(uv_ref, sg_ref, ws_ref, bs_ref, seg_ref, mix_ref, tile)
    _mix_pool(p_ref, band_ref, icnt_ref, wp_ref, ps_ref, mix_ref, tile)
    x1 = x_ref[...] + gate1 * _dot(mix_ref[...], wo_ref[...])
    hb = _modulated_norm(x1, g2_ref[...], shift2, scale2).astype(BF16)
    fc = FF_HIDDEN // FF_SPLIT
    ff = jnp.zeros((tile, D_MODEL), F32)
    for c in range(FF_SPLIT):
        gate = _dot(hb, wgu_ref[:, c * fc:(c + 1) * fc])
        up = _dot(hb, wgu_ref[:, FF_HIDDEN + c * fc:FF_HIDDEN + (c + 1) * fc])
        act = (_silu(gate) * up).astype(BF16)
        ff = ff + _dot(act, wd_ref[c * fc:(c + 1) * fc, :])
    x2 = x1 + gate2 * ff
    if final_norm:
        x2 = _rms_rows(x2) * nf_ref[...]
    y_ref[...] = x2


def _output_stage(x, o, z, uv, p, lw, pool_consts, mods_l, norm2_g, w_out, layer, w_gu_b,
                  w_down_b, norm_f, *, seq_len, row0, final_norm):
    n = x.shape[0]
    tile = TOKEN_TILE
    band, icnt = pool_consts
    row = lambda c: pl.BlockSpec((tile, c), lambda i: (i, 0))
    small = [lw["delta_g"], lw["sgu_g"], lw["w_spatial"], lw["b_spatial"], lw["seg"], band, icnt,
             lw["w_pool"], lw["pool_scale"], mods_l, norm2_g]
    consts = small + [w_out, w_gu_b, w_down_b, norm_f]
    in_specs = ([row(D_MODEL), row(A_WIDTH), row(A_WIDTH), row(2 * B_WIDTH), row(C_WIDTH)]
                + [_const_spec(a.shape) for a in small]
                + [_layer_spec(w_out.shape, layer), _const_spec(w_gu_b.shape),
                   _const_spec(w_down_b.shape), _const_spec(norm_f.shape)])
    return pl.pallas_call(
        functools.partial(_out_kernel, seq_len=seq_len, tile=tile, row0=row0,
                          final_norm=final_norm),
        grid=(n // tile,),
        in_specs=in_specs,
        out_specs=row(D_MODEL),
        out_shape=jax.ShapeDtypeStruct((n, D_MODEL), F32),
        scratch_shapes=[pltpu.VMEM((tile, D_MODEL), BF16),
                        pltpu.VMEM((D_MODEL, D_MODEL), BF16)],
        compiler_params=_params(1),
        name="output_stage",
    )(x, o, z, uv, p, *consts)


def _pool_constants(seg_len):
    pos = np.arange(POOL_TILE)
    seg = pos // seg_len
    band = np.zeros((len(POOL_WINDOWS), POOL_TILE, POOL_TILE), np.float32)
    icnt = np.zeros((POOL_TILE, C_WIDTH), np.float32)
    for g, win in enumerate(POOL_WINDOWS):
        lo = pos - win // 2
        hi = pos + win - win // 2
        inside = (pos[None, :] >= lo[:, None]) & (pos[None, :] < hi[:, None]) \
            & (seg[None, :] == seg[:, None])
        band[g] = inside
        icnt[:, g * C_GC:(g + 1) * C_GC] = (1.0 / inside.sum(axis=1))[:, None]
    band_cat = np.concatenate([band[g] for g in range(len(POOL_WINDOWS))] * 2, axis=1)
    return jnp.asarray(band_cat, BF16), jnp.asarray(icnt)


def _grid_pos_tables(rows, d):
    quarter = d // 4
    f = np.float32
    omega = (f(1.0) / (f(10000.0) ** (np.arange(quarter, dtype=f) / f(quarter)))).astype(f)
    r = np.arange(rows, dtype=f)[:, None] * omega
    cl = np.arange(GRID_W, dtype=f)[:, None] * omega
    row_emb = np.concatenate([np.sin(r), np.cos(r)], axis=-1).astype(f)
    col_emb = np.concatenate([np.sin(cl), np.cos(cl)], axis=-1).astype(f)
    return jnp.asarray(row_emb), jnp.asarray(col_emb)


def _pad_lanes(a, width):
    return jnp.pad(a, ((0, 0), (0, width - a.shape[1])))


def _layer_weights(l, w_pool, w_spatial, b_spatial):
    wp = jnp.zeros((C_WIDTH, C_WIDTH), F32)
    for g in range(len(POOL_WINDOWS)):
        wp = wp.at[g * C_GC:(g + 1) * C_GC, g * C_GC:(g + 1) * C_GC].set(w_pool[l, g])
    grp = np.arange(B_WIDTH) // B_GC
    seg1 = (grp[:, None] == grp[None, :]).astype(np.float32) / B_GC
    seg = jnp.asarray(np.concatenate([seg1, seg1], axis=0), BF16)
    bs = jnp.repeat(b_spatial[l].T, B_GC, axis=1)
    ws_cat = jnp.concatenate([w_spatial[l, g] for g in range(B_GROUPS)], axis=1)
    return wp.astype(BF16), seg, ws_cat.astype(BF16), bs


def kernel(x_prompt, x_sample, state_delta, c, c_ctx, w_in, conv_w, a_log, dt_bias, delta_norm_g,
           sgu_norm_g, w_spatial, b_spatial, w_pool, pool_scale, w_out, norm1_g, norm2_g, w_mod,
           b_mod, w_gu, w_down, norm_f):
    batch, seq, d = x_prompt.shape
    dec_batch, dec_seq, _ = x_sample.shape
    cvec = jnp.concatenate([c_ctx[None, :], c, jnp.zeros((8 - 1 - dec_batch, d), F32)], axis=0)
    mods = _modulation(cvec, w_mod, b_mod)
    row_emb, col_emb = _grid_pos_tables(dec_seq // GRID_W, d)
    xs = _add_pos(x_sample, row_emb, col_emb).reshape(dec_batch * dec_seq, d)
    xc = x_prompt.reshape(batch * seq, d)
    pool_ctx = _pool_constants(min(seq, POOL_TILE))
    pool_lat = _pool_constants(GRID_W)
    nf = norm_f.reshape(1, d)
    delta_consts = _delta_constants()
    w_in_b = w_in.astype(BF16)
    ctx_states = None
    for l in range(DEPTH):
        wp, seg, ws, bs = _layer_weights(l, w_pool, w_spatial, b_spatial)
        lw = {"delta_g": delta_norm_g[l].reshape(1, HEAD_DIM),
              "sgu_g": sgu_norm_g[l].reshape(1, B_WIDTH), "w_spatial": ws, "b_spatial": bs,
              "seg": seg, "w_pool": wp, "pool_scale": pool_scale[l].reshape(1, C_WIDTH)}
        alog_row = _pad_lanes(a_log[l].reshape(1, 2 * HEADS), LANES)
        dtb_row = _pad_lanes(dt_bias[l].reshape(1, 2 * HEADS), LANES)
        n1 = norm1_g[l].reshape(1, d)
        n2 = norm2_g[l].reshape(1, d)
        last = l == DEPTH - 1
        s0_lat = state_delta[:, l]
        streams = (("ctx", xc, seq, 0, None, pool_ctx), ("lat", xs, dec_seq, 1, s0_lat, pool_lat))
        outs = []
        for name, x, seq_len, row0, s0, pool_consts in streams:
            qkv, z, ba, uv, p = _input_stage(x, mods[l], n1, w_in_b, l, conv_w[l],
                                             seq_len=seq_len, row0=row0)
            if s0 is None:
                o, ctx_states, w_gu_b, w_down_b = _delta_stage(
                    qkv, ba, alog_row, dtb_row, None, delta_consts, seq_len=seq_len, layer=l,
                    st_prev=ctx_states, cast_weights=(w_gu, w_down))
            else:
                o = _delta_stage(qkv, ba, alog_row, dtb_row, s0, delta_consts, seq_len=seq_len)
            outs.append(_output_stage(x, o, z, uv, p, lw, pool_consts, mods[l], n2, w_out, l,
                                      w_gu_b, w_down_b, nf, seq_len=seq_len, row0=row0,
                                      final_norm=last))
        xc, xs = outs
    y_prompt = xc.reshape(batch, seq, d)
    y_sample = xs.reshape(dec_batch, dec_seq, d)
    return (y_prompt, y_sample, ctx_states)
```
